```python
import math
import jax
import jax.numpy as jnp
from jax import lax
import numpy as np

D_MODEL = 1024
BATCH = 8
SEQ = 2048
DEPTH = 2

F32 = jnp.float32
RMS_EPS = 1e-6
NEG = -1e30
N_ATTN_HEADS = 8
ATTN_HEAD_DIM = 64
ATTN_WIDTH = N_ATTN_HEADS * ATTN_HEAD_DIM
REL_BUCKETS = 32
REL_MAX_DIST = 1024
SSD_HEADS = 24
SSD_HEAD_DIM = 64
SSD_INNER = SSD_HEADS * SSD_HEAD_DIM
SSD_GROUPS = 4
SSD_STATE = 128
SSD_CONV = 4
SSD_CHUNK = 128
SSD_XBC = SSD_INNER + 2 * SSD_GROUPS * SSD_STATE
SWA_KV_HEADS = 2
SWA_WINDOW = 128
SWA_BLOCK = 128
GDN_HEADS = 12
GDN_HEAD_DIM = 128
GDN_INNER = GDN_HEADS * GDN_HEAD_DIM
GDN_CONV = 4
GDN_CHUNK = 64
MOBA_BLOCK = 256
MOBA_TOPK = 3
MOBA_QBLOCK = 128
FFN_DIM = 2816
FFN_CONV = 3
MIX_WIDTH = SSD_INNER + ATTN_WIDTH
AB_SIZES = (SSD_INNER, SSD_XBC, SSD_HEADS, ATTN_WIDTH, SWA_KV_HEADS * ATTN_HEAD_DIM, SWA_KV_HEADS * ATTN_HEAD_DIM)
AB_COLS = sum(AB_SIZES)
AB_SPLITS = tuple(int(v) for v in np.cumsum(AB_SIZES)[:-1])
CD_SIZES = (3 * GDN_INNER, GDN_INNER, GDN_HEADS, GDN_HEADS, ATTN_WIDTH, ATTN_WIDTH, ATTN_WIDTH)
CD_COLS = sum(CD_SIZES)
CD_SPLITS = tuple(int(v) for v in np.cumsum(CD_SIZES)[:-1])
N_EVEN = (DEPTH + 1) // 2
N_ODD = DEPTH // 2

kernel_name = 'hybrid_ssd_swa_gdn_moba_block'


def _rms(x):
    xf = x.astype(F32)
    return xf * lax.rsqrt(jnp.mean(xf * xf, axis=-1, keepdims=True) + RMS_EPS)


def rms_norm(x, w):
    return (_rms(x) * w.astype(F32)).astype(x.dtype)


def l2norm(t):
    t = t.astype(F32)
    return t * lax.rsqrt(jnp.sum(t * t, axis=-1, keepdims=True) + 1e-6)


def causal_dwconv(x, w, b=None):
    width, ch = w.shape
    y = lax.conv_general_dilated(x, w[:, None, :].astype(x.dtype), window_strides=(1,),
                                 padding=((width - 1, 0),), dimension_numbers=('NWC', 'WIO', 'NWC'),
                                 feature_group_count=ch)
    if b is not None:
        y = y + b.astype(x.dtype)
    return y


def rel_bucket(dist):
    max_exact = REL_BUCKETS // 2
    d = jnp.maximum(dist, 0)
    df = jnp.maximum(d, 1).astype(F32)
    large = max_exact + (jnp.log(df / max_exact) / math.log(REL_MAX_DIST / max_exact)
                         * (REL_BUCKETS - max_exact)).astype(jnp.int32)
    large = jnp.minimum(large, REL_BUCKETS - 1)
    return jnp.where(d < max_exact, d, large)


def ssd_chunked(x, dt, a, bm, cm):
    bsz, s, h, p = x.shape
    g, n = bm.shape[2], bm.shape[3]
    r = h // g
    q = SSD_CHUNK
    nc = s // q
    xdt = (x.astype(F32) * dt[..., None]).reshape(bsz, nc, q, g, r, p)
    acs = jnp.cumsum((dt * a).reshape(bsz, nc, q, g, r), axis=2)
    bc = bm.astype(F32).reshape(bsz, nc, q, g, n)
    cc = cm.astype(F32).reshape(bsz, nc, q, g, n)
    tri = jnp.tril(jnp.ones((q, q), bool))[:, :, None, None]
    diff = acs[:, :, :, None] - acs[:, :, None, :]
    lmat = jnp.where(tri, jnp.exp(jnp.where(tri, diff, 0.0)), 0.0)
    cb = jnp.einsum('bclgn,bcsgn->bclsg', cc, bc)
    y_diag = jnp.einsum('bclsg,bclsgr,bcsgrp->bclgrp', cb, lmat, xdt)
    decay_states = jnp.exp(acs[:, :, -1:] - acs)
    states = jnp.einsum('bclgn,bclgr,bclgrp->bcgrpn', bc, decay_states, xdt)
    chunk_decay = jnp.exp(acs[:, :, -1])

    def step(hs, inp):
        st, dec = inp
        return hs * dec[..., None, None] + st, hs

    h0 = jnp.zeros((bsz, g, r, p, n), F32)
    _, h_in = lax.scan(step, h0, (jnp.moveaxis(states, 1, 0), jnp.moveaxis(chunk_decay, 1, 0)))
    h_in = jnp.moveaxis(h_in, 0, 1)
    y_off = jnp.einsum('bclgn,bcgrpn,bclgr->bclgrp', cc, h_in, jnp.exp(acs))
    return (y_diag + y_off).reshape(bsz, s, h, p)


def mamba2_ssd_mixer(z, xbc, dt_raw, conv_w, conv_b, dt_bias, a_log, d_skip, norm_w):
    bsz, s, _ = z.shape
    xbc = jax.nn.silu(causal_dwconv(xbc, conv_w, conv_b))
    xs, bm, cm = jnp.split(xbc, [SSD_INNER, SSD_INNER + SSD_GROUPS * SSD_STATE], axis=-1)
    xs = xs.reshape(bsz, s, SSD_HEADS, SSD_HEAD_DIM)
    dt = jax.nn.softplus(dt_raw.astype(F32) + dt_bias.astype(F32))
    a = -jnp.exp(a_log.astype(F32))
    y = ssd_chunked(xs, dt, a, bm.reshape(bsz, s, SSD_GROUPS, SSD_STATE), cm.reshape(bsz, s, SSD_GROUPS, SSD_STATE))
    y = y + d_skip.astype(F32)[:, None] * xs.astype(F32)
    y = y.reshape(bsz, s, SSD_INNER) * jax.nn.silu(z.astype(F32))
    y = _rms(y.reshape(bsz, s, SSD_GROUPS, SSD_INNER // SSD_GROUPS)).reshape(bsz, s, SSD_INNER)
    return (y * norm_w.astype(F32)).astype(z.dtype)


def swa_sink_attention(q, k, v, sinks, rel_bias):
    bsz, s, hq, dh = q.shape
    hkv = k.shape[2]
    grp = hq // hkv
    blk = SWA_BLOCK
    nb = s // blk
    qb = q.reshape(bsz, nb, blk, hkv, grp, dh)
    kb = k.reshape(bsz, nb, blk, hkv, dh)
    vb = v.reshape(bsz, nb, blk, hkv, dh)

    def band(t):
        prev = jnp.pad(t, ((0, 0), (1, 0), (0, 0), (0, 0), (0, 0)))[:, :-1]
        return jnp.concatenate([prev, t], axis=2)

    kk, vv = band(kb), band(vb)
    dist = blk + jnp.arange(blk)[:, None] - jnp.arange(2 * blk)[None, :]
    kpos = jnp.arange(nb)[:, None] * blk - blk + jnp.arange(2 * blk)[None, :]
    mask = ((dist >= 0) & (dist < SWA_WINDOW))[None, :, :] & (kpos >= 0)[:, None, :]
    bias = rel_bias[rel_bucket(dist)].astype(F32).reshape(blk, 2 * blk, hkv, grp).transpose(2, 3, 0, 1)
    sc = jnp.einsum('bnqhgd,bnkhd->bnhgqk', qb, kk).astype(F32) * (dh ** -0.5) + bias
    sc = jnp.where(mask[None, :, None, None], sc, NEG)
    sink = jnp.broadcast_to(sinks.astype(F32).reshape(hkv, grp)[:, :, None, None], sc.shape[:-1] + (1,))
    p = jax.nn.softmax(jnp.concatenate([sc, sink], axis=-1), axis=-1)[..., :-1]
    o = jnp.einsum('bnhgqk,bnkhd->bnqhgd', p.astype(v.dtype), vv)
    return o.reshape(bsz, s, hq * dh)


def gated_delta_rule(q, k, v, g, beta):
    bsz, s, h, dk = q.shape
    dv = v.shape[-1]
    c = GDN_CHUNK
    nc = s // c

    def chunk(t):
        t = t.astype(F32).reshape((bsz, nc, c, h) + t.shape[3:])
        return jnp.moveaxis(t, (1, 3), (0, 2))

    q = chunk(q) * (dk ** -0.5)
    k, v, g, beta = chunk(k), chunk(v), chunk(g), chunk(beta)
    gc = jnp.cumsum(g, axis=-1)
    tri = jnp.tril(jnp.ones((c, c), bool))
    strict = jnp.tril(jnp.ones((c, c), bool), -1)
    diff = gc[..., :, None] - gc[..., None, :]
    decay = jnp.where(tri, jnp.exp(jnp.where(tri, diff, 0.0)), 0.0)
    kb = k * beta[..., None]
    m = jnp.where(strict, jnp.einsum('nbhcd,nbhsd->nbhcs', kb, k) * decay, 0.0)
    rhs = jnp.concatenate([v * beta[..., None], kb * jnp.exp(gc)[..., None]], axis=-1)
    sol = lax.linalg.triangular_solve(m + jnp.eye(c, dtype=F32), rhs, left_side=True, lower=True)
    u, w = sol[..., :dv], sol[..., dv:]
    attn = jnp.einsum('nbhcd,nbhsd->nbhcs', q, k) * decay
    g_last = gc[..., -1]
    q_dec = q * jnp.exp(gc)[..., None]
    k_dec = k * jnp.exp(g_last[..., None] - gc)[..., None]

    def step(state, xs):
        u_c, w_c, q_c, k_c, a_c, gl = xs
        v_new = u_c - jnp.einsum('bhcd,bhde->bhce', w_c, state)
        o = jnp.einsum('bhcd,bhde->bhce', q_c, state) + jnp.einsum('bhcs,bhse->bhce', a_c, v_new)
        state = state * jnp.exp(gl)[..., None, None] + jnp.einsum('bhcd,bhce->bhde', k_c, v_new)
        return state, o

    s0 = jnp.zeros((bsz, h, dk, dv), F32)
    _, o = lax.scan(step, s0, (u, w, q_dec, k_dec, attn, g_last))
    return jnp.moveaxis(o, (0, 2), (1, 3)).reshape(bsz, s, h, dv)


def moba_attention(q, k, v, rel_bias):
    bsz, s, h, dh = q.shape
    mb, qbs = MOBA_BLOCK, MOBA_QBLOCK
    nblk = -(-s // mb)
    sp = nblk * mb
    nq = sp // qbs
    topk = min(MOBA_TOPK, nblk - 1)
    scale = dh ** -0.5

    def prep(t):
        t = jnp.pad(t, ((0, 0), (0, sp - s), (0, 0), (0, 0)))
        return jnp.transpose(t, (0, 2, 1, 3))

    q, k, v = prep(q), prep(k), prep(v)
    kblk = k.reshape(bsz, h, nblk, mb, dh)
    vblk = v.reshape(bsz, h, nblk, mb, dh)
    kmean = jnp.mean(kblk.astype(F32), axis=3)
    qch = jnp.moveaxis(q.reshape(bsz, h, nq, qbs, dh), 2, 0)
    table = rel_bias.T.astype(F32)
    bi = jnp.arange(bsz)[:, None, None, None]
    hi = jnp.arange(h)[None, :, None, None]

    def one_block(args):
        ci, qi = args
        qpos = ci * qbs + jnp.arange(qbs)
        own = (ci * qbs) // mb
        k_own = lax.dynamic_index_in_dim(kblk, own, axis=2, keepdims=False)
        v_own = lax.dynamic_index_in_dim(vblk, own, axis=2, keepdims=False)
        dist_own = qpos[:, None] - (own * mb + jnp.arange(mb))[None, :]
        s_own = jnp.einsum('bhqd,bhtd->bhqt', qi, k_own).astype(F32) * scale + table[:, rel_bucket(dist_own)]
        s_own = jnp.where(dist_own >= 0, s_own, NEG)
        if topk == 0:
            p_own = jax.nn.softmax(s_own, axis=-1).astype(v_own.dtype)
            return jnp.einsum('bhqt,bhtd->bhqd', p_own, v_own)
        gate = jnp.einsum('bhqd,bhnd->bhqn', qi.astype(F32), kmean)
        gate = jnp.where(jnp.arange(nblk) < own, gate, NEG)
        _, idx = lax.top_k(gate, topk)
        kg = kblk[bi, hi, idx]
        vg = vblk[bi, hi, idx]
        dist_sel = qpos[None, None, :, None, None] - (idx[..., None] * mb + jnp.arange(mb))
        s_sel = (jnp.einsum('bhqd,bhqjtd->bhqjt', qi, kg).astype(F32) * scale
                 + table[hi[..., None], rel_bucket(dist_sel)])
        s_sel = jnp.where((idx < own)[..., None], s_sel, NEG).reshape(bsz, h, qbs, topk * mb)
        p = jax.nn.softmax(jnp.concatenate([s_sel, s_own], axis=-1), axis=-1).astype(v.dtype)
        p_sel = p[..., :topk * mb].reshape(bsz, h, qbs, topk, mb)
        p_own = p[..., topk * mb:]
        return jnp.einsum('bhqjt,bhqjtd->bhqd', p_sel, vg) + jnp.einsum('bhqt,bhtd->bhqd', p_own, v_own)

    out = lax.map(one_block, (jnp.arange(nq), qch))
    out = jnp.moveaxis(out, 0, 2).reshape(bsz, h, sp, dh)[:, :, :s]
    return jnp.transpose(out, (0, 2, 1, 3)).reshape(bsz, s, h * dh)


def ab_mixer(h, w_in, w_out, conv_w, conv_b, dt_bias, a_log, d_skip, norm_w, sinks, rel_bias):
    bsz, s, _ = h.shape
    proj = h @ w_in
    z, xbc, dt_raw, q, k, v = jnp.split(proj, AB_SPLITS, axis=-1)
    y_a = mamba2_ssd_mixer(z, xbc, dt_raw, conv_w, conv_b, dt_bias, a_log, d_skip, norm_w)
    y_b = swa_sink_attention(q.reshape(bsz, s, N_ATTN_HEADS, ATTN_HEAD_DIM),
                             k.reshape(bsz, s, SWA_KV_HEADS, ATTN_HEAD_DIM),
                             v.reshape(bsz, s, SWA_KV_HEADS, ATTN_HEAD_DIM), sinks, rel_bias)
    return jnp.concatenate([y_a, y_b.astype(h.dtype)], axis=-1) @ w_out


def cd_mixer(h, w_in, w_out, conv_w, dt_bias, a_log, norm_w, rel_bias):
    bsz, s, _ = h.shape
    proj = h @ w_in
    qkv, z, b_raw, a_raw, qd, kd, vd = jnp.split(proj, CD_SPLITS, axis=-1)
    qkv = jax.nn.silu(causal_dwconv(qkv, conv_w))
    qc, kc, vc = jnp.split(qkv, 3, axis=-1)
    hd = (bsz, s, GDN_HEADS, GDN_HEAD_DIM)
    beta = jax.nn.sigmoid(b_raw.astype(F32))
    g = -jnp.exp(a_log.astype(F32)) * jax.nn.softplus(a_raw.astype(F32) + dt_bias.astype(F32))
    o = gated_delta_rule(l2norm(qc.reshape(hd)), l2norm(kc.reshape(hd)), vc.reshape(hd), g, beta)
    o = _rms(o) * norm_w.astype(F32) * jax.nn.silu(z.astype(F32).reshape(hd))
    y_c = o.reshape(bsz, s, GDN_INNER).astype(h.dtype)
    ad = (bsz, s, N_ATTN_HEADS, ATTN_HEAD_DIM)
    y_d = moba_attention(qd.reshape(ad), kd.reshape(ad), vd.reshape(ad), rel_bias)
    return jnp.concatenate([y_c, y_d.astype(h.dtype)], axis=-1) @ w_out


def conv_ffn(h, w_up, conv_w, conv_b, w_down):
    u = causal_dwconv(h @ w_up, conv_w, conv_b)
    gate, val = jnp.split(u, 2, axis=-1)
    return (jax.nn.gelu(gate, approximate=True) * val) @ w_down


def setup_inputs(seed: int = 0) -> dict:
    key = jax.random.key(seed)
    ks = iter(jax.random.split(key, 32))

    def nrm(shape, scale):
        return jax.random.normal(next(ks), shape, F32) * scale

    def gain(shape):
        return 1.0 + 0.05 * jax.random.normal(next(ks), shape, F32)

    def dt_bias(shape):
        dt = jnp.exp(jax.random.uniform(next(ks), shape, F32, math.log(1e-3), math.log(1e-1)))
        return dt + jnp.log(-jnp.expm1(-dt))

    def a_log(shape):
        return jnp.log(jax.random.uniform(next(ks), shape, F32, 1.0, 16.0))

    return {
        'x': nrm((BATCH, SEQ, D_MODEL), 1.0),
        'c': nrm((BATCH, D_MODEL), 1.0),
        'rel_bias': nrm((REL_BUCKETS, N_ATTN_HEADS), 0.3),
        'norm_w': gain((DEPTH, 4, D_MODEL)),
        'ada_w': nrm((DEPTH, D_MODEL, 6 * D_MODEL), D_MODEL ** -0.5),
        'ada_b': nrm((DEPTH, 6 * D_MODEL), 0.01),
        'ab_w_in': nrm((N_EVEN, D_MODEL, AB_COLS), D_MODEL ** -0.5),
        'ab_w_out': nrm((N_EVEN, MIX_WIDTH, D_MODEL), MIX_WIDTH ** -0.5),
        'ssd_conv_w': nrm((N_EVEN, SSD_CONV, SSD_XBC), SSD_CONV ** -0.5),
        'ssd_conv_b': nrm((N_EVEN, SSD_XBC), 0.01),
        'ssd_dt_bias': dt_bias((N_EVEN, SSD_HEADS)),
        'ssd_a_log': a_log((N_EVEN, SSD_HEADS)),
        'ssd_d': gain((N_EVEN, SSD_HEADS)),
        'ssd_norm_w': gain((N_EVEN, SSD_INNER)),
        'swa_sinks': nrm((N_EVEN, N_ATTN_HEADS), 0.5),
        'cd_w_in': nrm((N_ODD, D_MODEL, CD_COLS), D_MODEL ** -0.5),
        'cd_w_out': nrm((N_ODD, MIX_WIDTH, D_MODEL), MIX_WIDTH ** -0.5),
        'gdn_conv_w': nrm((N_ODD, GDN_CONV, 3 * GDN_INNER), GDN_CONV ** -0.5),
        'gdn_dt_bias': dt_bias((N_ODD, GDN_HEADS)),
        'gdn_a_log': a_log((N_ODD, GDN_HEADS)),
        'gdn_norm_w': gain((N_ODD, GDN_HEAD_DIM)),
        'ffn_w_up': nrm((DEPTH, D_MODEL, 2 * FFN_DIM), D_MODEL ** -0.5),
        'ffn_conv_w': nrm((DEPTH, FFN_CONV, 2 * FFN_DIM), FFN_CONV ** -0.5),
        'ffn_conv_b': nrm((DEPTH, 2 * FFN_DIM), 0.01),
        'ffn_w_down': nrm((DEPTH, FFN_DIM, D_MODEL), FFN_DIM ** -0.5),
    }


def reference(x, c, rel_bias, norm_w, ada_w, ada_b, ab_w_in, ab_w_out, ssd_conv_w, ssd_conv_b,
              ssd_dt_bias, ssd_a_log, ssd_d, ssd_norm_w, swa_sinks, cd_w_in, cd_w_out, gdn_conv_w,
              gdn_dt_bias, gdn_a_log, gdn_norm_w, ffn_w_up, ffn_conv_w, ffn_conv_b, ffn_w_down):
    mods = jnp.einsum('bd,lde->lbe', jax.nn.silu(c), ada_w) + ada_b[:, None, :]
    for i in range(DEPTH):
        sh_m, sc_m, g_m, sh_f, sc_f, g_f = jnp.split(mods[i][:, None, :].astype(x.dtype), 6, axis=-1)
        h = rms_norm(x, norm_w[i, 0]) * (1 + sc_m) + sh_m
        if i % 2 == 0:
            j = i // 2
            y = ab_mixer(h, ab_w_in[j], ab_w_out[j], ssd_conv_w[j], ssd_conv_b[j], ssd_dt_bias[j],
                         ssd_a_log[j], ssd_d[j], ssd_norm_w[j], swa_sinks[j], rel_bias)
        else:
            j = i // 2
            y = cd_mixer(h, cd_w_in[j], cd_w_out[j], gdn_conv_w[j], gdn_dt_bias[j], gdn_a_log[j],
                         gdn_norm_w[j], rel_bias)
        x = x + g_m * rms_norm(y, norm_w[i, 1])
        h = rms_norm(x, norm_w[i, 2]) * (1 + sc_f) + sh_f
        y = conv_ffn(h, ffn_w_up[i], ffn_conv_w[i], ffn_conv_b[i], ffn_w_down[i])
        x = x + g_f * rms_norm(y, norm_w[i, 3])
    return x
```

```python
import functools
import math

import numpy as np
import jax
import jax.numpy as jnp
from jax import lax
from jax.experimental import pallas as pl
from jax.experimental.pallas import tpu as pltpu

F32 = jnp.float32
BF16 = jnp.bfloat16
HIGHEST = lax.Precision.HIGHEST

D_MODEL = 1024
RMS_EPS = 1e-6
NEG = -1e30
LANES = 128
N_ATTN_HEADS = 8
ATTN_HEAD_DIM = 64
ATTN_WIDTH = N_ATTN_HEADS * ATTN_HEAD_DIM
REL_BUCKETS = 32
REL_MAX_DIST = 1024
SSD_HEADS = 24
SSD_HEAD_DIM = 64
SSD_INNER = SSD_HEADS * SSD_HEAD_DIM
SSD_GROUPS = 4
SSD_STATE = 128
SSD_CONV = 4
SSD_CHUNK = 128
SSD_XBC = SSD_INNER + 2 * SSD_GROUPS * SSD_STATE
SWA_KV_HEADS = 2
SWA_BLOCK = 128
GDN_HEADS = 12
GDN_HEAD_DIM = 128
GDN_INNER = GDN_HEADS * GDN_HEAD_DIM
GDN_CONV = 4
GDN_CHUNK = 64
MOBA_BLOCK = 256
MOBA_TOPK = 3
FFN_DIM = 2816
FFN_CONV = 3

AB_Z, AB_XBC, AB_Q, AB_K, AB_V, AB_DT, AB_COLS_PAD = 0, 1536, 4096, 4608, 4736, 4864, 5120
CD_QKV, CD_Z, CD_QD, CD_KD, CD_VD, CD_BA, CD_COLS_PAD = 0, 4608, 6144, 6656, 7168, 7680, 8192

VMEM_LIMIT = 48 * 1024 * 1024


def _cparams(sem):
    return pltpu.CompilerParams(dimension_semantics=sem, vmem_limit_bytes=VMEM_LIMIT)


def _bdot(a, b):
    return jnp.dot(a.astype(BF16), b.astype(BF16), preferred_element_type=F32)


def _bdot_nt(a, b):
    return lax.dot_general(a.astype(BF16), b.astype(BF16), (((1,), (1,)), ((), ())),
                           preferred_element_type=F32)


def _fdot(a, b):
    return jnp.dot(a, b, preferred_element_type=F32, precision=HIGHEST)


def _softplus(x):
    return jnp.maximum(x, 0.0) + jnp.log(1.0 + jnp.exp(-jnp.abs(x)))


def _sigmoid(x):
    return 1.0 / (1.0 + jnp.exp(-x))


def _silu(x):
    return x * _sigmoid(x)


def _mods_kernel(c_ref, w_ref, b_ref, o_ref):
    o_ref[0] = _bdot(_silu(c_ref[...]), w_ref[0]) + b_ref[0]


def _mods(c, ada_w, ada_b):
    depth, d, n = ada_w.shape
    bsz = c.shape[0]
    tn = 512
    return pl.pallas_call(
        _mods_kernel,
        grid=(depth, n // tn),
        in_specs=[pl.BlockSpec((bsz, d), lambda l, j: (0, 0)),
                  pl.BlockSpec((1, d, tn), lambda l, j: (l, 0, j)),
                  pl.BlockSpec((1, 1, tn), lambda l, j: (l, 0, j))],
        out_specs=pl.BlockSpec((1, bsz, tn), lambda l, j: (l, 0, j)),
        out_shape=jax.ShapeDtypeStruct((depth, bsz, n), F32),
        compiler_params=_cparams(("arbitrary", "arbitrary")),
        name="adaln_mods",
    )(c, ada_w, ada_b.reshape(depth, 1, n))


def _nmm_kernel(x_ref, nw_ref, sc_ref, sh_ref, w_ref, o_ref, h_ref):
    @pl.when(pl.program_id(1) == 0)
    def _():
        x = x_ref[0]
        ms = jnp.mean(x * x, axis=-1, keepdims=True)
        h = x * lax.rsqrt(ms + RMS_EPS) * nw_ref[...]
        h_ref[...] = (h * (1.0 + sc_ref[0]) + sh_ref[0]).astype(BF16)

    o_ref[0] = jnp.dot(h_ref[...], w_ref[...], preferred_element_type=F32).astype(o_ref.dtype)


def _norm_mod_matmul(x, nw, sc, sh, w, tn, out_dtype=F32, tm=512):
    bsz, s, d = x.shape
    n = w.shape[1]
    spt = s // tm
    return pl.pallas_call(
        _nmm_kernel,
        grid=(bsz * spt, n // tn),
        in_specs=[pl.BlockSpec((1, tm, d), lambda i, j: (i // spt, i % spt, 0)),
                  pl.BlockSpec((1, d), lambda i, j: (0, 0)),
                  pl.BlockSpec((1, 1, d), lambda i, j: (i // spt, 0, 0)),
                  pl.BlockSpec((1, 1, d), lambda i, j: (i // spt, 0, 0)),
                  pl.BlockSpec((d, tn), lambda i, j: (0, j))],
        out_specs=pl.BlockSpec((1, tm, tn), lambda i, j: (i // spt, i % spt, j)),
        out_shape=jax.ShapeDtypeStruct((bsz, s, n), out_dtype),
        scratch_shapes=[pltpu.VMEM((tm, d), BF16)],
        compiler_params=_cparams(("arbitrary", "arbitrary")),
        name="norm_mod_matmul",
    )(x, nw.reshape(1, d), sc, sh, w)


def _mmres_kernel(*refs, splits):
    na = len(splits)
    a_refs = refs[:na]
    w_ref, x_ref, g_ref, nw_ref, o_ref = refs[na:]
    acc = None
    lo = 0
    for a_ref, k in zip(a_refs, splits):
        part = jnp.dot(a_ref[0].astype(BF16), w_ref[lo:lo + k, :], preferred_element_type=F32)
        acc = part if acc is None else acc + part
        lo += k
    ms = jnp.mean(acc * acc, axis=-1, keepdims=True)
    y = acc * lax.rsqrt(ms + RMS_EPS) * nw_ref[...]
    o_ref[0] = x_ref[0] + g_ref[0] * y


def _matmul_resid(a_list, w, x, gate, nw, tm=512):
    bsz, s, d = x.shape
    spt = s // tm
    splits = tuple(a.shape[-1] for a in a_list)
    ktot = sum(splits)
    in_specs = [pl.BlockSpec((1, tm, k), lambda i: (i // spt, i % spt, 0)) for k in splits]
    in_specs += [pl.BlockSpec((ktot, d), lambda i: (0, 0)),
                 pl.BlockSpec((1, tm, d), lambda i: (i // spt, i % spt, 0)),
                 pl.BlockSpec((1, 1, d), lambda i: (i // spt, 0, 0)),
                 pl.BlockSpec((1, d), lambda i: (0, 0))]
    return pl.pallas_call(
        functools.partial(_mmres_kernel, splits=splits),
        grid=(bsz * spt,),
        in_specs=in_specs,
        out_specs=pl.BlockSpec((1, tm, d), lambda i: (i // spt, i % spt, 0)),
        out_shape=jax.ShapeDtypeStruct((bsz, s, d), F32),
        compiler_params=_cparams(("arbitrary",)),
        name="matmul_resid",
    )(*a_list, w, x, gate, nw.reshape(1, d))


CONV_HALO = 8


def _causal_conv(halo, main, w, width):
    xin = jnp.concatenate([halo, main], axis=0)
    acc = main * w[width - 1:width, :]
    for s in range(1, width):
        shifted = pltpu.roll(xin, s, axis=0)[CONV_HALO:, :]
        acc = acc + shifted * w[width - 1 - s:width - s, :]
    return acc


def _conv_silu_kernel(halo_ref, x_ref, w_ref, b_ref, o_ref, *, width):
    halo = jnp.where(pl.program_id(1) > 0, halo_ref[0].astype(F32), 0.0)
    y = _causal_conv(halo, x_ref[0].astype(F32), w_ref[...], width) + b_ref[...]
    o_ref[0] = _silu(y).astype(o_ref.dtype)


def _conv_silu(x, col0, ncols, w, b, out_dtype=F32, tr=512, tc=256):
    bsz, s, _ = x.shape
    width = w.shape[0]
    cb0 = col0 // tc
    hb = tr // CONV_HALO
    return pl.pallas_call(
        functools.partial(_conv_silu_kernel, width=width),
        grid=(bsz, s // tr, ncols // tc),
        in_specs=[pl.BlockSpec((1, CONV_HALO, tc), lambda b_, r, j: (b_, jnp.maximum(r * hb - 1, 0), cb0 + j)),
                  pl.BlockSpec((1, tr, tc), lambda b_, r, j: (b_, r, cb0 + j)),
                  pl.BlockSpec((width, tc), lambda b_, r, j: (0, j)),
                  pl.BlockSpec((1, tc), lambda b_, r, j: (0, j))],
        out_specs=pl.BlockSpec((1, tr, tc), lambda b_, r, j: (b_, r, j)),
        out_shape=jax.ShapeDtypeStruct((bsz, s, ncols), out_dtype),
        compiler_params=_cparams(("arbitrary", "arbitrary", "arbitrary")),
        name="conv_silu",
    )(x, x, w, b.reshape(1, ncols))


def _conv_geglu_kernel(hg_ref, xg_ref, hv_ref, xv_ref, wg_ref, bg_ref, wv_ref, bv_ref, o_ref, *, width):
    first = pl.program_id(1) > 0
    hg = jnp.where(first, hg_ref[0].astype(F32), 0.0)
    hv = jnp.where(first, hv_ref[0].astype(F32), 0.0)
    g = _causal_conv(hg, xg_ref[0].astype(F32), wg_ref[...], width) + bg_ref[...]
    v = _causal_conv(hv, xv_ref[0].astype(F32), wv_ref[...], width) + bv_ref[...]
    c = math.sqrt(2.0 / math.pi)
    gelu = 0.5 * g * (1.0 + jnp.tanh(c * (g + 0.044715 * (g * g * g))))
    o_ref[0] = (gelu * v).astype(o_ref.dtype)


def _conv_geglu(u, w, b, out_dtype=BF16, tr=512, tc=256):
    bsz, s, n2 = u.shape
    f = n2 // 2
    width = w.shape[0]
    nb = f // tc
    hb = tr // CONV_HALO
    halo = lambda off: pl.BlockSpec((1, CONV_HALO, tc),
                                    lambda b_, r, j: (b_, jnp.maximum(r * hb - 1, 0), off + j))
    main = lambda off: pl.BlockSpec((1, tr, tc), lambda b_, r, j: (b_, r, off + j))
    wspec = lambda off: pl.BlockSpec((width, tc), lambda b_, r, j: (0, off + j))
    bspec = lambda off: pl.BlockSpec((1, tc), lambda b_, r, j: (0, off + j))
    b2 = b.reshape(1, n2)
    return pl.pallas_call(
        functools.partial(_conv_geglu_kernel, width=width),
        grid=(bsz, s // tr, nb),
        in_specs=[halo(0), main(0), halo(nb), main(nb), wspec(0), bspec(0), wspec(nb), bspec(nb)],
        out_specs=pl.BlockSpec((1, tr, tc), lambda b_, r, j: (b_, r, j)),
        out_shape=jax.ShapeDtypeStruct((bsz, s, f), out_dtype),
        compiler_params=_cparams(("arbitrary", "arbitrary", "arbitrary")),
        name="conv_geglu",
    )(u, u, u, u, w, b2, w, b2)


def _rel_bucket_np(d):
    max_exact = REL_BUCKETS // 2
    d = np.maximum(d, 0)
    df = np.maximum(d, 1).astype(np.float64)
    large = max_exact + (np.log(df / max_exact) / math.log(REL_MAX_DIST / max_exact)
                         * (REL_BUCKETS - max_exact)).astype(np.int32)
    large = np.minimum(large, REL_BUCKETS - 1)
    return np.where(d < max_exact, d, large).astype(np.int32)


def _bias_kernel(tab_ref, idx_ref, o_ref):
    h = pl.program_id(0)
    idx = idx_ref[0]
    acc = jnp.zeros(idx.shape, F32)
    for bkt in range(REL_BUCKETS):
        acc = jnp.where(idx == bkt, tab_ref[bkt, h], acc)
    o_ref[0, 0] = acc


def _bias_tiles(rel_bias, idx):
    t, r, c = idx.shape
    return pl.pallas_call(
        _bias_kernel,
        grid=(N_ATTN_HEADS, t),
        in_specs=[pl.BlockSpec(memory_space=pltpu.SMEM),
                  pl.BlockSpec((1, r, c), lambda h, i: (i, 0, 0))],
        out_specs=pl.BlockSpec((1, 1, r, c), lambda h, i: (h, i, 0, 0)),
        out_shape=jax.ShapeDtypeStruct((N_ATTN_HEADS, t, r, c), F32),
        compiler_params=_cparams(("arbitrary", "arbitrary")),
        name="rel_bias_tiles",
    )(rel_bias, idx)


def _swa_bucket_idx():
    c = np.arange(2 * SWA_BLOCK)[:, None]
    r = np.arange(SWA_BLOCK)[None, :]
    return _rel_bucket_np(SWA_BLOCK + r - c)[None]


def _moba_bucket_idx(nblk):
    c = np.arange(MOBA_BLOCK)[:, None]
    r = np.arange(MOBA_BLOCK)[None, :]
    return np.stack([_rel_bucket_np(m * MOBA_BLOCK + r - c) for m in range(nblk)])


def _expand_heads(v, e):
    hi = v.astype(BF16)
    lo = (v - hi.astype(F32)).astype(BF16)
    return (jnp.dot(hi, e, preferred_element_type=F32) + jnp.dot(lo, e, preferred_element_type=F32))


def _ssd_kernel(xbc_ref, z_ref, dtc_ref, dtr_ref, bias_c_ref, bias_r_ref, alog_c_ref, alog_r_ref,
                dskip_ref, nw_ref, e_ref, o_ref, state_ref):
    q = SSD_CHUNK
    gw = SSD_INNER // SSD_GROUPS
    hpg = SSD_HEADS // SSD_GROUPS

    @pl.when(pl.program_id(1) == 0)
    def _():
        state_ref[...] = jnp.zeros(state_ref.shape, F32)

    row = lax.broadcasted_iota(jnp.int32, (q, q), 0)
    col = lax.broadcasted_iota(jnp.int32, (q, q), 1)
    tril = row >= col
    tri_f = jnp.where(tril, 1.0, 0.0).astype(F32)
    triu_f = jnp.where(row <= col, 1.0, 0.0).astype(F32)

    dt_c = _softplus(dtc_ref[0] + bias_c_ref[...])
    da_c = dt_c * (-jnp.exp(alog_c_ref[...]))
    acs_c = _fdot(tri_f, da_c)
    dt_r = _softplus(dtr_ref[0] + bias_r_ref[...])
    da_r = dt_r * (-jnp.exp(alog_r_ref[...]))
    acs_r = _fdot(da_r, triu_f)

    acs_last = acs_c[q - 1:q, :]
    e = e_ref[...]
    dt_full = _expand_heads(dt_c, e)
    dtdec_full = _expand_heads(dt_c * jnp.exp(acs_last - acs_c), e)
    eacs_full = _expand_heads(jnp.exp(acs_c), e)
    cdecay_full = eacs_full[q - 1:q, :]

    xbc = xbc_ref[0]
    xs = xbc[:, :SSD_INNER]
    xdt = xs * dt_full
    xdec = xs * dtdec_full
    lane_half = lax.broadcasted_iota(jnp.int32, (1, LANES), 1) >> 6

    y_parts = []
    for g in range(SSD_GROUPS):
        b_g = xbc[:, SSD_INNER + g * SSD_STATE:SSD_INNER + (g + 1) * SSD_STATE]
        c_g = xbc[:, SSD_INNER + SSD_GROUPS * SSD_STATE + g * SSD_STATE:
                  SSD_INNER + SSD_GROUPS * SSD_STATE + (g + 1) * SSD_STATE]
        cb = _bdot_nt(c_g, b_g)
        st = state_ref[g]
        y_off = _bdot(c_g, st) * eacs_full[:, g * gw:(g + 1) * gw]
        state_ref[g] = st * cdecay_full[:, g * gw:(g + 1) * gw] + _bdot(b_g.T, xdec[:, g * gw:(g + 1) * gw])
        pair_parts = []
        for pr in range(hpg // 2):
            acc = None
            lo = g * gw + pr * LANES
            x_pair = xdt[:, lo:lo + LANES]
            for half in range(2):
                h = g * hpg + pr * 2 + half
                diff = acs_c[:, h:h + 1] - acs_r[h:h + 1, :]
                lmat = jnp.where(tril, jnp.exp(jnp.where(tril, diff, 0.0)), 0.0)
                part = _bdot(cb * lmat, jnp.where(lane_half == half, x_pair, 0.0))
                acc = part if acc is None else acc + part
            pair_parts.append(acc)
        y_diag = jnp.concatenate(pair_parts, axis=1)
        y = y_diag + y_off + dskip_ref[:, g * gw:(g + 1) * gw] * xs[:, g * gw:(g + 1) * gw]
        y = y * _silu(z_ref[0][:, g * gw:(g + 1) * gw])
        ms = jnp.mean(y * y, axis=-1, keepdims=True)
        y_parts.append(y * lax.rsqrt(ms + RMS_EPS) * nw_ref[:, g * gw:(g + 1) * gw])
    o_ref[0] = jnp.concatenate(y_parts, axis=1).astype(o_ref.dtype)


def _ssd_mixer(xbc_act, proj, dt_rows, dt_bias, a_log, d_skip, norm_w, out_dtype=F32):
    bsz, s, _ = xbc_act.shape
    q = SSD_CHUNK
    pad = LANES - SSD_HEADS
    bias_c = jnp.pad(dt_bias, (0, pad)).reshape(1, LANES)
    alog_c = jnp.pad(a_log, (0, pad)).reshape(1, LANES)
    e_np = np.zeros((LANES, SSD_INNER), np.float32)
    for h in range(SSD_HEADS):
        e_np[h, h * SSD_HEAD_DIM:(h + 1) * SSD_HEAD_DIM] = 1.0
    small = lambda shape: pl.BlockSpec(shape, lambda b_, c: (0, 0))
    return pl.pallas_call(
        _ssd_kernel,
        grid=(bsz, s // q),
        in_specs=[pl.BlockSpec((1, q, SSD_XBC), lambda b_, c: (b_, c, 0)),
                  pl.BlockSpec((1, q, SSD_INNER), lambda b_, c: (b_, c, AB_Z // SSD_INNER)),
                  pl.BlockSpec((1, q, LANES), lambda b_, c: (b_, c, AB_DT // LANES)),
                  pl.BlockSpec((1, LANES, q), lambda b_, c: (b_, 0, c)),
                  small((1, LANES)), small((LANES, 1)), small((1, LANES)), small((LANES, 1)),
                  small((1, SSD_INNER)), small((1, SSD_INNER)), small((LANES, SSD_INNER))],
        out_specs=pl.BlockSpec((1, q, SSD_INNER), lambda b_, c: (b_, c, 0)),
        out_shape=jax.ShapeDtypeStruct((bsz, s, SSD_INNER), out_dtype),
        scratch_shapes=[pltpu.VMEM((SSD_GROUPS, SSD_STATE, SSD_INNER // SSD_GROUPS), F32)],
        compiler_params=_cparams(("arbitrary", "arbitrary")),
        name="ssd_mixer",
    )(xbc_act, proj, proj, dt_rows, bias_c, bias_c.reshape(LANES, 1), alog_c, alog_c.reshape(LANES, 1),
      jnp.repeat(d_skip, SSD_HEAD_DIM).reshape(1, SSD_INNER), norm_w.reshape(1, SSD_INNER),
      jnp.asarray(e_np, BF16))


def _swa_kernel(sink_ref, q_ref, kp_ref, kc_ref, vp_ref, vc_ref, bias_ref, o_ref):
    blk = SWA_BLOCK
    n = pl.program_id(1)
    grp = N_ATTN_HEADS // SWA_KV_HEADS
    scale = ATTN_HEAD_DIM ** -0.5
    kk = jnp.concatenate([kp_ref[0], kc_ref[0]], axis=0)
    vv_t = jnp.concatenate([vp_ref[0], vc_ref[0]], axis=0).T
    lane_half = lax.broadcasted_iota(jnp.int32, (1, LANES), 1) >> 6
    c = lax.broadcasted_iota(jnp.int32, (2 * blk, blk), 0)
    r = lax.broadcasted_iota(jnp.int32, (2 * blk, blk), 1)
    dist = blk + r - c
    valid = (dist >= 0) & (dist < blk) & ((c >= blk) | (n > 0))
    q_all = q_ref[0]
    outs = []
    for kv in range(SWA_KV_HEADS):
        k_own = jnp.where(lane_half == kv, kk, 0.0)
        k_var = [None, None]
        k_var[kv] = k_own
        k_var[1 - kv] = pltpu.roll(k_own, ATTN_HEAD_DIM, axis=1)
        v_t = vv_t[kv * ATTN_HEAD_DIM:(kv + 1) * ATTN_HEAD_DIM, :]
        for gq in range(grp):
            h = kv * grp + gq
            q_tile = q_all[:, (h // 2) * LANES:(h // 2 + 1) * LANES]
            s_t = _bdot_nt(k_var[h % 2], q_tile) * scale + bias_ref[h, 0]
            s_t = jnp.where(valid, s_t, NEG)
            sink = sink_ref[h]
            m = jnp.maximum(jnp.max(s_t, axis=0, keepdims=True), sink)
            p = jnp.exp(s_t - m)
            l = jnp.sum(p, axis=0, keepdims=True) + jnp.exp(sink - m)
            outs.append(_bdot(v_t, p) / l)
    for t in range(N_ATTN_HEADS // 2):
        pair = jnp.concatenate([outs[2 * t], outs[2 * t + 1]], axis=0)
        o_ref[0, :, t * LANES:(t + 1) * LANES] = pair.T.astype(o_ref.dtype)


def _swa_attention(proj, sinks, bias, out_dtype=F32):
    bsz, s, _ = proj.shape
    blk = SWA_BLOCK
    kvw = SWA_KV_HEADS * ATTN_HEAD_DIM
    prev = lambda col: pl.BlockSpec((1, blk, kvw), lambda b_, n: (b_, jnp.maximum(n - 1, 0), col))
    cur = lambda col: pl.BlockSpec((1, blk, kvw), lambda b_, n: (b_, n, col))
    return pl.pallas_call(
        _swa_kernel,
        grid=(bsz, s // blk),
        in_specs=[pl.BlockSpec(memory_space=pltpu.SMEM),
                  pl.BlockSpec((1, blk, ATTN_WIDTH), lambda b_, n: (b_, n, AB_Q // ATTN_WIDTH)),
                  prev(AB_K // kvw), cur(AB_K // kvw), prev(AB_V // kvw), cur(AB_V // kvw),
                  pl.BlockSpec((N_ATTN_HEADS, 1, 2 * blk, blk), lambda b_, n: (0, 0, 0, 0))],
        out_specs=pl.BlockSpec((1, blk, ATTN_WIDTH), lambda b_, n: (b_, n, 0)),
        out_shape=jax.ShapeDtypeStruct((bsz, s, ATTN_WIDTH), out_dtype),
        compiler_params=_cparams(("arbitrary", "arbitrary")),
        name="swa_attention",
    )(sinks, proj, proj, proj, proj, proj, bias)


def _moba_kernel(q_ref, k_ref, v_ref, bias_ref, o_ref, vt_ref, kmean_ref, sel_ref, *, nblk):
    mb = MOBA_BLOCK
    dh = ATTN_HEAD_DIM
    own = pl.program_id(2)
    scale = dh ** -0.5

    @pl.when(own == 0)
    def _():
        means = []
        for j in range(nblk):
            vt_ref[j] = v_ref[0, j * mb:(j + 1) * mb, :].T
            means.append(jnp.mean(k_ref[0, j * mb:(j + 1) * mb, :], axis=0, keepdims=True))
        kmean_ref[...] = jnp.concatenate(means, axis=0)

    lane_half = lax.broadcasted_iota(jnp.int32, (1, LANES), 1) >> 6
    blk_id = lax.broadcasted_iota(jnp.int32, (nblk, mb), 0)
    c = lax.broadcasted_iota(jnp.int32, (mb, mb), 0)
    r = lax.broadcasted_iota(jnp.int32, (mb, mb), 1)
    causal = c <= r
    q = q_ref[0]
    k_own = k_ref[0, pl.ds(pl.multiple_of(own * mb, mb), mb), :]
    vt_own = vt_ref[own]

    carry = []
    qms = []
    for hh in range(2):
        qm = jnp.where(lane_half == hh, q, 0.0)
        qms.append(qm)
        gate = lax.dot_general(kmean_ref[...], qm, (((1,), (1,)), ((), ())),
                               preferred_element_type=F32, precision=HIGHEST)
        gate = jnp.where(blk_id < own, gate, NEG)
        rank = jnp.zeros((nblk, mb), jnp.int32)
        for i in range(nblk):
            gi = gate[i:i + 1, :]
            ahead = (gi > gate) | ((gi == gate) & (i < blk_id))
            rank = rank + jnp.where(ahead, 1, 0)
        sel = (rank < MOBA_TOPK) & (blk_id < own)
        sel_ref[hh] = jnp.where(sel, 1.0, 0.0).astype(F32)
        s_t = _bdot_nt(k_own, qm) * scale + bias_ref[hh, 0]
        s_t = jnp.where(causal, s_t, NEG)
        m = jnp.max(s_t, axis=0, keepdims=True)
        p = jnp.exp(s_t - m)
        l = jnp.sum(p, axis=0, keepdims=True)
        acc = _bdot(vt_own[hh * dh:(hh + 1) * dh, :], p)
        carry += [m, l, acc]

    def body(j, carry):
        k_j = k_ref[0, pl.ds(pl.multiple_of(j * mb, mb), mb), :]
        vt_j = vt_ref[j]
        new = []
        for hh in range(2):
            m, l, acc = carry[3 * hh:3 * hh + 3]
            s_t = _bdot_nt(k_j, qms[hh]) * scale + bias_ref[hh, own - j]
            s_t = jnp.where(sel_ref[hh, pl.ds(j, 1), :] > 0.5, s_t, NEG)
            m_new = jnp.maximum(m, jnp.max(s_t, axis=0, keepdims=True))
            alpha = jnp.exp(m - m_new)
            p = jnp.exp(s_t - m_new)
            l = l * alpha + jnp.sum(p, axis=0, keepdims=True)
            acc = acc * alpha + _bdot(vt_j[hh * dh:(hh + 1) * dh, :], p)
            new += [m_new, l, acc]
        return tuple(new)

    carry = lax.fori_loop(0, own, body, tuple(carry))
    out_t = jnp.concatenate([carry[2] / carry[1], carry[5] / carry[4]], axis=0)
    o_ref[0] = out_t.T.astype(o_ref.dtype)


def _moba_attention(proj, bias, out_dtype=F32):
    bsz, s, _ = proj.shape
    mb = MOBA_BLOCK
    nblk = s // mb
    npair = N_ATTN_HEADS // 2
    return pl.pallas_call(
        functools.partial(_moba_kernel, nblk=nblk),
        grid=(bsz, npair, nblk),
        in_specs=[pl.BlockSpec((1, mb, LANES), lambda b_, p, i: (b_, i, CD_QD // LANES + p)),
                  pl.BlockSpec((1, s, LANES), lambda b_, p, i: (b_, 0, CD_KD // LANES + p)),
                  pl.BlockSpec((1, s, LANES), lambda b_, p, i: (b_, 0, CD_VD // LANES + p)),
                  pl.BlockSpec((2, nblk, mb, mb), lambda b_, p, i: (p, 0, 0, 0))],
        out_specs=pl.BlockSpec((1, mb, LANES), lambda b_, p, i: (b_, i, p)),
        out_shape=jax.ShapeDtypeStruct((bsz, s, ATTN_WIDTH), out_dtype),
        scratch_shapes=[pltpu.VMEM((nblk, LANES, mb), F32),
                        pltpu.VMEM((nblk, LANES), F32),
                        pltpu.VMEM((2, nblk, mb), F32)],
        compiler_params=_cparams(("arbitrary", "arbitrary", "arbitrary")),
        name="moba_attention",
    )(proj, proj, proj, bias)


def _gdn_kernel(qkv_ref, z_ref, bac_ref, bar_ref, bias_c_ref, bias_r_ref, alog_c_ref, alog_r_ref,
                nw_ref, o_ref, state_ref):
    t = 2 * GDN_CHUNK
    ck = GDN_CHUNK
    dk = GDN_HEAD_DIM
    nh = GDN_HEADS

    @pl.when(pl.program_id(1) == 0)
    def _():
        state_ref[...] = jnp.zeros(state_ref.shape, F32)

    row = lax.broadcasted_iota(jnp.int32, (t, t), 0)
    col = lax.broadcasted_iota(jnp.int32, (t, t), 1)
    same = (row >> 6) == (col >> 6)
    tril = same & (row >= col)
    strict = same & (row > col)
    tri_f = jnp.where(tril, 1.0, 0.0).astype(F32)
    triu_f = jnp.where(same & (row <= col), 1.0, 0.0).astype(F32)
    blk_f = jnp.where(same, 1.0, 0.0).astype(F32)
    eye_f = jnp.where(row == col, 1.0, 0.0).astype(F32)
    merge_masks = [((row >> (l + 1)) == (col >> (l + 1))) & (((row >> l) & 1) == 1) & (((col >> l) & 1) == 0)
                   for l in range(int(math.log2(ck)))]

    ba_c = bac_ref[0]
    g_c = -jnp.exp(alog_c_ref[...]) * _softplus(ba_c + bias_c_ref[...])
    gc_c = _fdot(tri_f, g_c)
    gl_c = _fdot(blk_f, g_c)
    g_r = -jnp.exp(alog_r_ref[...]) * _softplus(bar_ref[0] + bias_r_ref[...])
    gc_r = _fdot(g_r, triu_f)

    qkv = qkv_ref[0]
    zeros_half = jnp.zeros((ck, dk), F32)
    for h in range(nh):
        q = qkv[:, h * dk:(h + 1) * dk]
        k = qkv[:, (nh + h) * dk:(nh + h + 1) * dk]
        v = qkv[:, (2 * nh + h) * dk:(2 * nh + h + 1) * dk]
        qn = q * lax.rsqrt(jnp.sum(q * q, axis=-1, keepdims=True) + 1e-6) * (dk ** -0.5)
        kn = k * lax.rsqrt(jnp.sum(k * k, axis=-1, keepdims=True) + 1e-6)
        beta = _sigmoid(ba_c[:, h:h + 1])
        gcc = gc_c[:, nh + h:nh + h + 1]
        gcr = gc_r[nh + h:nh + h + 1, :]
        glc = gl_c[:, nh + h:nh + h + 1]
        decay = jnp.where(tril, jnp.exp(jnp.where(tril, gcc - gcr, 0.0)), 0.0)
        kb = kn * beta
        kk = _bdot_nt(jnp.concatenate([kb, qn], axis=0), kn)
        mm = jnp.where(strict, kk[:t] * decay, 0.0)
        attn = kk[t:] * decay
        egc = jnp.exp(gcc)
        tinv = eye_f - jnp.where(merge_masks[0], mm, 0.0)
        for lvl in range(1, len(merge_masks)):
            c_l = jnp.where(merge_masks[lvl], mm, 0.0)
            tinv = tinv - _bdot(_bdot(tinv, c_l), tinv)
        sol = _bdot(tinv, jnp.concatenate([v * beta, kb * egc], axis=1))
        u = sol[:, :dk]
        w = sol[:, dk:]
        q_dec = qn * egc
        k_dec_t = (kn * jnp.exp(glc - gcc)).T
        st = state_ref[h]
        o_halves = []
        for a in range(2):
            sl = slice(a * ck, (a + 1) * ck)
            r_ = _bdot(jnp.concatenate([w[sl], q_dec[sl]], axis=0), st)
            v_new = u[sl] - r_[:ck]
            v_full = jnp.concatenate([v_new, zeros_half] if a == 0 else [zeros_half, v_new], axis=0)
            o_halves.append(r_[ck:] + _bdot(attn[sl], v_full))
            st = st * jnp.exp(glc[a * ck:a * ck + 1, :]) + _bdot(k_dec_t, v_full)
        state_ref[h] = st
        o = jnp.concatenate(o_halves, axis=0)
        ms = jnp.mean(o * o, axis=-1, keepdims=True)
        y = o * lax.rsqrt(ms + RMS_EPS) * nw_ref[...] * _silu(z_ref[0][:, h * dk:(h + 1) * dk])
        o_ref[0, :, h * dk:(h + 1) * dk] = y.astype(o_ref.dtype)


def _gdn_mixer(qkv_act, proj, ba_rows, dt_bias, a_log, norm_w, out_dtype=F32):
    bsz, s, _ = qkv_act.shape
    t = 2 * GDN_CHUNK
    nh = GDN_HEADS
    bias_c = jnp.pad(dt_bias, (nh, LANES - 2 * nh)).reshape(1, LANES)
    alog_c = jnp.pad(a_log, (nh, LANES - 2 * nh)).reshape(1, LANES)
    small = lambda shape: pl.BlockSpec(shape, lambda b_, c: (0, 0))
    return pl.pallas_call(
        _gdn_kernel,
        grid=(bsz, s // t),
        in_specs=[pl.BlockSpec((1, t, 3 * GDN_INNER), lambda b_, c: (b_, c, 0)),
                  pl.BlockSpec((1, t, GDN_INNER), lambda b_, c: (b_, c, CD_Z // GDN_INNER)),
                  pl.BlockSpec((1, t, LANES), lambda b_, c: (b_, c, CD_BA // LANES)),
                  pl.BlockSpec((1, LANES, t), lambda b_, c: (b_, 0, c)),
                  small((1, LANES)), small((LANES, 1)), small((1, LANES)), small((LANES, 1)),
                  small((1, GDN_HEAD_DIM))],
        out_specs=pl.BlockSpec((1, t, GDN_INNER), lambda b_, c: (b_, c, 0)),
        out_shape=jax.ShapeDtypeStruct((bsz, s, GDN_INNER), out_dtype),
        scratch_shapes=[pltpu.VMEM((nh, GDN_HEAD_DIM, GDN_HEAD_DIM), F32)],
        compiler_params=_cparams(("arbitrary", "arbitrary")),
        name="gdn_mixer",
    )(qkv_act, proj, proj, ba_rows, bias_c, bias_c.reshape(LANES, 1), alog_c, alog_c.reshape(LANES, 1),
      norm_w.reshape(1, GDN_HEAD_DIM))


def _pad_cols(w, n):
    return jnp.pad(w, ((0, 0), (0, n - w.shape[1])))


def kernel(x, c, rel_bias, norm_w, ada_w, ada_b, ab_w_in, ab_w_out, ssd_conv_w, ssd_conv_b,
           ssd_dt_bias, ssd_a_log, ssd_d, ssd_norm_w, swa_sinks, cd_w_in, cd_w_out, gdn_conv_w,
           gdn_dt_bias, gdn_a_log, gdn_norm_w, ffn_w_up, ffn_conv_w, ffn_conv_b, ffn_w_down):
    bsz, s, d = x.shape
    depth = norm_w.shape[0]
    mods = _mods(c, ada_w, ada_b)
    swa_bias = _bias_tiles(rel_bias, jnp.asarray(_swa_bucket_idx()))
    moba_bias = _bias_tiles(rel_bias, jnp.asarray(_moba_bucket_idx(s // MOBA_BLOCK)))

    for i in range(depth):
        sh_m, sc_m, g_m, sh_f, sc_f, g_f = [m.reshape(bsz, 1, d) for m in jnp.split(mods[i], 6, axis=-1)]
        j = i // 2
        if i % 2 == 0:
            w = ab_w_in[j]
            dt0 = SSD_INNER + SSD_XBC
            w_in = jnp.concatenate([w[:, :dt0], w[:, dt0 + SSD_HEADS:], w[:, dt0:dt0 + SSD_HEADS]], axis=1)
            w_in = _pad_cols(w_in, AB_COLS_PAD).astype(BF16)
            proj = _norm_mod_matmul(x, norm_w[i, 0], sc_m, sh_m, w_in, tn=1024)
            xbc_act = _conv_silu(proj, AB_XBC, SSD_XBC, ssd_conv_w[j], ssd_conv_b[j])
            dt_rows = jnp.swapaxes(proj[:, :, AB_DT:AB_DT + LANES], 1, 2)
            y_a = _ssd_mixer(xbc_act, proj, dt_rows, ssd_dt_bias[j], ssd_a_log[j], ssd_d[j], ssd_norm_w[j])
            y_b = _swa_attention(proj, swa_sinks[j], swa_bias)
            x = _matmul_resid([y_a, y_b], ab_w_out[j].astype(BF16), x, g_m, norm_w[i, 1])
        else:
            w = cd_w_in[j]
            ba0 = 4 * GDN_INNER
            w_in = jnp.concatenate([w[:, :ba0], w[:, ba0 + 2 * GDN_HEADS:], w[:, ba0:ba0 + 2 * GDN_HEADS]], axis=1)
            w_in = _pad_cols(w_in, CD_COLS_PAD).astype(BF16)
            proj = _norm_mod_matmul(x, norm_w[i, 0], sc_m, sh_m, w_in, tn=1024)
            zero_b = jnp.zeros((3 * GDN_INNER,), F32)
            qkv_act = _conv_silu(proj, CD_QKV, 3 * GDN_INNER, gdn_conv_w[j], zero_b)
            ba_rows = jnp.swapaxes(proj[:, :, CD_BA:CD_BA + LANES], 1, 2)
            y_c = _gdn_mixer(qkv_act, proj, ba_rows, gdn_dt_bias[j], gdn_a_log[j], gdn_norm_w[j])
            y_d = _moba_attention(proj, moba_bias)
            x = _matmul_resid([y_c, y_d], cd_w_out[j].astype(BF16), x, g_m, norm_w[i, 1])
        u = _norm_mod_matmul(x, norm_w[i, 2], sc_f, sh_f, ffn_w_up[i].astype(BF16), tn=512, out_dtype=BF16)
        act = _conv_geglu(u, ffn_conv_w[i], ffn_conv_b[i])
        x = _matmul_resid([act], ffn_w_down[i].astype(BF16), x, g_f, norm_w[i, 3])
    return x
```

```python
import functools
import math

import numpy as np
import jax
import jax.numpy as jnp
from jax import lax
from jax.experimental import pallas as pl
from jax.experimental.pallas import tpu as pltpu

F32 = jnp.float32
BF16 = jnp.bfloat16
HIGHEST = lax.Precision.HIGHEST

D_MODEL = 1024
RMS_EPS = 1e-6
NEG = -1e30
LANES = 128
N_ATTN_HEADS = 8
ATTN_HEAD_DIM = 64
ATTN_WIDTH = N_ATTN_HEADS * ATTN_HEAD_DIM
REL_BUCKETS = 32
REL_MAX_DIST = 1024
SSD_HEADS = 24
SSD_HEAD_DIM = 64
SSD_INNER = SSD_HEADS * SSD_HEAD_DIM
SSD_GROUPS = 4
SSD_STATE = 128
SSD_CONV = 4
SSD_CHUNK = 128
SSD_XBC = SSD_INNER + 2 * SSD_GROUPS * SSD_STATE
SWA_KV_HEADS = 2
SWA_BLOCK = 128
GDN_HEADS = 12
GDN_HEAD_DIM = 128
GDN_INNER = GDN_HEADS * GDN_HEAD_DIM
GDN_CONV = 4
GDN_CHUNK = 64
MOBA_BLOCK = 256
MOBA_TOPK = 3
FFN_DIM = 2816
FFN_CONV = 3

AB_Z, AB_XBC, AB_Q, AB_K, AB_V, AB_DT, AB_COLS_PAD = 0, 1536, 4096, 4608, 4736, 4864, 5120
CD_QKV, CD_Z, CD_QD, CD_KD, CD_VD, CD_BA, CD_COLS_PAD = 0, 4608, 6144, 6656, 7168, 7680, 8192

VMEM_LIMIT = 48 * 1024 * 1024


def _cparams(sem):
    return pltpu.CompilerParams(dimension_semantics=sem, vmem_limit_bytes=VMEM_LIMIT)


def _bdot(a, b):
    return jnp.dot(a.astype(BF16), b.astype(BF16), preferred_element_type=F32)


def _bdot_nt(a, b):
    return lax.dot_general(a.astype(BF16), b.astype(BF16), (((1,), (1,)), ((), ())),
                           preferred_element_type=F32)


def _fdot(a, b):
    return jnp.dot(a, b, preferred_element_type=F32, precision=HIGHEST)


def _softplus(x):
    return jnp.maximum(x, 0.0) + jnp.log(1.0 + jnp.exp(-jnp.abs(x)))


def _sigmoid(x):
    return 1.0 / (1.0 + jnp.exp(-x))


def _silu(x):
    return x * _sigmoid(x)


def _mods_kernel(c_ref, w_ref, b_ref, o_ref):
    o_ref[0] = _bdot(_silu(c_ref[...]), w_ref[0]) + b_ref[0]


def _mods(c, ada_w, ada_b):
    depth, d, n = ada_w.shape
    bsz = c.shape[0]
    tn = 512
    return pl.pallas_call(
        _mods_kernel,
        grid=(depth, n // tn),
        in_specs=[pl.BlockSpec((bsz, d), lambda l, j: (0, 0)),
                  pl.BlockSpec((1, d, tn), lambda l, j: (l, 0, j)),
                  pl.BlockSpec((1, 1, tn), lambda l, j: (l, 0, j))],
        out_specs=pl.BlockSpec((1, bsz, tn), lambda l, j: (l, 0, j)),
        out_shape=jax.ShapeDtypeStruct((depth, bsz, n), F32),
        compiler_params=_cparams(("arbitrary", "arbitrary")),
        name="adaln_mods",
    )(c, ada_w, ada_b.reshape(depth, 1, n))


def _nmm_kernel(x_ref, nw_ref, sc_ref, sh_ref, w_ref, o_ref, h_ref):
    @pl.when(pl.program_id(1) == 0)
    def _():
        x = x_ref[0]
        ms = jnp.mean(x * x, axis=-1, keepdims=True)
        h = x * lax.rsqrt(ms + RMS_EPS) * nw_ref[...]
        h_ref[...] = (h * (1.0 + sc_ref[0]) + sh_ref[0]).astype(BF16)

    o_ref[0] = jnp.dot(h_ref[...], w_ref[...], preferred_element_type=F32).astype(o_ref.dtype)


def _norm_mod_matmul(x, nw, sc, sh, w, tn, out_dtype=F32, tm=512):
    bsz, s, d = x.shape
    n = w.shape[1]
    spt = s // tm
    return pl.pallas_call(
        _nmm_kernel,
        grid=(bsz * spt, n // tn),
        in_specs=[pl.BlockSpec((1, tm, d), lambda i, j: (i // spt, i % spt, 0)),
                  pl.BlockSpec((1, d), lambda i, j: (0, 0)),
                  pl.BlockSpec((1, 1, d), lambda i, j: (i // spt, 0, 0)),
                  pl.BlockSpec((1, 1, d), lambda i, j: (i // spt, 0, 0)),
                  pl.BlockSpec((d, tn), lambda i, j: (0, j))],
        out_specs=pl.BlockSpec((1, tm, tn), lambda i, j: (i // spt, i % spt, j)),
        out_shape=jax.ShapeDtypeStruct((bsz, s, n), out_dtype),
        scratch_shapes=[pltpu.VMEM((tm, d), BF16)],
        compiler_params=_cparams(("arbitrary", "arbitrary")),
        name="norm_mod_matmul",
    )(x, nw.reshape(1, d), sc, sh, w)


def _mmres_kernel(*refs, splits):
    na = len(splits)
    a_refs = refs[:na]
    w_ref, x_ref, g_ref, nw_ref, o_ref = refs[na:]
    acc = None
    lo = 0
    for a_ref, k in zip(a_refs, splits):
        part = jnp.dot(a_ref[0].astype(BF16), w_ref[lo:lo + k, :], preferred_element_type=F32)
        acc = part if acc is None else acc + part
        lo += k
    ms = jnp.mean(acc * acc, axis=-1, keepdims=True)
    y = acc * lax.rsqrt(ms + RMS_EPS) * nw_ref[...]
    o_ref[0] = x_ref[0] + g_ref[0] * y


def _matmul_resid(a_list, w, x, gate, nw, tm=512):
    bsz, s, d = x.shape
    spt = s // tm
    splits = tuple(a.shape[-1] for a in a_list)
    ktot = sum(splits)
    in_specs = [pl.BlockSpec((1, tm, k), lambda i: (i // spt, i % spt, 0)) for k in splits]
    in_specs += [pl.BlockSpec((ktot, d), lambda i: (0, 0)),
                 pl.BlockSpec((1, tm, d), lambda i: (i // spt, i % spt, 0)),
                 pl.BlockSpec((1, 1, d), lambda i: (i // spt, 0, 0)),
                 pl.BlockSpec((1, d), lambda i: (0, 0))]
    return pl.pallas_call(
        functools.partial(_mmres_kernel, splits=splits),
        grid=(bsz * spt,),
        in_specs=in_specs,
        out_specs=pl.BlockSpec((1, tm, d), lambda i: (i // spt, i % spt, 0)),
        out_shape=jax.ShapeDtypeStruct((bsz, s, d), F32),
        compiler_params=_cparams(("arbitrary",)),
        name="matmul_resid",
    )(*a_list, w, x, gate, nw.reshape(1, d))


CONV_HALO = 8


def _causal_conv(halo, main, w, width):
    xin = jnp.concatenate([halo, main], axis=0)
    acc = main * w[width - 1:width, :]
    for s in range(1, width):
        shifted = pltpu.roll(xin, s, axis=0)[CONV_HALO:, :]
        acc = acc + shifted * w[width - 1 - s:width - s, :]
    return acc


def _conv_silu_kernel(halo_ref, x_ref, w_ref, b_ref, o_ref, *, width):
    halo = jnp.where(pl.program_id(1) > 0, halo_ref[0].astype(F32), 0.0)
    y = _causal_conv(halo, x_ref[0].astype(F32), w_ref[...], width) + b_ref[...]
    o_ref[0] = _silu(y).astype(o_ref.dtype)


def _conv_silu(x, col0, ncols, w, b, out_dtype=F32, tr=512, tc=256):
    bsz, s, _ = x.shape
    width = w.shape[0]
    cb0 = col0 // tc
    hb = tr // CONV_HALO
    return pl.pallas_call(
        functools.partial(_conv_silu_kernel, width=width),
        grid=(bsz, s // tr, ncols // tc),
        in_specs=[pl.BlockSpec((1, CONV_HALO, tc), lambda b_, r, j: (b_, jnp.maximum(r * hb - 1, 0), cb0 + j)),
                  pl.BlockSpec((1, tr, tc), lambda b_, r, j: (b_, r, cb0 + j)),
                  pl.BlockSpec((width, tc), lambda b_, r, j: (0, j)),
                  pl.BlockSpec((1, tc), lambda b_, r, j: (0, j))],
        out_specs=pl.BlockSpec((1, tr, tc), lambda b_, r, j: (b_, r, j)),
        out_shape=jax.ShapeDtypeStruct((bsz, s, ncols), out_dtype),
        compiler_params=_cparams(("arbitrary", "arbitrary", "arbitrary")),
        name="conv_silu",
    )(x, x, w, b.reshape(1, ncols))


def _conv_geglu_kernel(hg_ref, xg_ref, hv_ref, xv_ref, wg_ref, bg_ref, wv_ref, bv_ref, o_ref, *, width):
    first = pl.program_id(1) > 0
    hg = jnp.where(first, hg_ref[0].astype(F32), 0.0)
    hv = jnp.where(first, hv_ref[0].astype(F32), 0.0)
    g = _causal_conv(hg, xg_ref[0].astype(F32), wg_ref[...], width) + bg_ref[...]
    v = _causal_conv(hv, xv_ref[0].astype(F32), wv_ref[...], width) + bv_ref[...]
    c = math.sqrt(2.0 / math.pi)
    gelu = 0.5 * g * (1.0 + jnp.tanh(c * (g + 0.044715 * (g * g * g))))
    o_ref[0] = (gelu * v).astype(o_ref.dtype)


def _conv_geglu(u, w, b, out_dtype=BF16, tr=512, tc=256):
    bsz, s, n2 = u.shape
    f = n2 // 2
    width = w.shape[0]
    nb = f // tc
    hb = tr // CONV_HALO
    halo = lambda off: pl.BlockSpec((1, CONV_HALO, tc),
                                    lambda b_, r, j: (b_, jnp.maximum(r * hb - 1, 0), off + j))
    main = lambda off: pl.BlockSpec((1, tr, tc), lambda b_, r, j: (b_, r, off + j))
    wspec = lambda off: pl.BlockSpec((width, tc), lambda b_, r, j: (0, off + j))
    bspec = lambda off: pl.BlockSpec((1, tc), lambda b_, r, j: (0, off + j))
    b2 = b.reshape(1, n2)
    return pl.pallas_call(
        functools.partial(_conv_geglu_kernel, width=width),
        grid=(bsz, s // tr, nb),
        in_specs=[halo(0), main(0), halo(nb), main(nb), wspec(0), bspec(0), wspec(nb), bspec(nb)],
        out_specs=pl.BlockSpec((1, tr, tc), lambda b_, r, j: (b_, r, j)),
        out_shape=jax.ShapeDtypeStruct((bsz, s, f), out_dtype),
        compiler_params=_cparams(("arbitrary", "arbitrary", "arbitrary")),
        name="conv_geglu",
    )(u, u, u, u, w, b2, w, b2)


def _rel_bucket_np(d):
    max_exact = REL_BUCKETS // 2
    d = np.maximum(d, 0)
    df = np.maximum(d, 1).astype(np.float64)
    large = max_exact + (np.log(df / max_exact) / math.log(REL_MAX_DIST / max_exact)
                         * (REL_BUCKETS - max_exact)).astype(np.int32)
    large = np.minimum(large, REL_BUCKETS - 1)
    return np.where(d < max_exact, d, large).astype(np.int32)


def _bias_kernel(tab_ref, idx_ref, o_ref):
    h = pl.program_id(0)
    idx = idx_ref[0]
    acc = jnp.zeros(idx.shape, F32)
    for bkt in range(REL_BUCKETS):
        acc = jnp.where(idx == bkt, tab_ref[bkt, h], acc)
    o_ref[0, 0] = acc


def _bias_tiles(rel_bias, idx):
    t, r, c = idx.shape
    return pl.pallas_call(
        _bias_kernel,
        grid=(N_ATTN_HEADS, t),
        in_specs=[pl.BlockSpec(memory_space=pltpu.SMEM),
                  pl.BlockSpec((1, r, c), lambda h, i: (i, 0, 0))],
        out_specs=pl.BlockSpec((1, 1, r, c), lambda h, i: (h, i, 0, 0)),
        out_shape=jax.ShapeDtypeStruct((N_ATTN_HEADS, t, r, c), F32),
        compiler_params=_cparams(("arbitrary", "arbitrary")),
        name="rel_bias_tiles",
    )(rel_bias, idx)


def _swa_bucket_idx():
    c = np.arange(2 * SWA_BLOCK)[:, None]
    r = np.arange(SWA_BLOCK)[None, :]
    return _rel_bucket_np(SWA_BLOCK + r - c)[None]


def _moba_bucket_idx(nblk):
    c = np.arange(MOBA_BLOCK)[:, None]
    r = np.arange(MOBA_BLOCK)[None, :]
    return np.stack([_rel_bucket_np(m * MOBA_BLOCK + r - c) for m in range(nblk)])


def _expand_heads(v, e):
    hi = v.astype(BF16)
    lo = (v - hi.astype(F32)).astype(BF16)
    return (jnp.dot(hi, e, preferred_element_type=F32) + jnp.dot(lo, e, preferred_element_type=F32))


def _ssd_kernel(xbc_ref, z_ref, dtc_ref, dtr_ref, bias_c_ref, bias_r_ref, alog_c_ref, alog_r_ref,
                dskip_ref, nw_ref, e_ref, o_ref, state_ref):
    q = SSD_CHUNK
    gw = SSD_INNER // SSD_GROUPS
    hpg = SSD_HEADS // SSD_GROUPS

    @pl.when(pl.program_id(1) == 0)
    def _():
        state_ref[...] = jnp.zeros(state_ref.shape, F32)

    row = lax.broadcasted_iota(jnp.int32, (q, q), 0)
    col = lax.broadcasted_iota(jnp.int32, (q, q), 1)
    tril = row >= col
    tri_f = jnp.where(tril, 1.0, 0.0).astype(F32)
    triu_f = jnp.where(row <= col, 1.0, 0.0).astype(F32)

    dt_c = _softplus(dtc_ref[0] + bias_c_ref[...])
    da_c = dt_c * (-jnp.exp(alog_c_ref[...]))
    acs_c = _fdot(tri_f, da_c)
    dt_r = _softplus(dtr_ref[0] + bias_r_ref[...])
    da_r = dt_r * (-jnp.exp(alog_r_ref[...]))
    acs_r = _fdot(da_r, triu_f)

    acs_last = acs_c[q - 1:q, :]
    e = e_ref[...]
    dt_full = _expand_heads(dt_c, e)
    dtdec_full = _expand_heads(dt_c * jnp.exp(acs_last - acs_c), e)
    eacs_full = _expand_heads(jnp.exp(acs_c), e)
    cdecay_full = eacs_full[q - 1:q, :]

    xbc = xbc_ref[0]
    xs = xbc[:, :SSD_INNER]
    xdt = xs * dt_full
    xdec = xs * dtdec_full
    lane_half = lax.broadcasted_iota(jnp.int32, (1, LANES), 1) >> 6

    y_parts = []
    for g in range(SSD_GROUPS):
        b_g = xbc[:, SSD_INNER + g * SSD_STATE:SSD_INNER + (g + 1) * SSD_STATE]
        c_g = xbc[:, SSD_INNER + SSD_GROUPS * SSD_STATE + g * SSD_STATE:
                  SSD_INNER + SSD_GROUPS * SSD_STATE + (g + 1) * SSD_STATE]
        cb = _bdot_nt(c_g, b_g)
        st = state_ref[g]
        y_off = _bdot(c_g, st) * eacs_full[:, g * gw:(g + 1) * gw]
        state_ref[g] = st * cdecay_full[:, g * gw:(g + 1) * gw] + _bdot(b_g.T, xdec[:, g * gw:(g + 1) * gw])
        pair_parts = []
        for pr in range(hpg // 2):
            acc = None
            lo = g * gw + pr * LANES
            x_pair = xdt[:, lo:lo + LANES]
            for half in range(2):
                h = g * hpg + pr * 2 + half
                diff = acs_c[:, h:h + 1] - acs_r[h:h + 1, :]
                lmat = jnp.where(tril, jnp.exp(jnp.where(tril, diff, 0.0)), 0.0)
                part = _bdot(cb * lmat, jnp.where(lane_half == half, x_pair, 0.0))
                acc = part if acc is None else acc + part
            pair_parts.append(acc)
        y_diag = jnp.concatenate(pair_parts, axis=1)
        y = y_diag + y_off + dskip_ref[:, g * gw:(g + 1) * gw] * xs[:, g * gw:(g + 1) * gw]
        y = y * _silu(z_ref[0][:, g * gw:(g + 1) * gw])
        ms = jnp.mean(y * y, axis=-1, keepdims=True)
        y_parts.append(y * lax.rsqrt(ms + RMS_EPS) * nw_ref[:, g * gw:(g + 1) * gw])
    o_ref[0] = jnp.concatenate(y_parts, axis=1).astype(o_ref.dtype)


def _ssd_mixer(xbc_act, proj, dt_rows, dt_bias, a_log, d_skip, norm_w, out_dtype=F32):
    bsz, s, _ = xbc_act.shape
    q = SSD_CHUNK
    pad = LANES - SSD_HEADS
    bias_c = jnp.pad(dt_bias, (0, pad)).reshape(1, LANES)
    alog_c = jnp.pad(a_log, (0, pad)).reshape(1, LANES)
    e_np = np.zeros((LANES, SSD_INNER), np.float32)
    for h in range(SSD_HEADS):
        e_np[h, h * SSD_HEAD_DIM:(h + 1) * SSD_HEAD_DIM] = 1.0
    small = lambda shape: pl.BlockSpec(shape, lambda b_, c: (0, 0))
    return pl.pallas_call(
        _ssd_kernel,
        grid=(bsz, s // q),
        in_specs=[pl.BlockSpec((1, q, SSD_XBC), lambda b_, c: (b_, c, 0)),
                  pl.BlockSpec((1, q, SSD_INNER), lambda b_, c: (b_, c, AB_Z // SSD_INNER)),
                  pl.BlockSpec((1, q, LANES), lambda b_, c: (b_, c, AB_DT // LANES)),
                  pl.BlockSpec((1, LANES, q), lambda b_, c: (b_, 0, c)),
                  small((1, LANES)), small((LANES, 1)), small((1, LANES)), small((LANES, 1)),
                  small((1, SSD_INNER)), small((1, SSD_INNER)), small((LANES, SSD_INNER))],
        out_specs=pl.BlockSpec((1, q, SSD_INNER), lambda b_, c: (b_, c, 0)),
        out_shape=jax.ShapeDtypeStruct((bsz, s, SSD_INNER), out_dtype),
        scratch_shapes=[pltpu.VMEM((SSD_GROUPS, SSD_STATE, SSD_INNER // SSD_GROUPS), F32)],
        compiler_params=_cparams(("arbitrary", "arbitrary")),
        name="ssd_mixer",
    )(xbc_act, proj, proj, dt_rows, bias_c, bias_c.reshape(LANES, 1), alog_c, alog_c.reshape(LANES, 1),
      jnp.repeat(d_skip, SSD_HEAD_DIM).reshape(1, SSD_INNER), norm_w.reshape(1, SSD_INNER),
      jnp.asarray(e_np, BF16))


def _swa_kernel(sink_ref, q_ref, kp_ref, kc_ref, vp_ref, vc_ref, bias_ref, o_ref):
    blk = SWA_BLOCK
    n = pl.program_id(1)
    grp = N_ATTN_HEADS // SWA_KV_HEADS
    scale = ATTN_HEAD_DIM ** -0.5
    kk = jnp.concatenate([kp_ref[0], kc_ref[0]], axis=0)
    vv_t = jnp.concatenate([vp_ref[0], vc_ref[0]], axis=0).T
    lane_half = lax.broadcasted_iota(jnp.int32, (1, LANES), 1) >> 6
    c = lax.broadcasted_iota(jnp.int32, (2 * blk, blk), 0)
    r = lax.broadcasted_iota(jnp.int32, (2 * blk, blk), 1)
    dist = blk + r - c
    valid = (dist >= 0) & (dist < blk) & ((c >= blk) | (n > 0))
    q_all = q_ref[0]
    outs = []
    for kv in range(SWA_KV_HEADS):
        k_own = jnp.where(lane_half == kv, kk, 0.0)
        k_var = [None, None]
        k_var[kv] = k_own
        k_var[1 - kv] = pltpu.roll(k_own, ATTN_HEAD_DIM, axis=1)
        v_t = vv_t[kv * ATTN_HEAD_DIM:(kv + 1) * ATTN_HEAD_DIM, :]
        for gq in range(grp):
            h = kv * grp + gq
            q_tile = q_all[:, (h // 2) * LANES:(h // 2 + 1) * LANES]
            s_t = _bdot_nt(k_var[h % 2], q_tile) * scale + bias_ref[h, 0]
            s_t = jnp.where(valid, s_t, NEG)
            sink = sink_ref[h]
            m = jnp.maximum(jnp.max(s_t, axis=0, keepdims=True), sink)
            p = jnp.exp(s_t - m)
            l = jnp.sum(p, axis=0, keepdims=True) + jnp.exp(sink - m)
            outs.append(_bdot(v_t, p) / l)
    for t in range(N_ATTN_HEADS // 2):
        pair = jnp.concatenate([outs[2 * t], outs[2 * t + 1]], axis=0)
        o_ref[0, :, t * LANES:(t + 1) * LANES] = pair.T.astype(o_ref.dtype)


def _swa_attention(proj, sinks, bias, out_dtype=F32):
    bsz, s, _ = proj.shape
    blk = SWA_BLOCK
    kvw = SWA_KV_HEADS * ATTN_HEAD_DIM
    prev = lambda col: pl.BlockSpec((1, blk, kvw), lambda b_, n: (b_, jnp.maximum(n - 1, 0), col))
    cur = lambda col: pl.BlockSpec((1, blk, kvw), lambda b_, n: (b_, n, col))
    return pl.pallas_call(
        _swa_kernel,
        grid=(bsz, s // blk),
        in_specs=[pl.BlockSpec(memory_space=pltpu.SMEM),
                  pl.BlockSpec((1, blk, ATTN_WIDTH), lambda b_, n: (b_, n, AB_Q // ATTN_WIDTH)),
                  prev(AB_K // kvw), cur(AB_K // kvw), prev(AB_V // kvw), cur(AB_V // kvw),
                  pl.BlockSpec((N_ATTN_HEADS, 1, 2 * blk, blk), lambda b_, n: (0, 0, 0, 0))],
        out_specs=pl.BlockSpec((1, blk, ATTN_WIDTH), lambda b_, n: (b_, n, 0)),
        out_shape=jax.ShapeDtypeStruct((bsz, s, ATTN_WIDTH), out_dtype),
        compiler_params=_cparams(("arbitrary", "arbitrary")),
        name="swa_attention",
    )(sinks, proj, proj, proj, proj, proj, bias)


def _moba_kernel(q_ref, k_ref, v_ref, bias_ref, o_ref, vt_ref, kmean_ref, sel_ref, *, nblk):
    mb = MOBA_BLOCK
    dh = ATTN_HEAD_DIM
    own = pl.program_id(2)
    scale = dh ** -0.5

    @pl.when(own == 0)
    def _():
        means = []
        for j in range(nblk):
            vt_ref[j] = v_ref[0, j * mb:(j + 1) * mb, :].T
            means.append(jnp.mean(k_ref[0, j * mb:(j + 1) * mb, :], axis=0, keepdims=True))
        kmean_ref[...] = jnp.concatenate(means, axis=0)

    lane_half = lax.broadcasted_iota(jnp.int32, (1, LANES), 1) >> 6
    blk_id = lax.broadcasted_iota(jnp.int32, (nblk, mb), 0)
    c = lax.broadcasted_iota(jnp.int32, (mb, mb), 0)
    r = lax.broadcasted_iota(jnp.int32, (mb, mb), 1)
    causal = c <= r
    q = q_ref[0]
    k_own = k_ref[0, pl.ds(pl.multiple_of(own * mb, mb), mb), :]
    vt_own = vt_ref[own]

    carry = []
    qms = []
    for hh in range(2):
        qm = jnp.where(lane_half == hh, q, 0.0)
        qms.append(qm)
        gate = lax.dot_general(kmean_ref[...], qm, (((1,), (1,)), ((), ())),
                               preferred_element_type=F32, precision=HIGHEST)
        gate = jnp.where(blk_id < own, gate, NEG)
        rank = jnp.zeros((nblk, mb), jnp.int32)
        for i in range(nblk):
            gi = gate[i:i + 1, :]
            ahead = (gi > gate) | ((gi == gate) & (i < blk_id))
            rank = rank + jnp.where(ahead, 1, 0)
        sel = (rank < MOBA_TOPK) & (blk_id < own)
        sel_ref[hh] = jnp.where(sel, 1.0, 0.0).astype(F32)
        s_t = _bdot_nt(k_own, qm) * scale + bias_ref[hh, 0]
        s_t = jnp.where(causal, s_t, NEG)
        m = jnp.max(s_t, axis=0, keepdims=True)
        p = jnp.exp(s_t - m)
        l = jnp.sum(p, axis=0, keepdims=True)
        acc = _bdot(vt_own[hh * dh:(hh + 1) * dh, :], p)
        carry += [m, l, acc]

    def body(j, carry):
        k_j = k_ref[0, pl.ds(pl.multiple_of(j * mb, mb), mb), :]
        vt_j = vt_ref[j]
        new = []
        for hh in range(2):
            m, l, acc = carry[3 * hh:3 * hh + 3]
            s_t = _bdot_nt(k_j, qms[hh]) * scale + bias_ref[hh, own - j]
            s_t = jnp.where(sel_ref[hh, pl.ds(j, 1), :] > 0.5, s_t, NEG)
            m_new = jnp.maximum(m, jnp.max(s_t, axis=0, keepdims=True))
            alpha = jnp.exp(m - m_new)
            p = jnp.exp(s_t - m_new)
            l = l * alpha + jnp.sum(p, axis=0, keepdims=True)
            acc = acc * alpha + _bdot(vt_j[hh * dh:(hh + 1) * dh, :], p)
            new += [m_new, l, acc]
        return tuple(new)

    carry = lax.fori_loop(0, own, body, tuple(carry))
    out_t = jnp.concatenate([carry[2] / carry[1], carry[5] / carry[4]], axis=0)
    o_ref[0] = out_t.T.astype(o_ref.dtype)


def _moba_attention(proj, bias, out_dtype=F32):
    bsz, s, _ = proj.shape
    mb = MOBA_BLOCK
    nblk = s // mb
    npair = N_ATTN_HEADS // 2
    return pl.pallas_call(
        functools.partial(_moba_kernel, nblk=nblk),
        grid=(bsz, npair, nblk),
        in_specs=[pl.BlockSpec((1, mb, LANES), lambda b_, p, i: (b_, i, CD_QD // LANES + p)),
                  pl.BlockSpec((1, s, LANES), lambda b_, p, i: (b_, 0, CD_KD // LANES + p)),
                  pl.BlockSpec((1, s, LANES), lambda b_, p, i: (b_, 0, CD_VD // LANES + p)),
                  pl.BlockSpec((2, nblk, mb, mb), lambda b_, p, i: (p, 0, 0, 0))],
        out_specs=pl.BlockSpec((1, mb, LANES), lambda b_, p, i: (b_, i, p)),
        out_shape=jax.ShapeDtypeStruct((bsz, s, ATTN_WIDTH), out_dtype),
        scratch_shapes=[pltpu.VMEM((nblk, LANES, mb), F32),
                        pltpu.VMEM((nblk, LANES), F32),
                        pltpu.VMEM((2, nblk, mb), F32)],
        compiler_params=_cparams(("arbitrary", "arbitrary", "arbitrary")),
        name="moba_attention",
    )(proj, proj, proj, bias)


def _gdn_kernel(qkv_ref, z_ref, bac_ref, bar_ref, bias_c_ref, bias_r_ref, alog_c_ref, alog_r_ref,
                nw_ref, o_ref, state_ref, m_ref, attn_ref, t_ref, x_ref, rhs_ref, u_ref, wq_ref, kdt_ref,
                r_ref, glc_ref):
    t = 2 * GDN_CHUNK
    ck = GDN_CHUNK
    dk = GDN_HEAD_DIM
    nh = GDN_HEADS

    @pl.when(pl.program_id(1) == 0)
    def _():
        state_ref[...] = jnp.zeros(state_ref.shape, F32)

    row = lax.broadcasted_iota(jnp.int32, (t, t), 0)
    col = lax.broadcasted_iota(jnp.int32, (t, t), 1)
    same = (row >> 6) == (col >> 6)
    tril = same & (row >= col)
    strict = same & (row > col)
    tri_f = jnp.where(tril, 1.0, 0.0).astype(F32)
    triu_f = jnp.where(same & (row <= col), 1.0, 0.0).astype(F32)
    blk_f = jnp.where(same, 1.0, 0.0).astype(F32)
    eye_f = jnp.where(row == col, 1.0, 0.0).astype(F32)
    merge_masks = [((row >> (l + 1)) == (col >> (l + 1))) & (((row >> l) & 1) == 1) & (((col >> l) & 1) == 0)
                   for l in range(int(math.log2(ck)))]

    ba_c = bac_ref[0]
    g_c = -jnp.exp(alog_c_ref[...]) * _softplus(ba_c + bias_c_ref[...])
    gc_c = _fdot(tri_f, g_c)
    gl_c = _fdot(blk_f, g_c)
    g_r = -jnp.exp(alog_r_ref[...]) * _softplus(bar_ref[0] + bias_r_ref[...])
    gc_r = _fdot(g_r, triu_f)

    glc_ref[...] = gl_c
    mask_bf = [jnp.where(m, 1.0, 0.0).astype(BF16) for m in merge_masks[1:]]

    for h in range(nh):
        q = qkv_ref[0, :, h * dk:(h + 1) * dk]
        k = qkv_ref[0, :, (nh + h) * dk:(nh + h + 1) * dk]
        v = qkv_ref[0, :, (2 * nh + h) * dk:(2 * nh + h + 1) * dk]
        qn = q * lax.rsqrt(jnp.sum(q * q, axis=-1, keepdims=True) + 1e-6) * (dk ** -0.5)
        kn = k * lax.rsqrt(jnp.sum(k * k, axis=-1, keepdims=True) + 1e-6)
        beta = _sigmoid(ba_c[:, h:h + 1])
        gcc = gc_c[:, nh + h:nh + h + 1]
        gcr = gc_r[nh + h:nh + h + 1, :]
        glc = gl_c[:, nh + h:nh + h + 1]
        decay = jnp.where(tril, jnp.exp(jnp.where(tril, gcc - gcr, 0.0)), 0.0)
        kb = kn * beta
        kk = _bdot_nt(jnp.concatenate([kb, qn], axis=0), kn)
        mm = jnp.where(strict, kk[:t] * decay, 0.0)
        m_ref[h] = mm.astype(BF16)
        attn_ref[h] = (kk[t:] * decay).astype(BF16)
        t_ref[h] = eye_f - jnp.where(merge_masks[0], mm, 0.0)
        egc = jnp.exp(gcc)
        rhs_ref[h] = jnp.concatenate([v * beta, kb * egc], axis=1).astype(BF16)
        q_dec = (qn * egc).astype(BF16)
        for a in range(2):
            wq_ref[h, a, ck:, :] = q_dec[a * ck:(a + 1) * ck]
        kdt_ref[h] = (kn * jnp.exp(glc - gcc)).T.astype(BF16)

    for lvl in range(len(mask_bf)):
        for h in range(nh):
            x_ref[h] = jnp.dot(t_ref[h].astype(BF16), m_ref[h] * mask_bf[lvl],
                               preferred_element_type=F32).astype(BF16)
        for h in range(nh):
            t_h = t_ref[h]
            t_ref[h] = t_h - jnp.dot(x_ref[h], t_h.astype(BF16), preferred_element_type=F32)

    for h in range(nh):
        sol = jnp.dot(t_ref[h].astype(BF16), rhs_ref[h], preferred_element_type=F32)
        u_ref[h] = sol[:, :dk]
        for a in range(2):
            wq_ref[h, a, :ck, :] = sol[a * ck:(a + 1) * ck, dk:].astype(BF16)

    zeros_half = jnp.zeros((ck, dk), F32)
    for a in range(2):
        sl = slice(a * ck, (a + 1) * ck)
        for h in range(nh):
            r_ref[h] = jnp.dot(wq_ref[h, a], state_ref[h].astype(BF16), preferred_element_type=F32)
        for h in range(nh):
            v_new = u_ref[h, sl, :] - r_ref[h, :ck, :]
            v_full = jnp.concatenate([v_new, zeros_half] if a == 0 else [zeros_half, v_new], axis=0).astype(BF16)
            o = r_ref[h, ck:, :] + jnp.dot(attn_ref[h, sl, :], v_full, preferred_element_type=F32)
            gl = glc_ref[a * ck:a * ck + 1, nh + h:nh + h + 1]
            state_ref[h] = state_ref[h] * jnp.exp(gl) + jnp.dot(kdt_ref[h], v_full, preferred_element_type=F32)
            ms = jnp.mean(o * o, axis=-1, keepdims=True)
            y = o * lax.rsqrt(ms + RMS_EPS) * nw_ref[...] * _silu(z_ref[0, sl, h * dk:(h + 1) * dk])
            o_ref[0, sl, h * dk:(h + 1) * dk] = y.astype(o_ref.dtype)


def _gdn_mixer(qkv_act, proj, ba_rows, dt_bias, a_log, norm_w, out_dtype=F32):
    bsz, s, _ = qkv_act.shape
    t = 2 * GDN_CHUNK
    nh = GDN_HEADS
    bias_c = jnp.pad(dt_bias, (nh, LANES - 2 * nh)).reshape(1, LANES)
    alog_c = jnp.pad(a_log, (nh, LANES - 2 * nh)).reshape(1, LANES)
    small = lambda shape: pl.BlockSpec(shape, lambda b_, c: (0, 0))
    return pl.pallas_call(
        _gdn_kernel,
        grid=(bsz, s // t),
        in_specs=[pl.BlockSpec((1, t, 3 * GDN_INNER), lambda b_, c: (b_, c, 0)),
                  pl.BlockSpec((1, t, GDN_INNER), lambda b_, c: (b_, c, CD_Z // GDN_INNER)),
                  pl.BlockSpec((1, t, LANES), lambda b_, c: (b_, c, CD_BA // LANES)),
                  pl.BlockSpec((1, LANES, t), lambda b_, c: (b_, 0, c)),
                  small((1, LANES)), small((LANES, 1)), small((1, LANES)), small((LANES, 1)),
                  small((1, GDN_HEAD_DIM))],
        out_specs=pl.BlockSpec((1, t, GDN_INNER), lambda b_, c: (b_, c, 0)),
        out_shape=jax.ShapeDtypeStruct((bsz, s, GDN_INNER), out_dtype),
        scratch_shapes=[pltpu.VMEM((nh, GDN_HEAD_DIM, GDN_HEAD_DIM), F32),
                        pltpu.VMEM((nh, t, t), BF16),
                        pltpu.VMEM((nh, t, t), BF16),
                        pltpu.VMEM((nh, t, t), F32),
                        pltpu.VMEM((nh, t, t), BF16),
                        pltpu.VMEM((nh, t, 2 * GDN_HEAD_DIM), BF16),
                        pltpu.VMEM((nh, t, GDN_HEAD_DIM), F32),
                        pltpu.VMEM((nh, 2, t, GDN_HEAD_DIM), BF16),
                        pltpu.VMEM((nh, GDN_HEAD_DIM, t), BF16),
                        pltpu.VMEM((nh, t, GDN_HEAD_DIM), F32),
                        pltpu.VMEM((t, LANES), F32)],
        compiler_params=_cparams(("arbitrary", "arbitrary")),
        name="gdn_mixer",
    )(qkv_act, proj, proj, ba_rows, bias_c, bias_c.reshape(LANES, 1), alog_c, alog_c.reshape(LANES, 1),
      norm_w.reshape(1, GDN_HEAD_DIM))


def _pad_cols(w, n):
    return jnp.pad(w, ((0, 0), (0, n - w.shape[1])))


def kernel(x, c, rel_bias, norm_w, ada_w, ada_b, ab_w_in, ab_w_out, ssd_conv_w, ssd_conv_b,
           ssd_dt_bias, ssd_a_log, ssd_d, ssd_norm_w, swa_sinks, cd_w_in, cd_w_out, gdn_conv_w,
           gdn_dt_bias, gdn_a_log, gdn_norm_w, ffn_w_up, ffn_conv_w, ffn_conv_b, ffn_w_down):
    bsz, s, d = x.shape
    depth = norm_w.shape[0]
    mods = _mods(c, ada_w, ada_b)
    swa_bias = _bias_tiles(rel_bias, jnp.asarray(_swa_bucket_idx()))
    moba_bias = _bias_tiles(rel_bias, jnp.asarray(_moba_bucket_idx(s // MOBA_BLOCK)))

    for i in range(depth):
        sh_m, sc_m, g_m, sh_f, sc_f, g_f = [m.reshape(bsz, 1, d) for m in jnp.split(mods[i], 6, axis=-1)]
        j = i // 2
        if i % 2 == 0:
            w = ab_w_in[j]
            dt0 = SSD_INNER + SSD_XBC
            w_in = jnp.concatenate([w[:, :dt0], w[:, dt0 + SSD_HEADS:], w[:, dt0:dt0 + SSD_HEADS]], axis=1)
            w_in = _pad_cols(w_in, AB_COLS_PAD).astype(BF16)
            proj = _norm_mod_matmul(x, norm_w[i, 0], sc_m, sh_m, w_in, tn=1024)
            xbc_act = _conv_silu(proj, AB_XBC, SSD_XBC, ssd_conv_w[j], ssd_conv_b[j])
            dt_rows = jnp.swapaxes(proj[:, :, AB_DT:AB_DT + LANES], 1, 2)
            y_a = _ssd_mixer(xbc_act, proj, dt_rows, ssd_dt_bias[j], ssd_a_log[j], ssd_d[j], ssd_norm_w[j])
            y_b = _swa_attention(proj, swa_sinks[j], swa_bias)
            x = _matmul_resid([y_a, y_b], ab_w_out[j].astype(BF16), x, g_m, norm_w[i, 1])
        else:
            w = cd_w_in[j]
            ba0 = 4 * GDN_INNER
            w_in = jnp.concatenate([w[:, :ba0], w[:, ba0 + 2 * GDN_HEADS:], w[:, ba0:ba0 + 2 * GDN_HEADS]], axis=1)
            w_in = _pad_cols(w_in, CD_COLS_PAD).astype(BF16)
            proj = _norm_mod_matmul(x, norm_w[i, 0], sc_m, sh_m, w_in, tn=1024)
            zero_b = jnp.zeros((3 * GDN_INNER,), F32)
            qkv_act = _conv_silu(proj, CD_QKV, 3 * GDN_INNER, gdn_conv_w[j], zero_b)
            ba_rows = jnp.swapaxes(proj[:, :, CD_BA:CD_BA + LANES], 1, 2)
            y_c = _gdn_mixer(qkv_act, proj, ba_rows, gdn_dt_bias[j], gdn_a_log[j], gdn_norm_w[j])
            y_d = _moba_attention(proj, moba_bias)
            x = _matmul_resid([y_c, y_d], cd_w_out[j].astype(BF16), x, g_m, norm_w[i, 1])
        u = _norm_mod_matmul(x, norm_w[i, 2], sc_f, sh_f, ffn_w_up[i].astype(BF16), tn=512, out_dtype=BF16)
        act = _conv_geglu(u, ffn_conv_w[i], ffn_conv_b[i])
        x = _matmul_resid([act], ffn_w_down[i].astype(BF16), x, g_f, norm_w[i, 3])
    return x
```

```python
import functools
import math

import numpy as np
import jax
import jax.numpy as jnp
from jax import lax
from jax.experimental import pallas as pl
from jax.experimental.pallas import tpu as pltpu

F32 = jnp.float32
BF16 = jnp.bfloat16
HIGHEST = lax.Precision.HIGHEST

D_MODEL = 1024
RMS_EPS = 1e-6
NEG = -1e30
LANES = 128
N_ATTN_HEADS = 8
ATTN_HEAD_DIM = 64
ATTN_WIDTH = N_ATTN_HEADS * ATTN_HEAD_DIM
REL_BUCKETS = 32
REL_MAX_DIST = 1024
SSD_HEADS = 24
SSD_HEAD_DIM = 64
SSD_INNER = SSD_HEADS * SSD_HEAD_DIM
SSD_GROUPS = 4
SSD_STATE = 128
SSD_CONV = 4
SSD_CHUNK = 128
SSD_XBC = SSD_INNER + 2 * SSD_GROUPS * SSD_STATE
SWA_KV_HEADS = 2
SWA_BLOCK = 128
GDN_HEADS = 12
GDN_HEAD_DIM = 128
GDN_INNER = GDN_HEADS * GDN_HEAD_DIM
GDN_CONV = 4
GDN_CHUNK = 64
MOBA_BLOCK = 256
MOBA_TOPK = 3
FFN_DIM = 2816
FFN_CONV = 3

AB_Z, AB_XBC, AB_Q, AB_K, AB_V, AB_DT, AB_COLS_PAD = 0, 1536, 4096, 4608, 4736, 4864, 5120
CD_QKV, CD_Z, CD_QD, CD_KD, CD_VD, CD_BA, CD_COLS_PAD = 0, 4608, 6144, 6656, 7168, 7680, 8192

VMEM_LIMIT = 48 * 1024 * 1024
HALO = 16


def _cparams(sem):
    return pltpu.CompilerParams(dimension_semantics=sem, vmem_limit_bytes=VMEM_LIMIT)


def _bdot(a, b):
    return jnp.dot(a.astype(BF16), b.astype(BF16), preferred_element_type=F32)


def _bdot_nt(a, b):
    return lax.dot_general(a.astype(BF16), b.astype(BF16), (((1,), (1,)), ((), ())),
                           preferred_element_type=F32)


def _fdot(a, b):
    return jnp.dot(a, b, preferred_element_type=F32, precision=HIGHEST)


def _softplus(x):
    return jnp.maximum(x, 0.0) + jnp.log(1.0 + jnp.exp(-jnp.abs(x)))


def _sigmoid(x):
    return 1.0 / (1.0 + jnp.exp(-x))


def _silu(x):
    return x * _sigmoid(x)


def _shifted_conv(xin, w, width):
    acc = xin[HALO:, :] * w[width - 1:width, :]
    for s in range(1, width):
        acc = acc + pltpu.roll(xin, s, axis=0)[HALO:, :] * w[width - 1 - s:width - s, :]
    return acc


def _mods_kernel(c_ref, w_ref, b_ref, o_ref):
    o_ref[0] = _bdot(_silu(c_ref[...]), w_ref[0]) + b_ref[0]


def _mods(c, ada_w, ada_b):
    depth, d, n = ada_w.shape
    bsz = c.shape[0]
    tn = 512
    return pl.pallas_call(
        _mods_kernel,
        grid=(depth, n // tn),
        in_specs=[pl.BlockSpec((bsz, d), lambda l, j: (0, 0)),
                  pl.BlockSpec((1, d, tn), lambda l, j: (l, 0, j)),
                  pl.BlockSpec((1, 1, tn), lambda l, j: (l, 0, j))],
        out_specs=pl.BlockSpec((1, bsz, tn), lambda l, j: (l, 0, j)),
        out_shape=jax.ShapeDtypeStruct((depth, bsz, n), F32),
        compiler_params=_cparams(("arbitrary", "arbitrary")),
        name="adaln_mods",
    )(c, ada_w, ada_b.reshape(depth, 1, n))


def _modulated_norm(x, nw, sc, sh):
    ms = jnp.mean(x * x, axis=-1, keepdims=True)
    return x * lax.rsqrt(ms + RMS_EPS) * nw * (1.0 + sc) + sh


def _nmm_kernel(x_ref, nw_ref, sc_ref, sh_ref, w_ref, o_ref, h_ref):
    @pl.when(pl.program_id(1) == 0)
    def _():
        h_ref[...] = _modulated_norm(x_ref[0], nw_ref[...], sc_ref[0], sh_ref[0]).astype(BF16)

    o_ref[0] = jnp.dot(h_ref[...], w_ref[...], preferred_element_type=F32).astype(o_ref.dtype)


def _norm_mod_matmul(x, nw, sc, sh, w, tn=1024, tm=1024):
    bsz, s, d = x.shape
    n = w.shape[1]
    spt = s // tm
    return pl.pallas_call(
        _nmm_kernel,
        grid=(bsz * spt, n // tn),
        in_specs=[pl.BlockSpec((1, tm, d), lambda i, j: (i // spt, i % spt, 0)),
                  pl.BlockSpec((1, d), lambda i, j: (0, 0)),
                  pl.BlockSpec((1, 1, d), lambda i, j: (i // spt, 0, 0)),
                  pl.BlockSpec((1, 1, d), lambda i, j: (i // spt, 0, 0)),
                  pl.BlockSpec((d, tn), lambda i, j: (0, j))],
        out_specs=pl.BlockSpec((1, tm, tn), lambda i, j: (i // spt, i % spt, j)),
        out_shape=jax.ShapeDtypeStruct((bsz, s, n), BF16),
        scratch_shapes=[pltpu.VMEM((tm, d), BF16)],
        compiler_params=_cparams(("arbitrary", "arbitrary")),
        name="norm_mod_matmul",
    )(x, nw.reshape(1, d), sc, sh, w)


def _mmres_kernel(*refs, splits):
    na = len(splits)
    a_refs = refs[:na]
    w_ref, x_ref, g_ref, nw_ref, o_ref = refs[na:]
    acc = None
    lo = 0
    for a_ref, k in zip(a_refs, splits):
        part = jnp.dot(a_ref[0].astype(BF16), w_ref[lo:lo + k, :], preferred_element_type=F32)
        acc = part if acc is None else acc + part
        lo += k
    ms = jnp.mean(acc * acc, axis=-1, keepdims=True)
    y = acc * lax.rsqrt(ms + RMS_EPS) * nw_ref[...]
    o_ref[0] = x_ref[0] + g_ref[0] * y


def _matmul_resid(a_list, w, x, gate, nw, tm=512):
    bsz, s, d = x.shape
    spt = s // tm
    splits = tuple(a.shape[-1] for a in a_list)
    ktot = sum(splits)
    in_specs = [pl.BlockSpec((1, tm, k), lambda i: (i // spt, i % spt, 0)) for k in splits]
    in_specs += [pl.BlockSpec((ktot, d), lambda i: (0, 0)),
                 pl.BlockSpec((1, tm, d), lambda i: (i // spt, i % spt, 0)),
                 pl.BlockSpec((1, 1, d), lambda i: (i // spt, 0, 0)),
                 pl.BlockSpec((1, d), lambda i: (0, 0))]
    return pl.pallas_call(
        functools.partial(_mmres_kernel, splits=splits),
        grid=(bsz * spt,),
        in_specs=in_specs,
        out_specs=pl.BlockSpec((1, tm, d), lambda i: (i // spt, i % spt, 0)),
        out_shape=jax.ShapeDtypeStruct((bsz, s, d), F32),
        compiler_params=_cparams(("arbitrary",)),
        name="matmul_resid",
    )(*a_list, w, x, gate, nw.reshape(1, d))


def _ffn_up_kernel(xh_ref, x_ref, nw_ref, sc_ref, sh_ref, w_ref, cw_ref, cb_ref, o_ref, *, f, tc, width):
    h_main = _modulated_norm(x_ref[0], nw_ref[...], sc_ref[0], sh_ref[0])
    h_halo = _modulated_norm(xh_ref[0], nw_ref[...], sc_ref[0], sh_ref[0])
    h_halo = jnp.where(pl.program_id(1) > 0, h_halo, 0.0)
    h = jnp.concatenate([h_halo, h_main], axis=0).astype(BF16)
    c0 = math.sqrt(2.0 / math.pi)
    for c in range(f // tc):
        conv = []
        for off in (0, f):
            lo = off + c * tc
            u = jnp.dot(h, w_ref[:, lo:lo + tc], preferred_element_type=F32)
            conv.append(_shifted_conv(u, cw_ref[:, lo:lo + tc], width) + cb_ref[:, lo:lo + tc])
        g, v = conv
        gelu = 0.5 * g * (1.0 + jnp.tanh(c0 * (g + 0.044715 * (g * g * g))))
        o_ref[0, :, c * tc:(c + 1) * tc] = (gelu * v).astype(o_ref.dtype)


def _ffn_up_geglu(x, nw, sc, sh, w, cw, cb, tm=512, tc=256):
    bsz, s, d = x.shape
    n2 = w.shape[1]
    f = n2 // 2
    width = cw.shape[0]
    hb = tm // HALO
    return pl.pallas_call(
        functools.partial(_ffn_up_kernel, f=f, tc=tc, width=width),
        grid=(bsz, s // tm),
        in_specs=[pl.BlockSpec((1, HALO, d), lambda b_, r: (b_, jnp.maximum(r * hb - 1, 0), 0)),
                  pl.BlockSpec((1, tm, d), lambda b_, r: (b_, r, 0)),
                  pl.BlockSpec((1, d), lambda b_, r: (0, 0)),
                  pl.BlockSpec((1, 1, d), lambda b_, r: (b_, 0, 0)),
                  pl.BlockSpec((1, 1, d), lambda b_, r: (b_, 0, 0)),
                  pl.BlockSpec((d, n2), lambda b_, r: (0, 0)),
                  pl.BlockSpec((width, n2), lambda b_, r: (0, 0)),
                  pl.BlockSpec((1, n2), lambda b_, r: (0, 0))],
        out_specs=pl.BlockSpec((1, tm, f), lambda b_, r: (b_, r, 0)),
        out_shape=jax.ShapeDtypeStruct((bsz, s, f), BF16),
        compiler_params=_cparams(("arbitrary", "arbitrary")),
        name="ffn_up_conv_geglu",
    )(x, x, nw.reshape(1, d), sc, sh, w, cw, cb.reshape(1, n2))


def _rel_bucket_np(d):
    max_exact = REL_BUCKETS // 2
    d = np.maximum(d, 0)
    df = np.maximum(d, 1).astype(np.float64)
    large = max_exact + (np.log(df / max_exact) / math.log(REL_MAX_DIST / max_exact)
                         * (REL_BUCKETS - max_exact)).astype(np.int32)
    large = np.minimum(large, REL_BUCKETS - 1)
    return np.where(d < max_exact, d, large).astype(np.int32)


def _bias_kernel(tab_ref, idx_ref, o_ref):
    h = pl.program_id(0)
    idx = idx_ref[0]
    acc = jnp.zeros(idx.shape, F32)
    for bkt in range(REL_BUCKETS):
        acc = jnp.where(idx == bkt, tab_ref[bkt, h], acc)
    o_ref[0, 0] = acc


def _bias_tiles(rel_bias, idx):
    t, r, c = idx.shape
    return pl.pallas_call(
        _bias_kernel,
        grid=(N_ATTN_HEADS, t),
        in_specs=[pl.BlockSpec(memory_space=pltpu.SMEM),
                  pl.BlockSpec((1, r, c), lambda h, i: (i, 0, 0))],
        out_specs=pl.BlockSpec((1, 1, r, c), lambda h, i: (h, i, 0, 0)),
        out_shape=jax.ShapeDtypeStruct((N_ATTN_HEADS, t, r, c), F32),
        compiler_params=_cparams(("arbitrary", "arbitrary")),
        name="rel_bias_tiles",
    )(rel_bias, idx)


def _swa_bucket_idx():
    c = np.arange(2 * SWA_BLOCK)[:, None]
    r = np.arange(SWA_BLOCK)[None, :]
    return _rel_bucket_np(SWA_BLOCK + r - c)[None]


def _moba_bucket_idx(nblk):
    c = np.arange(MOBA_BLOCK)[:, None]
    r = np.arange(MOBA_BLOCK)[None, :]
    return np.stack([_rel_bucket_np(m * MOBA_BLOCK + r - c) for m in range(nblk)])


def _expand_heads(v, e):
    hi = v.astype(BF16)
    lo = (v - hi.astype(F32)).astype(BF16)
    return (jnp.dot(hi, e, preferred_element_type=F32) + jnp.dot(lo, e, preferred_element_type=F32))


def _ssd_kernel(zx_ref, dtc_ref, dtr_ref, cw_ref, cb_ref, bias_c_ref, bias_r_ref, alog_c_ref, alog_r_ref,
                dskip_ref, nw_ref, e_ref, o_ref, state_ref, tail_ref):
    q = SSD_CHUNK
    gw = SSD_INNER // SSD_GROUPS
    hpg = SSD_HEADS // SSD_GROUPS

    @pl.when(pl.program_id(1) == 0)
    def _():
        state_ref[...] = jnp.zeros(state_ref.shape, F32)
        tail_ref[...] = jnp.zeros(tail_ref.shape, F32)

    row = lax.broadcasted_iota(jnp.int32, (q, q), 0)
    col = lax.broadcasted_iota(jnp.int32, (q, q), 1)
    tril = row >= col
    tri_f = jnp.where(tril, 1.0, 0.0).astype(F32)
    triu_f = jnp.where(row <= col, 1.0, 0.0).astype(F32)

    dt_c = _softplus(dtc_ref[0].astype(F32) + bias_c_ref[...])
    da_c = dt_c * (-jnp.exp(alog_c_ref[...]))
    acs_c = _fdot(tri_f, da_c)
    dt_r = _softplus(dtr_ref[0] + bias_r_ref[...])
    da_r = dt_r * (-jnp.exp(alog_r_ref[...]))
    acs_r = _fdot(da_r, triu_f)

    acs_last = acs_c[q - 1:q, :]
    e = e_ref[...]
    dt_full = _expand_heads(dt_c, e)
    dtdec_full = _expand_heads(dt_c * jnp.exp(acs_last - acs_c), e)
    eacs_full = _expand_heads(jnp.exp(acs_c), e)
    cdecay_full = eacs_full[q - 1:q, :]

    raw = zx_ref[0, :, SSD_INNER:].astype(F32)
    xin = jnp.concatenate([tail_ref[...], raw], axis=0)
    tail_ref[...] = raw[q - HALO:, :]
    xbc = _silu(_shifted_conv(xin, cw_ref[...], SSD_CONV) + cb_ref[...])

    xs = xbc[:, :SSD_INNER]
    xdt = xs * dt_full
    xdec = xs * dtdec_full
    lane_half = lax.broadcasted_iota(jnp.int32, (1, LANES), 1) >> 6

    y_parts = []
    for g in range(SSD_GROUPS):
        b_g = xbc[:, SSD_INNER + g * SSD_STATE:SSD_INNER + (g + 1) * SSD_STATE]
        c_g = xbc[:, SSD_INNER + SSD_GROUPS * SSD_STATE + g * SSD_STATE:
                  SSD_INNER + SSD_GROUPS * SSD_STATE + (g + 1) * SSD_STATE]
        cb = _bdot_nt(c_g, b_g)
        st = state_ref[g]
        y_off = _bdot(c_g, st) * eacs_full[:, g * gw:(g + 1) * gw]
        state_ref[g] = st * cdecay_full[:, g * gw:(g + 1) * gw] + _bdot(b_g.T, xdec[:, g * gw:(g + 1) * gw])
        pair_parts = []
        for pr in range(hpg // 2):
            acc = None
            lo = g * gw + pr * LANES
            x_pair = xdt[:, lo:lo + LANES]
            for half in range(2):
                h = g * hpg + pr * 2 + half
                diff = acs_c[:, h:h + 1] - acs_r[h:h + 1, :]
                lmat = jnp.where(tril, jnp.exp(jnp.where(tril, diff, 0.0)), 0.0)
                part = _bdot(cb * lmat, jnp.where(lane_half == half, x_pair, 0.0))
                acc = part if acc is None else acc + part
            pair_parts.append(acc)
        y_diag = jnp.concatenate(pair_parts, axis=1)
        y = y_diag + y_off + dskip_ref[:, g * gw:(g + 1) * gw] * xs[:, g * gw:(g + 1) * gw]
        y = y * _silu(zx_ref[0, :, g * gw:(g + 1) * gw].astype(F32))
        ms = jnp.mean(y * y, axis=-1, keepdims=True)
        y_parts.append(y * lax.rsqrt(ms + RMS_EPS) * nw_ref[:, g * gw:(g + 1) * gw])
    o_ref[0] = jnp.concatenate(y_parts, axis=1).astype(o_ref.dtype)


def _ssd_mixer(proj, dt_rows, conv_w, conv_b, dt_bias, a_log, d_skip, norm_w):
    bsz, s, _ = proj.shape
    q = SSD_CHUNK
    zx = SSD_INNER + SSD_XBC
    pad = LANES - SSD_HEADS
    bias_c = jnp.pad(dt_bias, (0, pad)).reshape(1, LANES)
    alog_c = jnp.pad(a_log, (0, pad)).reshape(1, LANES)
    e_np = np.zeros((LANES, SSD_INNER), np.float32)
    for h in range(SSD_HEADS):
        e_np[h, h * SSD_HEAD_DIM:(h + 1) * SSD_HEAD_DIM] = 1.0
    small = lambda shape: pl.BlockSpec(shape, lambda b_, c: (0, 0))
    return pl.pallas_call(
        _ssd_kernel,
        grid=(bsz, s // q),
        in_specs=[pl.BlockSpec((1, q, zx), lambda b_, c: (b_, c, 0)),
                  pl.BlockSpec((1, q, LANES), lambda b_, c: (b_, c, AB_DT // LANES)),
                  pl.BlockSpec((1, LANES, q), lambda b_, c: (b_, 0, c)),
                  small((SSD_CONV, SSD_XBC)), small((1, SSD_XBC)),
                  small((1, LANES)), small((LANES, 1)), small((1, LANES)), small((LANES, 1)),
                  small((1, SSD_INNER)), small((1, SSD_INNER)), small((LANES, SSD_INNER))],
        out_specs=pl.BlockSpec((1, q, SSD_INNER), lambda b_, c: (b_, c, 0)),
        out_shape=jax.ShapeDtypeStruct((bsz, s, SSD_INNER), BF16),
        scratch_shapes=[pltpu.VMEM((SSD_GROUPS, SSD_STATE, SSD_INNER // SSD_GROUPS), F32),
                        pltpu.VMEM((HALO, SSD_XBC), F32)],
        compiler_params=_cparams(("arbitrary", "arbitrary")),
        name="ssd_mixer",
    )(proj, proj, dt_rows, conv_w, conv_b.reshape(1, SSD_XBC), bias_c, bias_c.reshape(LANES, 1),
      alog_c, alog_c.reshape(LANES, 1), jnp.repeat(d_skip, SSD_HEAD_DIM).reshape(1, SSD_INNER),
      norm_w.reshape(1, SSD_INNER), jnp.asarray(e_np, BF16))


def _swa_kernel(sink_ref, q_ref, kp_ref, kc_ref, vp_ref, vc_ref, bias_ref, o_ref):
    blk = SWA_BLOCK
    n = pl.program_id(1)
    grp = N_ATTN_HEADS // SWA_KV_HEADS
    scale = ATTN_HEAD_DIM ** -0.5
    kk = jnp.concatenate([kp_ref[0], kc_ref[0]], axis=0).astype(F32)
    vv_t = jnp.concatenate([vp_ref[0], vc_ref[0]], axis=0).astype(F32).T
    lane_half = lax.broadcasted_iota(jnp.int32, (1, LANES), 1) >> 6
    c = lax.broadcasted_iota(jnp.int32, (2 * blk, blk), 0)
    r = lax.broadcasted_iota(jnp.int32, (2 * blk, blk), 1)
    dist = blk + r - c
    valid = (dist >= 0) & (dist < blk) & ((c >= blk) | (n > 0))
    outs = []
    for kv in range(SWA_KV_HEADS):
        k_own = jnp.where(lane_half == kv, kk, 0.0)
        k_var = [None, None]
        k_var[kv] = k_own
        k_var[1 - kv] = pltpu.roll(k_own, ATTN_HEAD_DIM, axis=1)
        v_t = vv_t[kv * ATTN_HEAD_DIM:(kv + 1) * ATTN_HEAD_DIM, :]
        for gq in range(grp):
            h = kv * grp + gq
            q_tile = q_ref[0, :, (h // 2) * LANES:(h // 2 + 1) * LANES]
            s_t = _bdot_nt(k_var[h % 2], q_tile) * scale + bias_ref[h, 0]
            s_t = jnp.where(valid, s_t, NEG)
            sink = sink_ref[h]
            m = jnp.maximum(jnp.max(s_t, axis=0, keepdims=True), sink)
            p = jnp.exp(s_t - m)
            l = jnp.sum(p, axis=0, keepdims=True) + jnp.exp(sink - m)
            outs.append(_bdot(v_t, p) / l)
    for t in range(N_ATTN_HEADS // 2):
        pair = jnp.concatenate([outs[2 * t], outs[2 * t + 1]], axis=0)
        o_ref[0, :, t * LANES:(t + 1) * LANES] = pair.T.astype(o_ref.dtype)


def _swa_attention(proj, sinks, bias):
    bsz, s, _ = proj.shape
    blk = SWA_BLOCK
    kvw = SWA_KV_HEADS * ATTN_HEAD_DIM
    prev = lambda col: pl.BlockSpec((1, blk, kvw), lambda b_, n: (b_, jnp.maximum(n - 1, 0), col))
    cur = lambda col: pl.BlockSpec((1, blk, kvw), lambda b_, n: (b_, n, col))
    return pl.pallas_call(
        _swa_kernel,
        grid=(bsz, s // blk),
        in_specs=[pl.BlockSpec(memory_space=pltpu.SMEM),
                  pl.BlockSpec((1, blk, ATTN_WIDTH), lambda b_, n: (b_, n, AB_Q // ATTN_WIDTH)),
                  prev(AB_K // kvw), cur(AB_K // kvw), prev(AB_V // kvw), cur(AB_V // kvw),
                  pl.BlockSpec((N_ATTN_HEADS, 1, 2 * blk, blk), lambda b_, n: (0, 0, 0, 0))],
        out_specs=pl.BlockSpec((1, blk, ATTN_WIDTH), lambda b_, n: (b_, n, 0)),
        out_shape=jax.ShapeDtypeStruct((bsz, s, ATTN_WIDTH), BF16),
        compiler_params=_cparams(("arbitrary", "arbitrary")),
        name="swa_attention",
    )(sinks, proj, proj, proj, proj, proj, bias)


def _moba_kernel(q_ref, k_ref, v_ref, bias_ref, o_ref, vt_ref, kmean_ref, sel_ref, *, nblk):
    mb = MOBA_BLOCK
    dh = ATTN_HEAD_DIM
    own = pl.program_id(2)
    scale = dh ** -0.5

    @pl.when(own == 0)
    def _():
        means = []
        for j in range(nblk):
            vt_ref[j] = v_ref[0, j * mb:(j + 1) * mb, :].astype(F32).T.astype(BF16)
            means.append(jnp.mean(k_ref[0, j * mb:(j + 1) * mb, :].astype(F32), axis=0, keepdims=True))
        kmean_ref[...] = jnp.concatenate(means, axis=0)

    lane_half = lax.broadcasted_iota(jnp.int32, (1, LANES), 1) >> 6
    blk_id = lax.broadcasted_iota(jnp.int32, (nblk, mb), 0)
    c = lax.broadcasted_iota(jnp.int32, (mb, mb), 0)
    r = lax.broadcasted_iota(jnp.int32, (mb, mb), 1)
    causal = c <= r
    q = q_ref[0].astype(F32)
    k_own = k_ref[0, pl.ds(pl.multiple_of(own * mb, mb), mb), :]
    vt_own = vt_ref[own]

    carry = []
    qms = []
    for hh in range(2):
        qm = jnp.where(lane_half == hh, q, 0.0)
        qms.append(qm.astype(BF16))
        gate = lax.dot_general(kmean_ref[...], qm, (((1,), (1,)), ((), ())),
                               preferred_element_type=F32, precision=HIGHEST)
        gate = jnp.where(blk_id < own, gate, NEG)
        rank = jnp.zeros((nblk, mb), jnp.int32)
        for i in range(nblk):
            gi = gate[i:i + 1, :]
            ahead = (gi > gate) | ((gi == gate) & (i < blk_id))
            rank = rank + jnp.where(ahead, 1, 0)
        sel = (rank < MOBA_TOPK) & (blk_id < own)
        sel_ref[hh] = jnp.where(sel, 1.0, 0.0).astype(F32)
        s_t = _bdot_nt(k_own, qms[hh]) * scale + bias_ref[hh, 0]
        s_t = jnp.where(causal, s_t, NEG)
        m = jnp.max(s_t, axis=0, keepdims=True)
        p = jnp.exp(s_t - m)
        l = jnp.sum(p, axis=0, keepdims=True)
        acc = _bdot(vt_own[hh * dh:(hh + 1) * dh, :], p)
        carry += [m, l, acc]

    def body(j, carry):
        k_j = k_ref[0, pl.ds(pl.multiple_of(j * mb, mb), mb), :]
        vt_j = vt_ref[j]
        new = []
        for hh in range(2):
            m, l, acc = carry[3 * hh:3 * hh + 3]
            s_t = _bdot_nt(k_j, qms[hh]) * scale + bias_ref[hh, own - j]
            s_t = jnp.where(sel_ref[hh, pl.ds(j, 1), :] > 0.5, s_t, NEG)
            m_new = jnp.maximum(m, jnp.max(s_t, axis=0, keepdims=True))
            alpha = jnp.exp(m - m_new)
            p = jnp.exp(s_t - m_new)
            l = l * alpha + jnp.sum(p, axis=0, keepdims=True)
            acc = acc * alpha + _bdot(vt_j[hh * dh:(hh + 1) * dh, :], p)
            new += [m_new, l, acc]
        return tuple(new)

    carry = lax.fori_loop(0, own, body, tuple(carry))
    out_t = jnp.concatenate([carry[2] / carry[1], carry[5] / carry[4]], axis=0)
    o_ref[0] = out_t.T.astype(o_ref.dtype)


def _moba_attention(proj, bias):
    bsz, s, _ = proj.shape
    mb = MOBA_BLOCK
    nblk = s // mb
    npair = N_ATTN_HEADS // 2
    return pl.pallas_call(
        functools.partial(_moba_kernel, nblk=nblk),
        grid=(bsz, npair, nblk),
        in_specs=[pl.BlockSpec((1, mb, LANES), lambda b_, p, i: (b_, i, CD_QD // LANES + p)),
                  pl.BlockSpec((1, s, LANES), lambda b_, p, i: (b_, 0, CD_KD // LANES + p)),
                  pl.BlockSpec((1, s, LANES), lambda b_, p, i: (b_, 0, CD_VD // LANES + p)),
                  pl.BlockSpec((2, nblk, mb, mb), lambda b_, p, i: (p, 0, 0, 0))],
        out_specs=pl.BlockSpec((1, mb, LANES), lambda b_, p, i: (b_, i, p)),
        out_shape=jax.ShapeDtypeStruct((bsz, s, ATTN_WIDTH), BF16),
        scratch_shapes=[pltpu.VMEM((nblk, LANES, mb), BF16),
                        pltpu.VMEM((nblk, LANES), F32),
                        pltpu.VMEM((2, nblk, mb), F32)],
        compiler_params=_cparams(("arbitrary", "arbitrary", "arbitrary")),
        name="moba_attention",
    )(proj, proj, proj, bias)


def _gdn_kernel(qkv_ref, z_ref, bac_ref, bar_ref, cw_ref, bias_c_ref, bias_r_ref, alog_c_ref, alog_r_ref,
                nw_ref, o_ref, state_ref, tail_ref, m_ref, attn_ref, t_ref, x_ref, rhs_ref, u_ref, wq_ref,
                kdt_ref, r_ref, glc_ref):
    t = 2 * GDN_CHUNK
    ck = GDN_CHUNK
    dk = GDN_HEAD_DIM
    nh = GDN_HEADS

    @pl.when(pl.program_id(1) == 0)
    def _():
        state_ref[...] = jnp.zeros(state_ref.shape, F32)
        tail_ref[...] = jnp.zeros(tail_ref.shape, F32)

    row = lax.broadcasted_iota(jnp.int32, (t, t), 0)
    col = lax.broadcasted_iota(jnp.int32, (t, t), 1)
    same = (row >> 6) == (col >> 6)
    tril = same & (row >= col)
    strict = same & (row > col)
    tri_f = jnp.where(tril, 1.0, 0.0).astype(F32)
    triu_f = jnp.where(same & (row <= col), 1.0, 0.0).astype(F32)
    blk_f = jnp.where(same, 1.0, 0.0).astype(F32)
    eye_f = jnp.where(row == col, 1.0, 0.0).astype(F32)
    merge_masks = [((row >> (l + 1)) == (col >> (l + 1))) & (((row >> l) & 1) == 1) & (((col >> l) & 1) == 0)
                   for l in range(int(math.log2(ck)))]
    mask_bf = [jnp.where(m, 1.0, 0.0).astype(BF16) for m in merge_masks[1:]]

    ba_c = bac_ref[0].astype(F32)
    g_c = -jnp.exp(alog_c_ref[...]) * _softplus(ba_c + bias_c_ref[...])
    gc_c = _fdot(tri_f, g_c)
    gl_c = _fdot(blk_f, g_c)
    g_r = -jnp.exp(alog_r_ref[...]) * _softplus(bar_ref[0] + bias_r_ref[...])
    gc_r = _fdot(g_r, triu_f)
    glc_ref[...] = gl_c

    def conv_silu(lo):
        raw = qkv_ref[0, :, lo:lo + dk].astype(F32)
        xin = jnp.concatenate([tail_ref[:, lo:lo + dk], raw], axis=0)
        tail_ref[:, lo:lo + dk] = raw[t - HALO:, :]
        return _silu(_shifted_conv(xin, cw_ref[:, lo:lo + dk], GDN_CONV))

    for h in range(nh):
        q = conv_silu(h * dk)
        k = conv_silu((nh + h) * dk)
        v = conv_silu((2 * nh + h) * dk)
        qn = q * lax.rsqrt(jnp.sum(q * q, axis=-1, keepdims=True) + 1e-6) * (dk ** -0.5)
        kn = k * lax.rsqrt(jnp.sum(k * k, axis=-1, keepdims=True) + 1e-6)
        beta = _sigmoid(ba_c[:, h:h + 1])
        gcc = gc_c[:, nh + h:nh + h + 1]
        gcr = gc_r[nh + h:nh + h + 1, :]
        glc = gl_c[:, nh + h:nh + h + 1]
        decay = jnp.where(tril, jnp.exp(jnp.where(tril, gcc - gcr, 0.0)), 0.0)
        kb = kn * beta
        kk = _bdot_nt(jnp.concatenate([kb, qn], axis=0), kn)
        mm = jnp.where(strict, kk[:t] * decay, 0.0)
        m_ref[h] = mm.astype(BF16)
        attn_ref[h] = (kk[t:] * decay).astype(BF16)
        t_ref[h] = eye_f - jnp.where(merge_masks[0], mm, 0.0)
        egc = jnp.exp(gcc)
        rhs_ref[h] = jnp.concatenate([v * beta, kb * egc], axis=1).astype(BF16)
        q_dec = (qn * egc).astype(BF16)
        for a in range(2):
            wq_ref[h, a, ck:, :] = q_dec[a * ck:(a + 1) * ck]
        kdt_ref[h] = (kn * jnp.exp(glc - gcc)).T.astype(BF16)

    for lvl in range(len(mask_bf)):
        for h in range(nh):
            x_ref[h] = jnp.dot(t_ref[h].astype(BF16), m_ref[h] * mask_bf[lvl],
                               preferred_element_type=F32).astype(BF16)
        for h in range(nh):
            t_h = t_ref[h]
            t_ref[h] = t_h - jnp.dot(x_ref[h], t_h.astype(BF16), preferred_element_type=F32)

    for h in range(nh):
        sol = jnp.dot(t_ref[h].astype(BF16), rhs_ref[h], preferred_element_type=F32)
        u_ref[h] = sol[:, :dk]
        for a in range(2):
            wq_ref[h, a, :ck, :] = sol[a * ck:(a + 1) * ck, dk:].astype(BF16)

    zeros_half = jnp.zeros((ck, dk), F32)
    for a in range(2):
        sl = slice(a * ck, (a + 1) * ck)
        for h in range(nh):
            r_ref[h] = jnp.dot(wq_ref[h, a], state_ref[h].astype(BF16), preferred_element_type=F32)
        for h in range(nh):
            v_new = u_ref[h, sl, :] - r_ref[h, :ck, :]
            v_full = jnp.concatenate([v_new, zeros_half] if a == 0 else [zeros_half, v_new], axis=0).astype(BF16)
            o = r_ref[h, ck:, :] + jnp.dot(attn_ref[h, sl, :], v_full, preferred_element_type=F32)
            gl = glc_ref[a * ck:a * ck + 1, nh + h:nh + h + 1]
            state_ref[h] = state_ref[h] * jnp.exp(gl) + jnp.dot(kdt_ref[h], v_full, preferred_element_type=F32)
            ms = jnp.mean(o * o, axis=-1, keepdims=True)
            y = o * lax.rsqrt(ms + RMS_EPS) * nw_ref[...] * _silu(z_ref[0, sl, h * dk:(h + 1) * dk].astype(F32))
            o_ref[0, sl, h * dk:(h + 1) * dk] = y.astype(o_ref.dtype)


def _gdn_mixer(proj, ba_rows, conv_w, dt_bias, a_log, norm_w):
    bsz, s, _ = proj.shape
    t = 2 * GDN_CHUNK
    nh = GDN_HEADS
    dk = GDN_HEAD_DIM
    bias_c = jnp.pad(dt_bias, (nh, LANES - 2 * nh)).reshape(1, LANES)
    alog_c = jnp.pad(a_log, (nh, LANES - 2 * nh)).reshape(1, LANES)
    small = lambda shape: pl.BlockSpec(shape, lambda b_, c: (0, 0))
    return pl.pallas_call(
        _gdn_kernel,
        grid=(bsz, s // t),
        in_specs=[pl.BlockSpec((1, t, 3 * GDN_INNER), lambda b_, c: (b_, c, CD_QKV // (3 * GDN_INNER))),
                  pl.BlockSpec((1, t, GDN_INNER), lambda b_, c: (b_, c, CD_Z // GDN_INNER)),
                  pl.BlockSpec((1, t, LANES), lambda b_, c: (b_, c, CD_BA // LANES)),
                  pl.BlockSpec((1, LANES, t), lambda b_, c: (b_, 0, c)),
                  small((GDN_CONV, 3 * GDN_INNER)),
                  small((1, LANES)), small((LANES, 1)), small((1, LANES)), small((LANES, 1)),
                  small((1, dk))],
        out_specs=pl.BlockSpec((1, t, GDN_INNER), lambda b_, c: (b_, c, 0)),
        out_shape=jax.ShapeDtypeStruct((bsz, s, GDN_INNER), BF16),
        scratch_shapes=[pltpu.VMEM((nh, dk, dk), F32),
                        pltpu.VMEM((HALO, 3 * GDN_INNER), F32),
                        pltpu.VMEM((nh, t, t), BF16),
                        pltpu.VMEM((nh, t, t), BF16),
                        pltpu.VMEM((nh, t, t), F32),
                        pltpu.VMEM((nh, t, t), BF16),
                        pltpu.VMEM((nh, t, 2 * dk), BF16),
                        pltpu.VMEM((nh, t, dk), F32),
                        pltpu.VMEM((nh, 2, t, dk), BF16),
                        pltpu.VMEM((nh, dk, t), BF16),
                        pltpu.VMEM((nh, t, dk), F32),
                        pltpu.VMEM((t, LANES), F32)],
        compiler_params=_cparams(("arbitrary", "arbitrary")),
        name="gdn_mixer",
    )(proj, proj, proj, ba_rows, conv_w, bias_c, bias_c.reshape(LANES, 1), alog_c, alog_c.reshape(LANES, 1),
      norm_w.reshape(1, dk))


def _pad_cols(w, n):
    return jnp.pad(w, ((0, 0), (0, n - w.shape[1])))


def _gate_rows(proj, col0):
    return jnp.swapaxes(proj[:, :, col0:col0 + LANES].astype(F32), 1, 2)


def kernel(x, c, rel_bias, norm_w, ada_w, ada_b, ab_w_in, ab_w_out, ssd_conv_w, ssd_conv_b,
           ssd_dt_bias, ssd_a_log, ssd_d, ssd_norm_w, swa_sinks, cd_w_in, cd_w_out, gdn_conv_w,
           gdn_dt_bias, gdn_a_log, gdn_norm_w, ffn_w_up, ffn_conv_w, ffn_conv_b, ffn_w_down):
    bsz, s, d = x.shape
    depth = norm_w.shape[0]
    mods = _mods(c, ada_w, ada_b)
    swa_bias = _bias_tiles(rel_bias, jnp.asarray(_swa_bucket_idx()))
    moba_bias = _bias_tiles(rel_bias, jnp.asarray(_moba_bucket_idx(s // MOBA_BLOCK)))

    for i in range(depth):
        sh_m, sc_m, g_m, sh_f, sc_f, g_f = [m.reshape(bsz, 1, d) for m in jnp.split(mods[i], 6, axis=-1)]
        j = i // 2
        if i % 2 == 0:
            w = ab_w_in[j]
            dt0 = SSD_INNER + SSD_XBC
            w_in = jnp.concatenate([w[:, :dt0], w[:, dt0 + SSD_HEADS:], w[:, dt0:dt0 + SSD_HEADS]], axis=1)
            w_in = _pad_cols(w_in, AB_COLS_PAD).astype(BF16)
            proj = _norm_mod_matmul(x, norm_w[i, 0], sc_m, sh_m, w_in)
            y_a = _ssd_mixer(proj, _gate_rows(proj, AB_DT), ssd_conv_w[j], ssd_conv_b[j], ssd_dt_bias[j],
                             ssd_a_log[j], ssd_d[j], ssd_norm_w[j])
            y_b = _swa_attention(proj, swa_sinks[j], swa_bias)
            x = _matmul_resid([y_a, y_b], ab_w_out[j].astype(BF16), x, g_m, norm_w[i, 1])
        else:
            w = cd_w_in[j]
            ba0 = 4 * GDN_INNER
            w_in = jnp.concatenate([w[:, :ba0], w[:, ba0 + 2 * GDN_HEADS:], w[:, ba0:ba0 + 2 * GDN_HEADS]], axis=1)
            w_in = _pad_cols(w_in, CD_COLS_PAD).astype(BF16)
            proj = _norm_mod_matmul(x, norm_w[i, 0], sc_m, sh_m, w_in)
            y_c = _gdn_mixer(proj, _gate_rows(proj, CD_BA), gdn_conv_w[j], gdn_dt_bias[j], gdn_a_log[j],
                             gdn_norm_w[j])
            y_d = _moba_attention(proj, moba_bias)
            x = _matmul_resid([y_c, y_d], cd_w_out[j].astype(BF16), x, g_m, norm_w[i, 1])
        act = _ffn_up_geglu(x, norm_w[i, 2], sc_f, sh_f, ffn_w_up[i].astype(BF16), ffn_conv_w[i], ffn_conv_b[i])
        x = _matmul_resid([act], ffn_w_down[i].astype(BF16), x, g_f, norm_w[i, 3])
    return x
```

```python
import functools
import math

import numpy as np
import jax
import jax.numpy as jnp
from jax import lax
from jax.experimental import pallas as pl
from jax.experimental.pallas import tpu as pltpu

F32 = jnp.float32
BF16 = jnp.bfloat16
HIGHEST = lax.Precision.HIGHEST

D_MODEL = 1024
RMS_EPS = 1e-6
NEG = -1e30
LANES = 128
N_ATTN_HEADS = 8
ATTN_HEAD_DIM = 64
ATTN_WIDTH = N_ATTN_HEADS * ATTN_HEAD_DIM
REL_BUCKETS = 32
REL_MAX_DIST = 1024
SSD_HEADS = 24
SSD_HEAD_DIM = 64
SSD_INNER = SSD_HEADS * SSD_HEAD_DIM
SSD_GROUPS = 4
SSD_STATE = 128
SSD_CONV = 4
SSD_CHUNK = 128
SSD_XBC = SSD_INNER + 2 * SSD_GROUPS * SSD_STATE
SWA_KV_HEADS = 2
SWA_BLOCK = 128
GDN_HEADS = 12
GDN_HEAD_DIM = 128
GDN_INNER = GDN_HEADS * GDN_HEAD_DIM
GDN_CONV = 4
GDN_CHUNK = 64
MOBA_BLOCK = 256
MOBA_TOPK = 3
FFN_DIM = 2816
FFN_CONV = 3

AB_Z, AB_XBC, AB_Q, AB_K, AB_V, AB_DT, AB_COLS_PAD = 0, 1536, 4096, 4608, 4736, 4864, 5120
CD_QKV, CD_Z, CD_QD, CD_KD, CD_VD, CD_BA, CD_COLS_PAD = 0, 4608, 6144, 6656, 7168, 7680, 8192

VMEM_LIMIT = 48 * 1024 * 1024
HALO = 16


def _cparams(sem):
    return pltpu.CompilerParams(dimension_semantics=sem, vmem_limit_bytes=VMEM_LIMIT)


def _bdot(a, b):
    return jnp.dot(a.astype(BF16), b.astype(BF16), preferred_element_type=F32)


def _bdot_nt(a, b):
    return lax.dot_general(a.astype(BF16), b.astype(BF16), (((1,), (1,)), ((), ())),
                           preferred_element_type=F32)


def _split3(x):
    hi = x.astype(BF16)
    r1 = x - hi.astype(F32)
    mid = r1.astype(BF16)
    lo = (r1 - mid.astype(F32)).astype(BF16)
    return hi, mid, lo


def _sum01_left(m01, x):
    n = x.shape[1]
    y = jnp.dot(m01, jnp.concatenate(_split3(x), axis=1), preferred_element_type=F32)
    return y[:, :n] + y[:, n:2 * n] + y[:, 2 * n:]


def _sum01_right(x, m01):
    n = x.shape[0]
    y = jnp.dot(jnp.concatenate(_split3(x), axis=0), m01, preferred_element_type=F32)
    return y[:n] + y[n:2 * n] + y[2 * n:]


def _softplus(x):
    return jnp.maximum(x, 0.0) + jnp.log(1.0 + jnp.exp(-jnp.abs(x)))


def _sigmoid(x):
    return 1.0 / (1.0 + jnp.exp(-x))


def _silu(x):
    return x * _sigmoid(x)


def _window_conv(win_ref, lo, ncols, rows, w, width):
    acc = None
    for s in range(width):
        tap = win_ref[pl.ds(HALO - s, rows), lo:lo + ncols] * w[width - 1 - s:width - s, :]
        acc = tap if acc is None else acc + tap
    return acc


def _shifted_conv(xin, w, width):
    acc = xin[HALO:, :] * w[width - 1:width, :]
    for s in range(1, width):
        acc = acc + pltpu.roll(xin, s, axis=0)[HALO:, :] * w[width - 1 - s:width - s, :]
    return acc


def _mods_kernel(c_ref, w_ref, b_ref, o_ref):
    o_ref[0] = _bdot(_silu(c_ref[...]), w_ref[0]) + b_ref[0]


def _mods(c, ada_w, ada_b):
    depth, d, n = ada_w.shape
    bsz = c.shape[0]
    tn = 512
    return pl.pallas_call(
        _mods_kernel,
        grid=(depth, n // tn),
        in_specs=[pl.BlockSpec((bsz, d), lambda l, j: (0, 0)),
                  pl.BlockSpec((1, d, tn), lambda l, j: (l, 0, j)),
                  pl.BlockSpec((1, 1, tn), lambda l, j: (l, 0, j))],
        out_specs=pl.BlockSpec((1, bsz, tn), lambda l, j: (l, 0, j)),
        out_shape=jax.ShapeDtypeStruct((depth, bsz, n), F32),
        compiler_params=_cparams(("arbitrary", "arbitrary")),
        name="adaln_mods",
    )(c, ada_w, ada_b.reshape(depth, 1, n))


def _modulated_norm(x, nw, sc, sh):
    ms = jnp.mean(x * x, axis=-1, keepdims=True)
    return x * lax.rsqrt(ms + RMS_EPS) * nw * (1.0 + sc) + sh


def _nmm_kernel(x_ref, nw_ref, sc_ref, sh_ref, w_ref, o_ref, h_ref):
    @pl.when(pl.program_id(1) == 0)
    def _():
        h_ref[...] = _modulated_norm(x_ref[0], nw_ref[...], sc_ref[0], sh_ref[0]).astype(BF16)

    o_ref[0] = jnp.dot(h_ref[...], w_ref[...], preferred_element_type=F32).astype(o_ref.dtype)


def _norm_mod_matmul(x, nw, sc, sh, w, tn=1024, tm=1024):
    bsz, s, d = x.shape
    n = w.shape[1]
    spt = s // tm
    return pl.pallas_call(
        _nmm_kernel,
        grid=(bsz * spt, n // tn),
        in_specs=[pl.BlockSpec((1, tm, d), lambda i, j: (i // spt, i % spt, 0)),
                  pl.BlockSpec((1, d), lambda i, j: (0, 0)),
                  pl.BlockSpec((1, 1, d), lambda i, j: (i // spt, 0, 0)),
                  pl.BlockSpec((1, 1, d), lambda i, j: (i // spt, 0, 0)),
                  pl.BlockSpec((d, tn), lambda i, j: (0, j))],
        out_specs=pl.BlockSpec((1, tm, tn), lambda i, j: (i // spt, i % spt, j)),
        out_shape=jax.ShapeDtypeStruct((bsz, s, n), BF16),
        scratch_shapes=[pltpu.VMEM((tm, d), BF16)],
        compiler_params=_cparams(("arbitrary", "arbitrary")),
        name="norm_mod_matmul",
    )(x, nw.reshape(1, d), sc, sh, w)


def _mmres_kernel(*refs, splits):
    na = len(splits)
    a_refs = refs[:na]
    w_ref, x_ref, g_ref, nw_ref, o_ref = refs[na:]
    acc = None
    lo = 0
    for a_ref, k in zip(a_refs, splits):
        part = jnp.dot(a_ref[0].astype(BF16), w_ref[lo:lo + k, :], preferred_element_type=F32)
        acc = part if acc is None else acc + part
        lo += k
    ms = jnp.mean(acc * acc, axis=-1, keepdims=True)
    y = acc * lax.rsqrt(ms + RMS_EPS) * nw_ref[...]
    o_ref[0] = x_ref[0] + g_ref[0] * y


def _matmul_resid(a_list, w, x, gate, nw, tm=512):
    bsz, s, d = x.shape
    spt = s // tm
    splits = tuple(a.shape[-1] for a in a_list)
    ktot = sum(splits)
    in_specs = [pl.BlockSpec((1, tm, k), lambda i: (i // spt, i % spt, 0)) for k in splits]
    in_specs += [pl.BlockSpec((ktot, d), lambda i: (0, 0)),
                 pl.BlockSpec((1, tm, d), lambda i: (i // spt, i % spt, 0)),
                 pl.BlockSpec((1, 1, d), lambda i: (i // spt, 0, 0)),
                 pl.BlockSpec((1, d), lambda i: (0, 0))]
    return pl.pallas_call(
        functools.partial(_mmres_kernel, splits=splits),
        grid=(bsz * spt,),
        in_specs=in_specs,
        out_specs=pl.BlockSpec((1, tm, d), lambda i: (i // spt, i % spt, 0)),
        out_shape=jax.ShapeDtypeStruct((bsz, s, d), F32),
        compiler_params=_cparams(("arbitrary",)),
        name="matmul_resid",
    )(*a_list, w, x, gate, nw.reshape(1, d))


def _ffn_up_kernel(xh_ref, x_ref, nw_ref, sc_ref, sh_ref, w_ref, cw_ref, cb_ref, o_ref, *, f, tc, width):
    h_main = _modulated_norm(x_ref[0], nw_ref[...], sc_ref[0], sh_ref[0])
    h_halo = _modulated_norm(xh_ref[0], nw_ref[...], sc_ref[0], sh_ref[0])
    h_halo = jnp.where(pl.program_id(1) > 0, h_halo, 0.0)
    h = jnp.concatenate([h_halo, h_main], axis=0).astype(BF16)
    c0 = math.sqrt(2.0 / math.pi)
    for c in range(f // tc):
        conv = []
        for off in (0, f):
            lo = off + c * tc
            u = jnp.dot(h, w_ref[:, lo:lo + tc], preferred_element_type=F32)
            conv.append(_shifted_conv(u, cw_ref[:, lo:lo + tc], width) + cb_ref[:, lo:lo + tc])
        g, v = conv
        gelu = 0.5 * g * (1.0 + jnp.tanh(c0 * (g + 0.044715 * (g * g * g))))
        o_ref[0, :, c * tc:(c + 1) * tc] = (gelu * v).astype(o_ref.dtype)


def _ffn_up_geglu(x, nw, sc, sh, w, cw, cb, tm=512, tc=256):
    bsz, s, d = x.shape
    n2 = w.shape[1]
    f = n2 // 2
    width = cw.shape[0]
    hb = tm // HALO
    return pl.pallas_call(
        functools.partial(_ffn_up_kernel, f=f, tc=tc, width=width),
        grid=(bsz, s // tm),
        in_specs=[pl.BlockSpec((1, HALO, d), lambda b_, r: (b_, jnp.maximum(r * hb - 1, 0), 0)),
                  pl.BlockSpec((1, tm, d), lambda b_, r: (b_, r, 0)),
                  pl.BlockSpec((1, d), lambda b_, r: (0, 0)),
                  pl.BlockSpec((1, 1, d), lambda b_, r: (b_, 0, 0)),
                  pl.BlockSpec((1, 1, d), lambda b_, r: (b_, 0, 0)),
                  pl.BlockSpec((d, n2), lambda b_, r: (0, 0)),
                  pl.BlockSpec((width, n2), lambda b_, r: (0, 0)),
                  pl.BlockSpec((1, n2), lambda b_, r: (0, 0))],
        out_specs=pl.BlockSpec((1, tm, f), lambda b_, r: (b_, r, 0)),
        out_shape=jax.ShapeDtypeStruct((bsz, s, f), BF16),
        compiler_params=_cparams(("arbitrary", "arbitrary")),
        name="ffn_up_conv_geglu",
    )(x, x, nw.reshape(1, d), sc, sh, w, cw, cb.reshape(1, n2))


def _rel_bucket_np(d):
    max_exact = REL_BUCKETS // 2
    d = np.maximum(d, 0)
    df = np.maximum(d, 1).astype(np.float64)
    large = max_exact + (np.log(df / max_exact) / math.log(REL_MAX_DIST / max_exact)
                         * (REL_BUCKETS - max_exact)).astype(np.int32)
    large = np.minimum(large, REL_BUCKETS - 1)
    return np.where(d < max_exact, d, large).astype(np.int32)


LOG2E = math.log2(math.e)


def _bias_kernel(tab_ref, idx_ref, o_ref):
    h = pl.program_id(0)
    idx = idx_ref[0]
    acc = jnp.full(idx.shape, NEG, F32)
    for bkt in range(REL_BUCKETS):
        acc = jnp.where(idx == bkt, tab_ref[bkt, h] * LOG2E, acc)
    o_ref[0, 0] = acc


def _bias_tiles(rel_bias, idx):
    t, r, c = idx.shape
    return pl.pallas_call(
        _bias_kernel,
        grid=(N_ATTN_HEADS, t),
        in_specs=[pl.BlockSpec(memory_space=pltpu.SMEM),
                  pl.BlockSpec((1, r, c), lambda h, i: (i, 0, 0))],
        out_specs=pl.BlockSpec((1, 1, r, c), lambda h, i: (h, i, 0, 0)),
        out_shape=jax.ShapeDtypeStruct((N_ATTN_HEADS, t, r, c), F32),
        compiler_params=_cparams(("arbitrary", "arbitrary")),
        name="rel_bias_tiles",
    )(rel_bias, idx)


def _swa_bucket_idx():
    c = np.arange(2 * SWA_BLOCK)[:, None]
    r = np.arange(SWA_BLOCK)[None, :]
    dist = SWA_BLOCK + r - c
    return np.where((dist >= 0) & (dist < SWA_BLOCK), _rel_bucket_np(dist), -1).astype(np.int32)[None]


def _moba_bucket_idx(nblk):
    c = np.arange(MOBA_BLOCK)[:, None]
    r = np.arange(MOBA_BLOCK)[None, :]
    tiles = [np.where(m * MOBA_BLOCK + r - c >= 0, _rel_bucket_np(m * MOBA_BLOCK + r - c), -1)
             for m in range(nblk)]
    return np.stack(tiles).astype(np.int32)


def _expand_heads(v, e):
    hi = v.astype(BF16)
    lo = (v - hi.astype(F32)).astype(BF16)
    return (jnp.dot(hi, e, preferred_element_type=F32) + jnp.dot(lo, e, preferred_element_type=F32))


def _ssd_kernel(zx_ref, dtc_ref, dtr_ref, cw_ref, cb_ref, bias_c_ref, bias_r_ref, alog_c_ref, alog_r_ref,
                dskip_ref, nw_ref, e_ref, o_ref, state_ref, win_ref):
    q = SSD_CHUNK
    gw = SSD_INNER // SSD_GROUPS
    hpg = SSD_HEADS // SSD_GROUPS

    @pl.when(pl.program_id(1) == 0)
    def _():
        state_ref[...] = jnp.zeros(state_ref.shape, F32)
        win_ref[:HALO, :] = jnp.zeros((HALO, SSD_XBC), F32)

    row = lax.broadcasted_iota(jnp.int32, (q, q), 0)
    col = lax.broadcasted_iota(jnp.int32, (q, q), 1)
    tril = row >= col
    tri_b = jnp.where(tril, 1.0, 0.0).astype(BF16)
    triu_b = jnp.where(row <= col, 1.0, 0.0).astype(BF16)

    dt_c = _softplus(dtc_ref[0].astype(F32) + bias_c_ref[...])
    da_c = dt_c * (-jnp.exp(alog_c_ref[...]))
    acs_c = _sum01_left(tri_b, da_c)
    dt_r = _softplus(dtr_ref[0] + bias_r_ref[...])
    da_r = dt_r * (-jnp.exp(alog_r_ref[...]))
    acs_r = _sum01_right(da_r, triu_b)

    acs_last = acs_c[q - 1:q, :]
    e = e_ref[...]
    dt_full = _expand_heads(dt_c, e)
    dtdec_full = _expand_heads(dt_c * jnp.exp(acs_last - acs_c), e)
    eacs_full = _expand_heads(jnp.exp(acs_c), e)
    cdecay_full = eacs_full[q - 1:q, :]

    raw = zx_ref[0, :, SSD_INNER:].astype(F32)
    win_ref[HALO:, :] = raw
    xbc = _silu(_window_conv(win_ref, 0, SSD_XBC, q, cw_ref[...], SSD_CONV) + cb_ref[...])
    win_ref[:HALO, :] = raw[q - HALO:, :]

    xs = xbc[:, :SSD_INNER]
    xdt = xs * dt_full
    xdec = xs * dtdec_full
    lane_half = lax.broadcasted_iota(jnp.int32, (1, LANES), 1) >> 6

    y_parts = []
    for g in range(SSD_GROUPS):
        b_g = xbc[:, SSD_INNER + g * SSD_STATE:SSD_INNER + (g + 1) * SSD_STATE]
        c_g = xbc[:, SSD_INNER + SSD_GROUPS * SSD_STATE + g * SSD_STATE:
                  SSD_INNER + SSD_GROUPS * SSD_STATE + (g + 1) * SSD_STATE]
        cb = _bdot_nt(c_g, b_g)
        st = state_ref[g]
        y_off = _bdot(c_g, st) * eacs_full[:, g * gw:(g + 1) * gw]
        state_ref[g] = st * cdecay_full[:, g * gw:(g + 1) * gw] + _bdot(b_g.T, xdec[:, g * gw:(g + 1) * gw])
        pair_parts = []
        for pr in range(hpg // 2):
            acc = None
            lo = g * gw + pr * LANES
            x_pair = xdt[:, lo:lo + LANES]
            for half in range(2):
                h = g * hpg + pr * 2 + half
                diff = acs_c[:, h:h + 1] - acs_r[h:h + 1, :]
                lmat = jnp.where(tril, jnp.exp(jnp.where(tril, diff, 0.0)), 0.0)
                part = _bdot(cb * lmat, jnp.where(lane_half == half, x_pair, 0.0))
                acc = part if acc is None else acc + part
            pair_parts.append(acc)
        y_diag = jnp.concatenate(pair_parts, axis=1)
        y = y_diag + y_off + dskip_ref[:, g * gw:(g + 1) * gw] * xs[:, g * gw:(g + 1) * gw]
        y = y * _silu(zx_ref[0, :, g * gw:(g + 1) * gw].astype(F32))
        ms = jnp.mean(y * y, axis=-1, keepdims=True)
        y_parts.append(y * lax.rsqrt(ms + RMS_EPS) * nw_ref[:, g * gw:(g + 1) * gw])
    o_ref[0] = jnp.concatenate(y_parts, axis=1).astype(o_ref.dtype)


def _ssd_mixer(proj, dt_rows, conv_w, conv_b, dt_bias, a_log, d_skip, norm_w):
    bsz, s, _ = proj.shape
    q = SSD_CHUNK
    zx = SSD_INNER + SSD_XBC
    pad = LANES - SSD_HEADS
    bias_c = jnp.pad(dt_bias, (0, pad)).reshape(1, LANES)
    alog_c = jnp.pad(a_log, (0, pad)).reshape(1, LANES)
    e_np = np.zeros((LANES, SSD_INNER), np.float32)
    for h in range(SSD_HEADS):
        e_np[h, h * SSD_HEAD_DIM:(h + 1) * SSD_HEAD_DIM] = 1.0
    small = lambda shape: pl.BlockSpec(shape, lambda b_, c: (0, 0))
    return pl.pallas_call(
        _ssd_kernel,
        grid=(bsz, s // q),
        in_specs=[pl.BlockSpec((1, q, zx), lambda b_, c: (b_, c, 0)),
                  pl.BlockSpec((1, q, LANES), lambda b_, c: (b_, c, AB_DT // LANES)),
                  pl.BlockSpec((1, LANES, q), lambda b_, c: (b_, 0, c)),
                  small((SSD_CONV, SSD_XBC)), small((1, SSD_XBC)),
                  small((1, LANES)), small((LANES, 1)), small((1, LANES)), small((LANES, 1)),
                  small((1, SSD_INNER)), small((1, SSD_INNER)), small((LANES, SSD_INNER))],
        out_specs=pl.BlockSpec((1, q, SSD_INNER), lambda b_, c: (b_, c, 0)),
        out_shape=jax.ShapeDtypeStruct((bsz, s, SSD_INNER), BF16),
        scratch_shapes=[pltpu.VMEM((SSD_GROUPS, SSD_STATE, SSD_INNER // SSD_GROUPS), F32),
                        pltpu.VMEM((HALO + q, SSD_XBC), F32)],
        compiler_params=_cparams(("arbitrary", "arbitrary")),
        name="ssd_mixer",
    )(proj, proj, dt_rows, conv_w, conv_b.reshape(1, SSD_XBC), bias_c, bias_c.reshape(LANES, 1),
      alog_c, alog_c.reshape(LANES, 1), jnp.repeat(d_skip, SSD_HEAD_DIM).reshape(1, SSD_INNER),
      norm_w.reshape(1, SSD_INNER), jnp.asarray(e_np, BF16))


def _swa_kernel(sink_ref, q_ref, kp_ref, kc_ref, vp_ref, vc_ref, bias_ref, o_ref, s_ref):
    blk = SWA_BLOCK
    n = pl.program_id(1)
    grp = N_ATTN_HEADS // SWA_KV_HEADS
    scale = ATTN_HEAD_DIM ** -0.5
    kk = jnp.concatenate([kp_ref[0], kc_ref[0]], axis=0).astype(F32) * (scale * LOG2E)
    vv_t = jnp.concatenate([vp_ref[0], vc_ref[0]], axis=0).astype(F32).T
    lane_half = lax.broadcasted_iota(jnp.int32, (1, LANES), 1) >> 6
    c = lax.broadcasted_iota(jnp.int32, (2 * blk, blk), 0)
    valid = (c >= blk) | (n > 0)
    ms = []
    for kv in range(SWA_KV_HEADS):
        k_own = jnp.where(lane_half == kv, kk, 0.0).astype(BF16)
        k_var = [None, None]
        k_var[kv] = k_own
        k_var[1 - kv] = pltpu.roll(jnp.where(lane_half == kv, kk, 0.0), ATTN_HEAD_DIM, axis=1).astype(BF16)
        for gq in range(grp):
            h = kv * grp + gq
            q_tile = q_ref[0, :, (h // 2) * LANES:(h // 2 + 1) * LANES]
            s_t = _bdot_nt(k_var[h % 2], q_tile) + bias_ref[h, 0]
            s_t = jnp.where(valid, s_t, NEG)
            s_ref[h] = s_t
            ms.append(jnp.maximum(jnp.max(s_t, axis=0, keepdims=True), sink_ref[h] * LOG2E))
    outs = []
    for h in range(N_ATTN_HEADS):
        kv = h // grp
        p = jnp.exp2(s_ref[h] - ms[h])
        l = jnp.sum(p, axis=0, keepdims=True) + jnp.exp2(sink_ref[h] * LOG2E - ms[h])
        outs.append(_bdot(vv_t[kv * ATTN_HEAD_DIM:(kv + 1) * ATTN_HEAD_DIM, :], p) / l)
    for t in range(N_ATTN_HEADS // 2):
        pair = jnp.concatenate([outs[2 * t], outs[2 * t + 1]], axis=0)
        o_ref[0, :, t * LANES:(t + 1) * LANES] = pair.T.astype(o_ref.dtype)


def _swa_attention(proj, sinks, bias):
    bsz, s, _ = proj.shape
    blk = SWA_BLOCK
    kvw = SWA_KV_HEADS * ATTN_HEAD_DIM
    prev = lambda col: pl.BlockSpec((1, blk, kvw), lambda b_, n: (b_, jnp.maximum(n - 1, 0), col))
    cur = lambda col: pl.BlockSpec((1, blk, kvw), lambda b_, n: (b_, n, col))
    return pl.pallas_call(
        _swa_kernel,
        grid=(bsz, s // blk),
        in_specs=[pl.BlockSpec(memory_space=pltpu.SMEM),
                  pl.BlockSpec((1, blk, ATTN_WIDTH), lambda b_, n: (b_, n, AB_Q // ATTN_WIDTH)),
                  prev(AB_K // kvw), cur(AB_K // kvw), prev(AB_V // kvw), cur(AB_V // kvw),
                  pl.BlockSpec((N_ATTN_HEADS, 1, 2 * blk, blk), lambda b_, n: (0, 0, 0, 0))],
        out_specs=pl.BlockSpec((1, blk, ATTN_WIDTH), lambda b_, n: (b_, n, 0)),
        out_shape=jax.ShapeDtypeStruct((bsz, s, ATTN_WIDTH), BF16),
        scratch_shapes=[pltpu.VMEM((N_ATTN_HEADS, 2 * blk, blk), F32)],
        compiler_params=_cparams(("arbitrary", "arbitrary")),
        name="swa_attention",
    )(sinks, proj, proj, proj, proj, proj, bias)


def _moba_kernel(q_ref, k_ref, v_ref, bias_ref, o_ref, vt_ref, kmean_ref, s_ref, *, nblk):
    mb = MOBA_BLOCK
    dh = ATTN_HEAD_DIM
    own = pl.program_id(2)
    scale = dh ** -0.5

    @pl.when(own == 0)
    def _():
        means = []
        for j in range(nblk):
            vt_ref[j] = v_ref[0, j * mb:(j + 1) * mb, :].astype(F32).T.astype(BF16)
            means.append(jnp.mean(k_ref[0, j * mb:(j + 1) * mb, :].astype(F32), axis=0, keepdims=True))
        kmean_ref[...] = jnp.concatenate(means, axis=0)

    lane_half = lax.broadcasted_iota(jnp.int32, (1, LANES), 1) >> 6
    blk_id = lax.broadcasted_iota(jnp.int32, (nblk, mb), 0)
    q = q_ref[0].astype(F32) * (scale * LOG2E)

    qms = []
    negrows = []
    for hh in range(2):
        qm = jnp.where(lane_half == hh, q, 0.0)
        qms.append(qm.astype(BF16))
        gate = lax.dot_general(kmean_ref[...], qm, (((1,), (1,)), ((), ())),
                               preferred_element_type=F32, precision=HIGHEST)
        gate = jnp.where(blk_id < own, gate, NEG)
        rank = jnp.zeros((nblk, mb), jnp.int32)
        for i in range(nblk):
            gi = gate[i:i + 1, :]
            ahead = (gi > gate) | ((gi == gate) & (i < blk_id))
            rank = rank + jnp.where(ahead, 1, 0)
        keep = ((rank < MOBA_TOPK) & (blk_id < own)) | (blk_id == own)
        negrows.append(jnp.where(keep, 0.0, NEG).astype(F32))

    def attend(nb):
        m = [None, None]
        for j in range(nb):
            for hh in range(2):
                s_t = _bdot_nt(k_ref[0, j * mb:(j + 1) * mb, :], qms[hh]) + bias_ref[hh, jnp.maximum(own - j, 0)]
                s_t = s_t + negrows[hh][j:j + 1, :]
                s_ref[hh, j] = s_t
                m_j = jnp.max(s_t, axis=0, keepdims=True)
                m[hh] = m_j if m[hh] is None else jnp.maximum(m[hh], m_j)
        l = [None, None]
        acc = [None, None]
        for j in range(nb):
            for hh in range(2):
                p = jnp.exp2(s_ref[hh, j] - m[hh])
                l_j = jnp.sum(p, axis=0, keepdims=True)
                a_j = _bdot(vt_ref[j, hh * dh:(hh + 1) * dh, :], p)
                l[hh] = l_j if l[hh] is None else l[hh] + l_j
                acc[hh] = a_j if acc[hh] is None else acc[hh] + a_j
        out_t = jnp.concatenate([acc[0] / l[0], acc[1] / l[1]], axis=0)
        o_ref[0] = out_t.T.astype(o_ref.dtype)

    half = nblk // 2

    @pl.when(own < half)
    def _():
        attend(half)

    @pl.when(own >= half)
    def _():
        attend(nblk)


def _moba_attention(proj, bias):
    bsz, s, _ = proj.shape
    mb = MOBA_BLOCK
    nblk = s // mb
    npair = N_ATTN_HEADS // 2
    return pl.pallas_call(
        functools.partial(_moba_kernel, nblk=nblk),
        grid=(bsz, npair, nblk),
        in_specs=[pl.BlockSpec((1, mb, LANES), lambda b_, p, i: (b_, i, CD_QD // LANES + p)),
                  pl.BlockSpec((1, s, LANES), lambda b_, p, i: (b_, 0, CD_KD // LANES + p)),
                  pl.BlockSpec((1, s, LANES), lambda b_, p, i: (b_, 0, CD_VD // LANES + p)),
                  pl.BlockSpec((2, nblk, mb, mb), lambda b_, p, i: (p, 0, 0, 0))],
        out_specs=pl.BlockSpec((1, mb, LANES), lambda b_, p, i: (b_, i, p)),
        out_shape=jax.ShapeDtypeStruct((bsz, s, ATTN_WIDTH), BF16),
        scratch_shapes=[pltpu.VMEM((nblk, LANES, mb), BF16),
                        pltpu.VMEM((nblk, LANES), F32),
                        pltpu.VMEM((2, nblk, mb, mb), F32)],
        compiler_params=_cparams(("arbitrary", "arbitrary", "arbitrary")),
        name="moba_attention",
    )(proj, proj, proj, bias)


def _gdn_kernel(qkv_ref, z_ref, bac_ref, bar_ref, cw_ref, bias_c_ref, bias_r_ref, alog_c_ref, alog_r_ref,
                nw_ref, o_ref, state_ref, win_ref, m_ref, attn_ref, t_ref, x_ref, rhs_ref, u_ref, wq_ref,
                kdt_ref, r_ref, glc_ref):
    t = 2 * GDN_CHUNK
    ck = GDN_CHUNK
    dk = GDN_HEAD_DIM
    nh = GDN_HEADS

    @pl.when(pl.program_id(1) == 0)
    def _():
        state_ref[...] = jnp.zeros(state_ref.shape, F32)
        win_ref[:HALO, :] = jnp.zeros((HALO, 3 * GDN_INNER), F32)

    row = lax.broadcasted_iota(jnp.int32, (t, t), 0)
    col = lax.broadcasted_iota(jnp.int32, (t, t), 1)
    same = (row >> 6) == (col >> 6)
    tril = same & (row >= col)
    strict = same & (row > col)
    tri_b = jnp.where(tril, 1.0, 0.0).astype(BF16)
    triu_b = jnp.where(same & (row <= col), 1.0, 0.0).astype(BF16)
    blk_b = jnp.where(same, 1.0, 0.0).astype(BF16)
    eye_f = jnp.where(row == col, 1.0, 0.0).astype(F32)
    merge_masks = [((row >> (l + 1)) == (col >> (l + 1))) & (((row >> l) & 1) == 1) & (((col >> l) & 1) == 0)
                   for l in range(int(math.log2(ck)))]
    mask_bf = [jnp.where(m, 1.0, 0.0).astype(BF16) for m in merge_masks[1:]]

    ba_c = bac_ref[0].astype(F32)
    g_c = -jnp.exp(alog_c_ref[...]) * _softplus(ba_c + bias_c_ref[...])
    sums_c = _sum01_left(jnp.concatenate([tri_b, blk_b], axis=0), g_c)
    gc_c = sums_c[:t]
    gl_c = sums_c[t:]
    g_r = -jnp.exp(alog_r_ref[...]) * _softplus(bar_ref[0] + bias_r_ref[...])
    gc_r = _sum01_right(g_r, triu_b)
    glc_ref[...] = gl_c

    def conv_silu(lo):
        raw = qkv_ref[0, :, lo:lo + dk].astype(F32)
        win_ref[HALO:, lo:lo + dk] = raw
        out = _silu(_window_conv(win_ref, lo, dk, t, cw_ref[:, lo:lo + dk], GDN_CONV))
        win_ref[:HALO, lo:lo + dk] = raw[t - HALO:, :]
        return out

    for h in range(nh):
        q = conv_silu(h * dk)
        k = conv_silu((nh + h) * dk)
        v = conv_silu((2 * nh + h) * dk)
        qn = q * lax.rsqrt(jnp.sum(q * q, axis=-1, keepdims=True) + 1e-6) * (dk ** -0.5)
        kn = k * lax.rsqrt(jnp.sum(k * k, axis=-1, keepdims=True) + 1e-6)
        beta = _sigmoid(ba_c[:, h:h + 1])
        gcc = gc_c[:, nh + h:nh + h + 1]
        gcr = gc_r[nh + h:nh + h + 1, :]
        glc = gl_c[:, nh + h:nh + h + 1]
        decay = jnp.where(tril, jnp.exp(jnp.where(tril, gcc - gcr, 0.0)), 0.0)
        kb = kn * beta
        kk = _bdot_nt(jnp.concatenate([kb, qn], axis=0), kn)
        mm = jnp.where(strict, kk[:t] * decay, 0.0)
        m_ref[h] = mm.astype(BF16)
        attn_ref[h] = (kk[t:] * decay).astype(BF16)
        t_ref[h] = eye_f - jnp.where(merge_masks[0], mm, 0.0)
        egc = jnp.exp(gcc)
        rhs_ref[h] = jnp.concatenate([v * beta, kb * egc], axis=1).astype(BF16)
        q_dec = (qn * egc).astype(BF16)
        for a in range(2):
            wq_ref[h, a, ck:, :] = q_dec[a * ck:(a + 1) * ck]
        kdt_ref[h] = (kn * jnp.exp(glc - gcc)).T.astype(BF16)

    for lvl in range(len(mask_bf)):
        for h in range(nh):
            x_ref[h] = jnp.dot(t_ref[h].astype(BF16), m_ref[h] * mask_bf[lvl],
                               preferred_element_type=F32).astype(BF16)
        for h in range(nh):
            t_h = t_ref[h]
            t_ref[h] = t_h - jnp.dot(x_ref[h], t_h.astype(BF16), preferred_element_type=F32)

    for h in range(nh):
        sol = jnp.dot(t_ref[h].astype(BF16), rhs_ref[h], preferred_element_type=F32)
        u_ref[h] = sol[:, :dk]
        for a in range(2):
            wq_ref[h, a, :ck, :] = sol[a * ck:(a + 1) * ck, dk:].astype(BF16)

    zeros_half = jnp.zeros((ck, dk), F32)
    for a in range(2):
        sl = slice(a * ck, (a + 1) * ck)
        for h in range(nh):
            r_ref[h] = jnp.dot(wq_ref[h, a], state_ref[h].astype(BF16), preferred_element_type=F32)
        for h in range(nh):
            v_new = u_ref[h, sl, :] - r_ref[h, :ck, :]
            v_full = jnp.concatenate([v_new, zeros_half] if a == 0 else [zeros_half, v_new], axis=0).astype(BF16)
            o = r_ref[h, ck:, :] + jnp.dot(attn_ref[h, sl, :], v_full, preferred_element_type=F32)
            gl = glc_ref[a * ck:a * ck + 1, nh + h:nh + h + 1]
            state_ref[h] = state_ref[h] * jnp.exp(gl) + jnp.dot(kdt_ref[h], v_full, preferred_element_type=F32)
            ms = jnp.mean(o * o, axis=-1, keepdims=True)
            y = o * lax.rsqrt(ms + RMS_EPS) * nw_ref[...] * _silu(z_ref[0, sl, h * dk:(h + 1) * dk].astype(F32))
            o_ref[0, sl, h * dk:(h + 1) * dk] = y.astype(o_ref.dtype)


def _gdn_mixer(proj, ba_rows, conv_w, dt_bias, a_log, norm_w):
    bsz, s, _ = proj.shape
    t = 2 * GDN_CHUNK
    nh = GDN_HEADS
    dk = GDN_HEAD_DIM
    bias_c = jnp.pad(dt_bias, (nh, LANES - 2 * nh)).reshape(1, LANES)
    alog_c = jnp.pad(a_log, (nh, LANES - 2 * nh)).reshape(1, LANES)
    small = lambda shape: pl.BlockSpec(shape, lambda b_, c: (0, 0))
    return pl.pallas_call(
        _gdn_kernel,
        grid=(bsz, s // t),
        in_specs=[pl.BlockSpec((1, t, 3 * GDN_INNER), lambda b_, c: (b_, c, CD_QKV // (3 * GDN_INNER))),
                  pl.BlockSpec((1, t, GDN_INNER), lambda b_, c: (b_, c, CD_Z // GDN_INNER)),
                  pl.BlockSpec((1, t, LANES), lambda b_, c: (b_, c, CD_BA // LANES)),
                  pl.BlockSpec((1, LANES, t), lambda b_, c: (b_, 0, c)),
                  small((GDN_CONV, 3 * GDN_INNER)),
                  small((1, LANES)), small((LANES, 1)), small((1, LANES)), small((LANES, 1)),
                  small((1, dk))],
        out_specs=pl.BlockSpec((1, t, GDN_INNER), lambda b_, c: (b_, c, 0)),
        out_shape=jax.ShapeDtypeStruct((bsz, s, GDN_INNER), BF16),
        scratch_shapes=[pltpu.VMEM((nh, dk, dk), F32),
                        pltpu.VMEM((HALO + t, 3 * GDN_INNER), F32),
                        pltpu.VMEM((nh, t, t), BF16),
                        pltpu.VMEM((nh, t, t), BF16),
                        pltpu.VMEM((nh, t, t), F32),
                        pltpu.VMEM((nh, t, t), BF16),
                        pltpu.VMEM((nh, t, 2 * dk), BF16),
                        pltpu.VMEM((nh, t, dk), F32),
                        pltpu.VMEM((nh, 2, t, dk), BF16),
                        pltpu.VMEM((nh, dk, t), BF16),
                        pltpu.VMEM((nh, t, dk), F32),
                        pltpu.VMEM((t, LANES), F32)],
        compiler_params=_cparams(("arbitrary", "arbitrary")),
        name="gdn_mixer",
    )(proj, proj, proj, ba_rows, conv_w, bias_c, bias_c.reshape(LANES, 1), alog_c, alog_c.reshape(LANES, 1),
      norm_w.reshape(1, dk))


def _pad_cols(w, n):
    return jnp.pad(w, ((0, 0), (0, n - w.shape[1])))


def _gate_rows(proj, col0):
    return jnp.swapaxes(proj[:, :, col0:col0 + LANES].astype(F32), 1, 2)


def kernel(x, c, rel_bias, norm_w, ada_w, ada_b, ab_w_in, ab_w_out, ssd_conv_w, ssd_conv_b,
           ssd_dt_bias, ssd_a_log, ssd_d, ssd_norm_w, swa_sinks, cd_w_in, cd_w_out, gdn_conv_w,
           gdn_dt_bias, gdn_a_log, gdn_norm_w, ffn_w_up, ffn_conv_w, ffn_conv_b, ffn_w_down):
    bsz, s, d = x.shape
    depth = norm_w.shape[0]
    mods = _mods(c, ada_w, ada_b)
    swa_bias = _bias_tiles(rel_bias, jnp.asarray(_swa_bucket_idx()))
    moba_bias = _bias_tiles(rel_bias, jnp.asarray(_moba_bucket_idx(s // MOBA_BLOCK)))

    for i in range(depth):
        sh_m, sc_m, g_m, sh_f, sc_f, g_f = [m.reshape(bsz, 1, d) for m in jnp.split(mods[i], 6, axis=-1)]
        j = i // 2
        if i % 2 == 0:
            w = ab_w_in[j]
            dt0 = SSD_INNER + SSD_XBC
            w_in = jnp.concatenate([w[:, :dt0], w[:, dt0 + SSD_HEADS:], w[:, dt0:dt0 + SSD_HEADS]], axis=1)
            w_in = _pad_cols(w_in, AB_COLS_PAD).astype(BF16)
            proj = _norm_mod_matmul(x, norm_w[i, 0], sc_m, sh_m, w_in)
            y_a = _ssd_mixer(proj, _gate_rows(proj, AB_DT), ssd_conv_w[j], ssd_conv_b[j], ssd_dt_bias[j],
                             ssd_a_log[j], ssd_d[j], ssd_norm_w[j])
            y_b = _swa_attention(proj, swa_sinks[j], swa_bias)
            x = _matmul_resid([y_a, y_b], ab_w_out[j].astype(BF16), x, g_m, norm_w[i, 1])
        else:
            w = cd_w_in[j]
            ba0 = 4 * GDN_INNER
            w_in = jnp.concatenate([w[:, :ba0], w[:, ba0 + 2 * GDN_HEADS:], w[:, ba0:ba0 + 2 * GDN_HEADS]], axis=1)
            w_in = _pad_cols(w_in, CD_COLS_PAD).astype(BF16)
            proj = _norm_mod_matmul(x, norm_w[i, 0], sc_m, sh_m, w_in)
            y_c = _gdn_mixer(proj, _gate_rows(proj, CD_BA), gdn_conv_w[j], gdn_dt_bias[j], gdn_a_log[j],
                             gdn_norm_w[j])
            y_d = _moba_attention(proj, moba_bias)
            x = _matmul_resid([y_c, y_d], cd_w_out[j].astype(BF16), x, g_m, norm_w[i, 1])
        act = _ffn_up_geglu(x, norm_w[i, 2], sc_f, sh_f, ffn_w_up[i].astype(BF16), ffn_conv_w[i], ffn_conv_b[i])
        x = _matmul_resid([act], ffn_w_down[i].astype(BF16), x, g_f, norm_w[i, 3])
    return x
```

```python
import functools
import math

import numpy as np
import jax
import jax.numpy as jnp
from jax import lax
from jax.experimental import pallas as pl
from jax.experimental.pallas import tpu as pltpu

F32 = jnp.float32
BF16 = jnp.bfloat16
HIGHEST = lax.Precision.HIGHEST

D_MODEL = 1024
RMS_EPS = 1e-6
NEG = -1e30
LANES = 128
SUBLANES = 8
N_ATTN_HEADS = 8
ATTN_HEAD_DIM = 64
ATTN_WIDTH = N_ATTN_HEADS * ATTN_HEAD_DIM
REL_BUCKETS = 32
REL_MAX_DIST = 1024
SSD_HEADS = 24
SSD_HEAD_DIM = 64
SSD_INNER = SSD_HEADS * SSD_HEAD_DIM
SSD_GROUPS = 4
SSD_STATE = 128
SSD_CONV = 4
SSD_CHUNK = 128
SSD_XBC = SSD_INNER + 2 * SSD_GROUPS * SSD_STATE
SWA_KV_HEADS = 2
SWA_BLOCK = 128
GDN_HEADS = 12
GDN_HEAD_DIM = 128
GDN_INNER = GDN_HEADS * GDN_HEAD_DIM
GDN_CONV = 4
GDN_CHUNK = 64
MOBA_BLOCK = 256
MOBA_TOPK = 3
FFN_DIM = 2816
FFN_CONV = 3

AB_Z, AB_XBC, AB_Q, AB_K, AB_V, AB_DT, AB_COLS_PAD = 0, 1536, 4096, 4608, 4736, 4864, 5120
CD_QKV, CD_Z, CD_QD, CD_KD, CD_VD, CD_BA, CD_COLS_PAD = 0, 4608, 6144, 6656, 7168, 7680, 8192

VMEM_LIMIT = 48 * 1024 * 1024
HALO = 16


def _cparams(sem):
    return pltpu.CompilerParams(dimension_semantics=sem, vmem_limit_bytes=VMEM_LIMIT)


def _bdot(a, b):
    return jnp.dot(a.astype(BF16), b.astype(BF16), preferred_element_type=F32)


def _bdot_nt(a, b):
    return lax.dot_general(a.astype(BF16), b.astype(BF16), (((1,), (1,)), ((), ())),
                           preferred_element_type=F32)


def _split3(x):
    hi = x.astype(BF16)
    r1 = x - hi.astype(F32)
    mid = r1.astype(BF16)
    lo = (r1 - mid.astype(F32)).astype(BF16)
    return hi, mid, lo


def _sum01_left(m01, x):
    n = x.shape[1]
    y = jnp.dot(m01, jnp.concatenate(_split3(x), axis=1), preferred_element_type=F32)
    return y[:, :n] + y[:, n:2 * n] + y[:, 2 * n:]


def _sum01_right(x, m01):
    n = x.shape[0]
    y = jnp.dot(jnp.concatenate(_split3(x), axis=0), m01, preferred_element_type=F32)
    return y[:n] + y[n:2 * n] + y[2 * n:]


def _softplus(x):
    return jnp.maximum(x, 0.0) + jnp.log(1.0 + jnp.exp(-jnp.abs(x)))


def _sigmoid(x):
    return 1.0 / (1.0 + jnp.exp(-x))


def _silu(x):
    return x * _sigmoid(x)


def _window_conv(win_ref, lo, ncols, rows, w, width, row0=0, shifted_loads=False):
    if shifted_loads:
        acc = None
        for s in range(width):
            tap = win_ref[pl.ds(HALO + row0 - s, rows), lo:lo + ncols] * w[width - 1 - s:width - s, :]
            acc = tap if acc is None else acc + tap
        return acc
    xin = win_ref[pl.ds(HALO + row0 - SUBLANES, rows + SUBLANES), lo:lo + ncols]
    acc = xin[SUBLANES:, :] * w[width - 1:width, :]
    for s in range(1, width):
        acc = acc + pltpu.roll(xin, s, axis=0)[SUBLANES:, :] * w[width - 1 - s:width - s, :]
    return acc


def _mods_kernel(c_ref, w_ref, b_ref, o_ref):
    o_ref[0] = _bdot(_silu(c_ref[...]), w_ref[0]) + b_ref[0]


def _mods(c, ada_w, ada_b):
    depth, d, n = ada_w.shape
    bsz = c.shape[0]
    tn = 512
    return pl.pallas_call(
        _mods_kernel,
        grid=(depth, n // tn),
        in_specs=[pl.BlockSpec((bsz, d), lambda l, j: (0, 0)),
                  pl.BlockSpec((1, d, tn), lambda l, j: (l, 0, j)),
                  pl.BlockSpec((1, 1, tn), lambda l, j: (l, 0, j))],
        out_specs=pl.BlockSpec((1, bsz, tn), lambda l, j: (l, 0, j)),
        out_shape=jax.ShapeDtypeStruct((depth, bsz, n), F32),
        compiler_params=_cparams(("arbitrary", "arbitrary")),
        name="adaln_mods",
    )(c, ada_w, ada_b.reshape(depth, 1, n))


def _modulated_norm(x, nw, sc, sh):
    ms = jnp.mean(x * x, axis=-1, keepdims=True)
    return x * lax.rsqrt(ms + RMS_EPS) * nw * (1.0 + sc) + sh


def _nmm_kernel(x_ref, nw_ref, sc_ref, sh_ref, w_ref, o_ref, h_ref):
    @pl.when(pl.program_id(1) == 0)
    def _():
        h_ref[...] = _modulated_norm(x_ref[0], nw_ref[...], sc_ref[0], sh_ref[0]).astype(BF16)

    o_ref[0] = jnp.dot(h_ref[...], w_ref[...], preferred_element_type=F32).astype(o_ref.dtype)


def _norm_mod_matmul(x, nw, sc, sh, w, tn=1024, tm=1024):
    bsz, s, d = x.shape
    n = w.shape[1]
    spt = s // tm
    return pl.pallas_call(
        _nmm_kernel,
        grid=(bsz * spt, n // tn),
        in_specs=[pl.BlockSpec((1, tm, d), lambda i, j: (i // spt, i % spt, 0)),
                  pl.BlockSpec((1, d), lambda i, j: (0, 0)),
                  pl.BlockSpec((1, 1, d), lambda i, j: (i // spt, 0, 0)),
                  pl.BlockSpec((1, 1, d), lambda i, j: (i // spt, 0, 0)),
                  pl.BlockSpec((d, tn), lambda i, j: (0, j))],
        out_specs=pl.BlockSpec((1, tm, tn), lambda i, j: (i // spt, i % spt, j)),
        out_shape=jax.ShapeDtypeStruct((bsz, s, n), BF16),
        scratch_shapes=[pltpu.VMEM((tm, d), BF16)],
        compiler_params=_cparams(("arbitrary", "arbitrary")),
        name="norm_mod_matmul",
    )(x, nw.reshape(1, d), sc, sh, w)


MMRES_ROWS = 256


def _mmres_kernel(*refs, splits):
    na = len(splits)
    a_refs = refs[:na]
    w_ref, x_ref, g_ref, nw_ref, o_ref = refs[na:]
    tm = x_ref.shape[1]
    for r0 in range(0, tm, MMRES_ROWS):
        rows = slice(r0, r0 + MMRES_ROWS)
        acc = None
        lo = 0
        for a_ref, k in zip(a_refs, splits):
            part = jnp.dot(a_ref[0, rows, :].astype(BF16), w_ref[lo:lo + k, :], preferred_element_type=F32)
            acc = part if acc is None else acc + part
            lo += k
        ms = jnp.mean(acc * acc, axis=-1, keepdims=True)
        y = acc * lax.rsqrt(ms + RMS_EPS) * nw_ref[...]
        o_ref[0, rows, :] = x_ref[0, rows, :] + g_ref[0] * y


def _matmul_resid(a_list, w, x, gate, nw, tm=512):
    bsz, s, d = x.shape
    spt = s // tm
    splits = tuple(a.shape[-1] for a in a_list)
    ktot = sum(splits)
    in_specs = [pl.BlockSpec((1, tm, k), lambda i: (i // spt, i % spt, 0)) for k in splits]
    in_specs += [pl.BlockSpec((ktot, d), lambda i: (0, 0)),
                 pl.BlockSpec((1, tm, d), lambda i: (i // spt, i % spt, 0)),
                 pl.BlockSpec((1, 1, d), lambda i: (i // spt, 0, 0)),
                 pl.BlockSpec((1, d), lambda i: (0, 0))]
    return pl.pallas_call(
        functools.partial(_mmres_kernel, splits=splits),
        grid=(bsz * spt,),
        in_specs=in_specs,
        out_specs=pl.BlockSpec((1, tm, d), lambda i: (i // spt, i % spt, 0)),
        out_shape=jax.ShapeDtypeStruct((bsz, s, d), F32),
        compiler_params=_cparams(("arbitrary",)),
        name="matmul_resid",
    )(*a_list, w, x, gate, nw.reshape(1, d))


def _ffn_up_kernel(xh_ref, x_ref, nw_ref, sc_ref, sh_ref, w_ref, cw_ref, cb_ref, o_ref, h_ref, u_ref,
                   *, f, tc, width, rb):
    tm = x_ref.shape[1]
    h_halo = _modulated_norm(xh_ref[0], nw_ref[...], sc_ref[0], sh_ref[0])
    h_ref[:HALO, :] = jnp.where(pl.program_id(1) > 0, h_halo, 0.0).astype(BF16)
    h_ref[HALO:, :] = _modulated_norm(x_ref[0], nw_ref[...], sc_ref[0], sh_ref[0]).astype(BF16)
    c0 = math.sqrt(2.0 / math.pi)
    nchunk = f // tc

    def matmuls(c):
        for k, off in enumerate((0, f)):
            lo = off + c * tc
            u_ref[c % 2, k] = jnp.dot(h_ref[...], w_ref[:, lo:lo + tc], preferred_element_type=F32)

    def epilogue(c):
        for r0 in range(0, tm, rb):
            gv = []
            for k, off in enumerate((0, f)):
                lo = off + c * tc
                gv.append(_window_conv(u_ref.at[c % 2, k], 0, tc, rb, cw_ref[:, lo:lo + tc], width, row0=r0,
                                       shifted_loads=True) + cb_ref[:, lo:lo + tc])
            g, v = gv
            th = jnp.tanh(g * (c0 + (c0 * 0.044715) * (g * g)))
            hg = 0.5 * g
            o_ref[0, r0:r0 + rb, c * tc:(c + 1) * tc] = ((hg + hg * th) * v).astype(o_ref.dtype)

    matmuls(0)
    for c in range(nchunk):
        if c + 1 < nchunk:
            matmuls(c + 1)
        epilogue(c)


def _ffn_up_geglu(x, nw, sc, sh, w, cw, cb, tm=512, tc=256, rb=128):
    bsz, s, d = x.shape
    n2 = w.shape[1]
    f = n2 // 2
    width = cw.shape[0]
    hb = tm // HALO
    return pl.pallas_call(
        functools.partial(_ffn_up_kernel, f=f, tc=tc, width=width, rb=rb),
        grid=(bsz, s // tm),
        in_specs=[pl.BlockSpec((1, HALO, d), lambda b_, r: (b_, jnp.maximum(r * hb - 1, 0), 0)),
                  pl.BlockSpec((1, tm, d), lambda b_, r: (b_, r, 0)),
                  pl.BlockSpec((1, d), lambda b_, r: (0, 0)),
                  pl.BlockSpec((1, 1, d), lambda b_, r: (b_, 0, 0)),
                  pl.BlockSpec((1, 1, d), lambda b_, r: (b_, 0, 0)),
                  pl.BlockSpec((d, n2), lambda b_, r: (0, 0)),
                  pl.BlockSpec((width, n2), lambda b_, r: (0, 0)),
                  pl.BlockSpec((1, n2), lambda b_, r: (0, 0))],
        out_specs=pl.BlockSpec((1, tm, f), lambda b_, r: (b_, r, 0)),
        out_shape=jax.ShapeDtypeStruct((bsz, s, f), BF16),
        scratch_shapes=[pltpu.VMEM((HALO + tm, d), BF16),
                        pltpu.VMEM((2, 2, HALO + tm, tc), F32)],
        compiler_params=_cparams(("arbitrary", "arbitrary")),
        name="ffn_up_conv_geglu",
    )(x, x, nw.reshape(1, d), sc, sh, w, cw, cb.reshape(1, n2))


def _rel_bucket_np(d):
    max_exact = REL_BUCKETS // 2
    d = np.maximum(d, 0)
    df = np.maximum(d, 1).astype(np.float64)
    large = max_exact + (np.log(df / max_exact) / math.log(REL_MAX_DIST / max_exact)
                         * (REL_BUCKETS - max_exact)).astype(np.int32)
    large = np.minimum(large, REL_BUCKETS - 1)
    return np.where(d < max_exact, d, large).astype(np.int32)


LOG2E = math.log2(math.e)


def _bias_kernel(tab_ref, idx_ref, o_ref):
    h = pl.program_id(0)
    idx = idx_ref[0]
    acc = jnp.full(idx.shape, NEG, F32)
    for bkt in range(REL_BUCKETS):
        acc = jnp.where(idx == bkt, tab_ref[bkt, h] * LOG2E, acc)
    o_ref[0, 0] = acc


def _bias_tiles(rel_bias, idx):
    t, r, c = idx.shape
    return pl.pallas_call(
        _bias_kernel,
        grid=(N_ATTN_HEADS, t),
        in_specs=[pl.BlockSpec(memory_space=pltpu.SMEM),
                  pl.BlockSpec((1, r, c), lambda h, i: (i, 0, 0))],
        out_specs=pl.BlockSpec((1, 1, r, c), lambda h, i: (h, i, 0, 0)),
        out_shape=jax.ShapeDtypeStruct((N_ATTN_HEADS, t, r, c), F32),
        compiler_params=_cparams(("arbitrary", "arbitrary")),
        name="rel_bias_tiles",
    )(rel_bias, idx)


def _swa_bucket_idx():
    c = np.arange(2 * SWA_BLOCK)[:, None]
    r = np.arange(SWA_BLOCK)[None, :]
    dist = SWA_BLOCK + r - c
    return np.where((dist >= 0) & (dist < SWA_BLOCK), _rel_bucket_np(dist), -1).astype(np.int32)[None]


def _moba_bucket_idx(nblk):
    c = np.arange(MOBA_BLOCK)[:, None]
    r = np.arange(MOBA_BLOCK)[None, :]
    tiles = [np.where(m * MOBA_BLOCK + r - c >= 0, _rel_bucket_np(m * MOBA_BLOCK + r - c), -1)
             for m in range(nblk)]
    return np.stack(tiles).astype(np.int32)


def _expand_heads(v, e):
    hi = v.astype(BF16)
    lo = (v - hi.astype(F32)).astype(BF16)
    return (jnp.dot(hi, e, preferred_element_type=F32) + jnp.dot(lo, e, preferred_element_type=F32))


def _ssd_kernel(zx_ref, dtc_ref, dtr_ref, cw_ref, cb_ref, bias_c_ref, bias_r_ref, alog_c_ref, alog_r_ref,
                dskip_ref, nw_ref, e_ref, o_ref, state_ref, win_ref):
    q = SSD_CHUNK
    gw = SSD_INNER // SSD_GROUPS
    hpg = SSD_HEADS // SSD_GROUPS

    @pl.when(pl.program_id(1) == 0)
    def _():
        state_ref[...] = jnp.zeros(state_ref.shape, F32)
        win_ref[:HALO, :] = jnp.zeros((HALO, SSD_XBC), F32)

    row = lax.broadcasted_iota(jnp.int32, (q, q), 0)
    col = lax.broadcasted_iota(jnp.int32, (q, q), 1)
    tril = row >= col
    tri_b = jnp.where(tril, 1.0, 0.0).astype(BF16)
    triu_b = jnp.where(row <= col, 1.0, 0.0).astype(BF16)

    dt_c = _softplus(dtc_ref[0].astype(F32) + bias_c_ref[...])
    da_c = dt_c * (-jnp.exp(alog_c_ref[...]))
    acs_c = _sum01_left(tri_b, da_c)
    dt_r = _softplus(dtr_ref[0] + bias_r_ref[...])
    da_r = dt_r * (-jnp.exp(alog_r_ref[...]))
    acs_r = _sum01_right(da_r, triu_b)

    acs_last = acs_c[q - 1:q, :]
    e = e_ref[...]
    dt_full = _expand_heads(dt_c, e)
    dtdec_full = _expand_heads(dt_c * jnp.exp(acs_last - acs_c), e)
    eacs_full = _expand_heads(jnp.exp(acs_c), e)
    cdecay_full = eacs_full[q - 1:q, :]

    raw = zx_ref[0, :, SSD_INNER:].astype(F32)
    win_ref[HALO:, :] = raw
    xbc = _silu(_window_conv(win_ref, 0, SSD_XBC, q, cw_ref[...], SSD_CONV) + cb_ref[...])
    win_ref[:HALO, :] = raw[q - HALO:, :]

    xs = xbc[:, :SSD_INNER]
    xdt = xs * dt_full
    xdec = xs * dtdec_full
    lane_half = lax.broadcasted_iota(jnp.int32, (1, LANES), 1) >> 6

    y_parts = []
    for g in range(SSD_GROUPS):
        b_g = xbc[:, SSD_INNER + g * SSD_STATE:SSD_INNER + (g + 1) * SSD_STATE]
        c_g = xbc[:, SSD_INNER + SSD_GROUPS * SSD_STATE + g * SSD_STATE:
                  SSD_INNER + SSD_GROUPS * SSD_STATE + (g + 1) * SSD_STATE]
        cb = _bdot_nt(c_g, b_g)
        st = state_ref[g]
        y_off = _bdot(c_g, st) * eacs_full[:, g * gw:(g + 1) * gw]
        state_ref[g] = st * cdecay_full[:, g * gw:(g + 1) * gw] + _bdot(b_g.T, xdec[:, g * gw:(g + 1) * gw])
        pair_parts = []
        for pr in range(hpg // 2):
            acc = None
            lo = g * gw + pr * LANES
            x_pair = xdt[:, lo:lo + LANES]
            for half in range(2):
                h = g * hpg + pr * 2 + half
                diff = acs_c[:, h:h + 1] - acs_r[h:h + 1, :]
                lmat = jnp.where(tril, jnp.exp(jnp.where(tril, diff, 0.0)), 0.0)
                part = _bdot(cb * lmat, jnp.where(lane_half == half, x_pair, 0.0))
                acc = part if acc is None else acc + part
            pair_parts.append(acc)
        y_diag = jnp.concatenate(pair_parts, axis=1)
        y = y_diag + y_off + dskip_ref[:, g * gw:(g + 1) * gw] * xs[:, g * gw:(g + 1) * gw]
        y = y * _silu(zx_ref[0, :, g * gw:(g + 1) * gw].astype(F32))
        ms = jnp.mean(y * y, axis=-1, keepdims=True)
        y_parts.append(y * lax.rsqrt(ms + RMS_EPS) * nw_ref[:, g * gw:(g + 1) * gw])
    o_ref[0] = jnp.concatenate(y_parts, axis=1).astype(o_ref.dtype)


def _ssd_mixer(proj, dt_rows, conv_w, conv_b, dt_bias, a_log, d_skip, norm_w):
    bsz, s, _ = proj.shape
    q = SSD_CHUNK
    zx = SSD_INNER + SSD_XBC
    pad = LANES - SSD_HEADS
    bias_c = jnp.pad(dt_bias, (0, pad)).reshape(1, LANES)
    alog_c = jnp.pad(a_log, (0, pad)).reshape(1, LANES)
    e_np = np.zeros((LANES, SSD_INNER), np.float32)
    for h in range(SSD_HEADS):
        e_np[h, h * SSD_HEAD_DIM:(h + 1) * SSD_HEAD_DIM] = 1.0
    small = lambda shape: pl.BlockSpec(shape, lambda b_, c: (0, 0))
    return pl.pallas_call(
        _ssd_kernel,
        grid=(bsz, s // q),
        in_specs=[pl.BlockSpec((1, q, zx), lambda b_, c: (b_, c, 0)),
                  pl.BlockSpec((1, q, LANES), lambda b_, c: (b_, c, AB_DT // LANES)),
                  pl.BlockSpec((1, LANES, q), lambda b_, c: (b_, 0, c)),
                  small((SSD_CONV, SSD_XBC)), small((1, SSD_XBC)),
                  small((1, LANES)), small((LANES, 1)), small((1, LANES)), small((LANES, 1)),
                  small((1, SSD_INNER)), small((1, SSD_INNER)), small((LANES, SSD_INNER))],
        out_specs=pl.BlockSpec((1, q, SSD_INNER), lambda b_, c: (b_, c, 0)),
        out_shape=jax.ShapeDtypeStruct((bsz, s, SSD_INNER), BF16),
        scratch_shapes=[pltpu.VMEM((SSD_GROUPS, SSD_STATE, SSD_INNER // SSD_GROUPS), F32),
                        pltpu.VMEM((HALO + q, SSD_XBC), F32)],
        compiler_params=_cparams(("arbitrary", "arbitrary")),
        name="ssd_mixer",
    )(proj, proj, dt_rows, conv_w, conv_b.reshape(1, SSD_XBC), bias_c, bias_c.reshape(LANES, 1),
      alog_c, alog_c.reshape(LANES, 1), jnp.repeat(d_skip, SSD_HEAD_DIM).reshape(1, SSD_INNER),
      norm_w.reshape(1, SSD_INNER), jnp.asarray(e_np, BF16))


def _swa_kernel(sink_ref, q_ref, kp_ref, kc_ref, vp_ref, vc_ref, bias_ref, o_ref, s_ref):
    blk = SWA_BLOCK
    n = pl.program_id(1)
    grp = N_ATTN_HEADS // SWA_KV_HEADS
    scale = ATTN_HEAD_DIM ** -0.5
    kk = jnp.concatenate([kp_ref[0], kc_ref[0]], axis=0).astype(F32) * (scale * LOG2E)
    vv_t = jnp.concatenate([vp_ref[0], vc_ref[0]], axis=0).astype(F32).T
    lane_half = lax.broadcasted_iota(jnp.int32, (1, LANES), 1) >> 6
    c = lax.broadcasted_iota(jnp.int32, (2 * blk, blk), 0)
    valid = (c >= blk) | (n > 0)
    ms = []
    for kv in range(SWA_KV_HEADS):
        k_own = jnp.where(lane_half == kv, kk, 0.0).astype(BF16)
        k_var = [None, None]
        k_var[kv] = k_own
        k_var[1 - kv] = pltpu.roll(jnp.where(lane_half == kv, kk, 0.0), ATTN_HEAD_DIM, axis=1).astype(BF16)
        for gq in range(grp):
            h = kv * grp + gq
            q_tile = q_ref[0, :, (h // 2) * LANES:(h // 2 + 1) * LANES]
            s_t = _bdot_nt(k_var[h % 2], q_tile) + bias_ref[h, 0]
            s_t = jnp.where(valid, s_t, NEG)
            s_ref[h] = s_t
            ms.append(jnp.maximum(jnp.max(s_t, axis=0, keepdims=True), sink_ref[h] * LOG2E))
    outs = []
    for h in range(N_ATTN_HEADS):
        kv = h // grp
        p = jnp.exp2(s_ref[h] - ms[h])
        l = jnp.sum(p, axis=0, keepdims=True) + jnp.exp2(sink_ref[h] * LOG2E - ms[h])
        outs.append(_bdot(vv_t[kv * ATTN_HEAD_DIM:(kv + 1) * ATTN_HEAD_DIM, :], p) / l)
    for t in range(N_ATTN_HEADS // 2):
        pair = jnp.concatenate([outs[2 * t], outs[2 * t + 1]], axis=0)
        o_ref[0, :, t * LANES:(t + 1) * LANES] = pair.T.astype(o_ref.dtype)


def _swa_attention(proj, sinks, bias):
    bsz, s, _ = proj.shape
    blk = SWA_BLOCK
    kvw = SWA_KV_HEADS * ATTN_HEAD_DIM
    prev = lambda col: pl.BlockSpec((1, blk, kvw), lambda b_, n: (b_, jnp.maximum(n - 1, 0), col))
    cur = lambda col: pl.BlockSpec((1, blk, kvw), lambda b_, n: (b_, n, col))
    return pl.pallas_call(
        _swa_kernel,
        grid=(bsz, s // blk),
        in_specs=[pl.BlockSpec(memory_space=pltpu.SMEM),
                  pl.BlockSpec((1, blk, ATTN_WIDTH), lambda b_, n: (b_, n, AB_Q // ATTN_WIDTH)),
                  prev(AB_K // kvw), cur(AB_K // kvw), prev(AB_V // kvw), cur(AB_V // kvw),
                  pl.BlockSpec((N_ATTN_HEADS, 1, 2 * blk, blk), lambda b_, n: (0, 0, 0, 0))],
        out_specs=pl.BlockSpec((1, blk, ATTN_WIDTH), lambda b_, n: (b_, n, 0)),
        out_shape=jax.ShapeDtypeStruct((bsz, s, ATTN_WIDTH), BF16),
        scratch_shapes=[pltpu.VMEM((N_ATTN_HEADS, 2 * blk, blk), F32)],
        compiler_params=_cparams(("arbitrary", "arbitrary")),
        name="swa_attention",
    )(sinks, proj, proj, proj, proj, proj, bias)


def _moba_kernel(q_ref, k_ref, v_ref, bias_ref, o_ref, vt_ref, kmean_ref, s_ref, *, nblk):
    mb = MOBA_BLOCK
    dh = ATTN_HEAD_DIM
    own = pl.program_id(2)
    scale = dh ** -0.5

    @pl.when(own == 0)
    def _():
        means = []
        for j in range(nblk):
            vt_ref[j] = v_ref[0, j * mb:(j + 1) * mb, :].astype(F32).T.astype(BF16)
            means.append(jnp.mean(k_ref[0, j * mb:(j + 1) * mb, :].astype(F32), axis=0, keepdims=True))
        kmean_ref[...] = jnp.concatenate(means, axis=0)

    lane_half = lax.broadcasted_iota(jnp.int32, (1, LANES), 1) >> 6
    blk_id = lax.broadcasted_iota(jnp.int32, (nblk, mb), 0)
    q = q_ref[0].astype(F32) * (scale * LOG2E)

    qms = []
    negrows = []
    for hh in range(2):
        qm = jnp.where(lane_half == hh, q, 0.0)
        qms.append(qm.astype(BF16))
        gate = lax.dot_general(kmean_ref[...], qm, (((1,), (1,)), ((), ())),
                               preferred_element_type=F32, precision=HIGHEST)
        gate = jnp.where(blk_id < own, gate, NEG)
        rank = jnp.zeros((nblk, mb), jnp.int32)
        for i in range(nblk):
            gi = gate[i:i + 1, :]
            ahead = (gi > gate) | ((gi == gate) & (i < blk_id))
            rank = rank + jnp.where(ahead, 1, 0)
        keep = ((rank < MOBA_TOPK) & (blk_id < own)) | (blk_id == own)
        negrows.append(jnp.where(keep, 0.0, NEG).astype(F32))

    def attend(nb):
        m = [None, None]
        for j in range(nb):
            for hh in range(2):
                s_t = _bdot_nt(k_ref[0, j * mb:(j + 1) * mb, :], qms[hh]) + bias_ref[hh, jnp.maximum(own - j, 0)]
                s_t = s_t + negrows[hh][j:j + 1, :]
                s_ref[hh, j] = s_t
                m_j = jnp.max(s_t, axis=0, keepdims=True)
                m[hh] = m_j if m[hh] is None else jnp.maximum(m[hh], m_j)
        l = [None, None]
        acc = [None, None]
        for j in range(nb):
            for hh in range(2):
                p = jnp.exp2(s_ref[hh, j] - m[hh])
                l_j = jnp.sum(p, axis=0, keepdims=True)
                a_j = _bdot(vt_ref[j, hh * dh:(hh + 1) * dh, :], p)
                l[hh] = l_j if l[hh] is None else l[hh] + l_j
                acc[hh] = a_j if acc[hh] is None else acc[hh] + a_j
        out_t = jnp.concatenate([acc[0] / l[0], acc[1] / l[1]], axis=0)
        o_ref[0] = out_t.T.astype(o_ref.dtype)

    for nb in range(2, nblk + 1, 2):
        pl.when((own >= nb - 2) & (own < nb))(functools.partial(attend, nb))


def _moba_attention(proj, bias):
    bsz, s, _ = proj.shape
    mb = MOBA_BLOCK
    nblk = s // mb
    npair = N_ATTN_HEADS // 2
    return pl.pallas_call(
        functools.partial(_moba_kernel, nblk=nblk),
        grid=(bsz, npair, nblk),
        in_specs=[pl.BlockSpec((1, mb, LANES), lambda b_, p, i: (b_, i, CD_QD // LANES + p)),
                  pl.BlockSpec((1, s, LANES), lambda b_, p, i: (b_, 0, CD_KD // LANES + p)),
                  pl.BlockSpec((1, s, LANES), lambda b_, p, i: (b_, 0, CD_VD // LANES + p)),
                  pl.BlockSpec((2, nblk, mb, mb), lambda b_, p, i: (p, 0, 0, 0))],
        out_specs=pl.BlockSpec((1, mb, LANES), lambda b_, p, i: (b_, i, p)),
        out_shape=jax.ShapeDtypeStruct((bsz, s, ATTN_WIDTH), BF16),
        scratch_shapes=[pltpu.VMEM((nblk, LANES, mb), BF16),
                        pltpu.VMEM((nblk, LANES), F32),
                        pltpu.VMEM((2, nblk, mb, mb), F32)],
        compiler_params=_cparams(("arbitrary", "arbitrary", "arbitrary")),
        name="moba_attention",
    )(proj, proj, proj, bias)


def _gdn_kernel(qkv_ref, z_ref, bac_ref, bar_ref, cw_ref, bias_c_ref, bias_r_ref, alog_c_ref, alog_r_ref,
                nw_ref, o_ref, state_ref, win_ref, m_ref, attn_ref, t_ref, x_ref, rhs_ref, u_ref, wq_ref,
                kdt_ref, r_ref, glc_ref):
    t = 2 * GDN_CHUNK
    ck = GDN_CHUNK
    dk = GDN_HEAD_DIM
    nh = GDN_HEADS

    @pl.when(pl.program_id(1) == 0)
    def _():
        state_ref[...] = jnp.zeros(state_ref.shape, F32)
        win_ref[:HALO, :] = jnp.zeros((HALO, 3 * GDN_INNER), F32)

    row = lax.broadcasted_iota(jnp.int32, (t, t), 0)
    col = lax.broadcasted_iota(jnp.int32, (t, t), 1)
    same = (row >> 6) == (col >> 6)
    tril = same & (row >= col)
    strict = same & (row > col)
    tri_b = jnp.where(tril, 1.0, 0.0).astype(BF16)
    triu_b = jnp.where(same & (row <= col), 1.0, 0.0).astype(BF16)
    blk_b = jnp.where(same, 1.0, 0.0).astype(BF16)
    eye_f = jnp.where(row == col, 1.0, 0.0).astype(F32)
    merge_masks = [((row >> (l + 1)) == (col >> (l + 1))) & (((row >> l) & 1) == 1) & (((col >> l) & 1) == 0)
                   for l in range(int(math.log2(ck)))]
    mask_bf = [jnp.where(m, 1.0, 0.0).astype(BF16) for m in merge_masks[1:]]

    ba_c = bac_ref[0].astype(F32)
    g_c = -jnp.exp(alog_c_ref[...]) * _softplus(ba_c + bias_c_ref[...])
    sums_c = _sum01_left(jnp.concatenate([tri_b, blk_b], axis=0), g_c)
    gc_c = sums_c[:t]
    gl_c = sums_c[t:]
    g_r = -jnp.exp(alog_r_ref[...]) * _softplus(bar_ref[0] + bias_r_ref[...])
    gc_r = _sum01_right(g_r, triu_b)
    glc_ref[...] = gl_c

    def conv_silu(lo):
        raw = qkv_ref[0, :, lo:lo + dk].astype(F32)
        win_ref[HALO:, lo:lo + dk] = raw
        out = _silu(_window_conv(win_ref, lo, dk, t, cw_ref[:, lo:lo + dk], GDN_CONV))
        win_ref[:HALO, lo:lo + dk] = raw[t - HALO:, :]
        return out

    for h in range(nh):
        q = conv_silu(h * dk)
        k = conv_silu((nh + h) * dk)
        v = conv_silu((2 * nh + h) * dk)
        qn = q * lax.rsqrt(jnp.sum(q * q, axis=-1, keepdims=True) + 1e-6) * (dk ** -0.5)
        kn = k * lax.rsqrt(jnp.sum(k * k, axis=-1, keepdims=True) + 1e-6)
        beta = _sigmoid(ba_c[:, h:h + 1])
        gcc = gc_c[:, nh + h:nh + h + 1]
        gcr = gc_r[nh + h:nh + h + 1, :]
        glc = gl_c[:, nh + h:nh + h + 1]
        decay = jnp.where(tril, jnp.exp(jnp.where(tril, gcc - gcr, 0.0)), 0.0)
        kb = kn * beta
        kk = _bdot_nt(jnp.concatenate([kb, qn], axis=0), kn)
        mm = jnp.where(strict, kk[:t] * decay, 0.0)
        m_ref[h] = mm.astype(BF16)
        attn_ref[h] = (kk[t:] * decay).astype(BF16)
        t_ref[h] = eye_f - jnp.where(merge_masks[0], mm, 0.0)
        egc = jnp.exp(gcc)
        rhs_ref[h] = jnp.concatenate([v * beta, kb * egc], axis=1).astype(BF16)
        q_dec = (qn * egc).astype(BF16)
        for a in range(2):
            wq_ref[h, a, ck:, :] = q_dec[a * ck:(a + 1) * ck]
        kdt_ref[h] = (kn * jnp.exp(glc - gcc)).T.astype(BF16)

    for lvl in range(len(mask_bf)):
        for h in range(nh):
            x_ref[h] = jnp.dot(t_ref[h].astype(BF16), m_ref[h] * mask_bf[lvl],
                               preferred_element_type=F32).astype(BF16)
        for h in range(nh):
            t_h = t_ref[h]
            t_ref[h] = t_h - jnp.dot(x_ref[h], t_h.astype(BF16), preferred_element_type=F32)

    for h in range(nh):
        sol = jnp.dot(t_ref[h].astype(BF16), rhs_ref[h], preferred_element_type=F32)
        u_ref[h] = sol[:, :dk]
        for a in range(2):
            wq_ref[h, a, :ck, :] = sol[a * ck:(a + 1) * ck, dk:].astype(BF16)

    zeros_half = jnp.zeros((ck, dk), F32)
    for a in range(2):
        sl = slice(a * ck, (a + 1) * ck)
        for h in range(nh):
            r_ref[h] = jnp.dot(wq_ref[h, a], state_ref[h].astype(BF16), preferred_element_type=F32)
        for h in range(nh):
            v_new = u_ref[h, sl, :] - r_ref[h, :ck, :]
            v_full = jnp.concatenate([v_new, zeros_half] if a == 0 else [zeros_half, v_new], axis=0).astype(BF16)
            o = r_ref[h, ck:, :] + jnp.dot(attn_ref[h, sl, :], v_full, preferred_element_type=F32)
            gl = glc_ref[a * ck:a * ck + 1, nh + h:nh + h + 1]
            state_ref[h] = state_ref[h] * jnp.exp(gl) + jnp.dot(kdt_ref[h], v_full, preferred_element_type=F32)
            ms = jnp.mean(o * o, axis=-1, keepdims=True)
            y = o * lax.rsqrt(ms + RMS_EPS) * nw_ref[...] * _silu(z_ref[0, sl, h * dk:(h + 1) * dk].astype(F32))
            o_ref[0, sl, h * dk:(h + 1) * dk] = y.astype(o_ref.dtype)


def _gdn_mixer(proj, ba_rows, conv_w, dt_bias, a_log, norm_w):
    bsz, s, _ = proj.shape
    t = 2 * GDN_CHUNK
    nh = GDN_HEADS
    dk = GDN_HEAD_DIM
    bias_c = jnp.pad(dt_bias, (nh, LANES - 2 * nh)).reshape(1, LANES)
    alog_c = jnp.pad(a_log, (nh, LANES - 2 * nh)).reshape(1, LANES)
    small = lambda shape: pl.BlockSpec(shape, lambda b_, c: (0, 0))
    return pl.pallas_call(
        _gdn_kernel,
        grid=(bsz, s // t),
        in_specs=[pl.BlockSpec((1, t, 3 * GDN_INNER), lambda b_, c: (b_, c, CD_QKV // (3 * GDN_INNER))),
                  pl.BlockSpec((1, t, GDN_INNER), lambda b_, c: (b_, c, CD_Z // GDN_INNER)),
                  pl.BlockSpec((1, t, LANES), lambda b_, c: (b_, c, CD_BA // LANES)),
                  pl.BlockSpec((1, LANES, t), lambda b_, c: (b_, 0, c)),
                  small((GDN_CONV, 3 * GDN_INNER)),
                  small((1, LANES)), small((LANES, 1)), small((1, LANES)), small((LANES, 1)),
                  small((1, dk))],
        out_specs=pl.BlockSpec((1, t, GDN_INNER), lambda b_, c: (b_, c, 0)),
        out_shape=jax.ShapeDtypeStruct((bsz, s, GDN_INNER), BF16),
        scratch_shapes=[pltpu.VMEM((nh, dk, dk), F32),
                        pltpu.VMEM((HALO + t, 3 * GDN_INNER), F32),
                        pltpu.VMEM((nh, t, t), BF16),
                        pltpu.VMEM((nh, t, t), BF16),
                        pltpu.VMEM((nh, t, t), F32),
                        pltpu.VMEM((nh, t, t), BF16),
                        pltpu.VMEM((nh, t, 2 * dk), BF16),
                        pltpu.VMEM((nh, t, dk), F32),
                        pltpu.VMEM((nh, 2, t, dk), BF16),
                        pltpu.VMEM((nh, dk, t), BF16),
                        pltpu.VMEM((nh, t, dk), F32),
                        pltpu.VMEM((t, LANES), F32)],
        compiler_params=_cparams(("arbitrary", "arbitrary")),
        name="gdn_mixer",
    )(proj, proj, proj, ba_rows, conv_w, bias_c, bias_c.reshape(LANES, 1), alog_c, alog_c.reshape(LANES, 1),
      norm_w.reshape(1, dk))


def _pad_cols(w, n):
    return jnp.pad(w, ((0, 0), (0, n - w.shape[1])))


def _gate_rows(proj, col0):
    return jnp.swapaxes(proj[:, :, col0:col0 + LANES].astype(F32), 1, 2)


def kernel(x, c, rel_bias, norm_w, ada_w, ada_b, ab_w_in, ab_w_out, ssd_conv_w, ssd_conv_b,
           ssd_dt_bias, ssd_a_log, ssd_d, ssd_norm_w, swa_sinks, cd_w_in, cd_w_out, gdn_conv_w,
           gdn_dt_bias, gdn_a_log, gdn_norm_w, ffn_w_up, ffn_conv_w, ffn_conv_b, ffn_w_down):
    bsz, s, d = x.shape
    depth = norm_w.shape[0]
    mods = _mods(c, ada_w, ada_b)
    swa_bias = _bias_tiles(rel_bias, jnp.asarray(_swa_bucket_idx()))
    moba_bias = _bias_tiles(rel_bias, jnp.asarray(_moba_bucket_idx(s // MOBA_BLOCK)))

    for i in range(depth):
        sh_m, sc_m, g_m, sh_f, sc_f, g_f = [m.reshape(bsz, 1, d) for m in jnp.split(mods[i], 6, axis=-1)]
        j = i // 2
        if i % 2 == 0:
            w = ab_w_in[j]
            dt0 = SSD_INNER + SSD_XBC
            w_in = jnp.concatenate([w[:, :dt0], w[:, dt0 + SSD_HEADS:], w[:, dt0:dt0 + SSD_HEADS]], axis=1)
            w_in = _pad_cols(w_in, AB_COLS_PAD).astype(BF16)
            proj = _norm_mod_matmul(x, norm_w[i, 0], sc_m, sh_m, w_in)
            y_a = _ssd_mixer(proj, _gate_rows(proj, AB_DT), ssd_conv_w[j], ssd_conv_b[j], ssd_dt_bias[j],
                             ssd_a_log[j], ssd_d[j], ssd_norm_w[j])
            y_b = _swa_attention(proj, swa_sinks[j], swa_bias)
            x = _matmul_resid([y_a, y_b], ab_w_out[j].astype(BF16), x, g_m, norm_w[i, 1])
        else:
            w = cd_w_in[j]
            ba0 = 4 * GDN_INNER
            w_in = jnp.concatenate([w[:, :ba0], w[:, ba0 + 2 * GDN_HEADS:], w[:, ba0:ba0 + 2 * GDN_HEADS]], axis=1)
            w_in = _pad_cols(w_in, CD_COLS_PAD).astype(BF16)
            proj = _norm_mod_matmul(x, norm_w[i, 0], sc_m, sh_m, w_in, tn=2048)
            y_c = _gdn_mixer(proj, _gate_rows(proj, CD_BA), gdn_conv_w[j], gdn_dt_bias[j], gdn_a_log[j],
                             gdn_norm_w[j])
            y_d = _moba_attention(proj, moba_bias)
            x = _matmul_resid([y_c, y_d], cd_w_out[j].astype(BF16), x, g_m, norm_w[i, 1])
        act = _ffn_up_geglu(x, norm_w[i, 2], sc_f, sh_f, ffn_w_up[i].astype(BF16), ffn_conv_w[i], ffn_conv_b[i])
        x = _matmul_resid([act], ffn_w_down[i].astype(BF16), x, g_f, norm_w[i, 3])
    return x
```

```python
import functools
import math

import numpy as np
import jax
import jax.numpy as jnp
from jax import lax
from jax.experimental import pallas as pl
from jax.experimental.pallas import tpu as pltpu

F32 = jnp.float32
BF16 = jnp.bfloat16
HIGHEST = lax.Precision.HIGHEST

D_MODEL = 1024
RMS_EPS = 1e-6
NEG = -1e30
LANES = 128
SUBLANES = 8
N_ATTN_HEADS = 8
ATTN_HEAD_DIM = 64
ATTN_WIDTH = N_ATTN_HEADS * ATTN_HEAD_DIM
REL_BUCKETS = 32
REL_MAX_DIST = 1024
SSD_HEADS = 24
SSD_HEAD_DIM = 64
SSD_INNER = SSD_HEADS * SSD_HEAD_DIM
SSD_GROUPS = 4
SSD_STATE = 128
SSD_CONV = 4
SSD_CHUNK = 128
SSD_XBC = SSD_INNER + 2 * SSD_GROUPS * SSD_STATE
SWA_KV_HEADS = 2
SWA_BLOCK = 128
GDN_HEADS = 12
GDN_HEAD_DIM = 128
GDN_INNER = GDN_HEADS * GDN_HEAD_DIM
GDN_CONV = 4
GDN_CHUNK = 64
MOBA_BLOCK = 256
MOBA_TOPK = 3
FFN_DIM = 2816
FFN_CONV = 3

AB_Z, AB_XBC, AB_Q, AB_K, AB_V, AB_DT, AB_COLS_PAD = 0, 1536, 4096, 4608, 4736, 4864, 5120
CD_QKV, CD_Z, CD_QD, CD_KD, CD_VD, CD_BA, CD_COLS_PAD = 0, 4608, 6144, 6656, 7168, 7680, 8192

VMEM_LIMIT = 48 * 1024 * 1024
HALO = 16


def _cparams(sem):
    return pltpu.CompilerParams(dimension_semantics=sem, vmem_limit_bytes=VMEM_LIMIT)


def _bdot(a, b):
    return jnp.dot(a.astype(BF16), b.astype(BF16), preferred_element_type=F32)


def _bdot_nt(a, b):
    return lax.dot_general(a.astype(BF16), b.astype(BF16), (((1,), (1,)), ((), ())),
                           preferred_element_type=F32)


def _split3(x):
    hi = x.astype(BF16)
    r1 = x - hi.astype(F32)
    mid = r1.astype(BF16)
    lo = (r1 - mid.astype(F32)).astype(BF16)
    return hi, mid, lo


def _sum01_left(m01, x):
    n = x.shape[1]
    y = jnp.dot(m01, jnp.concatenate(_split3(x), axis=1), preferred_element_type=F32)
    return y[:, :n] + y[:, n:2 * n] + y[:, 2 * n:]


def _sum01_right(x, m01):
    n = x.shape[0]
    y = jnp.dot(jnp.concatenate(_split3(x), axis=0), m01, preferred_element_type=F32)
    return y[:n] + y[n:2 * n] + y[2 * n:]


def _softplus(x):
    return jnp.maximum(x, 0.0) + jnp.log(1.0 + jnp.exp(-jnp.abs(x)))


def _sigmoid(x):
    return 1.0 / (1.0 + jnp.exp(-x))


def _silu(x):
    return x * _sigmoid(x)


def _window_conv(win_ref, lo, ncols, rows, w, width, row0=0, shifted_loads=False):
    if shifted_loads:
        acc = None
        for s in range(width):
            tap = win_ref[pl.ds(HALO + row0 - s, rows), lo:lo + ncols] * w[width - 1 - s:width - s, :]
            acc = tap if acc is None else acc + tap
        return acc
    xin = win_ref[pl.ds(HALO + row0 - SUBLANES, rows + SUBLANES), lo:lo + ncols]
    acc = xin[SUBLANES:, :] * w[width - 1:width, :]
    for s in range(1, width):
        acc = acc + pltpu.roll(xin, s, axis=0)[SUBLANES:, :] * w[width - 1 - s:width - s, :]
    return acc


def _mods_kernel(c_ref, w_ref, b_ref, o_ref):
    o_ref[0] = _bdot(_silu(c_ref[...]), w_ref[0]) + b_ref[0]


def _mods(c, ada_w, ada_b):
    depth, d, n = ada_w.shape
    bsz = c.shape[0]
    tn = 512
    return pl.pallas_call(
        _mods_kernel,
        grid=(depth, n // tn),
        in_specs=[pl.BlockSpec((bsz, d), lambda l, j: (0, 0)),
                  pl.BlockSpec((1, d, tn), lambda l, j: (l, 0, j)),
                  pl.BlockSpec((1, 1, tn), lambda l, j: (l, 0, j))],
        out_specs=pl.BlockSpec((1, bsz, tn), lambda l, j: (l, 0, j)),
        out_shape=jax.ShapeDtypeStruct((depth, bsz, n), F32),
        compiler_params=_cparams(("arbitrary", "arbitrary")),
        name="adaln_mods",
    )(c, ada_w, ada_b.reshape(depth, 1, n))


def _modulated_norm(x, nw, sc, sh):
    ms = jnp.mean(x * x, axis=-1, keepdims=True)
    return x * lax.rsqrt(ms + RMS_EPS) * nw * (1.0 + sc) + sh


NMM_ROWS = 256


def _nmm_kernel(x_ref, nw_ref, sc_ref, sh_ref, w_ref, o_ref, h_ref):
    tm = x_ref.shape[1]

    @pl.when(pl.program_id(1) == 0)
    def _():
        for r0 in range(0, tm, NMM_ROWS):
            h = _modulated_norm(x_ref[0, r0:r0 + NMM_ROWS, :], nw_ref[...], sc_ref[0], sh_ref[0]).astype(BF16)
            h_ref[r0:r0 + NMM_ROWS, :] = h
            o_ref[0, r0:r0 + NMM_ROWS, :] = jnp.dot(h, w_ref[...], preferred_element_type=F32).astype(o_ref.dtype)

    @pl.when(pl.program_id(1) > 0)
    def _():
        o_ref[0] = jnp.dot(h_ref[...], w_ref[...], preferred_element_type=F32).astype(o_ref.dtype)


def _norm_mod_matmul(x, nw, sc, sh, w, tn=1024, tm=1024):
    bsz, s, d = x.shape
    n = w.shape[1]
    spt = s // tm
    return pl.pallas_call(
        _nmm_kernel,
        grid=(bsz * spt, n // tn),
        in_specs=[pl.BlockSpec((1, tm, d), lambda i, j: (i // spt, i % spt, 0)),
                  pl.BlockSpec((1, d), lambda i, j: (0, 0)),
                  pl.BlockSpec((1, 1, d), lambda i, j: (i // spt, 0, 0)),
                  pl.BlockSpec((1, 1, d), lambda i, j: (i // spt, 0, 0)),
                  pl.BlockSpec((d, tn), lambda i, j: (0, j))],
        out_specs=pl.BlockSpec((1, tm, tn), lambda i, j: (i // spt, i % spt, j)),
        out_shape=jax.ShapeDtypeStruct((bsz, s, n), BF16),
        scratch_shapes=[pltpu.VMEM((tm, d), BF16)],
        compiler_params=_cparams(("arbitrary", "arbitrary")),
        name="norm_mod_matmul",
    )(x, nw.reshape(1, d), sc, sh, w)


def _mmres_kernel(*refs, splits):
    na = len(splits)
    a_refs = refs[:na]
    w_ref, x_ref, g_ref, nw_ref, o_ref = refs[na:]
    acc = None
    lo = 0
    for a_ref, k in zip(a_refs, splits):
        part = jnp.dot(a_ref[0].astype(BF16), w_ref[lo:lo + k, :], preferred_element_type=F32)
        acc = part if acc is None else acc + part
        lo += k
    ms = jnp.mean(acc * acc, axis=-1, keepdims=True)
    y = acc * lax.rsqrt(ms + RMS_EPS) * nw_ref[...]
    o_ref[0] = x_ref[0] + g_ref[0] * y


def _matmul_resid(a_list, w, x, gate, nw, tm=512):
    bsz, s, d = x.shape
    spt = s // tm
    splits = tuple(a.shape[-1] for a in a_list)
    ktot = sum(splits)
    in_specs = [pl.BlockSpec((1, tm, k), lambda i: (i // spt, i % spt, 0)) for k in splits]
    in_specs += [pl.BlockSpec((ktot, d), lambda i: (0, 0)),
                 pl.BlockSpec((1, tm, d), lambda i: (i // spt, i % spt, 0)),
                 pl.BlockSpec((1, 1, d), lambda i: (i // spt, 0, 0)),
                 pl.BlockSpec((1, d), lambda i: (0, 0))]
    return pl.pallas_call(
        functools.partial(_mmres_kernel, splits=splits),
        grid=(bsz * spt,),
        in_specs=in_specs,
        out_specs=pl.BlockSpec((1, tm, d), lambda i: (i // spt, i % spt, 0)),
        out_shape=jax.ShapeDtypeStruct((bsz, s, d), F32),
        compiler_params=_cparams(("arbitrary",)),
        name="matmul_resid",
    )(*a_list, w, x, gate, nw.reshape(1, d))


def _ffn_up_kernel(xh_ref, x_ref, nw_ref, sc_ref, sh_ref, w_ref, cw_ref, cb_ref, o_ref, h_ref, u_ref,
                   *, f, tc, width, rb):
    tm = x_ref.shape[1]
    h_halo = _modulated_norm(xh_ref[0], nw_ref[...], sc_ref[0], sh_ref[0])
    h_ref[:HALO, :] = jnp.where(pl.program_id(1) > 0, h_halo, 0.0).astype(BF16)
    h_ref[HALO:, :] = _modulated_norm(x_ref[0], nw_ref[...], sc_ref[0], sh_ref[0]).astype(BF16)
    c0 = math.sqrt(2.0 / math.pi)
    nchunk = f // tc

    def matmuls(c):
        for k, off in enumerate((0, f)):
            lo = off + c * tc
            u_ref[c % 2, k] = jnp.dot(h_ref[...], w_ref[:, lo:lo + tc], preferred_element_type=F32)

    def epilogue(c):
        for r0 in range(0, tm, rb):
            gv = []
            for k, off in enumerate((0, f)):
                lo = off + c * tc
                gv.append(_window_conv(u_ref.at[c % 2, k], 0, tc, rb, cw_ref[:, lo:lo + tc], width, row0=r0,
                                       shifted_loads=True) + cb_ref[:, lo:lo + tc])
            g, v = gv
            th = jnp.tanh(g * (c0 + (c0 * 0.044715) * (g * g)))
            hg = 0.5 * g
            o_ref[0, r0:r0 + rb, c * tc:(c + 1) * tc] = ((hg + hg * th) * v).astype(o_ref.dtype)

    matmuls(0)
    for c in range(nchunk):
        if c + 1 < nchunk:
            matmuls(c + 1)
        epilogue(c)


def _ffn_up_geglu(x, nw, sc, sh, w, cw, cb, tm=512, tc=256, rb=128):
    bsz, s, d = x.shape
    n2 = w.shape[1]
    f = n2 // 2
    width = cw.shape[0]
    hb = tm // HALO
    return pl.pallas_call(
        functools.partial(_ffn_up_kernel, f=f, tc=tc, width=width, rb=rb),
        grid=(bsz, s // tm),
        in_specs=[pl.BlockSpec((1, HALO, d), lambda b_, r: (b_, jnp.maximum(r * hb - 1, 0), 0)),
                  pl.BlockSpec((1, tm, d), lambda b_, r: (b_, r, 0)),
                  pl.BlockSpec((1, d), lambda b_, r: (0, 0)),
                  pl.BlockSpec((1, 1, d), lambda b_, r: (b_, 0, 0)),
                  pl.BlockSpec((1, 1, d), lambda b_, r: (b_, 0, 0)),
                  pl.BlockSpec((d, n2), lambda b_, r: (0, 0)),
                  pl.BlockSpec((width, n2), lambda b_, r: (0, 0)),
                  pl.BlockSpec((1, n2), lambda b_, r: (0, 0))],
        out_specs=pl.BlockSpec((1, tm, f), lambda b_, r: (b_, r, 0)),
        out_shape=jax.ShapeDtypeStruct((bsz, s, f), BF16),
        scratch_shapes=[pltpu.VMEM((HALO + tm, d), BF16),
                        pltpu.VMEM((2, 2, HALO + tm, tc), F32)],
        compiler_params=_cparams(("arbitrary", "arbitrary")),
        name="ffn_up_conv_geglu",
    )(x, x, nw.reshape(1, d), sc, sh, w, cw, cb.reshape(1, n2))


def _rel_bucket_np(d):
    max_exact = REL_BUCKETS // 2
    d = np.maximum(d, 0)
    df = np.maximum(d, 1).astype(np.float64)
    large = max_exact + (np.log(df / max_exact) / math.log(REL_MAX_DIST / max_exact)
                         * (REL_BUCKETS - max_exact)).astype(np.int32)
    large = np.minimum(large, REL_BUCKETS - 1)
    return np.where(d < max_exact, d, large).astype(np.int32)


LOG2E = math.log2(math.e)


def _bias_kernel(tab_ref, idx_ref, o_ref, *, ranges):
    h = pl.program_id(0)
    for t, (lo, hi) in enumerate(ranges):
        idx = idx_ref[t]
        acc = jnp.full(idx.shape, NEG, F32)
        for bkt in range(lo, hi + 1):
            acc = jnp.where(idx == bkt, tab_ref[bkt, h] * LOG2E, acc)
        o_ref[0, t] = acc


def _bias_tiles(rel_bias, idx_np):
    t, r, c = idx_np.shape
    ranges = tuple((int(tile[tile >= 0].min()), int(tile.max())) for tile in idx_np)
    return pl.pallas_call(
        functools.partial(_bias_kernel, ranges=ranges),
        grid=(N_ATTN_HEADS,),
        in_specs=[pl.BlockSpec(memory_space=pltpu.SMEM),
                  pl.BlockSpec((t, r, c), lambda h: (0, 0, 0))],
        out_specs=pl.BlockSpec((1, t, r, c), lambda h: (h, 0, 0, 0)),
        out_shape=jax.ShapeDtypeStruct((N_ATTN_HEADS, t, r, c), F32),
        compiler_params=_cparams(("arbitrary",)),
        name="rel_bias_tiles",
    )(rel_bias, jnp.asarray(idx_np))


def _swa_bucket_idx():
    c = np.arange(2 * SWA_BLOCK)[:, None]
    r = np.arange(SWA_BLOCK)[None, :]
    dist = SWA_BLOCK + r - c
    return np.where((dist >= 0) & (dist < SWA_BLOCK), _rel_bucket_np(dist), -1).astype(np.int32)[None]


def _moba_bucket_idx(nblk):
    c = np.arange(MOBA_BLOCK)[:, None]
    r = np.arange(MOBA_BLOCK)[None, :]
    tiles = [np.where(m * MOBA_BLOCK + r - c >= 0, _rel_bucket_np(m * MOBA_BLOCK + r - c), -1)
             for m in range(nblk)]
    return np.stack(tiles).astype(np.int32)


def _expand_heads(v, e):
    hi = v.astype(BF16)
    lo = (v - hi.astype(F32)).astype(BF16)
    return (jnp.dot(hi, e, preferred_element_type=F32) + jnp.dot(lo, e, preferred_element_type=F32))


def _ssd_kernel(zx_ref, dtc_ref, dtr_ref, cw_ref, cb_ref, bias_c_ref, bias_r_ref, alog_c_ref, alog_r_ref,
                dskip_ref, nw_ref, e_ref, o_ref, state_ref, win_ref):
    q = SSD_CHUNK
    gw = SSD_INNER // SSD_GROUPS
    hpg = SSD_HEADS // SSD_GROUPS

    @pl.when(pl.program_id(1) == 0)
    def _():
        state_ref[...] = jnp.zeros(state_ref.shape, F32)
        win_ref[:HALO, :] = jnp.zeros((HALO, SSD_XBC), F32)

    row = lax.broadcasted_iota(jnp.int32, (q, q), 0)
    col = lax.broadcasted_iota(jnp.int32, (q, q), 1)
    tril = row >= col
    tri_b = jnp.where(tril, 1.0, 0.0).astype(BF16)
    triu_b = jnp.where(row <= col, 1.0, 0.0).astype(BF16)

    dt_c = _softplus(dtc_ref[0].astype(F32) + bias_c_ref[...])
    da_c = dt_c * (-jnp.exp(alog_c_ref[...]))
    acs_c = _sum01_left(tri_b, da_c)
    dt_r = _softplus(dtr_ref[0] + bias_r_ref[...])
    da_r = dt_r * (-jnp.exp(alog_r_ref[...]))
    acs_r = _sum01_right(da_r, triu_b)

    acs_last = acs_c[q - 1:q, :]
    e = e_ref[...]
    dt_full = _expand_heads(dt_c, e)
    dtdec_full = _expand_heads(dt_c * jnp.exp(acs_last - acs_c), e)
    eacs_full = _expand_heads(jnp.exp(acs_c), e)
    cdecay_full = eacs_full[q - 1:q, :]

    raw = zx_ref[0, :, SSD_INNER:].astype(F32)
    win_ref[HALO:, :] = raw
    xbc = _silu(_window_conv(win_ref, 0, SSD_XBC, q, cw_ref[...], SSD_CONV) + cb_ref[...])
    win_ref[:HALO, :] = raw[q - HALO:, :]

    xs = xbc[:, :SSD_INNER]
    xdt = xs * dt_full
    xdec = xs * dtdec_full
    lane_half = lax.broadcasted_iota(jnp.int32, (1, LANES), 1) >> 6

    y_parts = []
    for g in range(SSD_GROUPS):
        b_g = xbc[:, SSD_INNER + g * SSD_STATE:SSD_INNER + (g + 1) * SSD_STATE]
        c_g = xbc[:, SSD_INNER + SSD_GROUPS * SSD_STATE + g * SSD_STATE:
                  SSD_INNER + SSD_GROUPS * SSD_STATE + (g + 1) * SSD_STATE]
        cb = jnp.where(tril, _bdot_nt(c_g, b_g), 0.0)
        st = state_ref[g]
        y_off = _bdot(c_g, st) * eacs_full[:, g * gw:(g + 1) * gw]
        state_ref[g] = st * cdecay_full[:, g * gw:(g + 1) * gw] + _bdot(b_g.T, xdec[:, g * gw:(g + 1) * gw])
        pair_parts = []
        for pr in range(hpg // 2):
            acc = None
            lo = g * gw + pr * LANES
            x_pair = xdt[:, lo:lo + LANES]
            for half in range(2):
                h = g * hpg + pr * 2 + half
                diff = acs_c[:, h:h + 1] - acs_r[h:h + 1, :]
                lmat = jnp.exp(jnp.minimum(diff, 0.0))
                part = _bdot(cb * lmat, jnp.where(lane_half == half, x_pair, 0.0))
                acc = part if acc is None else acc + part
            pair_parts.append(acc)
        y_diag = jnp.concatenate(pair_parts, axis=1)
        y = y_diag + y_off + dskip_ref[:, g * gw:(g + 1) * gw] * xs[:, g * gw:(g + 1) * gw]
        y = y * _silu(zx_ref[0, :, g * gw:(g + 1) * gw].astype(F32))
        ms = jnp.mean(y * y, axis=-1, keepdims=True)
        y_parts.append(y * lax.rsqrt(ms + RMS_EPS) * nw_ref[:, g * gw:(g + 1) * gw])
    o_ref[0] = jnp.concatenate(y_parts, axis=1).astype(o_ref.dtype)


def _ssd_mixer(proj, dt_rows, conv_w, conv_b, dt_bias, a_log, d_skip, norm_w):
    bsz, s, _ = proj.shape
    q = SSD_CHUNK
    zx = SSD_INNER + SSD_XBC
    pad = LANES - SSD_HEADS
    bias_c = jnp.pad(dt_bias, (0, pad)).reshape(1, LANES)
    alog_c = jnp.pad(a_log, (0, pad)).reshape(1, LANES)
    e_np = np.zeros((LANES, SSD_INNER), np.float32)
    for h in range(SSD_HEADS):
        e_np[h, h * SSD_HEAD_DIM:(h + 1) * SSD_HEAD_DIM] = 1.0
    small = lambda shape: pl.BlockSpec(shape, lambda b_, c: (0, 0))
    return pl.pallas_call(
        _ssd_kernel,
        grid=(bsz, s // q),
        in_specs=[pl.BlockSpec((1, q, zx), lambda b_, c: (b_, c, 0)),
                  pl.BlockSpec((1, q, LANES), lambda b_, c: (b_, c, AB_DT // LANES)),
                  pl.BlockSpec((1, LANES, q), lambda b_, c: (b_, 0, c)),
                  small((SSD_CONV, SSD_XBC)), small((1, SSD_XBC)),
                  small((1, LANES)), small((LANES, 1)), small((1, LANES)), small((LANES, 1)),
                  small((1, SSD_INNER)), small((1, SSD_INNER)), small((LANES, SSD_INNER))],
        out_specs=pl.BlockSpec((1, q, SSD_INNER), lambda b_, c: (b_, c, 0)),
        out_shape=jax.ShapeDtypeStruct((bsz, s, SSD_INNER), BF16),
        scratch_shapes=[pltpu.VMEM((SSD_GROUPS, SSD_STATE, SSD_INNER // SSD_GROUPS), F32),
                        pltpu.VMEM((HALO + q, SSD_XBC), F32)],
        compiler_params=_cparams(("arbitrary", "arbitrary")),
        name="ssd_mixer",
    )(proj, proj, dt_rows, conv_w, conv_b.reshape(1, SSD_XBC), bias_c, bias_c.reshape(LANES, 1),
      alog_c, alog_c.reshape(LANES, 1), jnp.repeat(d_skip, SSD_HEAD_DIM).reshape(1, SSD_INNER),
      norm_w.reshape(1, SSD_INNER), jnp.asarray(e_np, BF16))


def _swa_kernel(sink_ref, q_ref, kp_ref, kc_ref, vp_ref, vc_ref, bias_ref, o_ref, s_ref):
    blk = SWA_BLOCK
    n = pl.program_id(1)
    grp = N_ATTN_HEADS // SWA_KV_HEADS
    scale = ATTN_HEAD_DIM ** -0.5
    kk = jnp.concatenate([kp_ref[0], kc_ref[0]], axis=0).astype(F32) * (scale * LOG2E)
    vv_t = jnp.concatenate([vp_ref[0], vc_ref[0]], axis=0).astype(F32).T
    lane_half = lax.broadcasted_iota(jnp.int32, (1, LANES), 1) >> 6
    c = lax.broadcasted_iota(jnp.int32, (2 * blk, blk), 0)
    valid = (c >= blk) | (n > 0)
    ms = []
    for kv in range(SWA_KV_HEADS):
        k_own = jnp.where(lane_half == kv, kk, 0.0).astype(BF16)
        k_var = [None, None]
        k_var[kv] = k_own
        k_var[1 - kv] = pltpu.roll(jnp.where(lane_half == kv, kk, 0.0), ATTN_HEAD_DIM, axis=1).astype(BF16)
        for gq in range(grp):
            h = kv * grp + gq
            q_tile = q_ref[0, :, (h // 2) * LANES:(h // 2 + 1) * LANES]
            s_t = _bdot_nt(k_var[h % 2], q_tile) + bias_ref[h, 0]
            s_t = jnp.where(valid, s_t, NEG)
            s_ref[h] = s_t
            ms.append(jnp.maximum(jnp.max(s_t, axis=0, keepdims=True), sink_ref[h] * LOG2E))
    outs = []
    for h in range(N_ATTN_HEADS):
        kv = h // grp
        p = jnp.exp2(s_ref[h] - ms[h])
        l = jnp.sum(p, axis=0, keepdims=True) + jnp.exp2(sink_ref[h] * LOG2E - ms[h])
        outs.append(_bdot(vv_t[kv * ATTN_HEAD_DIM:(kv + 1) * ATTN_HEAD_DIM, :], p) / l)
    for t in range(N_ATTN_HEADS // 2):
        pair = jnp.concatenate([outs[2 * t], outs[2 * t + 1]], axis=0)
        o_ref[0, :, t * LANES:(t + 1) * LANES] = pair.T.astype(o_ref.dtype)


def _swa_attention(proj, sinks, bias):
    bsz, s, _ = proj.shape
    blk = SWA_BLOCK
    kvw = SWA_KV_HEADS * ATTN_HEAD_DIM
    prev = lambda col: pl.BlockSpec((1, blk, kvw), lambda b_, n: (b_, jnp.maximum(n - 1, 0), col))
    cur = lambda col: pl.BlockSpec((1, blk, kvw), lambda b_, n: (b_, n, col))
    return pl.pallas_call(
        _swa_kernel,
        grid=(bsz, s // blk),
        in_specs=[pl.BlockSpec(memory_space=pltpu.SMEM),
                  pl.BlockSpec((1, blk, ATTN_WIDTH), lambda b_, n: (b_, n, AB_Q // ATTN_WIDTH)),
                  prev(AB_K // kvw), cur(AB_K // kvw), prev(AB_V // kvw), cur(AB_V // kvw),
                  pl.BlockSpec((N_ATTN_HEADS, 1, 2 * blk, blk), lambda b_, n: (0, 0, 0, 0))],
        out_specs=pl.BlockSpec((1, blk, ATTN_WIDTH), lambda b_, n: (b_, n, 0)),
        out_shape=jax.ShapeDtypeStruct((bsz, s, ATTN_WIDTH), BF16),
        scratch_shapes=[pltpu.VMEM((N_ATTN_HEADS, 2 * blk, blk), F32)],
        compiler_params=_cparams(("arbitrary", "arbitrary")),
        name="swa_attention",
    )(sinks, proj, proj, proj, proj, proj, bias)


def _moba_kernel(q_ref, k_ref, v_ref, bias_ref, o_ref, vt_ref, kmean_ref, s_ref, *, nblk):
    mb = MOBA_BLOCK
    dh = ATTN_HEAD_DIM
    own = pl.program_id(2)
    scale = dh ** -0.5

    @pl.when(own == 0)
    def _():
        means = []
        for j in range(nblk):
            vt_ref[j] = v_ref[0, j * mb:(j + 1) * mb, :].astype(F32).T.astype(BF16)
            means.append(jnp.mean(k_ref[0, j * mb:(j + 1) * mb, :].astype(F32), axis=0, keepdims=True))
        kmean_ref[...] = jnp.concatenate(means, axis=0)

    lane_half = lax.broadcasted_iota(jnp.int32, (1, LANES), 1) >> 6
    blk_id = lax.broadcasted_iota(jnp.int32, (nblk, mb), 0)
    q = q_ref[0].astype(F32) * (scale * LOG2E)

    qms = []
    negrows = []
    for hh in range(2):
        qm = jnp.where(lane_half == hh, q, 0.0)
        qms.append(qm.astype(BF16))
        gate = lax.dot_general(kmean_ref[...], qm, (((1,), (1,)), ((), ())),
                               preferred_element_type=F32, precision=HIGHEST)
        gate = jnp.where(blk_id < own, gate, NEG)
        rank = jnp.zeros((nblk, mb), jnp.int32)
        for i in range(nblk):
            gi = gate[i:i + 1, :]
            ahead = (gi > gate) | ((gi == gate) & (i < blk_id))
            rank = rank + jnp.where(ahead, 1, 0)
        keep = ((rank < MOBA_TOPK) & (blk_id < own)) | (blk_id == own)
        negrows.append(jnp.where(keep, 0.0, NEG).astype(F32))

    def attend(nb):
        m = [None, None]
        for j in range(nb):
            for hh in range(2):
                s_t = _bdot_nt(k_ref[0, j * mb:(j + 1) * mb, :], qms[hh]) + bias_ref[hh, jnp.maximum(own - j, 0)]
                s_t = s_t + negrows[hh][j:j + 1, :]
                s_ref[hh, j] = s_t
                m_j = jnp.max(s_t, axis=0, keepdims=True)
                m[hh] = m_j if m[hh] is None else jnp.maximum(m[hh], m_j)
        l = [None, None]
        acc = [None, None]
        for j in range(nb):
            for hh in range(2):
                p = jnp.exp2(s_ref[hh, j] - m[hh])
                l_j = jnp.sum(p, axis=0, keepdims=True)
                a_j = _bdot(vt_ref[j, hh * dh:(hh + 1) * dh, :], p)
                l[hh] = l_j if l[hh] is None else l[hh] + l_j
                acc[hh] = a_j if acc[hh] is None else acc[hh] + a_j
        out_t = jnp.concatenate([acc[0] / l[0], acc[1] / l[1]], axis=0)
        o_ref[0] = out_t.T.astype(o_ref.dtype)

    for nb in range(2, nblk + 1, 2):
        pl.when((own >= nb - 2) & (own < nb))(functools.partial(attend, nb))


def _moba_attention(proj, bias):
    bsz, s, _ = proj.shape
    mb = MOBA_BLOCK
    nblk = s // mb
    npair = N_ATTN_HEADS // 2
    return pl.pallas_call(
        functools.partial(_moba_kernel, nblk=nblk),
        grid=(bsz, npair, nblk),
        in_specs=[pl.BlockSpec((1, mb, LANES), lambda b_, p, i: (b_, i, CD_QD // LANES + p)),
                  pl.BlockSpec((1, s, LANES), lambda b_, p, i: (b_, 0, CD_KD // LANES + p)),
                  pl.BlockSpec((1, s, LANES), lambda b_, p, i: (b_, 0, CD_VD // LANES + p)),
                  pl.BlockSpec((2, nblk, mb, mb), lambda b_, p, i: (p, 0, 0, 0))],
        out_specs=pl.BlockSpec((1, mb, LANES), lambda b_, p, i: (b_, i, p)),
        out_shape=jax.ShapeDtypeStruct((bsz, s, ATTN_WIDTH), BF16),
        scratch_shapes=[pltpu.VMEM((nblk, LANES, mb), BF16),
                        pltpu.VMEM((nblk, LANES), F32),
                        pltpu.VMEM((2, nblk, mb, mb), F32)],
        compiler_params=_cparams(("arbitrary", "arbitrary", "arbitrary")),
        name="moba_attention",
    )(proj, proj, proj, bias)


def _gdn_kernel(qkv_ref, z_ref, bac_ref, bar_ref, cw_ref, bias_c_ref, bias_r_ref, alog_c_ref, alog_r_ref,
                nw_ref, o_ref, state_ref, win_ref, m_ref, attn_ref, t_ref, x_ref, rhs_ref, u_ref, wq_ref,
                kdt_ref, r_ref, glc_ref):
    t = 2 * GDN_CHUNK
    ck = GDN_CHUNK
    dk = GDN_HEAD_DIM
    nh = GDN_HEADS

    @pl.when(pl.program_id(1) == 0)
    def _():
        state_ref[...] = jnp.zeros(state_ref.shape, F32)
        win_ref[:HALO, :] = jnp.zeros((HALO, 3 * GDN_INNER), F32)

    row = lax.broadcasted_iota(jnp.int32, (t, t), 0)
    col = lax.broadcasted_iota(jnp.int32, (t, t), 1)
    same = (row >> 6) == (col >> 6)
    tril = same & (row >= col)
    strict = same & (row > col)
    tri_b = jnp.where(tril, 1.0, 0.0).astype(BF16)
    triu_b = jnp.where(same & (row <= col), 1.0, 0.0).astype(BF16)
    blk_b = jnp.where(same, 1.0, 0.0).astype(BF16)
    eye_f = jnp.where(row == col, 1.0, 0.0).astype(F32)
    merge_masks = [((row >> (l + 1)) == (col >> (l + 1))) & (((row >> l) & 1) == 1) & (((col >> l) & 1) == 0)
                   for l in range(int(math.log2(ck)))]
    mask_bf = [jnp.where(m, 1.0, 0.0).astype(BF16) for m in merge_masks[1:]]

    ba_c = bac_ref[0].astype(F32)
    g_c = -jnp.exp(alog_c_ref[...]) * _softplus(ba_c + bias_c_ref[...])
    sums_c = _sum01_left(jnp.concatenate([tri_b, blk_b], axis=0), g_c)
    gc_c = sums_c[:t]
    gl_c = sums_c[t:]
    g_r = -jnp.exp(alog_r_ref[...]) * _softplus(bar_ref[0] + bias_r_ref[...])
    gc_r = _sum01_right(g_r, triu_b)
    glc_ref[...] = gl_c

    def conv_silu(lo):
        raw = qkv_ref[0, :, lo:lo + dk].astype(F32)
        win_ref[HALO:, lo:lo + dk] = raw
        out = _silu(_window_conv(win_ref, lo, dk, t, cw_ref[:, lo:lo + dk], GDN_CONV))
        win_ref[:HALO, lo:lo + dk] = raw[t - HALO:, :]
        return out

    for h in range(nh):
        q = conv_silu(h * dk)
        k = conv_silu((nh + h) * dk)
        v = conv_silu((2 * nh + h) * dk)
        qn = q * lax.rsqrt(jnp.sum(q * q, axis=-1, keepdims=True) + 1e-6) * (dk ** -0.5)
        kn = k * lax.rsqrt(jnp.sum(k * k, axis=-1, keepdims=True) + 1e-6)
        beta = _sigmoid(ba_c[:, h:h + 1])
        gcc = gc_c[:, nh + h:nh + h + 1]
        gcr = gc_r[nh + h:nh + h + 1, :]
        glc = gl_c[:, nh + h:nh + h + 1]
        decay = jnp.where(tril, jnp.exp(jnp.where(tril, gcc - gcr, 0.0)), 0.0)
        kb = kn * beta
        kk = _bdot_nt(jnp.concatenate([kb, qn], axis=0), kn)
        mm = jnp.where(strict, kk[:t] * decay, 0.0)
        m_ref[h] = mm.astype(BF16)
        attn_ref[h] = (kk[t:] * decay).astype(BF16)
        t_ref[h] = eye_f - jnp.where(merge_masks[0], mm, 0.0)
        egc = jnp.exp(gcc)
        rhs_ref[h] = jnp.concatenate([v * beta, kb * egc], axis=1).astype(BF16)
        q_dec = (qn * egc).astype(BF16)
        for a in range(2):
            wq_ref[h, a, ck:, :] = q_dec[a * ck:(a + 1) * ck]
        kdt_ref[h] = (kn * jnp.exp(glc - gcc)).T.astype(BF16)

    for lvl in range(len(mask_bf)):
        for h in range(nh):
            x_ref[h] = jnp.dot(t_ref[h].astype(BF16), m_ref[h] * mask_bf[lvl],
                               preferred_element_type=F32).astype(BF16)
        for h in range(nh):
            t_h = t_ref[h]
            t_ref[h] = t_h - jnp.dot(x_ref[h], t_h.astype(BF16), preferred_element_type=F32)

    for h in range(nh):
        sol = jnp.dot(t_ref[h].astype(BF16), rhs_ref[h], preferred_element_type=F32)
        u_ref[h] = sol[:, :dk]
        for a in range(2):
            wq_ref[h, a, :ck, :] = sol[a * ck:(a + 1) * ck, dk:].astype(BF16)

    zeros_half = jnp.zeros((ck, dk), F32)
    for a in range(2):
        sl = slice(a * ck, (a + 1) * ck)
        for h in range(nh):
            r_ref[h] = jnp.dot(wq_ref[h, a], state_ref[h].astype(BF16), preferred_element_type=F32)
        for h in range(nh):
            v_new = u_ref[h, sl, :] - r_ref[h, :ck, :]
            v_full = jnp.concatenate([v_new, zeros_half] if a == 0 else [zeros_half, v_new], axis=0).astype(BF16)
            o = r_ref[h, ck:, :] + jnp.dot(attn_ref[h, sl, :], v_full, preferred_element_type=F32)
            gl = glc_ref[a * ck:a * ck + 1, nh + h:nh + h + 1]
            state_ref[h] = state_ref[h] * jnp.exp(gl) + jnp.dot(kdt_ref[h], v_full, preferred_element_type=F32)
            ms = jnp.mean(o * o, axis=-1, keepdims=True)
            y = o * lax.rsqrt(ms + RMS_EPS) * nw_ref[...] * _silu(z_ref[0, sl, h * dk:(h + 1) * dk].astype(F32))
            o_ref[0, sl, h * dk:(h + 1) * dk] = y.astype(o_ref.dtype)


def _gdn_mixer(proj, ba_rows, conv_w, dt_bias, a_log, norm_w):
    bsz, s, _ = proj.shape
    t = 2 * GDN_CHUNK
    nh = GDN_HEADS
    dk = GDN_HEAD_DIM
    bias_c = jnp.pad(dt_bias, (nh, LANES - 2 * nh)).reshape(1, LANES)
    alog_c = jnp.pad(a_log, (nh, LANES - 2 * nh)).reshape(1, LANES)
    small = lambda shape: pl.BlockSpec(shape, lambda b_, c: (0, 0))
    return pl.pallas_call(
        _gdn_kernel,
        grid=(bsz, s // t),
        in_specs=[pl.BlockSpec((1, t, 3 * GDN_INNER), lambda b_, c: (b_, c, CD_QKV // (3 * GDN_INNER))),
                  pl.BlockSpec((1, t, GDN_INNER), lambda b_, c: (b_, c, CD_Z // GDN_INNER)),
                  pl.BlockSpec((1, t, LANES), lambda b_, c: (b_, c, CD_BA // LANES)),
                  pl.BlockSpec((1, LANES, t), lambda b_, c: (b_, 0, c)),
                  small((GDN_CONV, 3 * GDN_INNER)),
                  small((1, LANES)), small((LANES, 1)), small((1, LANES)), small((LANES, 1)),
                  small((1, dk))],
        out_specs=pl.BlockSpec((1, t, GDN_INNER), lambda b_, c: (b_, c, 0)),
        out_shape=jax.ShapeDtypeStruct((bsz, s, GDN_INNER), BF16),
        scratch_shapes=[pltpu.VMEM((nh, dk, dk), F32),
                        pltpu.VMEM((HALO + t, 3 * GDN_INNER), F32),
                        pltpu.VMEM((nh, t, t), BF16),
                        pltpu.VMEM((nh, t, t), BF16),
                        pltpu.VMEM((nh, t, t), F32),
                        pltpu.VMEM((nh, t, t), BF16),
                        pltpu.VMEM((nh, t, 2 * dk), BF16),
                        pltpu.VMEM((nh, t, dk), F32),
                        pltpu.VMEM((nh, 2, t, dk), BF16),
                        pltpu.VMEM((nh, dk, t), BF16),
                        pltpu.VMEM((nh, t, dk), F32),
                        pltpu.VMEM((t, LANES), F32)],
        compiler_params=_cparams(("arbitrary", "arbitrary")),
        name="gdn_mixer",
    )(proj, proj, proj, ba_rows, conv_w, bias_c, bias_c.reshape(LANES, 1), alog_c, alog_c.reshape(LANES, 1),
      norm_w.reshape(1, dk))


def _pad_cols(w, n):
    return jnp.pad(w, ((0, 0), (0, n - w.shape[1])))


def _gate_rows(proj, col0):
    return jnp.swapaxes(proj[:, :, col0:col0 + LANES].astype(F32), 1, 2)


def kernel(x, c, rel_bias, norm_w, ada_w, ada_b, ab_w_in, ab_w_out, ssd_conv_w, ssd_conv_b,
           ssd_dt_bias, ssd_a_log, ssd_d, ssd_norm_w, swa_sinks, cd_w_in, cd_w_out, gdn_conv_w,
           gdn_dt_bias, gdn_a_log, gdn_norm_w, ffn_w_up, ffn_conv_w, ffn_conv_b, ffn_w_down):
    bsz, s, d = x.shape
    depth = norm_w.shape[0]
    mods = _mods(c, ada_w, ada_b)
    swa_bias = _bias_tiles(rel_bias, _swa_bucket_idx())
    moba_bias = _bias_tiles(rel_bias, _moba_bucket_idx(s // MOBA_BLOCK))

    for i in range(depth):
        sh_m, sc_m, g_m, sh_f, sc_f, g_f = [m.reshape(bsz, 1, d) for m in jnp.split(mods[i], 6, axis=-1)]
        j = i // 2
        if i % 2 == 0:
            w = ab_w_in[j]
            dt0 = SSD_INNER + SSD_XBC
            w_in = jnp.concatenate([w[:, :dt0], w[:, dt0 + SSD_HEADS:], w[:, dt0:dt0 + SSD_HEADS]], axis=1)
            w_in = _pad_cols(w_in, AB_COLS_PAD).astype(BF16)
            proj = _norm_mod_matmul(x, norm_w[i, 0], sc_m, sh_m, w_in, tn=AB_COLS_PAD // 2)
            y_a = _ssd_mixer(proj, _gate_rows(proj, AB_DT), ssd_conv_w[j], ssd_conv_b[j], ssd_dt_bias[j],
                             ssd_a_log[j], ssd_d[j], ssd_norm_w[j])
            y_b = _swa_attention(proj, swa_sinks[j], swa_bias)
            x = _matmul_resid([y_a, y_b], ab_w_out[j].astype(BF16), x, g_m, norm_w[i, 1])
        else:
            w = cd_w_in[j]
            ba0 = 4 * GDN_INNER
            w_in = jnp.concatenate([w[:, :ba0], w[:, ba0 + 2 * GDN_HEADS:], w[:, ba0:ba0 + 2 * GDN_HEADS]], axis=1)
            w_in = _pad_cols(w_in, CD_COLS_PAD).astype(BF16)
            proj = _norm_mod_matmul(x, norm_w[i, 0], sc_m, sh_m, w_in, tn=2048)
            y_c = _gdn_mixer(proj, _gate_rows(proj, CD_BA), gdn_conv_w[j], gdn_dt_bias[j], gdn_a_log[j],
                             gdn_norm_w[j])
            y_d = _moba_attention(proj, moba_bias)
            x = _matmul_resid([y_c, y_d], cd_w_out[j].astype(BF16), x, g_m, norm_w[i, 1])
        act = _ffn_up_geglu(x, norm_w[i, 2], sc_f, sh_f, ffn_w_up[i].astype(BF16), ffn_conv_w[i], ffn_conv_b[i])
        x = _matmul_resid([act], ffn_w_down[i].astype(BF16), x, g_f, norm_w[i, 3])
    return x
```

```python
import functools
import math

import numpy as np
import jax
import jax.numpy as jnp
from jax import lax
from jax.experimental import pallas as pl
from jax.experimental.pallas import tpu as pltpu

F32 = jnp.float32
BF16 = jnp.bfloat16
HIGHEST = lax.Precision.HIGHEST

D_MODEL = 1024
RMS_EPS = 1e-6
NEG = -1e30
LANES = 128
SUBLANES = 8
N_ATTN_HEADS = 8
ATTN_HEAD_DIM = 64
ATTN_WIDTH = N_ATTN_HEADS * ATTN_HEAD_DIM
REL_BUCKETS = 32
REL_MAX_DIST = 1024
SSD_HEADS = 24
SSD_HEAD_DIM = 64
SSD_INNER = SSD_HEADS * SSD_HEAD_DIM
SSD_GROUPS = 4
SSD_STATE = 128
SSD_CONV = 4
SSD_CHUNK = 128
SSD_XBC = SSD_INNER + 2 * SSD_GROUPS * SSD_STATE
SWA_KV_HEADS = 2
SWA_BLOCK = 128
GDN_HEADS = 12
GDN_HEAD_DIM = 128
GDN_INNER = GDN_HEADS * GDN_HEAD_DIM
GDN_CONV = 4
GDN_CHUNK = 64
MOBA_BLOCK = 256
MOBA_TOPK = 3
FFN_DIM = 2816
FFN_CONV = 3

AB_Z, AB_XBC, AB_Q, AB_K, AB_V, AB_DT, AB_COLS_PAD = 0, 1536, 4096, 4608, 4736, 4864, 5120
CD_QKV, CD_Z, CD_QD, CD_KD, CD_VD, CD_BA, CD_COLS_PAD = 0, 4608, 6144, 6656, 7168, 7680, 8192

VMEM_LIMIT = 48 * 1024 * 1024
HALO = 16


def _cparams(sem):
    return pltpu.CompilerParams(dimension_semantics=sem, vmem_limit_bytes=VMEM_LIMIT)


def _bdot(a, b):
    return jnp.dot(a.astype(BF16), b.astype(BF16), preferred_element_type=F32)


def _bdot_nt(a, b):
    return lax.dot_general(a.astype(BF16), b.astype(BF16), (((1,), (1,)), ((), ())),
                           preferred_element_type=F32)


def _split3(x):
    hi = x.astype(BF16)
    r1 = x - hi.astype(F32)
    mid = r1.astype(BF16)
    lo = (r1 - mid.astype(F32)).astype(BF16)
    return hi, mid, lo


def _sum01_left(m01, x):
    n = x.shape[1]
    y = jnp.dot(m01, jnp.concatenate(_split3(x), axis=1), preferred_element_type=F32)
    return y[:, :n] + y[:, n:2 * n] + y[:, 2 * n:]


def _sum01_right(x, m01):
    n = x.shape[0]
    y = jnp.dot(jnp.concatenate(_split3(x), axis=0), m01, preferred_element_type=F32)
    return y[:n] + y[n:2 * n] + y[2 * n:]


def _softplus(x):
    return jnp.maximum(x, 0.0) + jnp.log(1.0 + jnp.exp(-jnp.abs(x)))


def _sigmoid(x):
    return 1.0 / (1.0 + jnp.exp(-x))


def _silu(x):
    return x * _sigmoid(x)


def _window_conv(win_ref, lo, ncols, rows, w, width, row0=0, shifted_loads=False):
    if shifted_loads:
        acc = None
        for s in range(width):
            tap = win_ref[pl.ds(HALO + row0 - s, rows), lo:lo + ncols] * w[width - 1 - s:width - s, :]
            acc = tap if acc is None else acc + tap
        return acc
    xin = win_ref[pl.ds(HALO + row0 - SUBLANES, rows + SUBLANES), lo:lo + ncols]
    acc = xin[SUBLANES:, :] * w[width - 1:width, :]
    for s in range(1, width):
        acc = acc + pltpu.roll(xin, s, axis=0)[SUBLANES:, :] * w[width - 1 - s:width - s, :]
    return acc


def _mods_kernel(c_ref, w_ref, b_ref, o_ref):
    o_ref[0] = _bdot(_silu(c_ref[...]), w_ref[0]) + b_ref[0]


def _mods(c, ada_w, ada_b):
    depth, d, n = ada_w.shape
    bsz = c.shape[0]
    tn = 512
    return pl.pallas_call(
        _mods_kernel,
        grid=(depth, n // tn),
        in_specs=[pl.BlockSpec((bsz, d), lambda l, j: (0, 0)),
                  pl.BlockSpec((1, d, tn), lambda l, j: (l, 0, j)),
                  pl.BlockSpec((1, 1, tn), lambda l, j: (l, 0, j))],
        out_specs=pl.BlockSpec((1, bsz, tn), lambda l, j: (l, 0, j)),
        out_shape=jax.ShapeDtypeStruct((depth, bsz, n), F32),
        compiler_params=_cparams(("arbitrary", "arbitrary")),
        name="adaln_mods",
    )(c, ada_w, ada_b.reshape(depth, 1, n))


def _modulated_norm(x, nw, sc, sh):
    ms = jnp.mean(x * x, axis=-1, keepdims=True)
    return x * lax.rsqrt(ms + RMS_EPS) * nw * (1.0 + sc) + sh


NMM_ROWS = 256


def _nmm_kernel(x_ref, nw_ref, sc_ref, sh_ref, w_ref, o_ref, h_ref):
    tm = x_ref.shape[1]

    @pl.when(pl.program_id(1) == 0)
    def _():
        for r0 in range(0, tm, NMM_ROWS):
            h = _modulated_norm(x_ref[0, r0:r0 + NMM_ROWS, :], nw_ref[...], sc_ref[0], sh_ref[0]).astype(BF16)
            h_ref[r0:r0 + NMM_ROWS, :] = h
            o_ref[0, r0:r0 + NMM_ROWS, :] = jnp.dot(h, w_ref[...], preferred_element_type=F32).astype(o_ref.dtype)

    @pl.when(pl.program_id(1) > 0)
    def _():
        o_ref[0] = jnp.dot(h_ref[...], w_ref[...], preferred_element_type=F32).astype(o_ref.dtype)


def _norm_mod_matmul(x, nw, sc, sh, w, tn=1024, tm=1024):
    bsz, s, d = x.shape
    n = w.shape[1]
    spt = s // tm
    return pl.pallas_call(
        _nmm_kernel,
        grid=(bsz * spt, n // tn),
        in_specs=[pl.BlockSpec((1, tm, d), lambda i, j: (i // spt, i % spt, 0)),
                  pl.BlockSpec((1, d), lambda i, j: (0, 0)),
                  pl.BlockSpec((1, 1, d), lambda i, j: (i // spt, 0, 0)),
                  pl.BlockSpec((1, 1, d), lambda i, j: (i // spt, 0, 0)),
                  pl.BlockSpec((d, tn), lambda i, j: (0, j))],
        out_specs=pl.BlockSpec((1, tm, tn), lambda i, j: (i // spt, i % spt, j)),
        out_shape=jax.ShapeDtypeStruct((bsz, s, n), BF16),
        scratch_shapes=[pltpu.VMEM((tm, d), BF16)],
        compiler_params=_cparams(("arbitrary", "arbitrary")),
        name="norm_mod_matmul",
    )(x, nw.reshape(1, d), sc, sh, w)


def _mmres_kernel(*refs, splits):
    na = len(splits)
    a_refs = refs[:na]
    w_ref, x_ref, g_ref, nw_ref, o_ref = refs[na:]
    acc = None
    lo = 0
    for a_ref, k in zip(a_refs, splits):
        part = jnp.dot(a_ref[0].astype(BF16), w_ref[lo:lo + k, :].astype(BF16), preferred_element_type=F32)
        acc = part if acc is None else acc + part
        lo += k
    ms = jnp.mean(acc * acc, axis=-1, keepdims=True)
    y = acc * lax.rsqrt(ms + RMS_EPS) * nw_ref[...]
    o_ref[0] = x_ref[0] + g_ref[0] * y


def _matmul_resid(a_list, w_stack, layer, x, gate, nw, tm=512):
    bsz, s, d = x.shape
    spt = s // tm
    splits = tuple(a.shape[-1] for a in a_list)
    ktot = sum(splits)
    in_specs = [pl.BlockSpec((1, tm, k), lambda i: (i // spt, i % spt, 0)) for k in splits]
    in_specs += [pl.BlockSpec((None, ktot, d), lambda i: (layer, 0, 0)),
                 pl.BlockSpec((1, tm, d), lambda i: (i // spt, i % spt, 0)),
                 pl.BlockSpec((1, 1, d), lambda i: (i // spt, 0, 0)),
                 pl.BlockSpec((1, d), lambda i: (0, 0))]
    return pl.pallas_call(
        functools.partial(_mmres_kernel, splits=splits),
        grid=(bsz * spt,),
        in_specs=in_specs,
        out_specs=pl.BlockSpec((1, tm, d), lambda i: (i // spt, i % spt, 0)),
        out_shape=jax.ShapeDtypeStruct((bsz, s, d), F32),
        compiler_params=_cparams(("arbitrary",)),
        name="matmul_resid",
    )(*a_list, w_stack, x, gate, nw.reshape(1, d))


def _ffn_up_kernel(xh_ref, x_ref, nw_ref, sc_ref, sh_ref, w_ref, cw_ref, cb_ref, o_ref, h_ref, u_ref,
                   *, f, tc, width, rb):
    tm = x_ref.shape[1]
    h_halo = _modulated_norm(xh_ref[0], nw_ref[...], sc_ref[0], sh_ref[0])
    h_ref[:HALO, :] = jnp.where(pl.program_id(1) > 0, h_halo, 0.0).astype(BF16)
    h_ref[HALO:, :] = _modulated_norm(x_ref[0], nw_ref[...], sc_ref[0], sh_ref[0]).astype(BF16)
    c0 = math.sqrt(2.0 / math.pi)
    nchunk = f // tc

    def matmuls(c):
        for k, off in enumerate((0, f)):
            lo = off + c * tc
            u_ref[c % 2, k] = jnp.dot(h_ref[...], w_ref[:, lo:lo + tc], preferred_element_type=F32)

    def epilogue(c):
        for r0 in range(0, tm, rb):
            gv = []
            for k, off in enumerate((0, f)):
                lo = off + c * tc
                gv.append(_window_conv(u_ref.at[c % 2, k], 0, tc, rb, cw_ref[:, lo:lo + tc], width, row0=r0,
                                       shifted_loads=True) + cb_ref[:, lo:lo + tc])
            g, v = gv
            th = jnp.tanh(g * (c0 + (c0 * 0.044715) * (g * g)))
            hg = 0.5 * g
            o_ref[0, r0:r0 + rb, c * tc:(c + 1) * tc] = ((hg + hg * th) * v).astype(o_ref.dtype)

    matmuls(0)
    for c in range(nchunk):
        if c + 1 < nchunk:
            matmuls(c + 1)
        epilogue(c)


def _ffn_up_geglu(x, nw, sc, sh, w, cw, cb, tm=512, tc=256, rb=128):
    bsz, s, d = x.shape
    n2 = w.shape[1]
    f = n2 // 2
    width = cw.shape[0]
    hb = tm // HALO
    return pl.pallas_call(
        functools.partial(_ffn_up_kernel, f=f, tc=tc, width=width, rb=rb),
        grid=(bsz, s // tm),
        in_specs=[pl.BlockSpec((1, HALO, d), lambda b_, r: (b_, jnp.maximum(r * hb - 1, 0), 0)),
                  pl.BlockSpec((1, tm, d), lambda b_, r: (b_, r, 0)),
                  pl.BlockSpec((1, d), lambda b_, r: (0, 0)),
                  pl.BlockSpec((1, 1, d), lambda b_, r: (b_, 0, 0)),
                  pl.BlockSpec((1, 1, d), lambda b_, r: (b_, 0, 0)),
                  pl.BlockSpec((d, n2), lambda b_, r: (0, 0)),
                  pl.BlockSpec((width, n2), lambda b_, r: (0, 0)),
                  pl.BlockSpec((1, n2), lambda b_, r: (0, 0))],
        out_specs=pl.BlockSpec((1, tm, f), lambda b_, r: (b_, r, 0)),
        out_shape=jax.ShapeDtypeStruct((bsz, s, f), BF16),
        scratch_shapes=[pltpu.VMEM((HALO + tm, d), BF16),
                        pltpu.VMEM((2, 2, HALO + tm, tc), F32)],
        compiler_params=_cparams(("arbitrary", "arbitrary")),
        name="ffn_up_conv_geglu",
    )(x, x, nw.reshape(1, d), sc, sh, w, cw, cb.reshape(1, n2))


def _rel_bucket_np(d):
    max_exact = REL_BUCKETS // 2
    d = np.maximum(d, 0)
    df = np.maximum(d, 1).astype(np.float64)
    large = max_exact + (np.log(df / max_exact) / math.log(REL_MAX_DIST / max_exact)
                         * (REL_BUCKETS - max_exact)).astype(np.int32)
    large = np.minimum(large, REL_BUCKETS - 1)
    return np.where(d < max_exact, d, large).astype(np.int32)


LOG2E = math.log2(math.e)


def _bias_kernel(tab_ref, idx_ref, o_ref, *, ranges):
    h = pl.program_id(0)
    for t, (lo, hi) in enumerate(ranges):
        idx = idx_ref[t]
        acc = jnp.full(idx.shape, NEG, F32)
        for bkt in range(lo, hi + 1):
            acc = jnp.where(idx == bkt, tab_ref[bkt, h] * LOG2E, acc)
        o_ref[0, t] = acc


def _bias_tiles(rel_bias, idx_np):
    t, r, c = idx_np.shape
    ranges = tuple((int(tile[tile >= 0].min()), int(tile.max())) for tile in idx_np)
    return pl.pallas_call(
        functools.partial(_bias_kernel, ranges=ranges),
        grid=(N_ATTN_HEADS,),
        in_specs=[pl.BlockSpec(memory_space=pltpu.SMEM),
                  pl.BlockSpec((t, r, c), lambda h: (0, 0, 0))],
        out_specs=pl.BlockSpec((1, t, r, c), lambda h: (h, 0, 0, 0)),
        out_shape=jax.ShapeDtypeStruct((N_ATTN_HEADS, t, r, c), F32),
        compiler_params=_cparams(("arbitrary",)),
        name="rel_bias_tiles",
    )(rel_bias, jnp.asarray(idx_np))


def _swa_bucket_idx():
    c = np.arange(2 * SWA_BLOCK)[:, None]
    r = np.arange(SWA_BLOCK)[None, :]
    dist = SWA_BLOCK + r - c
    return np.where((dist >= 0) & (dist < SWA_BLOCK), _rel_bucket_np(dist), -1).astype(np.int32)[None]


def _moba_bucket_idx(nblk):
    c = np.arange(MOBA_BLOCK)[:, None]
    r = np.arange(MOBA_BLOCK)[None, :]
    tiles = [np.where(m * MOBA_BLOCK + r - c >= 0, _rel_bucket_np(m * MOBA_BLOCK + r - c), -1)
             for m in range(nblk)]
    return np.stack(tiles).astype(np.int32)


def _expand_heads(v, e):
    hi = v.astype(BF16)
    lo = (v - hi.astype(F32)).astype(BF16)
    return (jnp.dot(hi, e, preferred_element_type=F32) + jnp.dot(lo, e, preferred_element_type=F32))


def _ssd_kernel(zx_ref, dtc_ref, dtr_ref, cw_ref, cb_ref, bias_c_ref, bias_r_ref, alog_c_ref, alog_r_ref,
                dskip_ref, nw_ref, e_ref, o_ref, state_ref, win_ref):
    q = SSD_CHUNK
    gw = SSD_INNER // SSD_GROUPS
    hpg = SSD_HEADS // SSD_GROUPS

    @pl.when(pl.program_id(1) == 0)
    def _():
        state_ref[...] = jnp.zeros(state_ref.shape, F32)
        win_ref[:HALO, :] = jnp.zeros((HALO, SSD_XBC), F32)

    row = lax.broadcasted_iota(jnp.int32, (q, q), 0)
    col = lax.broadcasted_iota(jnp.int32, (q, q), 1)
    tril = row >= col
    tri_b = jnp.where(tril, 1.0, 0.0).astype(BF16)
    triu_b = jnp.where(row <= col, 1.0, 0.0).astype(BF16)

    dt_c = _softplus(dtc_ref[0].astype(F32) + bias_c_ref[...])
    da_c = dt_c * (-jnp.exp(alog_c_ref[...]))
    acs_c = _sum01_left(tri_b, da_c)
    dt_r = _softplus(dtr_ref[0] + bias_r_ref[...])
    da_r = dt_r * (-jnp.exp(alog_r_ref[...]))
    acs_r = _sum01_right(da_r, triu_b)

    acs_last = acs_c[q - 1:q, :]
    e = e_ref[...]
    dt_full = _expand_heads(dt_c, e)
    dtdec_full = _expand_heads(dt_c * jnp.exp(acs_last - acs_c), e)
    eacs_full = _expand_heads(jnp.exp(acs_c), e)
    cdecay_full = eacs_full[q - 1:q, :]

    raw = zx_ref[0, :, SSD_INNER:].astype(F32)
    win_ref[HALO:, :] = raw
    xbc = _silu(_window_conv(win_ref, 0, SSD_XBC, q, cw_ref[...], SSD_CONV) + cb_ref[...])
    win_ref[:HALO, :] = raw[q - HALO:, :]

    xs = xbc[:, :SSD_INNER]
    xdt = xs * dt_full
    xdec = xs * dtdec_full
    lane_half = lax.broadcasted_iota(jnp.int32, (1, LANES), 1) >> 6

    y_parts = []
    for g in range(SSD_GROUPS):
        b_g = xbc[:, SSD_INNER + g * SSD_STATE:SSD_INNER + (g + 1) * SSD_STATE]
        c_g = xbc[:, SSD_INNER + SSD_GROUPS * SSD_STATE + g * SSD_STATE:
                  SSD_INNER + SSD_GROUPS * SSD_STATE + (g + 1) * SSD_STATE]
        cb = jnp.where(tril, _bdot_nt(c_g, b_g), 0.0)
        st = state_ref[g]
        y_off = _bdot(c_g, st) * eacs_full[:, g * gw:(g + 1) * gw]
        state_ref[g] = st * cdecay_full[:, g * gw:(g + 1) * gw] + _bdot(b_g.T, xdec[:, g * gw:(g + 1) * gw])
        pair_parts = []
        for pr in range(hpg // 2):
            acc = None
            lo = g * gw + pr * LANES
            x_pair = xdt[:, lo:lo + LANES]
            for half in range(2):
                h = g * hpg + pr * 2 + half
                diff = acs_c[:, h:h + 1] - acs_r[h:h + 1, :]
                lmat = jnp.exp(jnp.minimum(diff, 0.0))
                part = _bdot(cb * lmat, jnp.where(lane_half == half, x_pair, 0.0))
                acc = part if acc is None else acc + part
            pair_parts.append(acc)
        y_diag = jnp.concatenate(pair_parts, axis=1)
        y = y_diag + y_off + dskip_ref[:, g * gw:(g + 1) * gw] * xs[:, g * gw:(g + 1) * gw]
        y = y * _silu(zx_ref[0, :, g * gw:(g + 1) * gw].astype(F32))
        ms = jnp.mean(y * y, axis=-1, keepdims=True)
        y_parts.append(y * lax.rsqrt(ms + RMS_EPS) * nw_ref[:, g * gw:(g + 1) * gw])
    o_ref[0] = jnp.concatenate(y_parts, axis=1).astype(o_ref.dtype)


def _ssd_mixer(proj, dt_rows, conv_w, conv_b, dt_bias, a_log, d_skip, norm_w):
    bsz, s, _ = proj.shape
    q = SSD_CHUNK
    zx = SSD_INNER + SSD_XBC
    pad = LANES - SSD_HEADS
    bias_c = jnp.pad(dt_bias, (0, pad)).reshape(1, LANES)
    alog_c = jnp.pad(a_log, (0, pad)).reshape(1, LANES)
    e_np = np.zeros((LANES, SSD_INNER), np.float32)
    for h in range(SSD_HEADS):
        e_np[h, h * SSD_HEAD_DIM:(h + 1) * SSD_HEAD_DIM] = 1.0
    small = lambda shape: pl.BlockSpec(shape, lambda b_, c: (0, 0))
    return pl.pallas_call(
        _ssd_kernel,
        grid=(bsz, s // q),
        in_specs=[pl.BlockSpec((1, q, zx), lambda b_, c: (b_, c, 0)),
                  pl.BlockSpec((1, q, LANES), lambda b_, c: (b_, c, AB_DT // LANES)),
                  pl.BlockSpec((1, LANES, q), lambda b_, c: (b_, 0, c)),
                  small((SSD_CONV, SSD_XBC)), small((1, SSD_XBC)),
                  small((1, LANES)), small((LANES, 1)), small((1, LANES)), small((LANES, 1)),
                  small((1, SSD_INNER)), small((1, SSD_INNER)), small((LANES, SSD_INNER))],
        out_specs=pl.BlockSpec((1, q, SSD_INNER), lambda b_, c: (b_, c, 0)),
        out_shape=jax.ShapeDtypeStruct((bsz, s, SSD_INNER), BF16),
        scratch_shapes=[pltpu.VMEM((SSD_GROUPS, SSD_STATE, SSD_INNER // SSD_GROUPS), F32),
                        pltpu.VMEM((HALO + q, SSD_XBC), F32)],
        compiler_params=_cparams(("arbitrary", "arbitrary")),
        name="ssd_mixer",
    )(proj, proj, dt_rows, conv_w, conv_b.reshape(1, SSD_XBC), bias_c, bias_c.reshape(LANES, 1),
      alog_c, alog_c.reshape(LANES, 1), jnp.repeat(d_skip, SSD_HEAD_DIM).reshape(1, SSD_INNER),
      norm_w.reshape(1, SSD_INNER), jnp.asarray(e_np, BF16))


def _swa_kernel(sink_ref, q_ref, kp_ref, kc_ref, vp_ref, vc_ref, bias_ref, o_ref, s_ref):
    blk = SWA_BLOCK
    n = pl.program_id(1)
    grp = N_ATTN_HEADS // SWA_KV_HEADS
    scale = ATTN_HEAD_DIM ** -0.5
    kk = jnp.concatenate([kp_ref[0], kc_ref[0]], axis=0).astype(F32) * (scale * LOG2E)
    vv_t = jnp.concatenate([vp_ref[0], vc_ref[0]], axis=0).astype(F32).T
    lane_half = lax.broadcasted_iota(jnp.int32, (1, LANES), 1) >> 6
    c = lax.broadcasted_iota(jnp.int32, (2 * blk, blk), 0)
    valid = (c >= blk) | (n > 0)
    ms = []
    for kv in range(SWA_KV_HEADS):
        k_own = jnp.where(lane_half == kv, kk, 0.0).astype(BF16)
        k_var = [None, None]
        k_var[kv] = k_own
        k_var[1 - kv] = pltpu.roll(jnp.where(lane_half == kv, kk, 0.0), ATTN_HEAD_DIM, axis=1).astype(BF16)
        for gq in range(grp):
            h = kv * grp + gq
            q_tile = q_ref[0, :, (h // 2) * LANES:(h // 2 + 1) * LANES]
            s_t = _bdot_nt(k_var[h % 2], q_tile) + bias_ref[h, 0]
            s_t = jnp.where(valid, s_t, NEG)
            s_ref[h] = s_t
            ms.append(jnp.maximum(jnp.max(s_t, axis=0, keepdims=True), sink_ref[h] * LOG2E))
    outs = []
    for h in range(N_ATTN_HEADS):
        kv = h // grp
        p = jnp.exp2(s_ref[h] - ms[h])
        l = jnp.sum(p, axis=0, keepdims=True) + jnp.exp2(sink_ref[h] * LOG2E - ms[h])
        outs.append(_bdot(vv_t[kv * ATTN_HEAD_DIM:(kv + 1) * ATTN_HEAD_DIM, :], p) / l)
    for t in range(N_ATTN_HEADS // 2):
        pair = jnp.concatenate([outs[2 * t], outs[2 * t + 1]], axis=0)
        o_ref[0, :, t * LANES:(t + 1) * LANES] = pair.T.astype(o_ref.dtype)


def _swa_attention(proj, sinks, bias):
    bsz, s, _ = proj.shape
    blk = SWA_BLOCK
    kvw = SWA_KV_HEADS * ATTN_HEAD_DIM
    prev = lambda col: pl.BlockSpec((1, blk, kvw), lambda b_, n: (b_, jnp.maximum(n - 1, 0), col))
    cur = lambda col: pl.BlockSpec((1, blk, kvw), lambda b_, n: (b_, n, col))
    return pl.pallas_call(
        _swa_kernel,
        grid=(bsz, s // blk),
        in_specs=[pl.BlockSpec(memory_space=pltpu.SMEM),
                  pl.BlockSpec((1, blk, ATTN_WIDTH), lambda b_, n: (b_, n, AB_Q // ATTN_WIDTH)),
                  prev(AB_K // kvw), cur(AB_K // kvw), prev(AB_V // kvw), cur(AB_V // kvw),
                  pl.BlockSpec((N_ATTN_HEADS, 1, 2 * blk, blk), lambda b_, n: (0, 0, 0, 0))],
        out_specs=pl.BlockSpec((1, blk, ATTN_WIDTH), lambda b_, n: (b_, n, 0)),
        out_shape=jax.ShapeDtypeStruct((bsz, s, ATTN_WIDTH), BF16),
        scratch_shapes=[pltpu.VMEM((N_ATTN_HEADS, 2 * blk, blk), F32)],
        compiler_params=_cparams(("arbitrary", "arbitrary")),
        name="swa_attention",
    )(sinks, proj, proj, proj, proj, proj, bias)


def _moba_kernel(q_ref, k_ref, v_ref, bias_ref, o_ref, vt_ref, qm_ref, negrow_ref, s_ref, *, nblk):
    mb = MOBA_BLOCK
    dh = ATTN_HEAD_DIM
    own = pl.program_id(2)
    scale = dh ** -0.5

    @pl.when(own == 0)
    def _():
        means = []
        for j in range(nblk):
            vt_ref[j] = v_ref[0, j * mb:(j + 1) * mb, :].astype(F32).T.astype(BF16)
            means.append(jnp.mean(k_ref[0, j * mb:(j + 1) * mb, :].astype(F32), axis=0, keepdims=True))
        kmean = jnp.concatenate(means, axis=0)
        lane_half = lax.broadcasted_iota(jnp.int32, (1, LANES), 1) >> 6
        blk_id = lax.broadcasted_iota(jnp.int32, (nblk, mb), 0)
        for qb in range(nblk):
            q = q_ref[0, qb * mb:(qb + 1) * mb, :].astype(F32) * (scale * LOG2E)
            for hh in range(2):
                qm = jnp.where(lane_half == hh, q, 0.0)
                qm_ref[hh, qb] = qm.astype(BF16)
                gate = lax.dot_general(kmean, qm, (((1,), (1,)), ((), ())),
                                       preferred_element_type=F32, precision=HIGHEST)
                gate = jnp.where(blk_id < qb, gate, NEG)
                rank = jnp.zeros((nblk, mb), jnp.int32)
                for i in range(nblk):
                    gi = gate[i:i + 1, :]
                    ahead = (gi > gate) | ((gi == gate) & (i < blk_id))
                    rank = rank + jnp.where(ahead, 1, 0)
                keep = ((rank < MOBA_TOPK) & (blk_id < qb)) | (blk_id == qb)
                negrow_ref[hh, qb] = jnp.where(keep, 0.0, NEG).astype(F32)

    qms = [qm_ref[hh, own] for hh in range(2)]
    negrows = [negrow_ref[hh, own] for hh in range(2)]

    def attend(nb):
        m8 = [None, None]
        l8 = [None, None]
        acc = [None, None]

        def scores(hh, j):
            s_t = _bdot_nt(k_ref[0, j * mb:(j + 1) * mb, :], qms[hh]) + bias_ref[hh, jnp.maximum(own - j, 0)]
            s_t = s_t + negrows[hh][j:j + 1, :]
            s_ref[hh, j] = s_t
            m_j = jnp.max(s_t.reshape(mb // SUBLANES, SUBLANES, mb), axis=0)
            m8[hh] = m_j if m8[hh] is None else jnp.maximum(m8[hh], m_j)

        def weights(hh, j, m):
            p = jnp.exp2(s_ref[hh, j] - m)
            l_j = jnp.sum(p.reshape(mb // SUBLANES, SUBLANES, mb), axis=0)
            a_j = _bdot(vt_ref[j, hh * dh:(hh + 1) * dh, :], p)
            l8[hh] = l_j if l8[hh] is None else l8[hh] + l_j
            acc[hh] = a_j if acc[hh] is None else acc[hh] + a_j

        for j in range(nb):
            for hh in range(2):
                scores(hh, j)
        m = [jnp.max(m8[hh], axis=0, keepdims=True) for hh in range(2)]
        for j in range(nb):
            for hh in range(2):
                weights(hh, j, m[hh])
        out_t = jnp.concatenate([acc[hh] / jnp.sum(l8[hh], axis=0, keepdims=True) for hh in range(2)], axis=0)
        o_ref[0] = out_t.T.astype(o_ref.dtype)

    for nb in range(2, nblk + 1, 2):
        pl.when((own >= nb - 2) & (own < nb))(functools.partial(attend, nb))


def _moba_attention(proj, bias):
    bsz, s, _ = proj.shape
    mb = MOBA_BLOCK
    nblk = s // mb
    npair = N_ATTN_HEADS // 2
    return pl.pallas_call(
        functools.partial(_moba_kernel, nblk=nblk),
        grid=(bsz, npair, nblk),
        in_specs=[pl.BlockSpec((1, s, LANES), lambda b_, p, i: (b_, 0, CD_QD // LANES + p)),
                  pl.BlockSpec((1, s, LANES), lambda b_, p, i: (b_, 0, CD_KD // LANES + p)),
                  pl.BlockSpec((1, s, LANES), lambda b_, p, i: (b_, 0, CD_VD // LANES + p)),
                  pl.BlockSpec((2, nblk, mb, mb), lambda b_, p, i: (p, 0, 0, 0))],
        out_specs=pl.BlockSpec((1, mb, LANES), lambda b_, p, i: (b_, i, p)),
        out_shape=jax.ShapeDtypeStruct((bsz, s, ATTN_WIDTH), BF16),
        scratch_shapes=[pltpu.VMEM((nblk, LANES, mb), BF16),
                        pltpu.VMEM((2, nblk, mb, LANES), BF16),
                        pltpu.VMEM((2, nblk, nblk, mb), F32),
                        pltpu.VMEM((2, nblk, mb, mb), F32)],
        compiler_params=_cparams(("arbitrary", "arbitrary", "arbitrary")),
        name="moba_attention",
    )(proj, proj, proj, bias)


def _gdn_kernel(qkv_ref, z_ref, bac_ref, bar_ref, cw_ref, bias_c_ref, bias_r_ref, alog_c_ref, alog_r_ref,
                nw_ref, o_ref, state_ref, win_ref, m_ref, attn_ref, t_ref, x_ref, rhs_ref, u_ref, wq_ref,
                kdt_ref, r_ref, glc_ref):
    t = 2 * GDN_CHUNK
    ck = GDN_CHUNK
    dk = GDN_HEAD_DIM
    nh = GDN_HEADS

    @pl.when(pl.program_id(1) == 0)
    def _():
        state_ref[...] = jnp.zeros(state_ref.shape, F32)
        win_ref[:HALO, :] = jnp.zeros((HALO, 3 * GDN_INNER), F32)

    row = lax.broadcasted_iota(jnp.int32, (t, t), 0)
    col = lax.broadcasted_iota(jnp.int32, (t, t), 1)
    same = (row >> 6) == (col >> 6)
    tril = same & (row >= col)
    strict = same & (row > col)
    tri_b = jnp.where(tril, 1.0, 0.0).astype(BF16)
    triu_b = jnp.where(same & (row <= col), 1.0, 0.0).astype(BF16)
    blk_b = jnp.where(same, 1.0, 0.0).astype(BF16)
    eye_f = jnp.where(row == col, 1.0, 0.0).astype(F32)
    merge_masks = [((row >> (l + 1)) == (col >> (l + 1))) & (((row >> l) & 1) == 1) & (((col >> l) & 1) == 0)
                   for l in range(int(math.log2(ck)))]
    mask_bf = [jnp.where(m, 1.0, 0.0).astype(BF16) for m in merge_masks[1:]]

    ba_c = bac_ref[0].astype(F32)
    g_c = -jnp.exp(alog_c_ref[...]) * _softplus(ba_c + bias_c_ref[...])
    sums_c = _sum01_left(jnp.concatenate([tri_b, blk_b], axis=0), g_c)
    gc_c = sums_c[:t]
    gl_c = sums_c[t:]
    g_r = -jnp.exp(alog_r_ref[...]) * _softplus(bar_ref[0] + bias_r_ref[...])
    gc_r = _sum01_right(g_r, triu_b)
    glc_ref[...] = gl_c

    def conv_silu(lo):
        raw = qkv_ref[0, :, lo:lo + dk].astype(F32)
        win_ref[HALO:, lo:lo + dk] = raw
        out = _silu(_window_conv(win_ref, lo, dk, t, cw_ref[:, lo:lo + dk], GDN_CONV))
        win_ref[:HALO, lo:lo + dk] = raw[t - HALO:, :]
        return out

    for h in range(nh):
        q = conv_silu(h * dk)
        k = conv_silu((nh + h) * dk)
        v = conv_silu((2 * nh + h) * dk)
        qn = q * lax.rsqrt(jnp.sum(q * q, axis=-1, keepdims=True) + 1e-6) * (dk ** -0.5)
        kn = k * lax.rsqrt(jnp.sum(k * k, axis=-1, keepdims=True) + 1e-6)
        beta = _sigmoid(ba_c[:, h:h + 1])
        gcc = gc_c[:, nh + h:nh + h + 1]
        gcr = gc_r[nh + h:nh + h + 1, :]
        glc = gl_c[:, nh + h:nh + h + 1]
        decay = jnp.where(tril, jnp.exp(jnp.where(tril, gcc - gcr, 0.0)), 0.0)
        kb = kn * beta
        kk = _bdot_nt(jnp.concatenate([kb, qn], axis=0), kn)
        mm = jnp.where(strict, kk[:t] * decay, 0.0)
        m_ref[h] = mm.astype(BF16)
        attn_ref[h] = (kk[t:] * decay).astype(BF16)
        t_ref[h] = eye_f - jnp.where(merge_masks[0], mm, 0.0)
        egc = jnp.exp(gcc)
        rhs_ref[h] = jnp.concatenate([v * beta, kb * egc], axis=1).astype(BF16)
        q_dec = (qn * egc).astype(BF16)
        for a in range(2):
            wq_ref[h, a, ck:, :] = q_dec[a * ck:(a + 1) * ck]
        kdt_ref[h] = (kn * jnp.exp(glc - gcc)).T.astype(BF16)

    for lvl in range(len(mask_bf)):
        for h in range(nh):
            x_ref[h] = jnp.dot(t_ref[h].astype(BF16), m_ref[h] * mask_bf[lvl],
                               preferred_element_type=F32).astype(BF16)
        for h in range(nh):
            t_h = t_ref[h]
            t_ref[h] = t_h - jnp.dot(x_ref[h], t_h.astype(BF16), preferred_element_type=F32)

    for h in range(nh):
        sol = jnp.dot(t_ref[h].astype(BF16), rhs_ref[h], preferred_element_type=F32)
        u_ref[h] = sol[:, :dk]
        for a in range(2):
            wq_ref[h, a, :ck, :] = sol[a * ck:(a + 1) * ck, dk:].astype(BF16)

    zeros_half = jnp.zeros((ck, dk), F32)
    for a in range(2):
        sl = slice(a * ck, (a + 1) * ck)
        for h in range(nh):
            r_ref[h] = jnp.dot(wq_ref[h, a], state_ref[h].astype(BF16), preferred_element_type=F32)
        for h in range(nh):
            v_new = u_ref[h, sl, :] - r_ref[h, :ck, :]
            v_full = jnp.concatenate([v_new, zeros_half] if a == 0 else [zeros_half, v_new], axis=0).astype(BF16)
            o = r_ref[h, ck:, :] + jnp.dot(attn_ref[h, sl, :], v_full, preferred_element_type=F32)
            gl = glc_ref[a * ck:a * ck + 1, nh + h:nh + h + 1]
            state_ref[h] = state_ref[h] * jnp.exp(gl) + jnp.dot(kdt_ref[h], v_full, preferred_element_type=F32)
            ms = jnp.mean(o * o, axis=-1, keepdims=True)
            y = o * lax.rsqrt(ms + RMS_EPS) * nw_ref[...] * _silu(z_ref[0, sl, h * dk:(h + 1) * dk].astype(F32))
            o_ref[0, sl, h * dk:(h + 1) * dk] = y.astype(o_ref.dtype)


def _gdn_mixer(proj, ba_rows, conv_w, dt_bias, a_log, norm_w):
    bsz, s, _ = proj.shape
    t = 2 * GDN_CHUNK
    nh = GDN_HEADS
    dk = GDN_HEAD_DIM
    bias_c = jnp.pad(dt_bias, (nh, LANES - 2 * nh)).reshape(1, LANES)
    alog_c = jnp.pad(a_log, (nh, LANES - 2 * nh)).reshape(1, LANES)
    small = lambda shape: pl.BlockSpec(shape, lambda b_, c: (0, 0))
    return pl.pallas_call(
        _gdn_kernel,
        grid=(bsz, s // t),
        in_specs=[pl.BlockSpec((1, t, 3 * GDN_INNER), lambda b_, c: (b_, c, CD_QKV // (3 * GDN_INNER))),
                  pl.BlockSpec((1, t, GDN_INNER), lambda b_, c: (b_, c, CD_Z // GDN_INNER)),
                  pl.BlockSpec((1, t, LANES), lambda b_, c: (b_, c, CD_BA // LANES)),
                  pl.BlockSpec((1, LANES, t), lambda b_, c: (b_, 0, c)),
                  small((GDN_CONV, 3 * GDN_INNER)),
                  small((1, LANES)), small((LANES, 1)), small((1, LANES)), small((LANES, 1)),
                  small((1, dk))],
        out_specs=pl.BlockSpec((1, t, GDN_INNER), lambda b_, c: (b_, c, 0)),
        out_shape=jax.ShapeDtypeStruct((bsz, s, GDN_INNER), BF16),
        scratch_shapes=[pltpu.VMEM((nh, dk, dk), F32),
                        pltpu.VMEM((HALO + t, 3 * GDN_INNER), F32),
                        pltpu.VMEM((nh, t, t), BF16),
                        pltpu.VMEM((nh, t, t), BF16),
                        pltpu.VMEM((nh, t, t), F32),
                        pltpu.VMEM((nh, t, t), BF16),
                        pltpu.VMEM((nh, t, 2 * dk), BF16),
                        pltpu.VMEM((nh, t, dk), F32),
                        pltpu.VMEM((nh, 2, t, dk), BF16),
                        pltpu.VMEM((nh, dk, t), BF16),
                        pltpu.VMEM((nh, t, dk), F32),
                        pltpu.VMEM((t, LANES), F32)],
        compiler_params=_cparams(("arbitrary", "arbitrary")),
        name="gdn_mixer",
    )(proj, proj, proj, ba_rows, conv_w, bias_c, bias_c.reshape(LANES, 1), alog_c, alog_c.reshape(LANES, 1),
      norm_w.reshape(1, dk))


def _pad_cols(w, n):
    return jnp.pad(w, ((0, 0), (0, n - w.shape[1])))


def _gate_rows(proj, col0):
    return jnp.swapaxes(proj[:, :, col0:col0 + LANES].astype(F32), 1, 2)


def kernel(x, c, rel_bias, norm_w, ada_w, ada_b, ab_w_in, ab_w_out, ssd_conv_w, ssd_conv_b,
           ssd_dt_bias, ssd_a_log, ssd_d, ssd_norm_w, swa_sinks, cd_w_in, cd_w_out, gdn_conv_w,
           gdn_dt_bias, gdn_a_log, gdn_norm_w, ffn_w_up, ffn_conv_w, ffn_conv_b, ffn_w_down):
    bsz, s, d = x.shape
    depth = norm_w.shape[0]
    mods = _mods(c, ada_w, ada_b)
    swa_bias = _bias_tiles(rel_bias, _swa_bucket_idx())
    moba_bias = _bias_tiles(rel_bias, _moba_bucket_idx(s // MOBA_BLOCK))

    for i in range(depth):
        sh_m, sc_m, g_m, sh_f, sc_f, g_f = [m.reshape(bsz, 1, d) for m in jnp.split(mods[i], 6, axis=-1)]
        j = i // 2
        if i % 2 == 0:
            w = ab_w_in[j]
            dt0 = SSD_INNER + SSD_XBC
            w_in = jnp.concatenate([w[:, :dt0], w[:, dt0 + SSD_HEADS:], w[:, dt0:dt0 + SSD_HEADS]], axis=1)
            w_in = _pad_cols(w_in, AB_COLS_PAD).astype(BF16)
            proj = _norm_mod_matmul(x, norm_w[i, 0], sc_m, sh_m, w_in, tn=AB_COLS_PAD // 2)
            y_a = _ssd_mixer(proj, _gate_rows(proj, AB_DT), ssd_conv_w[j], ssd_conv_b[j], ssd_dt_bias[j],
                             ssd_a_log[j], ssd_d[j], ssd_norm_w[j])
            y_b = _swa_attention(proj, swa_sinks[j], swa_bias)
            x = _matmul_resid([y_a, y_b], ab_w_out, j, x, g_m, norm_w[i, 1])
        else:
            w = cd_w_in[j]
            ba0 = 4 * GDN_INNER
            w_in = jnp.concatenate([w[:, :ba0], w[:, ba0 + 2 * GDN_HEADS:], w[:, ba0:ba0 + 2 * GDN_HEADS]], axis=1)
            w_in = _pad_cols(w_in, CD_COLS_PAD).astype(BF16)
            proj = _norm_mod_matmul(x, norm_w[i, 0], sc_m, sh_m, w_in, tn=2048)
            y_c = _gdn_mixer(proj, _gate_rows(proj, CD_BA), gdn_conv_w[j], gdn_dt_bias[j], gdn_a_log[j],
                             gdn_norm_w[j])
            y_d = _moba_attention(proj, moba_bias)
            x = _matmul_resid([y_c, y_d], cd_w_out, j, x, g_m, norm_w[i, 1])
        act = _ffn_up_geglu(x, norm_w[i, 2], sc_f, sh_f, ffn_w_up[i].astype(BF16), ffn_conv_w[i], ffn_conv_b[i])
        x = _matmul_resid([act], ffn_w_down, i, x, g_f, norm_w[i, 3])
    return x
```

```python
import functools
import math

import numpy as np
import jax
import jax.numpy as jnp
from jax import lax
from jax.experimental import pallas as pl
from jax.experimental.pallas import tpu as pltpu

F32 = jnp.float32
BF16 = jnp.bfloat16
HIGHEST = lax.Precision.HIGHEST

D_MODEL = 1024
RMS_EPS = 1e-6
NEG = -1e30
LANES = 128
SUBLANES = 8
N_ATTN_HEADS = 8
ATTN_HEAD_DIM = 64
ATTN_WIDTH = N_ATTN_HEADS * ATTN_HEAD_DIM
REL_BUCKETS = 32
REL_MAX_DIST = 1024
SSD_HEADS = 24
SSD_HEAD_DIM = 64
SSD_INNER = SSD_HEADS * SSD_HEAD_DIM
SSD_GROUPS = 4
SSD_STATE = 128
SSD_CONV = 4
SSD_CHUNK = 128
SSD_XBC = SSD_INNER + 2 * SSD_GROUPS * SSD_STATE
SWA_KV_HEADS = 2
SWA_BLOCK = 128
GDN_HEADS = 12
GDN_HEAD_DIM = 128
GDN_INNER = GDN_HEADS * GDN_HEAD_DIM
GDN_CONV = 4
GDN_CHUNK = 64
MOBA_BLOCK = 256
MOBA_TOPK = 3
FFN_DIM = 2816
FFN_CONV = 3

AB_Z, AB_Q, AB_K, AB_V, AB_DT, AB_COLS = 0, 1536, 2048, 2176, 2304, 2560
CD_Z, CD_QD, CD_KD, CD_VD, CD_BA, CD_COLS = 0, 1536, 2048, 2560, 3072, 3328

VMEM_LIMIT = 48 * 1024 * 1024
HALO = 16


def _cparams(sem):
    return pltpu.CompilerParams(dimension_semantics=sem, vmem_limit_bytes=VMEM_LIMIT)


def _bdot(a, b):
    return jnp.dot(a.astype(BF16), b.astype(BF16), preferred_element_type=F32)


def _bdot_nt(a, b):
    return lax.dot_general(a.astype(BF16), b.astype(BF16), (((1,), (1,)), ((), ())),
                           preferred_element_type=F32)


def _split3(x):
    hi = x.astype(BF16)
    r1 = x - hi.astype(F32)
    mid = r1.astype(BF16)
    lo = (r1 - mid.astype(F32)).astype(BF16)
    return hi, mid, lo


def _sum01_left(m01, x):
    n = x.shape[1]
    y = jnp.dot(m01, jnp.concatenate(_split3(x), axis=1), preferred_element_type=F32)
    return y[:, :n] + y[:, n:2 * n] + y[:, 2 * n:]


def _sum01_right(x, m01):
    n = x.shape[0]
    y = jnp.dot(jnp.concatenate(_split3(x), axis=0), m01, preferred_element_type=F32)
    return y[:n] + y[n:2 * n] + y[2 * n:]


def _softplus(x):
    return jnp.maximum(x, 0.0) + jnp.log(1.0 + jnp.exp(-jnp.abs(x)))


def _sigmoid(x):
    return 1.0 / (1.0 + jnp.exp(-x))


def _silu(x):
    return x * _sigmoid(x)


def _window_conv(win_ref, lo, ncols, rows, w, width, row0=0, shifted_loads=False):
    if shifted_loads:
        acc = None
        for s in range(width):
            tap = win_ref[pl.ds(HALO + row0 - s, rows), lo:lo + ncols] * w[width - 1 - s:width - s, :]
            acc = tap if acc is None else acc + tap
        return acc
    xin = win_ref[pl.ds(HALO + row0 - SUBLANES, rows + SUBLANES), lo:lo + ncols]
    acc = xin[SUBLANES:, :] * w[width - 1:width, :]
    for s in range(1, width):
        acc = acc + pltpu.roll(xin, s, axis=0)[SUBLANES:, :] * w[width - 1 - s:width - s, :]
    return acc


def _mods_kernel(c_ref, w_ref, b_ref, o_ref):
    o_ref[0] = _bdot(_silu(c_ref[...]), w_ref[0]) + b_ref[0]


def _mods(c, ada_w, ada_b):
    depth, d, n = ada_w.shape
    bsz = c.shape[0]
    tn = 512
    return pl.pallas_call(
        _mods_kernel,
        grid=(depth, n // tn),
        in_specs=[pl.BlockSpec((bsz, d), lambda l, j: (0, 0)),
                  pl.BlockSpec((1, d, tn), lambda l, j: (l, 0, j)),
                  pl.BlockSpec((1, 1, tn), lambda l, j: (l, 0, j))],
        out_specs=pl.BlockSpec((1, bsz, tn), lambda l, j: (l, 0, j)),
        out_shape=jax.ShapeDtypeStruct((depth, bsz, n), F32),
        compiler_params=_cparams(("arbitrary", "arbitrary")),
        name="adaln_mods",
    )(c, ada_w, ada_b.reshape(depth, 1, n))


def _modulated_norm(x, nw, sc, sh):
    ms = jnp.mean(x * x, axis=-1, keepdims=True)
    return x * lax.rsqrt(ms + RMS_EPS) * nw * (1.0 + sc) + sh


NMM_ROWS = 256


def _nmm_kernel(x_ref, nw_ref, sc_ref, sh_ref, w_ref, o_ref, h_ref):
    tm = x_ref.shape[1]

    @pl.when(pl.program_id(1) == 0)
    def _():
        for r0 in range(0, tm, NMM_ROWS):
            h = _modulated_norm(x_ref[0, r0:r0 + NMM_ROWS, :], nw_ref[...], sc_ref[0], sh_ref[0]).astype(BF16)
            h_ref[r0:r0 + NMM_ROWS, :] = h
            o_ref[0, r0:r0 + NMM_ROWS, :] = jnp.dot(h, w_ref[...], preferred_element_type=F32).astype(o_ref.dtype)

    @pl.when(pl.program_id(1) > 0)
    def _():
        o_ref[0] = jnp.dot(h_ref[...], w_ref[...], preferred_element_type=F32).astype(o_ref.dtype)


def _norm_mod_matmul(x, nw, sc, sh, w, tn=1024, tm=1024):
    bsz, s, d = x.shape
    n = w.shape[1]
    spt = s // tm
    return pl.pallas_call(
        _nmm_kernel,
        grid=(bsz * spt, n // tn),
        in_specs=[pl.BlockSpec((1, tm, d), lambda i, j: (i // spt, i % spt, 0)),
                  pl.BlockSpec((1, d), lambda i, j: (0, 0)),
                  pl.BlockSpec((1, 1, d), lambda i, j: (i // spt, 0, 0)),
                  pl.BlockSpec((1, 1, d), lambda i, j: (i // spt, 0, 0)),
                  pl.BlockSpec((d, tn), lambda i, j: (0, j))],
        out_specs=pl.BlockSpec((1, tm, tn), lambda i, j: (i // spt, i % spt, j)),
        out_shape=jax.ShapeDtypeStruct((bsz, s, n), BF16),
        scratch_shapes=[pltpu.VMEM((tm, d), BF16)],
        compiler_params=_cparams(("arbitrary", "arbitrary")),
        name="norm_mod_matmul",
    )(x, nw.reshape(1, d), sc, sh, w)


def _mmres_kernel(*refs, splits):
    na = len(splits)
    a_refs = refs[:na]
    w_ref, x_ref, g_ref, nw_ref, o_ref = refs[na:]
    acc = None
    lo = 0
    for a_ref, k in zip(a_refs, splits):
        part = jnp.dot(a_ref[0].astype(BF16), w_ref[lo:lo + k, :].astype(BF16), preferred_element_type=F32)
        acc = part if acc is None else acc + part
        lo += k
    ms = jnp.mean(acc * acc, axis=-1, keepdims=True)
    y = acc * lax.rsqrt(ms + RMS_EPS) * nw_ref[...]
    o_ref[0] = x_ref[0] + g_ref[0] * y


def _matmul_resid(a_list, w_stack, layer, x, gate, nw, tm=512):
    bsz, s, d = x.shape
    spt = s // tm
    splits = tuple(a.shape[-1] for a in a_list)
    ktot = sum(splits)
    in_specs = [pl.BlockSpec((1, tm, k), lambda i: (i // spt, i % spt, 0)) for k in splits]
    in_specs += [pl.BlockSpec((None, ktot, d), lambda i: (layer, 0, 0)),
                 pl.BlockSpec((1, tm, d), lambda i: (i // spt, i % spt, 0)),
                 pl.BlockSpec((1, 1, d), lambda i: (i // spt, 0, 0)),
                 pl.BlockSpec((1, d), lambda i: (0, 0))]
    return pl.pallas_call(
        functools.partial(_mmres_kernel, splits=splits),
        grid=(bsz * spt,),
        in_specs=in_specs,
        out_specs=pl.BlockSpec((1, tm, d), lambda i: (i // spt, i % spt, 0)),
        out_shape=jax.ShapeDtypeStruct((bsz, s, d), F32),
        compiler_params=_cparams(("arbitrary",)),
        name="matmul_resid",
    )(*a_list, w_stack, x, gate, nw.reshape(1, d))


def _proj_conv_kernel(*refs, offs, tc, width, rb, act, has_plain):
    if has_plain:
        xh_ref, x_ref, nw_ref, sc_ref, sh_ref, w_ref, cw_ref, cb_ref, wp_ref, o_ref, op_ref, h_ref, u_ref = refs
    else:
        xh_ref, x_ref, nw_ref, sc_ref, sh_ref, w_ref, cw_ref, cb_ref, o_ref, h_ref, u_ref = refs
    tm = x_ref.shape[1]
    nchunk = o_ref.shape[2] // tc
    nplain = op_ref.shape[2] // tc if has_plain else 0
    h_halo = _modulated_norm(xh_ref[0], nw_ref[...], sc_ref[0], sh_ref[0])
    h_ref[:HALO, :] = jnp.where(pl.program_id(1) > 0, h_halo, 0.0).astype(BF16)
    h_ref[HALO:, :] = _modulated_norm(x_ref[0], nw_ref[...], sc_ref[0], sh_ref[0]).astype(BF16)
    c0 = math.sqrt(2.0 / math.pi)

    def matmuls(c):
        for k, off in enumerate(offs):
            lo = off + c * tc
            u_ref[c % 2, k] = jnp.dot(h_ref[...], w_ref[:, lo:lo + tc], preferred_element_type=F32)

    def epilogue(c):
        for r0 in range(0, tm, rb):
            conv = []
            for k, off in enumerate(offs):
                lo = off + c * tc
                conv.append(_window_conv(u_ref.at[c % 2, k], 0, tc, rb, cw_ref[:, lo:lo + tc], width, row0=r0,
                                         shifted_loads=True) + cb_ref[:, lo:lo + tc])
            if act == "geglu":
                g, v = conv
                th = jnp.tanh(g * (c0 + (c0 * 0.044715) * (g * g)))
                hg = 0.5 * g
                out = (hg + hg * th) * v
            else:
                out = _silu(conv[0])
            o_ref[0, r0:r0 + rb, c * tc:(c + 1) * tc] = out.astype(o_ref.dtype)

    def plain(p):
        op_ref[0, :, p * tc:(p + 1) * tc] = jnp.dot(
            h_ref[HALO:, :], wp_ref[:, p * tc:(p + 1) * tc], preferred_element_type=F32).astype(op_ref.dtype)

    matmuls(0)
    done = 0
    for c in range(nchunk):
        if c + 1 < nchunk:
            matmuls(c + 1)
        epilogue(c)
        upto = (c + 1) * nplain // nchunk
        for p in range(done, upto):
            plain(p)
        done = upto


def _proj_conv_act(x, nw, sc, sh, w, cw, cb, act, w_plain=None, tm=512, tc=256, rb=128):
    bsz, s, d = x.shape
    n2 = w.shape[1]
    f = n2 // 2 if act == "geglu" else n2
    offs = (0, f) if act == "geglu" else (0,)
    width = cw.shape[0]
    hb = tm // HALO
    const = lambda shape: pl.BlockSpec(shape, lambda b_, r: (0, 0))
    rows = lambda n: pl.BlockSpec((1, tm, n), lambda b_, r: (b_, r, 0))
    in_specs = [pl.BlockSpec((1, HALO, d), lambda b_, r: (b_, jnp.maximum(r * hb - 1, 0), 0)),
                rows(d), const((1, d)),
                pl.BlockSpec((1, 1, d), lambda b_, r: (b_, 0, 0)),
                pl.BlockSpec((1, 1, d), lambda b_, r: (b_, 0, 0)),
                const((d, n2)), const((width, n2)), const((1, n2))]
    args = [x, x, nw.reshape(1, d), sc, sh, w, cw, cb.reshape(1, n2)]
    out_specs = rows(f)
    out_shape = jax.ShapeDtypeStruct((bsz, s, f), BF16)
    if w_plain is not None:
        npl = w_plain.shape[1]
        in_specs.append(const((d, npl)))
        args.append(w_plain)
        out_specs = (out_specs, rows(npl))
        out_shape = (out_shape, jax.ShapeDtypeStruct((bsz, s, npl), BF16))
    return pl.pallas_call(
        functools.partial(_proj_conv_kernel, offs=offs, tc=tc, width=width, rb=rb, act=act,
                          has_plain=w_plain is not None),
        grid=(bsz, s // tm),
        in_specs=in_specs,
        out_specs=out_specs,
        out_shape=out_shape,
        scratch_shapes=[pltpu.VMEM((HALO + tm, d), BF16),
                        pltpu.VMEM((2, len(offs), HALO + tm, tc), F32)],
        compiler_params=_cparams(("arbitrary", "arbitrary")),
        name="proj_conv_" + act,
    )(*args)


def _rel_bucket_np(d):
    max_exact = REL_BUCKETS // 2
    d = np.maximum(d, 0)
    df = np.maximum(d, 1).astype(np.float64)
    large = max_exact + (np.log(df / max_exact) / math.log(REL_MAX_DIST / max_exact)
                         * (REL_BUCKETS - max_exact)).astype(np.int32)
    large = np.minimum(large, REL_BUCKETS - 1)
    return np.where(d < max_exact, d, large).astype(np.int32)


LOG2E = math.log2(math.e)


def _bias_kernel(tab_ref, idx_ref, o_ref, *, ranges):
    h = pl.program_id(0)
    for t, (lo, hi) in enumerate(ranges):
        idx = idx_ref[t]
        acc = jnp.full(idx.shape, NEG, F32)
        for bkt in range(lo, hi + 1):
            acc = jnp.where(idx == bkt, tab_ref[bkt, h] * LOG2E, acc)
        o_ref[0, t] = acc


def _bias_tiles(rel_bias, idx_np):
    t, r, c = idx_np.shape
    ranges = tuple((int(tile[tile >= 0].min()), int(tile.max())) for tile in idx_np)
    return pl.pallas_call(
        functools.partial(_bias_kernel, ranges=ranges),
        grid=(N_ATTN_HEADS,),
        in_specs=[pl.BlockSpec(memory_space=pltpu.SMEM),
                  pl.BlockSpec((t, r, c), lambda h: (0, 0, 0))],
        out_specs=pl.BlockSpec((1, t, r, c), lambda h: (h, 0, 0, 0)),
        out_shape=jax.ShapeDtypeStruct((N_ATTN_HEADS, t, r, c), F32),
        compiler_params=_cparams(("arbitrary",)),
        name="rel_bias_tiles",
    )(rel_bias, jnp.asarray(idx_np))


def _swa_bucket_idx():
    c = np.arange(2 * SWA_BLOCK)[:, None]
    r = np.arange(SWA_BLOCK)[None, :]
    dist = SWA_BLOCK + r - c
    return np.where((dist >= 0) & (dist < SWA_BLOCK), _rel_bucket_np(dist), -1).astype(np.int32)[None]


def _moba_bucket_idx(nblk):
    c = np.arange(MOBA_BLOCK)[:, None]
    r = np.arange(MOBA_BLOCK)[None, :]
    tiles = [np.where(m * MOBA_BLOCK + r - c >= 0, _rel_bucket_np(m * MOBA_BLOCK + r - c), -1)
             for m in range(nblk)]
    return np.stack(tiles).astype(np.int32)


def _expand_heads(v, e):
    hi = v.astype(BF16)
    lo = (v - hi.astype(F32)).astype(BF16)
    return (jnp.dot(hi, e, preferred_element_type=F32) + jnp.dot(lo, e, preferred_element_type=F32))


def _ssd_kernel(xbc_ref, z_ref, dtc_ref, dtr_ref, bias_c_ref, bias_r_ref, alog_c_ref, alog_r_ref,
                dskip_ref, nw_ref, e_ref, o_ref, state_ref):
    q = SSD_CHUNK
    gw = SSD_INNER // SSD_GROUPS
    hpg = SSD_HEADS // SSD_GROUPS

    @pl.when(pl.program_id(1) == 0)
    def _():
        state_ref[...] = jnp.zeros(state_ref.shape, F32)

    row = lax.broadcasted_iota(jnp.int32, (q, q), 0)
    col = lax.broadcasted_iota(jnp.int32, (q, q), 1)
    tril = row >= col
    tri_b = jnp.where(tril, 1.0, 0.0).astype(BF16)
    triu_b = jnp.where(row <= col, 1.0, 0.0).astype(BF16)

    dt_c = _softplus(dtc_ref[0].astype(F32) + bias_c_ref[...])
    da_c = dt_c * (-jnp.exp(alog_c_ref[...]))
    acs_c = _sum01_left(tri_b, da_c)
    dt_r = _softplus(dtr_ref[0] + bias_r_ref[...])
    da_r = dt_r * (-jnp.exp(alog_r_ref[...]))
    acs_r = _sum01_right(da_r, triu_b)

    acs_last = acs_c[q - 1:q, :]
    e = e_ref[...]
    dt_full = _expand_heads(dt_c, e)
    dtdec_full = _expand_heads(dt_c * jnp.exp(acs_last - acs_c), e)
    eacs_full = _expand_heads(jnp.exp(acs_c), e)
    cdecay_full = eacs_full[q - 1:q, :]

    xbc = xbc_ref[0].astype(F32)
    xs = xbc[:, :SSD_INNER]
    xdt = xs * dt_full
    xdec = xs * dtdec_full
    lane_half = lax.broadcasted_iota(jnp.int32, (1, LANES), 1) >> 6

    y_parts = []
    for g in range(SSD_GROUPS):
        b_g = xbc[:, SSD_INNER + g * SSD_STATE:SSD_INNER + (g + 1) * SSD_STATE]
        c_g = xbc[:, SSD_INNER + SSD_GROUPS * SSD_STATE + g * SSD_STATE:
                  SSD_INNER + SSD_GROUPS * SSD_STATE + (g + 1) * SSD_STATE]
        cb = jnp.where(tril, _bdot_nt(c_g, b_g), 0.0)
        st = state_ref[g]
        y_off = _bdot(c_g, st) * eacs_full[:, g * gw:(g + 1) * gw]
        state_ref[g] = st * cdecay_full[:, g * gw:(g + 1) * gw] + _bdot(b_g.T, xdec[:, g * gw:(g + 1) * gw])
        pair_parts = []
        for pr in range(hpg // 2):
            acc = None
            lo = g * gw + pr * LANES
            x_pair = xdt[:, lo:lo + LANES]
            for half in range(2):
                h = g * hpg + pr * 2 + half
                diff = acs_c[:, h:h + 1] - acs_r[h:h + 1, :]
                lmat = jnp.exp(jnp.minimum(diff, 0.0))
                part = _bdot(cb * lmat, jnp.where(lane_half == half, x_pair, 0.0))
                acc = part if acc is None else acc + part
            pair_parts.append(acc)
        y_diag = jnp.concatenate(pair_parts, axis=1)
        y = y_diag + y_off + dskip_ref[:, g * gw:(g + 1) * gw] * xs[:, g * gw:(g + 1) * gw]
        y = y * _silu(z_ref[0, :, g * gw:(g + 1) * gw].astype(F32))
        ms = jnp.mean(y * y, axis=-1, keepdims=True)
        y_parts.append(y * lax.rsqrt(ms + RMS_EPS) * nw_ref[:, g * gw:(g + 1) * gw])
    o_ref[0] = jnp.concatenate(y_parts, axis=1).astype(o_ref.dtype)


def _ssd_mixer(xbc_act, proj, dt_rows, dt_bias, a_log, d_skip, norm_w):
    bsz, s, _ = proj.shape
    q = SSD_CHUNK
    pad = LANES - SSD_HEADS
    bias_c = jnp.pad(dt_bias, (0, pad)).reshape(1, LANES)
    alog_c = jnp.pad(a_log, (0, pad)).reshape(1, LANES)
    e_np = np.zeros((LANES, SSD_INNER), np.float32)
    for h in range(SSD_HEADS):
        e_np[h, h * SSD_HEAD_DIM:(h + 1) * SSD_HEAD_DIM] = 1.0
    small = lambda shape: pl.BlockSpec(shape, lambda b_, c: (0, 0))
    return pl.pallas_call(
        _ssd_kernel,
        grid=(bsz, s // q),
        in_specs=[pl.BlockSpec((1, q, SSD_XBC), lambda b_, c: (b_, c, 0)),
                  pl.BlockSpec((1, q, SSD_INNER), lambda b_, c: (b_, c, AB_Z // SSD_INNER)),
                  pl.BlockSpec((1, q, LANES), lambda b_, c: (b_, c, AB_DT // LANES)),
                  pl.BlockSpec((1, LANES, q), lambda b_, c: (b_, 0, c)),
                  small((1, LANES)), small((LANES, 1)), small((1, LANES)), small((LANES, 1)),
                  small((1, SSD_INNER)), small((1, SSD_INNER)), small((LANES, SSD_INNER))],
        out_specs=pl.BlockSpec((1, q, SSD_INNER), lambda b_, c: (b_, c, 0)),
        out_shape=jax.ShapeDtypeStruct((bsz, s, SSD_INNER), BF16),
        scratch_shapes=[pltpu.VMEM((SSD_GROUPS, SSD_STATE, SSD_INNER // SSD_GROUPS), F32)],
        compiler_params=_cparams(("arbitrary", "arbitrary")),
        name="ssd_mixer",
    )(xbc_act, proj, proj, dt_rows, bias_c, bias_c.reshape(LANES, 1),
      alog_c, alog_c.reshape(LANES, 1), jnp.repeat(d_skip, SSD_HEAD_DIM).reshape(1, SSD_INNER),
      norm_w.reshape(1, SSD_INNER), jnp.asarray(e_np, BF16))


def _swa_kernel(sink_ref, q_ref, kp_ref, kc_ref, vp_ref, vc_ref, bias_ref, o_ref, s_ref):
    blk = SWA_BLOCK
    n = pl.program_id(1)
    grp = N_ATTN_HEADS // SWA_KV_HEADS
    scale = ATTN_HEAD_DIM ** -0.5
    kk = jnp.concatenate([kp_ref[0], kc_ref[0]], axis=0).astype(F32) * (scale * LOG2E)
    vv_t = jnp.concatenate([vp_ref[0], vc_ref[0]], axis=0).astype(F32).T
    lane_half = lax.broadcasted_iota(jnp.int32, (1, LANES), 1) >> 6
    c = lax.broadcasted_iota(jnp.int32, (2 * blk, blk), 0)
    valid = (c >= blk) | (n > 0)
    ms = []
    for kv in range(SWA_KV_HEADS):
        k_own = jnp.where(lane_half == kv, kk, 0.0).astype(BF16)
        k_var = [None, None]
        k_var[kv] = k_own
        k_var[1 - kv] = pltpu.roll(jnp.where(lane_half == kv, kk, 0.0), ATTN_HEAD_DIM, axis=1).astype(BF16)
        for gq in range(grp):
            h = kv * grp + gq
            q_tile = q_ref[0, :, (h // 2) * LANES:(h // 2 + 1) * LANES]
            s_t = _bdot_nt(k_var[h % 2], q_tile) + bias_ref[h, 0]
            s_t = jnp.where(valid, s_t, NEG)
            s_ref[h] = s_t
            ms.append(jnp.maximum(jnp.max(s_t, axis=0, keepdims=True), sink_ref[h] * LOG2E))
    outs = []
    for h in range(N_ATTN_HEADS):
        kv = h // grp
        p = jnp.exp2(s_ref[h] - ms[h])
        l = jnp.sum(p, axis=0, keepdims=True) + jnp.exp2(sink_ref[h] * LOG2E - ms[h])
        outs.append(_bdot(vv_t[kv * ATTN_HEAD_DIM:(kv + 1) * ATTN_HEAD_DIM, :], p) / l)
    for t in range(N_ATTN_HEADS // 2):
        pair = jnp.concatenate([outs[2 * t], outs[2 * t + 1]], axis=0)
        o_ref[0, :, t * LANES:(t + 1) * LANES] = pair.T.astype(o_ref.dtype)


def _swa_attention(proj, sinks, bias):
    bsz, s, _ = proj.shape
    blk = SWA_BLOCK
    kvw = SWA_KV_HEADS * ATTN_HEAD_DIM
    prev = lambda col: pl.BlockSpec((1, blk, kvw), lambda b_, n: (b_, jnp.maximum(n - 1, 0), col))
    cur = lambda col: pl.BlockSpec((1, blk, kvw), lambda b_, n: (b_, n, col))
    return pl.pallas_call(
        _swa_kernel,
        grid=(bsz, s // blk),
        in_specs=[pl.BlockSpec(memory_space=pltpu.SMEM),
                  pl.BlockSpec((1, blk, ATTN_WIDTH), lambda b_, n: (b_, n, AB_Q // ATTN_WIDTH)),
                  prev(AB_K // kvw), cur(AB_K // kvw), prev(AB_V // kvw), cur(AB_V // kvw),
                  pl.BlockSpec((N_ATTN_HEADS, 1, 2 * blk, blk), lambda b_, n: (0, 0, 0, 0))],
        out_specs=pl.BlockSpec((1, blk, ATTN_WIDTH), lambda b_, n: (b_, n, 0)),
        out_shape=jax.ShapeDtypeStruct((bsz, s, ATTN_WIDTH), BF16),
        scratch_shapes=[pltpu.VMEM((N_ATTN_HEADS, 2 * blk, blk), F32)],
        compiler_params=_cparams(("arbitrary", "arbitrary")),
        name="swa_attention",
    )(sinks, proj, proj, proj, proj, proj, bias)


def _moba_kernel(q_ref, k_ref, v_ref, bias_ref, o_ref, vt_ref, qm_ref, negrow_ref, s_ref, *, nblk):
    mb = MOBA_BLOCK
    dh = ATTN_HEAD_DIM
    own = pl.program_id(2)
    scale = dh ** -0.5

    @pl.when(own == 0)
    def _():
        means = []
        for j in range(nblk):
            vt_ref[j] = v_ref[0, j * mb:(j + 1) * mb, :].astype(F32).T.astype(BF16)
            means.append(jnp.mean(k_ref[0, j * mb:(j + 1) * mb, :].astype(F32), axis=0, keepdims=True))
        kmean = jnp.concatenate(means, axis=0)
        lane_half = lax.broadcasted_iota(jnp.int32, (1, LANES), 1) >> 6
        blk_id = lax.broadcasted_iota(jnp.int32, (nblk, mb), 0)
        for qb in range(nblk):
            q = q_ref[0, qb * mb:(qb + 1) * mb, :].astype(F32) * (scale * LOG2E)
            for hh in range(2):
                qm = jnp.where(lane_half == hh, q, 0.0)
                qm_ref[hh, qb] = qm.astype(BF16)
                gate = lax.dot_general(kmean, qm, (((1,), (1,)), ((), ())),
                                       preferred_element_type=F32, precision=HIGHEST)
                gate = jnp.where(blk_id < qb, gate, NEG)
                rank = jnp.zeros((nblk, mb), jnp.int32)
                for i in range(nblk):
                    gi = gate[i:i + 1, :]
                    ahead = (gi > gate) | ((gi == gate) & (i < blk_id))
                    rank = rank + jnp.where(ahead, 1, 0)
                keep = ((rank < MOBA_TOPK) & (blk_id < qb)) | (blk_id == qb)
                negrow_ref[hh, qb] = jnp.where(keep, 0.0, NEG).astype(F32)

    qms = [qm_ref[hh, own] for hh in range(2)]
    negrows = [negrow_ref[hh, own] for hh in range(2)]

    def attend(nb):
        m8 = [None, None]
        l8 = [None, None]
        acc = [None, None]

        def scores(hh, j):
            s_t = _bdot_nt(k_ref[0, j * mb:(j + 1) * mb, :], qms[hh]) + bias_ref[hh, jnp.maximum(own - j, 0)]
            s_t = s_t + negrows[hh][j:j + 1, :]
            s_ref[hh, j] = s_t
            m_j = jnp.max(s_t.reshape(mb // SUBLANES, SUBLANES, mb), axis=0)
            m8[hh] = m_j if m8[hh] is None else jnp.maximum(m8[hh], m_j)

        def weights(hh, j, m):
            p = jnp.exp2(s_ref[hh, j] - m)
            l_j = jnp.sum(p.reshape(mb // SUBLANES, SUBLANES, mb), axis=0)
            a_j = _bdot(vt_ref[j, hh * dh:(hh + 1) * dh, :], p)
            l8[hh] = l_j if l8[hh] is None else l8[hh] + l_j
            acc[hh] = a_j if acc[hh] is None else acc[hh] + a_j

        for j in range(nb):
            for hh in range(2):
                scores(hh, j)
        m = [jnp.max(m8[hh], axis=0, keepdims=True) for hh in range(2)]
        for j in range(nb):
            for hh in range(2):
                weights(hh, j, m[hh])
        out_t = jnp.concatenate([acc[hh] / jnp.sum(l8[hh], axis=0, keepdims=True) for hh in range(2)], axis=0)
        o_ref[0] = out_t.T.astype(o_ref.dtype)

    for nb in range(2, nblk + 1, 2):
        pl.when((own >= nb - 2) & (own < nb))(functools.partial(attend, nb))


def _moba_attention(proj, bias):
    bsz, s, _ = proj.shape
    mb = MOBA_BLOCK
    nblk = s // mb
    npair = N_ATTN_HEADS // 2
    return pl.pallas_call(
        functools.partial(_moba_kernel, nblk=nblk),
        grid=(bsz, npair, nblk),
        in_specs=[pl.BlockSpec((1, s, LANES), lambda b_, p, i: (b_, 0, CD_QD // LANES + p)),
                  pl.BlockSpec((1, s, LANES), lambda b_, p, i: (b_, 0, CD_KD // LANES + p)),
                  pl.BlockSpec((1, s, LANES), lambda b_, p, i: (b_, 0, CD_VD // LANES + p)),
                  pl.BlockSpec((2, nblk, mb, mb), lambda b_, p, i: (p, 0, 0, 0))],
        out_specs=pl.BlockSpec((1, mb, LANES), lambda b_, p, i: (b_, i, p)),
        out_shape=jax.ShapeDtypeStruct((bsz, s, ATTN_WIDTH), BF16),
        scratch_shapes=[pltpu.VMEM((nblk, LANES, mb), BF16),
                        pltpu.VMEM((2, nblk, mb, LANES), BF16),
                        pltpu.VMEM((2, nblk, nblk, mb), F32),
                        pltpu.VMEM((2, nblk, mb, mb), F32)],
        compiler_params=_cparams(("arbitrary", "arbitrary", "arbitrary")),
        name="moba_attention",
    )(proj, proj, proj, bias)


def _gdn_kernel(qkv_ref, z_ref, bac_ref, bar_ref, bias_c_ref, bias_r_ref, alog_c_ref, alog_r_ref,
                nw_ref, o_ref, state_ref, m_ref, attn_ref, t_ref, x_ref, rhs_ref, u_ref, wq_ref,
                kdt_ref, r_ref, glc_ref):
    t = 2 * GDN_CHUNK
    ck = GDN_CHUNK
    dk = GDN_HEAD_DIM
    nh = GDN_HEADS

    @pl.when(pl.program_id(1) == 0)
    def _():
        state_ref[...] = jnp.zeros(state_ref.shape, F32)

    row = lax.broadcasted_iota(jnp.int32, (t, t), 0)
    col = lax.broadcasted_iota(jnp.int32, (t, t), 1)
    same = (row >> 6) == (col >> 6)
    tril = same & (row >= col)
    strict = same & (row > col)
    tri_b = jnp.where(tril, 1.0, 0.0).astype(BF16)
    triu_b = jnp.where(same & (row <= col), 1.0, 0.0).astype(BF16)
    blk_b = jnp.where(same, 1.0, 0.0).astype(BF16)
    eye_f = jnp.where(row == col, 1.0, 0.0).astype(F32)
    merge_masks = [((row >> (l + 1)) == (col >> (l + 1))) & (((row >> l) & 1) == 1) & (((col >> l) & 1) == 0)
                   for l in range(int(math.log2(ck)))]
    mask_bf = [jnp.where(m, 1.0, 0.0).astype(BF16) for m in merge_masks[1:]]

    ba_c = bac_ref[0].astype(F32)
    g_c = -jnp.exp(alog_c_ref[...]) * _softplus(ba_c + bias_c_ref[...])
    sums_c = _sum01_left(jnp.concatenate([tri_b, blk_b], axis=0), g_c)
    gc_c = sums_c[:t]
    gl_c = sums_c[t:]
    g_r = -jnp.exp(alog_r_ref[...]) * _softplus(bar_ref[0] + bias_r_ref[...])
    gc_r = _sum01_right(g_r, triu_b)
    glc_ref[...] = gl_c

    for h in range(nh):
        q = qkv_ref[0, :, h * dk:(h + 1) * dk].astype(F32)
        k = qkv_ref[0, :, (nh + h) * dk:(nh + h + 1) * dk].astype(F32)
        v = qkv_ref[0, :, (2 * nh + h) * dk:(2 * nh + h + 1) * dk].astype(F32)
        qn = q * lax.rsqrt(jnp.sum(q * q, axis=-1, keepdims=True) + 1e-6) * (dk ** -0.5)
        kn = k * lax.rsqrt(jnp.sum(k * k, axis=-1, keepdims=True) + 1e-6)
        beta = _sigmoid(ba_c[:, h:h + 1])
        gcc = gc_c[:, nh + h:nh + h + 1]
        gcr = gc_r[nh + h:nh + h + 1, :]
        glc = gl_c[:, nh + h:nh + h + 1]
        decay = jnp.where(tril, jnp.exp(jnp.where(tril, gcc - gcr, 0.0)), 0.0)
        kb = kn * beta
        kk = _bdot_nt(jnp.concatenate([kb, qn], axis=0), kn)
        mm = jnp.where(strict, kk[:t] * decay, 0.0)
        m_ref[h] = mm.astype(BF16)
        attn_ref[h] = (kk[t:] * decay).astype(BF16)
        t_ref[h] = eye_f - jnp.where(merge_masks[0], mm, 0.0)
        egc = jnp.exp(gcc)
        rhs_ref[h] = jnp.concatenate([v * beta, kb * egc], axis=1).astype(BF16)
        q_dec = (qn * egc).astype(BF16)
        for a in range(2):
            wq_ref[h, a, ck:, :] = q_dec[a * ck:(a + 1) * ck]
        kdt_ref[h] = (kn * jnp.exp(glc - gcc)).T.astype(BF16)

    for lvl in range(len(mask_bf)):
        for h in range(nh):
            x_ref[h] = jnp.dot(t_ref[h].astype(BF16), m_ref[h] * mask_bf[lvl],
                               preferred_element_type=F32).astype(BF16)
        for h in range(nh):
            t_h = t_ref[h]
            t_ref[h] = t_h - jnp.dot(x_ref[h], t_h.astype(BF16), preferred_element_type=F32)

    for h in range(nh):
        sol = jnp.dot(t_ref[h].astype(BF16), rhs_ref[h], preferred_element_type=F32)
        u_ref[h] = sol[:, :dk]
        for a in range(2):
            wq_ref[h, a, :ck, :] = sol[a * ck:(a + 1) * ck, dk:].astype(BF16)

    zeros_half = jnp.zeros((ck, dk), F32)
    for a in range(2):
        sl = slice(a * ck, (a + 1) * ck)
        for h in range(nh):
            r_ref[h] = jnp.dot(wq_ref[h, a], state_ref[h].astype(BF16), preferred_element_type=F32)
        for h in range(nh):
            v_new = u_ref[h, sl, :] - r_ref[h, :ck, :]
            v_full = jnp.concatenate([v_new, zeros_half] if a == 0 else [zeros_half, v_new], axis=0).astype(BF16)
            o = r_ref[h, ck:, :] + jnp.dot(attn_ref[h, sl, :], v_full, preferred_element_type=F32)
            gl = glc_ref[a * ck:a * ck + 1, nh + h:nh + h + 1]
            state_ref[h] = state_ref[h] * jnp.exp(gl) + jnp.dot(kdt_ref[h], v_full, preferred_element_type=F32)
            ms = jnp.mean(o * o, axis=-1, keepdims=True)
            y = o * lax.rsqrt(ms + RMS_EPS) * nw_ref[...] * _silu(z_ref[0, sl, h * dk:(h + 1) * dk].astype(F32))
            o_ref[0, sl, h * dk:(h + 1) * dk] = y.astype(o_ref.dtype)


def _gdn_mixer(qkv_act, proj, ba_rows, dt_bias, a_log, norm_w):
    bsz, s, _ = proj.shape
    t = 2 * GDN_CHUNK
    nh = GDN_HEADS
    dk = GDN_HEAD_DIM
    bias_c = jnp.pad(dt_bias, (nh, LANES - 2 * nh)).reshape(1, LANES)
    alog_c = jnp.pad(a_log, (nh, LANES - 2 * nh)).reshape(1, LANES)
    small = lambda shape: pl.BlockSpec(shape, lambda b_, c: (0, 0))
    return pl.pallas_call(
        _gdn_kernel,
        grid=(bsz, s // t),
        in_specs=[pl.BlockSpec((1, t, 3 * GDN_INNER), lambda b_, c: (b_, c, 0)),
                  pl.BlockSpec((1, t, GDN_INNER), lambda b_, c: (b_, c, CD_Z // GDN_INNER)),
                  pl.BlockSpec((1, t, LANES), lambda b_, c: (b_, c, CD_BA // LANES)),
                  pl.BlockSpec((1, LANES, t), lambda b_, c: (b_, 0, c)),
                  small((1, LANES)), small((LANES, 1)), small((1, LANES)), small((LANES, 1)),
                  small((1, dk))],
        out_specs=pl.BlockSpec((1, t, GDN_INNER), lambda b_, c: (b_, c, 0)),
        out_shape=jax.ShapeDtypeStruct((bsz, s, GDN_INNER), BF16),
        scratch_shapes=[pltpu.VMEM((nh, dk, dk), F32),
                        pltpu.VMEM((nh, t, t), BF16),
                        pltpu.VMEM((nh, t, t), BF16),
                        pltpu.VMEM((nh, t, t), F32),
                        pltpu.VMEM((nh, t, t), BF16),
                        pltpu.VMEM((nh, t, 2 * dk), BF16),
                        pltpu.VMEM((nh, t, dk), F32),
                        pltpu.VMEM((nh, 2, t, dk), BF16),
                        pltpu.VMEM((nh, dk, t), BF16),
                        pltpu.VMEM((nh, t, dk), F32),
                        pltpu.VMEM((t, LANES), F32)],
        compiler_params=_cparams(("arbitrary", "arbitrary")),
        name="gdn_mixer",
    )(qkv_act, proj, proj, ba_rows, bias_c, bias_c.reshape(LANES, 1), alog_c, alog_c.reshape(LANES, 1),
      norm_w.reshape(1, dk))


def _pad_cols(w, n):
    return jnp.pad(w, ((0, 0), (0, n - w.shape[1])))


def _gate_rows(proj, col0):
    return jnp.swapaxes(proj[:, :, col0:col0 + LANES].astype(F32), 1, 2)


def kernel(x, c, rel_bias, norm_w, ada_w, ada_b, ab_w_in, ab_w_out, ssd_conv_w, ssd_conv_b,
           ssd_dt_bias, ssd_a_log, ssd_d, ssd_norm_w, swa_sinks, cd_w_in, cd_w_out, gdn_conv_w,
           gdn_dt_bias, gdn_a_log, gdn_norm_w, ffn_w_up, ffn_conv_w, ffn_conv_b, ffn_w_down):
    bsz, s, d = x.shape
    depth = norm_w.shape[0]
    mods = _mods(c, ada_w, ada_b)
    swa_bias = _bias_tiles(rel_bias, _swa_bucket_idx())
    moba_bias = _bias_tiles(rel_bias, _moba_bucket_idx(s // MOBA_BLOCK))

    for i in range(depth):
        sh_m, sc_m, g_m, sh_f, sc_f, g_f = [m.reshape(bsz, 1, d) for m in jnp.split(mods[i], 6, axis=-1)]
        j = i // 2
        if i % 2 == 0:
            w = ab_w_in[j]
            dt0 = SSD_INNER + SSD_XBC
            w_rest = jnp.concatenate([w[:, :SSD_INNER], w[:, dt0 + SSD_HEADS:], w[:, dt0:dt0 + SSD_HEADS]], axis=1)
            w_rest = _pad_cols(w_rest, AB_COLS).astype(BF16)
            xbc_act, proj = _proj_conv_act(x, norm_w[i, 0], sc_m, sh_m, w[:, SSD_INNER:dt0].astype(BF16),
                                           ssd_conv_w[j], ssd_conv_b[j], "silu", w_plain=w_rest)
            y_a = _ssd_mixer(xbc_act, proj, _gate_rows(proj, AB_DT), ssd_dt_bias[j], ssd_a_log[j], ssd_d[j],
                             ssd_norm_w[j])
            y_b = _swa_attention(proj, swa_sinks[j], swa_bias)
            x = _matmul_resid([y_a, y_b], ab_w_out, j, x, g_m, norm_w[i, 1])
        else:
            w = cd_w_in[j]
            nqkv = 3 * GDN_INNER
            ba0 = nqkv + GDN_INNER
            w_rest = jnp.concatenate([w[:, nqkv:ba0], w[:, ba0 + 2 * GDN_HEADS:], w[:, ba0:ba0 + 2 * GDN_HEADS]],
                                     axis=1)
            w_rest = _pad_cols(w_rest, CD_COLS).astype(BF16)
            qkv_act, proj = _proj_conv_act(x, norm_w[i, 0], sc_m, sh_m, w[:, :nqkv].astype(BF16), gdn_conv_w[j],
                                           jnp.zeros((nqkv,), F32), "silu", w_plain=w_rest)
            y_c = _gdn_mixer(qkv_act, proj, _gate_rows(proj, CD_BA), gdn_dt_bias[j], gdn_a_log[j], gdn_norm_w[j])
            y_d = _moba_attention(proj, moba_bias)
            x = _matmul_resid([y_c, y_d], cd_w_out, j, x, g_m, norm_w[i, 1])
        act = _proj_conv_act(x, norm_w[i, 2], sc_f, sh_f, ffn_w_up[i].astype(BF16), ffn_conv_w[i], ffn_conv_b[i],
                             "geglu")
        x = _matmul_resid([act], ffn_w_down, i, x, g_f, norm_w[i, 3])
    return x
```

```python
import functools
import math

import numpy as np
import jax
import jax.numpy as jnp
from jax import lax
from jax.experimental import pallas as pl
from jax.experimental.pallas import tpu as pltpu

F32 = jnp.float32
BF16 = jnp.bfloat16
HIGHEST = lax.Precision.HIGHEST

D_MODEL = 1024
RMS_EPS = 1e-6
NEG = -1e30
LANES = 128
SUBLANES = 8
N_ATTN_HEADS = 8
ATTN_HEAD_DIM = 64
ATTN_WIDTH = N_ATTN_HEADS * ATTN_HEAD_DIM
REL_BUCKETS = 32
REL_MAX_DIST = 1024
SSD_HEADS = 24
SSD_HEAD_DIM = 64
SSD_INNER = SSD_HEADS * SSD_HEAD_DIM
SSD_GROUPS = 4
SSD_STATE = 128
SSD_CONV = 4
SSD_CHUNK = 128
SSD_XBC = SSD_INNER + 2 * SSD_GROUPS * SSD_STATE
SWA_KV_HEADS = 2
SWA_BLOCK = 128
GDN_HEADS = 12
GDN_HEAD_DIM = 128
GDN_INNER = GDN_HEADS * GDN_HEAD_DIM
GDN_CONV = 4
GDN_CHUNK = 64
MOBA_BLOCK = 256
MOBA_TOPK = 3
FFN_DIM = 2816
FFN_CONV = 3

AB_Z, AB_Q, AB_K, AB_V, AB_DT, AB_COLS = 0, 1536, 2048, 2176, 2304, 2560
CD_Z, CD_QD, CD_KD, CD_VD, CD_BA, CD_COLS = 0, 1536, 2048, 2560, 3072, 3328

VMEM_LIMIT = 48 * 1024 * 1024
HALO = 16


def _cparams(sem):
    return pltpu.CompilerParams(dimension_semantics=sem, vmem_limit_bytes=VMEM_LIMIT)


def _bdot(a, b):
    return jnp.dot(a.astype(BF16), b.astype(BF16), preferred_element_type=F32)


def _bdot_nt(a, b):
    return lax.dot_general(a.astype(BF16), b.astype(BF16), (((1,), (1,)), ((), ())),
                           preferred_element_type=F32)


def _split3(x):
    hi = x.astype(BF16)
    r1 = x - hi.astype(F32)
    mid = r1.astype(BF16)
    lo = (r1 - mid.astype(F32)).astype(BF16)
    return hi, mid, lo


def _sum01_left(m01, x):
    n = x.shape[1]
    y = jnp.dot(m01, jnp.concatenate(_split3(x), axis=1), preferred_element_type=F32)
    return y[:, :n] + y[:, n:2 * n] + y[:, 2 * n:]


def _sum01_right(x, m01):
    n = x.shape[0]
    y = jnp.dot(jnp.concatenate(_split3(x), axis=0), m01, preferred_element_type=F32)
    return y[:n] + y[n:2 * n] + y[2 * n:]


def _softplus(x):
    return jnp.maximum(x, 0.0) + jnp.log(1.0 + jnp.exp(-jnp.abs(x)))


def _sigmoid(x):
    return 1.0 / (1.0 + jnp.exp(-x))


def _silu(x):
    return x * _sigmoid(x)


def _window_conv(win_ref, lo, ncols, rows, w, width, row0=0, shifted_loads=False):
    if shifted_loads:
        acc = None
        for s in range(width):
            tap = win_ref[pl.ds(HALO + row0 - s, rows), lo:lo + ncols] * w[width - 1 - s:width - s, :]
            acc = tap if acc is None else acc + tap
        return acc
    xin = win_ref[pl.ds(HALO + row0 - SUBLANES, rows + SUBLANES), lo:lo + ncols]
    acc = xin[SUBLANES:, :] * w[width - 1:width, :]
    for s in range(1, width):
        acc = acc + pltpu.roll(xin, s, axis=0)[SUBLANES:, :] * w[width - 1 - s:width - s, :]
    return acc


def _mods_kernel(c_ref, w_ref, b_ref, o_ref):
    o_ref[0] = _bdot(_silu(c_ref[...]), w_ref[0]) + b_ref[0]


def _mods(c, ada_w, ada_b):
    depth, d, n = ada_w.shape
    bsz = c.shape[0]
    tn = 512
    return pl.pallas_call(
        _mods_kernel,
        grid=(depth, n // tn),
        in_specs=[pl.BlockSpec((bsz, d), lambda l, j: (0, 0)),
                  pl.BlockSpec((1, d, tn), lambda l, j: (l, 0, j)),
                  pl.BlockSpec((1, 1, tn), lambda l, j: (l, 0, j))],
        out_specs=pl.BlockSpec((1, bsz, tn), lambda l, j: (l, 0, j)),
        out_shape=jax.ShapeDtypeStruct((depth, bsz, n), F32),
        compiler_params=_cparams(("arbitrary", "arbitrary")),
        name="adaln_mods",
    )(c, ada_w, ada_b.reshape(depth, 1, n))


def _modulated_norm(x, nw, sc, sh):
    ms = jnp.mean(x * x, axis=-1, keepdims=True)
    return x * lax.rsqrt(ms + RMS_EPS) * nw * (1.0 + sc) + sh


NMM_ROWS = 256


def _nmm_kernel(x_ref, nw_ref, sc_ref, sh_ref, w_ref, o_ref, h_ref):
    tm = x_ref.shape[1]

    @pl.when(pl.program_id(1) == 0)
    def _():
        for r0 in range(0, tm, NMM_ROWS):
            h = _modulated_norm(x_ref[0, r0:r0 + NMM_ROWS, :], nw_ref[...], sc_ref[0], sh_ref[0]).astype(BF16)
            h_ref[r0:r0 + NMM_ROWS, :] = h
            o_ref[0, r0:r0 + NMM_ROWS, :] = jnp.dot(h, w_ref[...], preferred_element_type=F32).astype(o_ref.dtype)

    @pl.when(pl.program_id(1) > 0)
    def _():
        o_ref[0] = jnp.dot(h_ref[...], w_ref[...], preferred_element_type=F32).astype(o_ref.dtype)


def _norm_mod_matmul(x, nw, sc, sh, w, tn=1024, tm=1024):
    bsz, s, d = x.shape
    n = w.shape[1]
    spt = s // tm
    return pl.pallas_call(
        _nmm_kernel,
        grid=(bsz * spt, n // tn),
        in_specs=[pl.BlockSpec((1, tm, d), lambda i, j: (i // spt, i % spt, 0)),
                  pl.BlockSpec((1, d), lambda i, j: (0, 0)),
                  pl.BlockSpec((1, 1, d), lambda i, j: (i // spt, 0, 0)),
                  pl.BlockSpec((1, 1, d), lambda i, j: (i // spt, 0, 0)),
                  pl.BlockSpec((d, tn), lambda i, j: (0, j))],
        out_specs=pl.BlockSpec((1, tm, tn), lambda i, j: (i // spt, i % spt, j)),
        out_shape=jax.ShapeDtypeStruct((bsz, s, n), BF16),
        scratch_shapes=[pltpu.VMEM((tm, d), BF16)],
        compiler_params=_cparams(("arbitrary", "arbitrary")),
        name="norm_mod_matmul",
    )(x, nw.reshape(1, d), sc, sh, w)


def _mmres_kernel(*refs, splits):
    na = len(splits)
    a_refs = refs[:na]
    w_ref, x_ref, g_ref, nw_ref, o_ref = refs[na:]
    acc = None
    lo = 0
    for a_ref, k in zip(a_refs, splits):
        part = jnp.dot(a_ref[0].astype(BF16), w_ref[lo:lo + k, :].astype(BF16), preferred_element_type=F32)
        acc = part if acc is None else acc + part
        lo += k
    ms = jnp.mean(acc * acc, axis=-1, keepdims=True)
    y = acc * lax.rsqrt(ms + RMS_EPS) * nw_ref[...]
    o_ref[0] = x_ref[0] + g_ref[0] * y


def _matmul_resid(a_list, w_stack, layer, x, gate, nw, tm=512):
    bsz, s, d = x.shape
    spt = s // tm
    splits = tuple(a.shape[-1] for a in a_list)
    ktot = sum(splits)
    in_specs = [pl.BlockSpec((1, tm, k), lambda i: (i // spt, i % spt, 0)) for k in splits]
    in_specs += [pl.BlockSpec((None, ktot, d), lambda i: (layer, 0, 0)),
                 pl.BlockSpec((1, tm, d), lambda i: (i // spt, i % spt, 0)),
                 pl.BlockSpec((1, 1, d), lambda i: (i // spt, 0, 0)),
                 pl.BlockSpec((1, d), lambda i: (0, 0))]
    return pl.pallas_call(
        functools.partial(_mmres_kernel, splits=splits),
        grid=(bsz * spt,),
        in_specs=in_specs,
        out_specs=pl.BlockSpec((1, tm, d), lambda i: (i // spt, i % spt, 0)),
        out_shape=jax.ShapeDtypeStruct((bsz, s, d), F32),
        compiler_params=_cparams(("arbitrary",)),
        name="matmul_resid",
    )(*a_list, w_stack, x, gate, nw.reshape(1, d))


def _proj_conv_kernel(*refs, offs, tc, width, rb, act, n_plain):
    if n_plain:
        xh_ref, x_ref, nw_ref, sc_ref, sh_ref, w_ref, cw_ref, cb_ref, o_ref, op_ref, h_ref, u_ref = refs
    else:
        xh_ref, x_ref, nw_ref, sc_ref, sh_ref, w_ref, cw_ref, cb_ref, o_ref, h_ref, u_ref = refs
    tm = x_ref.shape[1]
    nchunk = o_ref.shape[2] // tc
    nplain = n_plain // tc
    plain0 = w_ref.shape[1] - n_plain
    h_halo = _modulated_norm(xh_ref[0], nw_ref[...], sc_ref[0], sh_ref[0])
    h_ref[:HALO, :] = jnp.where(pl.program_id(1) > 0, h_halo, 0.0).astype(BF16)
    h_ref[HALO:, :] = _modulated_norm(x_ref[0], nw_ref[...], sc_ref[0], sh_ref[0]).astype(BF16)
    c0 = math.sqrt(2.0 / math.pi)

    def matmuls(c):
        for k, off in enumerate(offs):
            lo = off + c * tc
            u_ref[c % 2, k] = jnp.dot(h_ref[...], w_ref[:, lo:lo + tc], preferred_element_type=F32)

    def epilogue(c):
        for r0 in range(0, tm, rb):
            conv = []
            for k, off in enumerate(offs):
                lo = off + c * tc
                conv.append(_window_conv(u_ref.at[c % 2, k], 0, tc, rb, cw_ref[:, lo:lo + tc], width, row0=r0,
                                         shifted_loads=True) + cb_ref[:, lo:lo + tc])
            if act == "geglu":
                g, v = conv
                th = jnp.tanh(g * (c0 + (c0 * 0.044715) * (g * g)))
                hg = 0.5 * g
                out = (hg + hg * th) * v
            else:
                out = _silu(conv[0])
            o_ref[0, r0:r0 + rb, c * tc:(c + 1) * tc] = out.astype(o_ref.dtype)

    def plain(p):
        lo = plain0 + p * tc
        op_ref[0, :, p * tc:(p + 1) * tc] = jnp.dot(
            h_ref[HALO:, :], w_ref[:, lo:lo + tc], preferred_element_type=F32).astype(op_ref.dtype)

    matmuls(0)
    done = 0
    for c in range(nchunk):
        if c + 1 < nchunk:
            matmuls(c + 1)
        epilogue(c)
        upto = (c + 1) * nplain // nchunk
        for p in range(done, upto):
            plain(p)
        done = upto


def _proj_conv_act(x, nw, sc, sh, w, cw, cb, act, n_plain=0, tm=512, tc=256, rb=128):
    bsz, s, d = x.shape
    n2 = w.shape[1] - n_plain
    f = n2 // 2 if act == "geglu" else n2
    offs = (0, f) if act == "geglu" else (0,)
    width = cw.shape[0]
    hb = tm // HALO
    const = lambda shape: pl.BlockSpec(shape, lambda b_, r: (0, 0))
    rows = lambda n: pl.BlockSpec((1, tm, n), lambda b_, r: (b_, r, 0))
    in_specs = [pl.BlockSpec((1, HALO, d), lambda b_, r: (b_, jnp.maximum(r * hb - 1, 0), 0)),
                rows(d), const((1, d)),
                pl.BlockSpec((1, 1, d), lambda b_, r: (b_, 0, 0)),
                pl.BlockSpec((1, 1, d), lambda b_, r: (b_, 0, 0)),
                const((d, n2 + n_plain)), const((width, n2)), const((1, n2))]
    args = [x, x, nw.reshape(1, d), sc, sh, w, cw, cb.reshape(1, n2)]
    out_specs = rows(f)
    out_shape = jax.ShapeDtypeStruct((bsz, s, f), BF16)
    if n_plain:
        out_specs = (out_specs, rows(n_plain))
        out_shape = (out_shape, jax.ShapeDtypeStruct((bsz, s, n_plain), BF16))
    return pl.pallas_call(
        functools.partial(_proj_conv_kernel, offs=offs, tc=tc, width=width, rb=rb, act=act, n_plain=n_plain),
        grid=(bsz, s // tm),
        in_specs=in_specs,
        out_specs=out_specs,
        out_shape=out_shape,
        scratch_shapes=[pltpu.VMEM((HALO + tm, d), BF16),
                        pltpu.VMEM((2, len(offs), HALO + tm, tc), F32)],
        compiler_params=_cparams(("arbitrary", "arbitrary")),
        name="proj_conv_" + act,
    )(*args)


def _rel_bucket_np(d):
    max_exact = REL_BUCKETS // 2
    d = np.maximum(d, 0)
    df = np.maximum(d, 1).astype(np.float64)
    large = max_exact + (np.log(df / max_exact) / math.log(REL_MAX_DIST / max_exact)
                         * (REL_BUCKETS - max_exact)).astype(np.int32)
    large = np.minimum(large, REL_BUCKETS - 1)
    return np.where(d < max_exact, d, large).astype(np.int32)


LOG2E = math.log2(math.e)


def _bias_kernel(tab_ref, idx_ref, o_ref, *, ranges):
    h = pl.program_id(0)
    for t, (lo, hi) in enumerate(ranges):
        idx = idx_ref[t]
        acc = jnp.full(idx.shape, NEG, F32)
        for bkt in range(lo, hi + 1):
            acc = jnp.where(idx == bkt, tab_ref[bkt, h] * LOG2E, acc)
        o_ref[0, t] = acc


def _bias_tiles(rel_bias, idx_np):
    t, r, c = idx_np.shape
    ranges = tuple((int(tile[tile >= 0].min()), int(tile.max())) for tile in idx_np)
    return pl.pallas_call(
        functools.partial(_bias_kernel, ranges=ranges),
        grid=(N_ATTN_HEADS,),
        in_specs=[pl.BlockSpec(memory_space=pltpu.SMEM),
                  pl.BlockSpec((t, r, c), lambda h: (0, 0, 0))],
        out_specs=pl.BlockSpec((1, t, r, c), lambda h: (h, 0, 0, 0)),
        out_shape=jax.ShapeDtypeStruct((N_ATTN_HEADS, t, r, c), F32),
        compiler_params=_cparams(("arbitrary",)),
        name="rel_bias_tiles",
    )(rel_bias, jnp.asarray(idx_np))


def _swa_bucket_idx():
    c = np.arange(2 * SWA_BLOCK)[:, None]
    r = np.arange(SWA_BLOCK)[None, :]
    dist = SWA_BLOCK + r - c
    return np.where((dist >= 0) & (dist < SWA_BLOCK), _rel_bucket_np(dist), -1).astype(np.int32)[None]


def _moba_bucket_idx(nblk):
    c = np.arange(MOBA_BLOCK)[:, None]
    r = np.arange(MOBA_BLOCK)[None, :]
    tiles = [np.where(m * MOBA_BLOCK + r - c >= 0, _rel_bucket_np(m * MOBA_BLOCK + r - c), -1)
             for m in range(nblk)]
    return np.stack(tiles).astype(np.int32)


def _expand_heads(v, e):
    hi = v.astype(BF16)
    lo = (v - hi.astype(F32)).astype(BF16)
    return (jnp.dot(hi, e, preferred_element_type=F32) + jnp.dot(lo, e, preferred_element_type=F32))


def _ssd_kernel(xbc_ref, z_ref, dtc_ref, dtr_ref, bias_c_ref, bias_r_ref, alog_c_ref, alog_r_ref,
                dskip_ref, nw_ref, e_ref, o_ref, state_ref):
    q = SSD_CHUNK
    gw = SSD_INNER // SSD_GROUPS
    hpg = SSD_HEADS // SSD_GROUPS

    @pl.when(pl.program_id(1) == 0)
    def _():
        state_ref[...] = jnp.zeros(state_ref.shape, F32)

    row = lax.broadcasted_iota(jnp.int32, (q, q), 0)
    col = lax.broadcasted_iota(jnp.int32, (q, q), 1)
    tril = row >= col
    tri_b = jnp.where(tril, 1.0, 0.0).astype(BF16)
    triu_b = jnp.where(row <= col, 1.0, 0.0).astype(BF16)

    dt_c = _softplus(dtc_ref[0].astype(F32) + bias_c_ref[...])
    da_c = dt_c * (-jnp.exp(alog_c_ref[...]))
    acs_c = _sum01_left(tri_b, da_c)
    dt_r = _softplus(dtr_ref[0] + bias_r_ref[...])
    da_r = dt_r * (-jnp.exp(alog_r_ref[...]))
    acs_r = _sum01_right(da_r, triu_b)

    acs_last = acs_c[q - 1:q, :]
    e = e_ref[...]
    dt_full = _expand_heads(dt_c, e)
    dtdec_full = _expand_heads(dt_c * jnp.exp(acs_last - acs_c), e)
    eacs_full = _expand_heads(jnp.exp(acs_c), e)
    cdecay_full = eacs_full[q - 1:q, :]

    xbc = xbc_ref[0].astype(F32)
    xs = xbc[:, :SSD_INNER]
    xdt = xs * dt_full
    xdec = xs * dtdec_full
    lane_half = lax.broadcasted_iota(jnp.int32, (1, LANES), 1) >> 6

    y_parts = []
    for g in range(SSD_GROUPS):
        b_g = xbc[:, SSD_INNER + g * SSD_STATE:SSD_INNER + (g + 1) * SSD_STATE]
        c_g = xbc[:, SSD_INNER + SSD_GROUPS * SSD_STATE + g * SSD_STATE:
                  SSD_INNER + SSD_GROUPS * SSD_STATE + (g + 1) * SSD_STATE]
        cb = jnp.where(tril, _bdot_nt(c_g, b_g), 0.0)
        st = state_ref[g]
        y_off = _bdot(c_g, st) * eacs_full[:, g * gw:(g + 1) * gw]
        state_ref[g] = st * cdecay_full[:, g * gw:(g + 1) * gw] + _bdot(b_g.T, xdec[:, g * gw:(g + 1) * gw])
        pair_parts = []
        for pr in range(hpg // 2):
            acc = None
            lo = g * gw + pr * LANES
            x_pair = xdt[:, lo:lo + LANES]
            for half in range(2):
                h = g * hpg + pr * 2 + half
                diff = acs_c[:, h:h + 1] - acs_r[h:h + 1, :]
                lmat = jnp.exp(jnp.minimum(diff, 0.0))
                part = _bdot(cb * lmat, jnp.where(lane_half == half, x_pair, 0.0))
                acc = part if acc is None else acc + part
            pair_parts.append(acc)
        y_diag = jnp.concatenate(pair_parts, axis=1)
        y = y_diag + y_off + dskip_ref[:, g * gw:(g + 1) * gw] * xs[:, g * gw:(g + 1) * gw]
        y = y * _silu(z_ref[0, :, g * gw:(g + 1) * gw].astype(F32))
        ms = jnp.mean(y * y, axis=-1, keepdims=True)
        y_parts.append(y * lax.rsqrt(ms + RMS_EPS) * nw_ref[:, g * gw:(g + 1) * gw])
    o_ref[0] = jnp.concatenate(y_parts, axis=1).astype(o_ref.dtype)


def _ssd_mixer(xbc_act, proj, dt_rows, dt_bias, a_log, d_skip, norm_w):
    bsz, s, _ = proj.shape
    q = SSD_CHUNK
    pad = LANES - SSD_HEADS
    bias_c = jnp.pad(dt_bias, (0, pad)).reshape(1, LANES)
    alog_c = jnp.pad(a_log, (0, pad)).reshape(1, LANES)
    e_np = np.zeros((LANES, SSD_INNER), np.float32)
    for h in range(SSD_HEADS):
        e_np[h, h * SSD_HEAD_DIM:(h + 1) * SSD_HEAD_DIM] = 1.0
    small = lambda shape: pl.BlockSpec(shape, lambda b_, c: (0, 0))
    return pl.pallas_call(
        _ssd_kernel,
        grid=(bsz, s // q),
        in_specs=[pl.BlockSpec((1, q, SSD_XBC), lambda b_, c: (b_, c, 0)),
                  pl.BlockSpec((1, q, SSD_INNER), lambda b_, c: (b_, c, AB_Z // SSD_INNER)),
                  pl.BlockSpec((1, q, LANES), lambda b_, c: (b_, c, AB_DT // LANES)),
                  pl.BlockSpec((1, LANES, q), lambda b_, c: (b_, 0, c)),
                  small((1, LANES)), small((LANES, 1)), small((1, LANES)), small((LANES, 1)),
                  small((1, SSD_INNER)), small((1, SSD_INNER)), small((LANES, SSD_INNER))],
        out_specs=pl.BlockSpec((1, q, SSD_INNER), lambda b_, c: (b_, c, 0)),
        out_shape=jax.ShapeDtypeStruct((bsz, s, SSD_INNER), BF16),
        scratch_shapes=[pltpu.VMEM((SSD_GROUPS, SSD_STATE, SSD_INNER // SSD_GROUPS), F32)],
        compiler_params=_cparams(("arbitrary", "arbitrary")),
        name="ssd_mixer",
    )(xbc_act, proj, proj, dt_rows, bias_c, bias_c.reshape(LANES, 1),
      alog_c, alog_c.reshape(LANES, 1), jnp.repeat(d_skip, SSD_HEAD_DIM).reshape(1, SSD_INNER),
      norm_w.reshape(1, SSD_INNER), jnp.asarray(e_np, BF16))


def _swa_kernel(sink_ref, q_ref, kp_ref, kc_ref, vp_ref, vc_ref, bias_ref, o_ref, s_ref):
    blk = SWA_BLOCK
    n = pl.program_id(1)
    grp = N_ATTN_HEADS // SWA_KV_HEADS
    scale = ATTN_HEAD_DIM ** -0.5
    kk = jnp.concatenate([kp_ref[0], kc_ref[0]], axis=0).astype(F32) * (scale * LOG2E)
    vv_t = jnp.concatenate([vp_ref[0], vc_ref[0]], axis=0).astype(F32).T
    lane_half = lax.broadcasted_iota(jnp.int32, (1, LANES), 1) >> 6
    c = lax.broadcasted_iota(jnp.int32, (2 * blk, blk), 0)
    valid = (c >= blk) | (n > 0)
    ms = []
    for kv in range(SWA_KV_HEADS):
        k_own = jnp.where(lane_half == kv, kk, 0.0).astype(BF16)
        k_var = [None, None]
        k_var[kv] = k_own
        k_var[1 - kv] = pltpu.roll(jnp.where(lane_half == kv, kk, 0.0), ATTN_HEAD_DIM, axis=1).astype(BF16)
        for gq in range(grp):
            h = kv * grp + gq
            q_tile = q_ref[0, :, (h // 2) * LANES:(h // 2 + 1) * LANES]
            s_t = _bdot_nt(k_var[h % 2], q_tile) + bias_ref[h, 0]
            s_t = jnp.where(valid, s_t, NEG)
            s_ref[h] = s_t
            ms.append(jnp.maximum(jnp.max(s_t, axis=0, keepdims=True), sink_ref[h] * LOG2E))
    outs = []
    for h in range(N_ATTN_HEADS):
        kv = h // grp
        p = jnp.exp2(s_ref[h] - ms[h])
        l = jnp.sum(p, axis=0, keepdims=True) + jnp.exp2(sink_ref[h] * LOG2E - ms[h])
        outs.append(_bdot(vv_t[kv * ATTN_HEAD_DIM:(kv + 1) * ATTN_HEAD_DIM, :], p) / l)
    for t in range(N_ATTN_HEADS // 2):
        pair = jnp.concatenate([outs[2 * t], outs[2 * t + 1]], axis=0)
        o_ref[0, :, t * LANES:(t + 1) * LANES] = pair.T.astype(o_ref.dtype)


def _swa_attention(proj, sinks, bias):
    bsz, s, _ = proj.shape
    blk = SWA_BLOCK
    kvw = SWA_KV_HEADS * ATTN_HEAD_DIM
    prev = lambda col: pl.BlockSpec((1, blk, kvw), lambda b_, n: (b_, jnp.maximum(n - 1, 0), col))
    cur = lambda col: pl.BlockSpec((1, blk, kvw), lambda b_, n: (b_, n, col))
    return pl.pallas_call(
        _swa_kernel,
        grid=(bsz, s // blk),
        in_specs=[pl.BlockSpec(memory_space=pltpu.SMEM),
                  pl.BlockSpec((1, blk, ATTN_WIDTH), lambda b_, n: (b_, n, AB_Q // ATTN_WIDTH)),
                  prev(AB_K // kvw), cur(AB_K // kvw), prev(AB_V // kvw), cur(AB_V // kvw),
                  pl.BlockSpec((N_ATTN_HEADS, 1, 2 * blk, blk), lambda b_, n: (0, 0, 0, 0))],
        out_specs=pl.BlockSpec((1, blk, ATTN_WIDTH), lambda b_, n: (b_, n, 0)),
        out_shape=jax.ShapeDtypeStruct((bsz, s, ATTN_WIDTH), BF16),
        scratch_shapes=[pltpu.VMEM((N_ATTN_HEADS, 2 * blk, blk), F32)],
        compiler_params=_cparams(("arbitrary", "arbitrary")),
        name="swa_attention",
    )(sinks, proj, proj, proj, proj, proj, bias)


def _moba_kernel(q_ref, k_ref, v_ref, bias_ref, o_ref, vt_ref, qm_ref, negrow_ref, s_ref, *, nblk):
    mb = MOBA_BLOCK
    dh = ATTN_HEAD_DIM
    own = pl.program_id(2)
    scale = dh ** -0.5

    @pl.when(own == 0)
    def _():
        means = []
        for j in range(nblk):
            vt_ref[j] = v_ref[0, j * mb:(j + 1) * mb, :].astype(F32).T.astype(BF16)
            means.append(jnp.mean(k_ref[0, j * mb:(j + 1) * mb, :].astype(F32), axis=0, keepdims=True))
        kmean = jnp.concatenate(means, axis=0)
        lane_half = lax.broadcasted_iota(jnp.int32, (1, LANES), 1) >> 6
        blk_id = lax.broadcasted_iota(jnp.int32, (nblk, mb), 0)
        for qb in range(nblk):
            q = q_ref[0, qb * mb:(qb + 1) * mb, :].astype(F32) * (scale * LOG2E)
            for hh in range(2):
                qm = jnp.where(lane_half == hh, q, 0.0)
                qm_ref[hh, qb] = qm.astype(BF16)
                gate = lax.dot_general(kmean, qm, (((1,), (1,)), ((), ())),
                                       preferred_element_type=F32, precision=HIGHEST)
                gate = jnp.where(blk_id < qb, gate, NEG)
                rank = jnp.zeros((nblk, mb), jnp.int32)
                for i in range(nblk):
                    gi = gate[i:i + 1, :]
                    ahead = (gi > gate) | ((gi == gate) & (i < blk_id))
                    rank = rank + jnp.where(ahead, 1, 0)
                keep = ((rank < MOBA_TOPK) & (blk_id < qb)) | (blk_id == qb)
                negrow_ref[hh, qb] = jnp.where(keep, 0.0, NEG).astype(F32)

    qms = [qm_ref[hh, own] for hh in range(2)]
    negrows = [negrow_ref[hh, own] for hh in range(2)]

    def attend(nb):
        m8 = [None, None]
        l8 = [None, None]
        acc = [None, None]

        def scores(hh, j):
            s_t = _bdot_nt(k_ref[0, j * mb:(j + 1) * mb, :], qms[hh]) + bias_ref[hh, jnp.maximum(own - j, 0)]
            s_t = s_t + negrows[hh][j:j + 1, :]
            s_ref[hh, j] = s_t
            m_j = jnp.max(s_t.reshape(mb // SUBLANES, SUBLANES, mb), axis=0)
            m8[hh] = m_j if m8[hh] is None else jnp.maximum(m8[hh], m_j)

        def weights(hh, j, m):
            p = jnp.exp2(s_ref[hh, j] - m)
            l_j = jnp.sum(p.reshape(mb // SUBLANES, SUBLANES, mb), axis=0)
            a_j = _bdot(vt_ref[j, hh * dh:(hh + 1) * dh, :], p)
            l8[hh] = l_j if l8[hh] is None else l8[hh] + l_j
            acc[hh] = a_j if acc[hh] is None else acc[hh] + a_j

        for j in range(nb):
            for hh in range(2):
                scores(hh, j)
        m = [jnp.max(m8[hh], axis=0, keepdims=True) for hh in range(2)]
        for j in range(nb):
            for hh in range(2):
                weights(hh, j, m[hh])
        out_t = jnp.concatenate([acc[hh] / jnp.sum(l8[hh], axis=0, keepdims=True) for hh in range(2)], axis=0)
        o_ref[0] = out_t.T.astype(o_ref.dtype)

    for nb in range(2, nblk + 1, 2):
        pl.when((own >= nb - 2) & (own < nb))(functools.partial(attend, nb))


def _moba_attention(proj, bias):
    bsz, s, _ = proj.shape
    mb = MOBA_BLOCK
    nblk = s // mb
    npair = N_ATTN_HEADS // 2
    return pl.pallas_call(
        functools.partial(_moba_kernel, nblk=nblk),
        grid=(bsz, npair, nblk),
        in_specs=[pl.BlockSpec((1, s, LANES), lambda b_, p, i: (b_, 0, CD_QD // LANES + p)),
                  pl.BlockSpec((1, s, LANES), lambda b_, p, i: (b_, 0, CD_KD // LANES + p)),
                  pl.BlockSpec((1, s, LANES), lambda b_, p, i: (b_, 0, CD_VD // LANES + p)),
                  pl.BlockSpec((2, nblk, mb, mb), lambda b_, p, i: (p, 0, 0, 0))],
        out_specs=pl.BlockSpec((1, mb, LANES), lambda b_, p, i: (b_, i, p)),
        out_shape=jax.ShapeDtypeStruct((bsz, s, ATTN_WIDTH), BF16),
        scratch_shapes=[pltpu.VMEM((nblk, LANES, mb), BF16),
                        pltpu.VMEM((2, nblk, mb, LANES), BF16),
                        pltpu.VMEM((2, nblk, nblk, mb), F32),
                        pltpu.VMEM((2, nblk, mb, mb), F32)],
        compiler_params=_cparams(("arbitrary", "arbitrary", "arbitrary")),
        name="moba_attention",
    )(proj, proj, proj, bias)


def _gdn_kernel(qkv_ref, z_ref, bac_ref, bar_ref, bias_c_ref, bias_r_ref, alog_c_ref, alog_r_ref,
                nw_ref, o_ref, state_ref, m_ref, attn_ref, t_ref, x_ref, rhs_ref, u_ref, wq_ref,
                kdt_ref, r_ref, glc_ref):
    t = 2 * GDN_CHUNK
    ck = GDN_CHUNK
    dk = GDN_HEAD_DIM
    nh = GDN_HEADS

    @pl.when(pl.program_id(1) == 0)
    def _():
        state_ref[...] = jnp.zeros(state_ref.shape, F32)

    row = lax.broadcasted_iota(jnp.int32, (t, t), 0)
    col = lax.broadcasted_iota(jnp.int32, (t, t), 1)
    same = (row >> 6) == (col >> 6)
    tril = same & (row >= col)
    strict = same & (row > col)
    tri_b = jnp.where(tril, 1.0, 0.0).astype(BF16)
    triu_b = jnp.where(same & (row <= col), 1.0, 0.0).astype(BF16)
    blk_b = jnp.where(same, 1.0, 0.0).astype(BF16)
    eye_f = jnp.where(row == col, 1.0, 0.0).astype(F32)
    merge_masks = [((row >> (l + 1)) == (col >> (l + 1))) & (((row >> l) & 1) == 1) & (((col >> l) & 1) == 0)
                   for l in range(int(math.log2(ck)))]
    mask_bf = [jnp.where(m, 1.0, 0.0).astype(BF16) for m in merge_masks[1:]]

    ba_c = bac_ref[0].astype(F32)
    g_c = -jnp.exp(alog_c_ref[...]) * _softplus(ba_c + bias_c_ref[...])
    sums_c = _sum01_left(jnp.concatenate([tri_b, blk_b], axis=0), g_c)
    gc_c = sums_c[:t]
    gl_c = sums_c[t:]
    g_r = -jnp.exp(alog_r_ref[...]) * _softplus(bar_ref[0] + bias_r_ref[...])
    gc_r = _sum01_right(g_r, triu_b)
    glc_ref[...] = gl_c

    def prep(h):
        q = qkv_ref[0, :, h * dk:(h + 1) * dk].astype(F32)
        k = qkv_ref[0, :, (nh + h) * dk:(nh + h + 1) * dk].astype(F32)
        v = qkv_ref[0, :, (2 * nh + h) * dk:(2 * nh + h + 1) * dk].astype(F32)
        qn = q * lax.rsqrt(jnp.sum(q * q, axis=-1, keepdims=True) + 1e-6) * (dk ** -0.5)
        kn = k * lax.rsqrt(jnp.sum(k * k, axis=-1, keepdims=True) + 1e-6)
        beta = _sigmoid(ba_c[:, h:h + 1])
        gcc = gc_c[:, nh + h:nh + h + 1]
        gcr = gc_r[nh + h:nh + h + 1, :]
        glc = gl_c[:, nh + h:nh + h + 1]
        decay = jnp.where(tril, jnp.exp(jnp.where(tril, gcc - gcr, 0.0)), 0.0)
        kb = kn * beta
        kk = _bdot_nt(jnp.concatenate([kb, qn], axis=0), kn)
        mm = jnp.where(strict, kk[:t] * decay, 0.0)
        m_ref[h] = mm.astype(BF16)
        attn_ref[h] = (kk[t:] * decay).astype(BF16)
        t_ref[h] = eye_f - jnp.where(merge_masks[0], mm, 0.0)
        egc = jnp.exp(gcc)
        rhs_ref[h] = jnp.concatenate([v * beta, kb * egc], axis=1).astype(BF16)
        q_dec = (qn * egc).astype(BF16)
        for a in range(2):
            wq_ref[h, a, ck:, :] = q_dec[a * ck:(a + 1) * ck]
        kdt_ref[h] = (kn * jnp.exp(glc - gcc)).T.astype(BF16)

    def merge_a(lvl, h):
        x_ref[h] = jnp.dot(t_ref[h].astype(BF16), m_ref[h] * mask_bf[lvl],
                           preferred_element_type=F32).astype(BF16)

    def merge_b(lvl, h):
        t_h = t_ref[h]
        t_ref[h] = t_h - jnp.dot(x_ref[h], t_h.astype(BF16), preferred_element_type=F32)

    def solve(h):
        sol = jnp.dot(t_ref[h].astype(BF16), rhs_ref[h], preferred_element_type=F32)
        u_ref[h] = sol[:, :dk]
        for a in range(2):
            wq_ref[h, a, :ck, :] = sol[a * ck:(a + 1) * ck, dk:].astype(BF16)

    zeros_half = jnp.zeros((ck, dk), F32)

    def read_state(a, h):
        r_ref[h] = jnp.dot(wq_ref[h, a], state_ref[h].astype(BF16), preferred_element_type=F32)

    def update(a, h):
        sl = slice(a * ck, (a + 1) * ck)
        v_new = u_ref[h, sl, :] - r_ref[h, :ck, :]
        v_full = jnp.concatenate([v_new, zeros_half] if a == 0 else [zeros_half, v_new], axis=0).astype(BF16)
        o = r_ref[h, ck:, :] + jnp.dot(attn_ref[h, sl, :], v_full, preferred_element_type=F32)
        gl = glc_ref[a * ck:a * ck + 1, nh + h:nh + h + 1]
        state_ref[h] = state_ref[h] * jnp.exp(gl) + jnp.dot(kdt_ref[h], v_full, preferred_element_type=F32)
        ms = jnp.mean(o * o, axis=-1, keepdims=True)
        y = o * lax.rsqrt(ms + RMS_EPS) * nw_ref[...] * _silu(z_ref[0, sl, h * dk:(h + 1) * dk].astype(F32))
        o_ref[0, sl, h * dk:(h + 1) * dk] = y.astype(o_ref.dtype)

    def steps(phase, heads):
        if phase == "prep":
            return [functools.partial(prep, h) for h in heads]
        if phase == "merge":
            return [functools.partial(fn, lvl, h) for lvl in range(len(mask_bf))
                    for fn in (merge_a, merge_b) for h in heads]
        return ([functools.partial(solve, h) for h in heads]
                + [functools.partial(fn, a, h) for a in range(2) for fn in (read_state, update) for h in heads])

    for phase in ("prep", "merge", "tail"):
        for step in steps(phase, range(nh)):
            step()


def _gdn_mixer(qkv_act, proj, ba_rows, dt_bias, a_log, norm_w):
    bsz, s, _ = proj.shape
    t = 2 * GDN_CHUNK
    nh = GDN_HEADS
    dk = GDN_HEAD_DIM
    bias_c = jnp.pad(dt_bias, (nh, LANES - 2 * nh)).reshape(1, LANES)
    alog_c = jnp.pad(a_log, (nh, LANES - 2 * nh)).reshape(1, LANES)
    small = lambda shape: pl.BlockSpec(shape, lambda b_, c: (0, 0))
    return pl.pallas_call(
        _gdn_kernel,
        grid=(bsz, s // t),
        in_specs=[pl.BlockSpec((1, t, 3 * GDN_INNER), lambda b_, c: (b_, c, 0)),
                  pl.BlockSpec((1, t, GDN_INNER), lambda b_, c: (b_, c, CD_Z // GDN_INNER)),
                  pl.BlockSpec((1, t, LANES), lambda b_, c: (b_, c, CD_BA // LANES)),
                  pl.BlockSpec((1, LANES, t), lambda b_, c: (b_, 0, c)),
                  small((1, LANES)), small((LANES, 1)), small((1, LANES)), small((LANES, 1)),
                  small((1, dk))],
        out_specs=pl.BlockSpec((1, t, GDN_INNER), lambda b_, c: (b_, c, 0)),
        out_shape=jax.ShapeDtypeStruct((bsz, s, GDN_INNER), BF16),
        scratch_shapes=[pltpu.VMEM((nh, dk, dk), F32),
                        pltpu.VMEM((nh, t, t), BF16),
                        pltpu.VMEM((nh, t, t), BF16),
                        pltpu.VMEM((nh, t, t), F32),
                        pltpu.VMEM((nh, t, t), BF16),
                        pltpu.VMEM((nh, t, 2 * dk), BF16),
                        pltpu.VMEM((nh, t, dk), F32),
                        pltpu.VMEM((nh, 2, t, dk), BF16),
                        pltpu.VMEM((nh, dk, t), BF16),
                        pltpu.VMEM((nh, t, dk), F32),
                        pltpu.VMEM((t, LANES), F32)],
        compiler_params=_cparams(("arbitrary", "arbitrary")),
        name="gdn_mixer",
    )(qkv_act, proj, proj, ba_rows, bias_c, bias_c.reshape(LANES, 1), alog_c, alog_c.reshape(LANES, 1),
      norm_w.reshape(1, dk))


def _pad_cols(w, n):
    return jnp.pad(w, ((0, 0), (0, n - w.shape[1])))


def _gate_rows(proj, col0):
    return jnp.swapaxes(proj[:, :, col0:col0 + LANES].astype(F32), 1, 2)


def kernel(x, c, rel_bias, norm_w, ada_w, ada_b, ab_w_in, ab_w_out, ssd_conv_w, ssd_conv_b,
           ssd_dt_bias, ssd_a_log, ssd_d, ssd_norm_w, swa_sinks, cd_w_in, cd_w_out, gdn_conv_w,
           gdn_dt_bias, gdn_a_log, gdn_norm_w, ffn_w_up, ffn_conv_w, ffn_conv_b, ffn_w_down):
    bsz, s, d = x.shape
    depth = norm_w.shape[0]
    mods = _mods(c, ada_w, ada_b)
    swa_bias = _bias_tiles(rel_bias, _swa_bucket_idx())
    moba_bias = _bias_tiles(rel_bias, _moba_bucket_idx(s // MOBA_BLOCK))

    for i in range(depth):
        sh_m, sc_m, g_m, sh_f, sc_f, g_f = [m.reshape(bsz, 1, d) for m in jnp.split(mods[i], 6, axis=-1)]
        j = i // 2
        if i % 2 == 0:
            w = ab_w_in[j]
            dt0 = SSD_INNER + SSD_XBC
            w_in = jnp.concatenate([w[:, SSD_INNER:dt0], w[:, :SSD_INNER], w[:, dt0 + SSD_HEADS:],
                                    w[:, dt0:dt0 + SSD_HEADS]], axis=1)
            w_in = _pad_cols(w_in, SSD_XBC + AB_COLS).astype(BF16)
            xbc_act, proj = _proj_conv_act(x, norm_w[i, 0], sc_m, sh_m, w_in, ssd_conv_w[j], ssd_conv_b[j],
                                           "silu", n_plain=AB_COLS)
            y_a = _ssd_mixer(xbc_act, proj, _gate_rows(proj, AB_DT), ssd_dt_bias[j], ssd_a_log[j], ssd_d[j],
                             ssd_norm_w[j])
            y_b = _swa_attention(proj, swa_sinks[j], swa_bias)
            x = _matmul_resid([y_a, y_b], ab_w_out, j, x, g_m, norm_w[i, 1])
        else:
            w = cd_w_in[j]
            nqkv = 3 * GDN_INNER
            ba0 = nqkv + GDN_INNER
            w_in = jnp.concatenate([w[:, :ba0], w[:, ba0 + 2 * GDN_HEADS:], w[:, ba0:ba0 + 2 * GDN_HEADS]], axis=1)
            w_in = _pad_cols(w_in, nqkv + CD_COLS).astype(BF16)
            qkv_act, proj = _proj_conv_act(x, norm_w[i, 0], sc_m, sh_m, w_in, gdn_conv_w[j],
                                           jnp.zeros((nqkv,), F32), "silu", n_plain=CD_COLS)
            y_c = _gdn_mixer(qkv_act, proj, _gate_rows(proj, CD_BA), gdn_dt_bias[j], gdn_a_log[j], gdn_norm_w[j])
            y_d = _moba_attention(proj, moba_bias)
            x = _matmul_resid([y_c, y_d], cd_w_out, j, x, g_m, norm_w[i, 1])
        act = _proj_conv_act(x, norm_w[i, 2], sc_f, sh_f, ffn_w_up[i].astype(BF16), ffn_conv_w[i], ffn_conv_b[i],
                             "geglu")
        x = _matmul_resid([act], ffn_w_down, i, x, g_f, norm_w[i, 3])
    return x
```

```python
import functools
import math

import numpy as np
import jax
import jax.numpy as jnp
from jax import lax
from jax.experimental import pallas as pl
from jax.experimental.pallas import tpu as pltpu

F32 = jnp.float32
BF16 = jnp.bfloat16
HIGHEST = lax.Precision.HIGHEST

D_MODEL = 1024
RMS_EPS = 1e-6
NEG = -1e30
LANES = 128
SUBLANES = 8
N_ATTN_HEADS = 8
ATTN_HEAD_DIM = 64
ATTN_WIDTH = N_ATTN_HEADS * ATTN_HEAD_DIM
REL_BUCKETS = 32
REL_MAX_DIST = 1024
SSD_HEADS = 24
SSD_HEAD_DIM = 64
SSD_INNER = SSD_HEADS * SSD_HEAD_DIM
SSD_GROUPS = 4
SSD_STATE = 128
SSD_CONV = 4
SSD_CHUNK = 128
SSD_XBC = SSD_INNER + 2 * SSD_GROUPS * SSD_STATE
SWA_KV_HEADS = 2
SWA_BLOCK = 128
GDN_HEADS = 12
GDN_HEAD_DIM = 128
GDN_INNER = GDN_HEADS * GDN_HEAD_DIM
GDN_CONV = 4
GDN_CHUNK = 64
MOBA_BLOCK = 256
MOBA_TOPK = 3
FFN_DIM = 2816
FFN_CONV = 3

AB_Z, AB_Q, AB_K, AB_V, AB_DT, AB_COLS = 0, 1536, 2048, 2176, 2304, 2560
CD_Z, CD_QD, CD_KD, CD_VD, CD_BA, CD_COLS = 0, 1536, 2048, 2560, 3072, 3328

VMEM_LIMIT = 48 * 1024 * 1024
HALO = 16


def _cparams(sem):
    return pltpu.CompilerParams(dimension_semantics=sem, vmem_limit_bytes=VMEM_LIMIT)


def _bdot(a, b):
    return jnp.dot(a.astype(BF16), b.astype(BF16), preferred_element_type=F32)


def _bdot_nt(a, b):
    return lax.dot_general(a.astype(BF16), b.astype(BF16), (((1,), (1,)), ((), ())),
                           preferred_element_type=F32)


def _split3(x):
    hi = x.astype(BF16)
    r1 = x - hi.astype(F32)
    mid = r1.astype(BF16)
    lo = (r1 - mid.astype(F32)).astype(BF16)
    return hi, mid, lo


def _sum01_left(m01, x):
    n = x.shape[1]
    y = jnp.dot(m01, jnp.concatenate(_split3(x), axis=1), preferred_element_type=F32)
    return y[:, :n] + y[:, n:2 * n] + y[:, 2 * n:]


def _sum01_right(x, m01):
    n = x.shape[0]
    y = jnp.dot(jnp.concatenate(_split3(x), axis=0), m01, preferred_element_type=F32)
    return y[:n] + y[n:2 * n] + y[2 * n:]


def _softplus(x):
    return jnp.maximum(x, 0.0) + jnp.log(1.0 + jnp.exp(-jnp.abs(x)))


def _sigmoid(x):
    return 1.0 / (1.0 + jnp.exp(-x))


def _silu(x):
    return x * _sigmoid(x)


def _window_conv(win_ref, lo, ncols, rows, w, width, row0=0, shifted_loads=False):
    if shifted_loads:
        acc = None
        for s in range(width):
            tap = win_ref[pl.ds(HALO + row0 - s, rows), lo:lo + ncols] * w[width - 1 - s:width - s, :]
            acc = tap if acc is None else acc + tap
        return acc
    xin = win_ref[pl.ds(HALO + row0 - SUBLANES, rows + SUBLANES), lo:lo + ncols]
    acc = xin[SUBLANES:, :] * w[width - 1:width, :]
    for s in range(1, width):
        acc = acc + pltpu.roll(xin, s, axis=0)[SUBLANES:, :] * w[width - 1 - s:width - s, :]
    return acc


def _mods_kernel(c_ref, w_ref, b_ref, o_ref):
    o_ref[0] = _bdot(_silu(c_ref[...]), w_ref[0]) + b_ref[0]


def _mods(c, ada_w, ada_b):
    depth, d, n = ada_w.shape
    bsz = c.shape[0]
    tn = 512
    return pl.pallas_call(
        _mods_kernel,
        grid=(depth, n // tn),
        in_specs=[pl.BlockSpec((bsz, d), lambda l, j: (0, 0)),
                  pl.BlockSpec((1, d, tn), lambda l, j: (l, 0, j)),
                  pl.BlockSpec((1, 1, tn), lambda l, j: (l, 0, j))],
        out_specs=pl.BlockSpec((1, bsz, tn), lambda l, j: (l, 0, j)),
        out_shape=jax.ShapeDtypeStruct((depth, bsz, n), F32),
        compiler_params=_cparams(("arbitrary", "arbitrary")),
        name="adaln_mods",
    )(c, ada_w, ada_b.reshape(depth, 1, n))


def _modulated_norm(x, nw, sc, sh):
    ms = jnp.mean(x * x, axis=-1, keepdims=True)
    return x * lax.rsqrt(ms + RMS_EPS) * nw * (1.0 + sc) + sh


def _mmres_kernel(*refs, splits):
    na = len(splits)
    a_refs = refs[:na]
    w_ref, x_ref, g_ref, nw_ref, o_ref = refs[na:]
    acc = None
    lo = 0
    for a_ref, k in zip(a_refs, splits):
        part = jnp.dot(a_ref[0].astype(BF16), w_ref[lo:lo + k, :].astype(BF16), preferred_element_type=F32)
        acc = part if acc is None else acc + part
        lo += k
    ms = jnp.mean(acc * acc, axis=-1, keepdims=True)
    y = acc * lax.rsqrt(ms + RMS_EPS) * nw_ref[...]
    o_ref[0] = x_ref[0] + g_ref[0] * y


def _matmul_resid(a_list, w_stack, layer, x, gate, nw, tm=1024):
    bsz, s, d = x.shape
    spt = s // tm
    splits = tuple(a.shape[-1] for a in a_list)
    ktot = sum(splits)
    in_specs = [pl.BlockSpec((1, tm, k), lambda i: (i // spt, i % spt, 0)) for k in splits]
    in_specs += [pl.BlockSpec((None, ktot, d), lambda i: (layer, 0, 0)),
                 pl.BlockSpec((1, tm, d), lambda i: (i // spt, i % spt, 0)),
                 pl.BlockSpec((1, 1, d), lambda i: (i // spt, 0, 0)),
                 pl.BlockSpec((1, d), lambda i: (0, 0))]
    return pl.pallas_call(
        functools.partial(_mmres_kernel, splits=splits),
        grid=(bsz * spt,),
        in_specs=in_specs,
        out_specs=pl.BlockSpec((1, tm, d), lambda i: (i // spt, i % spt, 0)),
        out_shape=jax.ShapeDtypeStruct((bsz, s, d), F32),
        compiler_params=_cparams(("arbitrary",)),
        name="matmul_resid",
    )(*a_list, w_stack, x, gate, nw.reshape(1, d))


def _proj_conv_kernel(*refs, offs, tc, width, rb, act, n_plain):
    if n_plain:
        xh_ref, x_ref, nw_ref, sc_ref, sh_ref, w_ref, cw_ref, cb_ref, o_ref, op_ref, h_ref, u_ref = refs
    else:
        xh_ref, x_ref, nw_ref, sc_ref, sh_ref, w_ref, cw_ref, cb_ref, o_ref, h_ref, u_ref = refs
    tm = x_ref.shape[1]
    nchunk = o_ref.shape[2] // tc
    nplain = n_plain // tc
    plain0 = w_ref.shape[1] - n_plain
    h_halo = _modulated_norm(xh_ref[0], nw_ref[...], sc_ref[0], sh_ref[0])
    h_ref[:HALO, :] = jnp.where(pl.program_id(1) > 0, h_halo, 0.0).astype(BF16)
    h_ref[HALO:, :] = _modulated_norm(x_ref[0], nw_ref[...], sc_ref[0], sh_ref[0]).astype(BF16)
    c0 = math.sqrt(2.0 / math.pi)

    def matmuls(c):
        for k, off in enumerate(offs):
            lo = off + c * tc
            u_ref[c % 2, k] = jnp.dot(h_ref[...], w_ref[:, lo:lo + tc], preferred_element_type=F32)

    def epilogue(c):
        for r0 in range(0, tm, rb):
            conv = []
            for k, off in enumerate(offs):
                lo = off + c * tc
                conv.append(_window_conv(u_ref.at[c % 2, k], 0, tc, rb, cw_ref[:, lo:lo + tc], width, row0=r0,
                                         shifted_loads=True) + cb_ref[:, lo:lo + tc])
            if act == "geglu":
                g, v = conv
                th = jnp.tanh(g * (c0 + (c0 * 0.044715) * (g * g)))
                hg = 0.5 * g
                out = (hg + hg * th) * v
            else:
                out = _silu(conv[0])
            o_ref[0, r0:r0 + rb, c * tc:(c + 1) * tc] = out.astype(o_ref.dtype)

    def plain(p):
        lo = plain0 + p * tc
        op_ref[0, :, p * tc:(p + 1) * tc] = jnp.dot(
            h_ref[HALO:, :], w_ref[:, lo:lo + tc], preferred_element_type=F32).astype(op_ref.dtype)

    matmuls(0)
    done = 0
    for c in range(nchunk):
        if c + 1 < nchunk:
            matmuls(c + 1)
        epilogue(c)
        upto = (c + 1) * nplain // nchunk
        for p in range(done, upto):
            plain(p)
        done = upto


def _proj_conv_act(x, nw, sc, sh, w, cw, cb, act, n_plain=0, w_layer=None, tm=512, tc=256, rb=128):
    bsz, s, d = x.shape
    n2 = w.shape[-1] - n_plain
    f = n2 // 2 if act == "geglu" else n2
    offs = (0, f) if act == "geglu" else (0,)
    width = cw.shape[0]
    hb = tm // HALO
    const = lambda shape: pl.BlockSpec(shape, lambda b_, r: (0, 0))
    rows = lambda n: pl.BlockSpec((1, tm, n), lambda b_, r: (b_, r, 0))
    in_specs = [pl.BlockSpec((1, HALO, d), lambda b_, r: (b_, jnp.maximum(r * hb - 1, 0), 0)),
                rows(d), const((1, d)),
                pl.BlockSpec((1, 1, d), lambda b_, r: (b_, 0, 0)),
                pl.BlockSpec((1, 1, d), lambda b_, r: (b_, 0, 0)),
                const((d, n2 + n_plain)) if w_layer is None else
                pl.BlockSpec((None, d, n2 + n_plain), lambda b_, r: (w_layer, 0, 0)),
                const((width, n2)), const((1, n2))]
    args = [x, x, nw.reshape(1, d), sc, sh, w, cw, cb.reshape(1, n2)]
    out_specs = rows(f)
    out_shape = jax.ShapeDtypeStruct((bsz, s, f), BF16)
    if n_plain:
        out_specs = (out_specs, rows(n_plain))
        out_shape = (out_shape, jax.ShapeDtypeStruct((bsz, s, n_plain), BF16))
    return pl.pallas_call(
        functools.partial(_proj_conv_kernel, offs=offs, tc=tc, width=width, rb=rb, act=act, n_plain=n_plain),
        grid=(bsz, s // tm),
        in_specs=in_specs,
        out_specs=out_specs,
        out_shape=out_shape,
        scratch_shapes=[pltpu.VMEM((HALO + tm, d), BF16),
                        pltpu.VMEM((2, len(offs), HALO + tm, tc), F32)],
        compiler_params=_cparams(("arbitrary", "arbitrary")),
        name="proj_conv_" + act,
    )(*args)


def _rel_bucket_np(d):
    max_exact = REL_BUCKETS // 2
    d = np.maximum(d, 0)
    df = np.maximum(d, 1).astype(np.float64)
    large = max_exact + (np.log(df / max_exact) / math.log(REL_MAX_DIST / max_exact)
                         * (REL_BUCKETS - max_exact)).astype(np.int32)
    large = np.minimum(large, REL_BUCKETS - 1)
    return np.where(d < max_exact, d, large).astype(np.int32)


LOG2E = math.log2(math.e)


def _bias_kernel(tab_ref, idx_ref, o_ref, *, ranges):
    h = pl.program_id(0)
    for t, (lo, hi) in enumerate(ranges):
        idx = idx_ref[t]
        acc = jnp.full(idx.shape, NEG, F32)
        for bkt in range(lo, hi + 1):
            acc = jnp.where(idx == bkt, tab_ref[bkt, h] * LOG2E, acc)
        o_ref[0, t] = acc


def _bias_tiles(rel_bias, idx_np):
    t, r, c = idx_np.shape
    ranges = tuple((int(tile[tile >= 0].min()), int(tile.max())) for tile in idx_np)
    return pl.pallas_call(
        functools.partial(_bias_kernel, ranges=ranges),
        grid=(N_ATTN_HEADS,),
        in_specs=[pl.BlockSpec(memory_space=pltpu.SMEM),
                  pl.BlockSpec((t, r, c), lambda h: (0, 0, 0))],
        out_specs=pl.BlockSpec((1, t, r, c), lambda h: (h, 0, 0, 0)),
        out_shape=jax.ShapeDtypeStruct((N_ATTN_HEADS, t, r, c), F32),
        compiler_params=_cparams(("arbitrary",)),
        name="rel_bias_tiles",
    )(rel_bias, jnp.asarray(idx_np))


def _swa_bucket_idx():
    c = np.arange(2 * SWA_BLOCK)[:, None]
    r = np.arange(SWA_BLOCK)[None, :]
    dist = SWA_BLOCK + r - c
    return np.where((dist >= 0) & (dist < SWA_BLOCK), _rel_bucket_np(dist), -1).astype(np.int32)[None]


def _moba_bucket_idx(nblk):
    c = np.arange(MOBA_BLOCK)[:, None]
    r = np.arange(MOBA_BLOCK)[None, :]
    tiles = [np.where(m * MOBA_BLOCK + r - c >= 0, _rel_bucket_np(m * MOBA_BLOCK + r - c), -1)
             for m in range(nblk)]
    return np.stack(tiles).astype(np.int32)


def _expand_heads(v, e):
    hi = v.astype(BF16)
    lo = (v - hi.astype(F32)).astype(BF16)
    return (jnp.dot(hi, e, preferred_element_type=F32) + jnp.dot(lo, e, preferred_element_type=F32))


def _ssd_kernel(xbc_ref, z_ref, dtc_ref, dtr_ref, bias_c_ref, bias_r_ref, alog_c_ref, alog_r_ref,
                dskip_ref, nw_ref, e_ref, o_ref, state_ref):
    q = SSD_CHUNK
    gw = SSD_INNER // SSD_GROUPS
    hpg = SSD_HEADS // SSD_GROUPS

    @pl.when(pl.program_id(1) == 0)
    def _():
        state_ref[...] = jnp.zeros(state_ref.shape, F32)

    row = lax.broadcasted_iota(jnp.int32, (q, q), 0)
    col = lax.broadcasted_iota(jnp.int32, (q, q), 1)
    tril = row >= col
    tri_b = jnp.where(tril, 1.0, 0.0).astype(BF16)
    triu_b = jnp.where(row <= col, 1.0, 0.0).astype(BF16)

    dt_c = _softplus(dtc_ref[0].astype(F32) + bias_c_ref[...])
    da_c = dt_c * (-jnp.exp(alog_c_ref[...]))
    acs_c = _sum01_left(tri_b, da_c)
    dt_r = _softplus(dtr_ref[0] + bias_r_ref[...])
    da_r = dt_r * (-jnp.exp(alog_r_ref[...]))
    acs_r = _sum01_right(da_r, triu_b)

    acs_last = acs_c[q - 1:q, :]
    e = e_ref[...]
    dt_full = _expand_heads(dt_c, e)
    dtdec_full = _expand_heads(dt_c * jnp.exp(acs_last - acs_c), e)
    eacs_full = _expand_heads(jnp.exp(acs_c), e)
    cdecay_full = eacs_full[q - 1:q, :]

    xbc = xbc_ref[0].astype(F32)
    xs = xbc[:, :SSD_INNER]
    xdt = xs * dt_full
    xdec = xs * dtdec_full
    lane_half = lax.broadcasted_iota(jnp.int32, (1, LANES), 1) >> 6

    y_parts = []
    for g in range(SSD_GROUPS):
        b_g = xbc[:, SSD_INNER + g * SSD_STATE:SSD_INNER + (g + 1) * SSD_STATE]
        c_g = xbc[:, SSD_INNER + SSD_GROUPS * SSD_STATE + g * SSD_STATE:
                  SSD_INNER + SSD_GROUPS * SSD_STATE + (g + 1) * SSD_STATE]
        cb = jnp.where(tril, _bdot_nt(c_g, b_g), 0.0)
        st = state_ref[g]
        y_off = _bdot(c_g, st) * eacs_full[:, g * gw:(g + 1) * gw]
        state_ref[g] = st * cdecay_full[:, g * gw:(g + 1) * gw] + _bdot(b_g.T, xdec[:, g * gw:(g + 1) * gw])
        pair_parts = []
        for pr in range(hpg // 2):
            acc = None
            lo = g * gw + pr * LANES
            x_pair = xdt[:, lo:lo + LANES]
            for half in range(2):
                h = g * hpg + pr * 2 + half
                diff = acs_c[:, h:h + 1] - acs_r[h:h + 1, :]
                lmat = jnp.exp(jnp.minimum(diff, 0.0))
                part = _bdot(cb * lmat, jnp.where(lane_half == half, x_pair, 0.0))
                acc = part if acc is None else acc + part
            pair_parts.append(acc)
        y_diag = jnp.concatenate(pair_parts, axis=1)
        y = y_diag + y_off + dskip_ref[:, g * gw:(g + 1) * gw] * xs[:, g * gw:(g + 1) * gw]
        y = y * _silu(z_ref[0, :, g * gw:(g + 1) * gw].astype(F32))
        ms = jnp.mean(y * y, axis=-1, keepdims=True)
        y_parts.append(y * lax.rsqrt(ms + RMS_EPS) * nw_ref[:, g * gw:(g + 1) * gw])
    o_ref[0] = jnp.concatenate(y_parts, axis=1).astype(o_ref.dtype)


def _ssd_mixer(xbc_act, proj, dt_rows, dt_bias, a_log, d_skip, norm_w):
    bsz, s, _ = proj.shape
    q = SSD_CHUNK
    pad = LANES - SSD_HEADS
    bias_c = jnp.pad(dt_bias, (0, pad)).reshape(1, LANES)
    alog_c = jnp.pad(a_log, (0, pad)).reshape(1, LANES)
    e_np = np.zeros((LANES, SSD_INNER), np.float32)
    for h in range(SSD_HEADS):
        e_np[h, h * SSD_HEAD_DIM:(h + 1) * SSD_HEAD_DIM] = 1.0
    small = lambda shape: pl.BlockSpec(shape, lambda b_, c: (0, 0))
    return pl.pallas_call(
        _ssd_kernel,
        grid=(bsz, s // q),
        in_specs=[pl.BlockSpec((1, q, SSD_XBC), lambda b_, c: (b_, c, 0)),
                  pl.BlockSpec((1, q, SSD_INNER), lambda b_, c: (b_, c, AB_Z // SSD_INNER)),
                  pl.BlockSpec((1, q, LANES), lambda b_, c: (b_, c, AB_DT // LANES)),
                  pl.BlockSpec((1, LANES, q), lambda b_, c: (b_, 0, c)),
                  small((1, LANES)), small((LANES, 1)), small((1, LANES)), small((LANES, 1)),
                  small((1, SSD_INNER)), small((1, SSD_INNER)), small((LANES, SSD_INNER))],
        out_specs=pl.BlockSpec((1, q, SSD_INNER), lambda b_, c: (b_, c, 0)),
        out_shape=jax.ShapeDtypeStruct((bsz, s, SSD_INNER), BF16),
        scratch_shapes=[pltpu.VMEM((SSD_GROUPS, SSD_STATE, SSD_INNER // SSD_GROUPS), F32)],
        compiler_params=_cparams(("arbitrary", "arbitrary")),
        name="ssd_mixer",
    )(xbc_act, proj, proj, dt_rows, bias_c, bias_c.reshape(LANES, 1),
      alog_c, alog_c.reshape(LANES, 1), jnp.repeat(d_skip, SSD_HEAD_DIM).reshape(1, SSD_INNER),
      norm_w.reshape(1, SSD_INNER), jnp.asarray(e_np, BF16))


def _swa_kernel(sink_ref, q_ref, kp_ref, kc_ref, vp_ref, vc_ref, bias_ref, o_ref, s_ref):
    blk = SWA_BLOCK
    n = pl.program_id(1)
    grp = N_ATTN_HEADS // SWA_KV_HEADS
    scale = ATTN_HEAD_DIM ** -0.5
    kk = jnp.concatenate([kp_ref[0], kc_ref[0]], axis=0).astype(F32) * (scale * LOG2E)
    vv_t = jnp.concatenate([vp_ref[0], vc_ref[0]], axis=0).astype(F32).T
    lane_half = lax.broadcasted_iota(jnp.int32, (1, LANES), 1) >> 6
    c = lax.broadcasted_iota(jnp.int32, (2 * blk, blk), 0)
    valid = (c >= blk) | (n > 0)
    ms = []
    for kv in range(SWA_KV_HEADS):
        k_own = jnp.where(lane_half == kv, kk, 0.0).astype(BF16)
        k_var = [None, None]
        k_var[kv] = k_own
        k_var[1 - kv] = pltpu.roll(jnp.where(lane_half == kv, kk, 0.0), ATTN_HEAD_DIM, axis=1).astype(BF16)
        for gq in range(grp):
            h = kv * grp + gq
            q_tile = q_ref[0, :, (h // 2) * LANES:(h // 2 + 1) * LANES]
            s_t = _bdot_nt(k_var[h % 2], q_tile) + bias_ref[h, 0]
            s_t = jnp.where(valid, s_t, NEG)
            s_ref[h] = s_t
            ms.append(jnp.maximum(jnp.max(s_t, axis=0, keepdims=True), sink_ref[h] * LOG2E))
    outs = []
    for h in range(N_ATTN_HEADS):
        kv = h // grp
        p = jnp.exp2(s_ref[h] - ms[h])
        l = jnp.sum(p, axis=0, keepdims=True) + jnp.exp2(sink_ref[h] * LOG2E - ms[h])
        outs.append(_bdot(vv_t[kv * ATTN_HEAD_DIM:(kv + 1) * ATTN_HEAD_DIM, :], p) / l)
    for t in range(N_ATTN_HEADS // 2):
        pair = jnp.concatenate([outs[2 * t], outs[2 * t + 1]], axis=0)
        o_ref[0, :, t * LANES:(t + 1) * LANES] = pair.T.astype(o_ref.dtype)


def _swa_attention(proj, sinks, bias):
    bsz, s, _ = proj.shape
    blk = SWA_BLOCK
    kvw = SWA_KV_HEADS * ATTN_HEAD_DIM
    prev = lambda col: pl.BlockSpec((1, blk, kvw), lambda b_, n: (b_, jnp.maximum(n - 1, 0), col))
    cur = lambda col: pl.BlockSpec((1, blk, kvw), lambda b_, n: (b_, n, col))
    return pl.pallas_call(
        _swa_kernel,
        grid=(bsz, s // blk),
        in_specs=[pl.BlockSpec(memory_space=pltpu.SMEM),
                  pl.BlockSpec((1, blk, ATTN_WIDTH), lambda b_, n: (b_, n, AB_Q // ATTN_WIDTH)),
                  prev(AB_K // kvw), cur(AB_K // kvw), prev(AB_V // kvw), cur(AB_V // kvw),
                  pl.BlockSpec((N_ATTN_HEADS, 1, 2 * blk, blk), lambda b_, n: (0, 0, 0, 0))],
        out_specs=pl.BlockSpec((1, blk, ATTN_WIDTH), lambda b_, n: (b_, n, 0)),
        out_shape=jax.ShapeDtypeStruct((bsz, s, ATTN_WIDTH), BF16),
        scratch_shapes=[pltpu.VMEM((N_ATTN_HEADS, 2 * blk, blk), F32)],
        compiler_params=_cparams(("arbitrary", "arbitrary")),
        name="swa_attention",
    )(sinks, proj, proj, proj, proj, proj, bias)


def _moba_kernel(q_ref, k_ref, v_ref, bias_ref, o_ref, vt_ref, qm_ref, negrow_ref, s_ref, *, nblk):
    mb = MOBA_BLOCK
    dh = ATTN_HEAD_DIM
    own = pl.program_id(2)
    scale = dh ** -0.5

    @pl.when(own == 0)
    def _():
        means = []
        for j in range(nblk):
            vt_ref[j] = v_ref[0, j * mb:(j + 1) * mb, :].astype(F32).T.astype(BF16)
            means.append(jnp.mean(k_ref[0, j * mb:(j + 1) * mb, :].astype(F32), axis=0, keepdims=True))
        kmean = jnp.concatenate(means, axis=0)
        lane_half = lax.broadcasted_iota(jnp.int32, (1, LANES), 1) >> 6
        blk_id = lax.broadcasted_iota(jnp.int32, (nblk, mb), 0)
        for qb in range(nblk):
            q = q_ref[0, qb * mb:(qb + 1) * mb, :].astype(F32) * (scale * LOG2E)
            for hh in range(2):
                qm = jnp.where(lane_half == hh, q, 0.0)
                qm_ref[hh, qb] = qm.astype(BF16)
                gate = lax.dot_general(kmean, qm, (((1,), (1,)), ((), ())),
                                       preferred_element_type=F32, precision=HIGHEST)
                gate = jnp.where(blk_id < qb, gate, NEG)
                rank = jnp.zeros((nblk, mb), jnp.int32)
                for i in range(nblk):
                    gi = gate[i:i + 1, :]
                    ahead = (gi > gate) | ((gi == gate) & (i < blk_id))
                    rank = rank + jnp.where(ahead, 1, 0)
                keep = ((rank < MOBA_TOPK) & (blk_id < qb)) | (blk_id == qb)
                negrow_ref[hh, qb] = jnp.where(keep, 0.0, NEG).astype(F32)

    qms = [qm_ref[hh, own] for hh in range(2)]
    negrows = [negrow_ref[hh, own] for hh in range(2)]

    def attend(nb):
        m8 = [None, None]
        l8 = [None, None]
        acc = [None, None]

        def scores(hh, j):
            s_t = _bdot_nt(k_ref[0, j * mb:(j + 1) * mb, :], qms[hh]) + bias_ref[hh, jnp.maximum(own - j, 0)]
            s_t = s_t + negrows[hh][j:j + 1, :]
            s_ref[hh, j] = s_t
            m_j = jnp.max(s_t.reshape(mb // SUBLANES, SUBLANES, mb), axis=0)
            m8[hh] = m_j if m8[hh] is None else jnp.maximum(m8[hh], m_j)

        def weights(hh, j, m):
            p = jnp.exp2(s_ref[hh, j] - m)
            l_j = jnp.sum(p.reshape(mb // SUBLANES, SUBLANES, mb), axis=0)
            a_j = _bdot(vt_ref[j, hh * dh:(hh + 1) * dh, :], p)
            l8[hh] = l_j if l8[hh] is None else l8[hh] + l_j
            acc[hh] = a_j if acc[hh] is None else acc[hh] + a_j

        for j in range(nb):
            for hh in range(2):
                scores(hh, j)
        m = [jnp.max(m8[hh], axis=0, keepdims=True) for hh in range(2)]
        for j in range(nb):
            for hh in range(2):
                weights(hh, j, m[hh])
        out_t = jnp.concatenate([acc[hh] / jnp.sum(l8[hh], axis=0, keepdims=True) for hh in range(2)], axis=0)
        o_ref[0] = out_t.T.astype(o_ref.dtype)

    for nb in range(2, nblk + 1, 2):
        pl.when((own >= nb - 2) & (own < nb))(functools.partial(attend, nb))


def _moba_attention(proj, bias):
    bsz, s, _ = proj.shape
    mb = MOBA_BLOCK
    nblk = s // mb
    npair = N_ATTN_HEADS // 2
    return pl.pallas_call(
        functools.partial(_moba_kernel, nblk=nblk),
        grid=(bsz, npair, nblk),
        in_specs=[pl.BlockSpec((1, s, LANES), lambda b_, p, i: (b_, 0, CD_QD // LANES + p)),
                  pl.BlockSpec((1, s, LANES), lambda b_, p, i: (b_, 0, CD_KD // LANES + p)),
                  pl.BlockSpec((1, s, LANES), lambda b_, p, i: (b_, 0, CD_VD // LANES + p)),
                  pl.BlockSpec((2, nblk, mb, mb), lambda b_, p, i: (p, 0, 0, 0))],
        out_specs=pl.BlockSpec((1, mb, LANES), lambda b_, p, i: (b_, i, p)),
        out_shape=jax.ShapeDtypeStruct((bsz, s, ATTN_WIDTH), BF16),
        scratch_shapes=[pltpu.VMEM((nblk, LANES, mb), BF16),
                        pltpu.VMEM((2, nblk, mb, LANES), BF16),
                        pltpu.VMEM((2, nblk, nblk, mb), F32),
                        pltpu.VMEM((2, nblk, mb, mb), F32)],
        compiler_params=_cparams(("arbitrary", "arbitrary", "arbitrary")),
        name="moba_attention",
    )(proj, proj, proj, bias)


def _gdn_kernel(qkv_ref, z_ref, bac_ref, bar_ref, bias_c_ref, bias_r_ref, alog_c_ref, alog_r_ref,
                nw_ref, o_ref, state_ref, m_ref, attn_ref, t_ref, x_ref, rhs_ref, u_ref, wq_ref,
                kdt_ref, r_ref, glc_ref):
    t = 2 * GDN_CHUNK
    ck = GDN_CHUNK
    dk = GDN_HEAD_DIM
    nh = GDN_HEADS

    @pl.when(pl.program_id(1) == 0)
    def _():
        state_ref[...] = jnp.zeros(state_ref.shape, F32)

    row = lax.broadcasted_iota(jnp.int32, (t, t), 0)
    col = lax.broadcasted_iota(jnp.int32, (t, t), 1)
    same = (row >> 6) == (col >> 6)
    tril = same & (row >= col)
    strict = same & (row > col)
    tri_b = jnp.where(tril, 1.0, 0.0).astype(BF16)
    triu_b = jnp.where(same & (row <= col), 1.0, 0.0).astype(BF16)
    blk_b = jnp.where(same, 1.0, 0.0).astype(BF16)
    eye_f = jnp.where(row == col, 1.0, 0.0).astype(F32)
    merge_masks = [((row >> (l + 1)) == (col >> (l + 1))) & (((row >> l) & 1) == 1) & (((col >> l) & 1) == 0)
                   for l in range(int(math.log2(ck)))]
    mask_bf = [jnp.where(m, 1.0, 0.0).astype(BF16) for m in merge_masks[1:]]

    ba_c = bac_ref[0].astype(F32)
    g_c = -jnp.exp(alog_c_ref[...]) * _softplus(ba_c + bias_c_ref[...])
    sums_c = _sum01_left(jnp.concatenate([tri_b, blk_b], axis=0), g_c)
    gc_c = sums_c[:t]
    gl_c = sums_c[t:]
    g_r = -jnp.exp(alog_r_ref[...]) * _softplus(bar_ref[0] + bias_r_ref[...])
    gc_r = _sum01_right(g_r, triu_b)
    glc_ref[...] = gl_c

    def prep(h):
        q = qkv_ref[0, :, h * dk:(h + 1) * dk].astype(F32)
        k = qkv_ref[0, :, (nh + h) * dk:(nh + h + 1) * dk].astype(F32)
        v = qkv_ref[0, :, (2 * nh + h) * dk:(2 * nh + h + 1) * dk].astype(F32)
        qn = q * lax.rsqrt(jnp.sum(q * q, axis=-1, keepdims=True) + 1e-6) * (dk ** -0.5)
        kn = k * lax.rsqrt(jnp.sum(k * k, axis=-1, keepdims=True) + 1e-6)
        beta = _sigmoid(ba_c[:, h:h + 1])
        gcc = gc_c[:, nh + h:nh + h + 1]
        gcr = gc_r[nh + h:nh + h + 1, :]
        glc = gl_c[:, nh + h:nh + h + 1]
        decay = jnp.where(tril, jnp.exp(jnp.where(tril, gcc - gcr, 0.0)), 0.0)
        kb = kn * beta
        kk = _bdot_nt(jnp.concatenate([kb, qn], axis=0), kn)
        mm = jnp.where(strict, kk[:t] * decay, 0.0)
        m_ref[h] = mm.astype(BF16)
        attn_ref[h] = (kk[t:] * decay).astype(BF16)
        t_ref[h] = eye_f - jnp.where(merge_masks[0], mm, 0.0)
        egc = jnp.exp(gcc)
        rhs_ref[h] = jnp.concatenate([v * beta, kb * egc], axis=1).astype(BF16)
        q_dec = (qn * egc).astype(BF16)
        for a in range(2):
            wq_ref[h, a, ck:, :] = q_dec[a * ck:(a + 1) * ck]
        kdt_ref[h] = (kn * jnp.exp(glc - gcc)).T.astype(BF16)

    def merge_a(lvl, h):
        x_ref[h] = jnp.dot(t_ref[h].astype(BF16), m_ref[h] * mask_bf[lvl],
                           preferred_element_type=F32).astype(BF16)

    def merge_b(lvl, h):
        t_h = t_ref[h]
        t_ref[h] = t_h - jnp.dot(x_ref[h], t_h.astype(BF16), preferred_element_type=F32)

    def solve(h):
        sol = jnp.dot(t_ref[h].astype(BF16), rhs_ref[h], preferred_element_type=F32)
        u_ref[h] = sol[:, :dk]
        for a in range(2):
            wq_ref[h, a, :ck, :] = sol[a * ck:(a + 1) * ck, dk:].astype(BF16)

    zeros_half = jnp.zeros((ck, dk), F32)

    def read_state(a, h):
        r_ref[h] = jnp.dot(wq_ref[h, a], state_ref[h].astype(BF16), preferred_element_type=F32)

    def update(a, h):
        sl = slice(a * ck, (a + 1) * ck)
        v_new = u_ref[h, sl, :] - r_ref[h, :ck, :]
        v_full = jnp.concatenate([v_new, zeros_half] if a == 0 else [zeros_half, v_new], axis=0).astype(BF16)
        o = r_ref[h, ck:, :] + jnp.dot(attn_ref[h, sl, :], v_full, preferred_element_type=F32)
        gl = glc_ref[a * ck:a * ck + 1, nh + h:nh + h + 1]
        state_ref[h] = state_ref[h] * jnp.exp(gl) + jnp.dot(kdt_ref[h], v_full, preferred_element_type=F32)
        ms = jnp.mean(o * o, axis=-1, keepdims=True)
        y = o * lax.rsqrt(ms + RMS_EPS) * nw_ref[...] * _silu(z_ref[0, sl, h * dk:(h + 1) * dk].astype(F32))
        o_ref[0, sl, h * dk:(h + 1) * dk] = y.astype(o_ref.dtype)

    def steps(phase, heads):
        if phase == "prep":
            return [functools.partial(prep, h) for h in heads]
        if phase == "merge":
            return [functools.partial(fn, lvl, h) for lvl in range(len(mask_bf))
                    for fn in (merge_a, merge_b) for h in heads]
        return ([functools.partial(solve, h) for h in heads]
                + [functools.partial(fn, a, h) for a in range(2) for fn in (read_state, update) for h in heads])

    for phase in ("prep", "merge", "tail"):
        for step in steps(phase, range(nh)):
            step()


def _gdn_mixer(qkv_act, proj, ba_rows, dt_bias, a_log, norm_w):
    bsz, s, _ = proj.shape
    t = 2 * GDN_CHUNK
    nh = GDN_HEADS
    dk = GDN_HEAD_DIM
    bias_c = jnp.pad(dt_bias, (nh, LANES - 2 * nh)).reshape(1, LANES)
    alog_c = jnp.pad(a_log, (nh, LANES - 2 * nh)).reshape(1, LANES)
    small = lambda shape: pl.BlockSpec(shape, lambda b_, c: (0, 0))
    return pl.pallas_call(
        _gdn_kernel,
        grid=(bsz, s // t),
        in_specs=[pl.BlockSpec((1, t, 3 * GDN_INNER), lambda b_, c: (b_, c, 0)),
                  pl.BlockSpec((1, t, GDN_INNER), lambda b_, c: (b_, c, CD_Z // GDN_INNER)),
                  pl.BlockSpec((1, t, LANES), lambda b_, c: (b_, c, CD_BA // LANES)),
                  pl.BlockSpec((1, LANES, t), lambda b_, c: (b_, 0, c)),
                  small((1, LANES)), small((LANES, 1)), small((1, LANES)), small((LANES, 1)),
                  small((1, dk))],
        out_specs=pl.BlockSpec((1, t, GDN_INNER), lambda b_, c: (b_, c, 0)),
        out_shape=jax.ShapeDtypeStruct((bsz, s, GDN_INNER), BF16),
        scratch_shapes=[pltpu.VMEM((nh, dk, dk), F32),
                        pltpu.VMEM((nh, t, t), BF16),
                        pltpu.VMEM((nh, t, t), BF16),
                        pltpu.VMEM((nh, t, t), F32),
                        pltpu.VMEM((nh, t, t), BF16),
                        pltpu.VMEM((nh, t, 2 * dk), BF16),
                        pltpu.VMEM((nh, t, dk), F32),
                        pltpu.VMEM((nh, 2, t, dk), BF16),
                        pltpu.VMEM((nh, dk, t), BF16),
                        pltpu.VMEM((nh, t, dk), F32),
                        pltpu.VMEM((t, LANES), F32)],
        compiler_params=_cparams(("arbitrary", "arbitrary")),
        name="gdn_mixer",
    )(qkv_act, proj, proj, ba_rows, bias_c, bias_c.reshape(LANES, 1), alog_c, alog_c.reshape(LANES, 1),
      norm_w.reshape(1, dk))


def _pad_cols(w, n):
    return jnp.pad(w, ((0, 0), (0, n - w.shape[1])))


def _gate_rows(proj, col0):
    return jnp.swapaxes(proj[:, :, col0:col0 + LANES].astype(F32), 1, 2)


def kernel(x, c, rel_bias, norm_w, ada_w, ada_b, ab_w_in, ab_w_out, ssd_conv_w, ssd_conv_b,
           ssd_dt_bias, ssd_a_log, ssd_d, ssd_norm_w, swa_sinks, cd_w_in, cd_w_out, gdn_conv_w,
           gdn_dt_bias, gdn_a_log, gdn_norm_w, ffn_w_up, ffn_conv_w, ffn_conv_b, ffn_w_down):
    bsz, s, d = x.shape
    depth = norm_w.shape[0]
    mods = _mods(c, ada_w, ada_b)
    swa_bias = _bias_tiles(rel_bias, _swa_bucket_idx())
    moba_bias = _bias_tiles(rel_bias, _moba_bucket_idx(s // MOBA_BLOCK))
    ffn_w_up_bf = ffn_w_up.astype(BF16)

    for i in range(depth):
        sh_m, sc_m, g_m, sh_f, sc_f, g_f = [m.reshape(bsz, 1, d) for m in jnp.split(mods[i], 6, axis=-1)]
        j = i // 2
        if i % 2 == 0:
            w = ab_w_in[j].astype(BF16)
            dt0 = SSD_INNER + SSD_XBC
            w_in = jnp.concatenate([w[:, SSD_INNER:dt0], w[:, :SSD_INNER], w[:, dt0 + SSD_HEADS:],
                                    w[:, dt0:dt0 + SSD_HEADS]], axis=1)
            w_in = _pad_cols(w_in, SSD_XBC + AB_COLS)
            xbc_act, proj = _proj_conv_act(x, norm_w[i, 0], sc_m, sh_m, w_in, ssd_conv_w[j], ssd_conv_b[j],
                                           "silu", n_plain=AB_COLS)
            y_a = _ssd_mixer(xbc_act, proj, _gate_rows(proj, AB_DT), ssd_dt_bias[j], ssd_a_log[j], ssd_d[j],
                             ssd_norm_w[j])
            y_b = _swa_attention(proj, swa_sinks[j], swa_bias)
            x = _matmul_resid([y_a, y_b], ab_w_out, j, x, g_m, norm_w[i, 1])
        else:
            w = cd_w_in[j].astype(BF16)
            nqkv = 3 * GDN_INNER
            ba0 = nqkv + GDN_INNER
            w_in = jnp.concatenate([w[:, :ba0], w[:, ba0 + 2 * GDN_HEADS:], w[:, ba0:ba0 + 2 * GDN_HEADS]], axis=1)
            w_in = _pad_cols(w_in, nqkv + CD_COLS)
            qkv_act, proj = _proj_conv_act(x, norm_w[i, 0], sc_m, sh_m, w_in, gdn_conv_w[j],
                                           jnp.zeros((nqkv,), F32), "silu", n_plain=CD_COLS)
            y_c = _gdn_mixer(qkv_act, proj, _gate_rows(proj, CD_BA), gdn_dt_bias[j], gdn_a_log[j], gdn_norm_w[j])
            y_d = _moba_attention(proj, moba_bias)
            x = _matmul_resid([y_c, y_d], cd_w_out, j, x, g_m, norm_w[i, 1])
        act = _proj_conv_act(x, norm_w[i, 2], sc_f, sh_f, ffn_w_up_bf, ffn_conv_w[i], ffn_conv_b[i], "geglu",
                             w_layer=i)
        x = _matmul_resid([act], ffn_w_down, i, x, g_f, norm_w[i, 3])
    return x
```

```python
import functools
import math

import numpy as np
import jax
import jax.numpy as jnp
from jax import lax
from jax.experimental import pallas as pl
from jax.experimental.pallas import tpu as pltpu

F32 = jnp.float32
BF16 = jnp.bfloat16
HIGHEST = lax.Precision.HIGHEST

D_MODEL = 1024
RMS_EPS = 1e-6
NEG = -1e30
LANES = 128
SUBLANES = 8
N_ATTN_HEADS = 8
ATTN_HEAD_DIM = 64
ATTN_WIDTH = N_ATTN_HEADS * ATTN_HEAD_DIM
REL_BUCKETS = 32
REL_MAX_DIST = 1024
SSD_HEADS = 24
SSD_HEAD_DIM = 64
SSD_INNER = SSD_HEADS * SSD_HEAD_DIM
SSD_GROUPS = 4
SSD_STATE = 128
SSD_CONV = 4
SSD_CHUNK = 128
SSD_XBC = SSD_INNER + 2 * SSD_GROUPS * SSD_STATE
SWA_KV_HEADS = 2
SWA_BLOCK = 128
GDN_HEADS = 12
GDN_HEAD_DIM = 128
GDN_INNER = GDN_HEADS * GDN_HEAD_DIM
GDN_CONV = 4
GDN_CHUNK = 64
MOBA_BLOCK = 256
MOBA_TOPK = 3
FFN_DIM = 2816
FFN_CONV = 3

AB_Z, AB_Q, AB_K, AB_V, AB_DT, AB_COLS = 0, 1536, 2048, 2176, 2304, 2560
CD_Z, CD_QD, CD_KD, CD_VD, CD_BA, CD_COLS = 0, 1536, 2048, 2560, 3072, 3328

VMEM_LIMIT = 48 * 1024 * 1024
HALO = 16


def _cparams(sem):
    return pltpu.CompilerParams(dimension_semantics=sem, vmem_limit_bytes=VMEM_LIMIT)


def _bdot(a, b):
    return jnp.dot(a.astype(BF16), b.astype(BF16), preferred_element_type=F32)


def _bdot_nt(a, b):
    return lax.dot_general(a.astype(BF16), b.astype(BF16), (((1,), (1,)), ((), ())),
                           preferred_element_type=F32)


def _split3(x):
    hi = x.astype(BF16)
    r1 = x - hi.astype(F32)
    mid = r1.astype(BF16)
    lo = (r1 - mid.astype(F32)).astype(BF16)
    return hi, mid, lo


def _sum01_left(m01, x):
    n = x.shape[1]
    y = jnp.dot(m01, jnp.concatenate(_split3(x), axis=1), preferred_element_type=F32)
    return y[:, :n] + y[:, n:2 * n] + y[:, 2 * n:]


def _sum01_right(x, m01):
    n = x.shape[0]
    y = jnp.dot(jnp.concatenate(_split3(x), axis=0), m01, preferred_element_type=F32)
    return y[:n] + y[n:2 * n] + y[2 * n:]


def _softplus(x):
    return jnp.maximum(x, 0.0) + jnp.log(1.0 + jnp.exp(-jnp.abs(x)))


def _sigmoid(x):
    return 1.0 / (1.0 + jnp.exp(-x))


def _silu(x):
    return x * _sigmoid(x)


def _window_conv(win_ref, lo, ncols, rows, w, width, row0=0, shifted_loads=False):
    if shifted_loads:
        acc = None
        for s in range(width):
            tap = win_ref[pl.ds(HALO + row0 - s, rows), lo:lo + ncols] * w[width - 1 - s:width - s, :]
            acc = tap if acc is None else acc + tap
        return acc
    xin = win_ref[pl.ds(HALO + row0 - SUBLANES, rows + SUBLANES), lo:lo + ncols]
    acc = xin[SUBLANES:, :] * w[width - 1:width, :]
    for s in range(1, width):
        acc = acc + pltpu.roll(xin, s, axis=0)[SUBLANES:, :] * w[width - 1 - s:width - s, :]
    return acc


def _mods_kernel(c_ref, w_ref, b_ref, o_ref):
    o_ref[0] = _bdot(_silu(c_ref[...]), w_ref[0]) + b_ref[0]


def _mods(c, ada_w, ada_b):
    depth, d, n = ada_w.shape
    bsz = c.shape[0]
    tn = 512
    return pl.pallas_call(
        _mods_kernel,
        grid=(depth, n // tn),
        in_specs=[pl.BlockSpec((bsz, d), lambda l, j: (0, 0)),
                  pl.BlockSpec((1, d, tn), lambda l, j: (l, 0, j)),
                  pl.BlockSpec((1, 1, tn), lambda l, j: (l, 0, j))],
        out_specs=pl.BlockSpec((1, bsz, tn), lambda l, j: (l, 0, j)),
        out_shape=jax.ShapeDtypeStruct((depth, bsz, n), F32),
        compiler_params=_cparams(("arbitrary", "arbitrary")),
        name="adaln_mods",
    )(c, ada_w, ada_b.reshape(depth, 1, n))


def _modulated_norm(x, nw, sc, sh):
    ms = jnp.mean(x * x, axis=-1, keepdims=True)
    return x * lax.rsqrt(ms + RMS_EPS) * nw * (1.0 + sc) + sh


def _mmres_kernel(*refs, splits):
    na = len(splits)
    a_refs = refs[:na]
    w_ref, x_ref, g_ref, nw_ref, o_ref = refs[na:]
    acc = None
    lo = 0
    for a_ref, k in zip(a_refs, splits):
        part = jnp.dot(a_ref[0].astype(BF16), w_ref[lo:lo + k, :].astype(BF16), preferred_element_type=F32)
        acc = part if acc is None else acc + part
        lo += k
    ms = jnp.mean(acc * acc, axis=-1, keepdims=True)
    y = acc * lax.rsqrt(ms + RMS_EPS) * nw_ref[...]
    o_ref[0] = x_ref[0] + g_ref[0] * y


def _matmul_resid(a_list, w_stack, layer, x, gate, nw, tm=1024):
    bsz, s, d = x.shape
    spt = s // tm
    splits = tuple(a.shape[-1] for a in a_list)
    ktot = sum(splits)
    in_specs = [pl.BlockSpec((1, tm, k), lambda i: (i // spt, i % spt, 0)) for k in splits]
    in_specs += [pl.BlockSpec((None, ktot, d), lambda i: (layer, 0, 0)),
                 pl.BlockSpec((1, tm, d), lambda i: (i // spt, i % spt, 0)),
                 pl.BlockSpec((1, 1, d), lambda i: (i // spt, 0, 0)),
                 pl.BlockSpec((1, d), lambda i: (0, 0))]
    return pl.pallas_call(
        functools.partial(_mmres_kernel, splits=splits),
        grid=(bsz * spt,),
        in_specs=in_specs,
        out_specs=pl.BlockSpec((1, tm, d), lambda i: (i // spt, i % spt, 0)),
        out_shape=jax.ShapeDtypeStruct((bsz, s, d), F32),
        compiler_params=_cparams(("arbitrary",)),
        name="matmul_resid",
    )(*a_list, w_stack, x, gate, nw.reshape(1, d))


def _proj_conv_kernel(*refs, offs, tc, width, rb, act, n_plain):
    if n_plain:
        xh_ref, x_ref, nw_ref, sc_ref, sh_ref, w_ref, cw_ref, cb_ref, o_ref, op_ref, h_ref, u_ref = refs
    else:
        xh_ref, x_ref, nw_ref, sc_ref, sh_ref, w_ref, cw_ref, cb_ref, o_ref, h_ref, u_ref = refs
    tm = x_ref.shape[1]
    nchunk = o_ref.shape[2] // tc
    nplain = n_plain // tc
    plain0 = w_ref.shape[1] - n_plain
    h_halo = _modulated_norm(xh_ref[0], nw_ref[...], sc_ref[0], sh_ref[0])
    h_ref[:HALO, :] = jnp.where(pl.program_id(1) > 0, h_halo, 0.0).astype(BF16)
    h_ref[HALO:, :] = _modulated_norm(x_ref[0], nw_ref[...], sc_ref[0], sh_ref[0]).astype(BF16)
    c0 = math.sqrt(2.0 / math.pi)

    def matmuls(c):
        for k, off in enumerate(offs):
            lo = off + c * tc
            u_ref[c % 2, k] = jnp.dot(h_ref[...], w_ref[:, lo:lo + tc], preferred_element_type=F32)

    def epilogue(c):
        for r0 in range(0, tm, rb):
            conv = []
            for k, off in enumerate(offs):
                lo = off + c * tc
                conv.append(_window_conv(u_ref.at[c % 2, k], 0, tc, rb, cw_ref[:, lo:lo + tc], width, row0=r0,
                                         shifted_loads=True) + cb_ref[:, lo:lo + tc])
            if act == "geglu":
                g, v = conv
                th = jnp.tanh(g * (c0 + (c0 * 0.044715) * (g * g)))
                hg = 0.5 * g
                out = (hg + hg * th) * v
            else:
                out = _silu(conv[0])
            o_ref[0, r0:r0 + rb, c * tc:(c + 1) * tc] = out.astype(o_ref.dtype)

    def plain(p):
        lo = plain0 + p * tc
        op_ref[0, :, p * tc:(p + 1) * tc] = jnp.dot(
            h_ref[HALO:, :], w_ref[:, lo:lo + tc], preferred_element_type=F32).astype(op_ref.dtype)

    matmuls(0)
    done = 0
    for c in range(nchunk):
        if c + 1 < nchunk:
            matmuls(c + 1)
        epilogue(c)
        upto = (c + 1) * nplain // nchunk
        for p in range(done, upto):
            plain(p)
        done = upto


def _proj_conv_act(x, nw, sc, sh, w, cw, cb, act, n_plain=0, w_layer=None, tm=512, tc=256, rb=128):
    bsz, s, d = x.shape
    n2 = w.shape[-1] - n_plain
    f = n2 // 2 if act == "geglu" else n2
    offs = (0, f) if act == "geglu" else (0,)
    width = cw.shape[0]
    hb = tm // HALO
    const = lambda shape: pl.BlockSpec(shape, lambda b_, r: (0, 0))
    rows = lambda n: pl.BlockSpec((1, tm, n), lambda b_, r: (b_, r, 0))
    in_specs = [pl.BlockSpec((1, HALO, d), lambda b_, r: (b_, jnp.maximum(r * hb - 1, 0), 0)),
                rows(d), const((1, d)),
                pl.BlockSpec((1, 1, d), lambda b_, r: (b_, 0, 0)),
                pl.BlockSpec((1, 1, d), lambda b_, r: (b_, 0, 0)),
                const((d, n2 + n_plain)) if w_layer is None else
                pl.BlockSpec((None, d, n2 + n_plain), lambda b_, r: (w_layer, 0, 0)),
                const((width, n2)), const((1, n2))]
    args = [x, x, nw.reshape(1, d), sc, sh, w, cw, cb.reshape(1, n2)]
    out_specs = rows(f)
    out_shape = jax.ShapeDtypeStruct((bsz, s, f), BF16)
    if n_plain:
        out_specs = (out_specs, rows(n_plain))
        out_shape = (out_shape, jax.ShapeDtypeStruct((bsz, s, n_plain), BF16))
    return pl.pallas_call(
        functools.partial(_proj_conv_kernel, offs=offs, tc=tc, width=width, rb=rb, act=act, n_plain=n_plain),
        grid=(bsz, s // tm),
        in_specs=in_specs,
        out_specs=out_specs,
        out_shape=out_shape,
        scratch_shapes=[pltpu.VMEM((HALO + tm, d), BF16),
                        pltpu.VMEM((2, len(offs), HALO + tm, tc), F32)],
        compiler_params=_cparams(("arbitrary", "arbitrary")),
        name="proj_conv_" + act,
    )(*args)


def _rel_bucket_np(d):
    max_exact = REL_BUCKETS // 2
    d = np.maximum(d, 0)
    df = np.maximum(d, 1).astype(np.float64)
    large = max_exact + (np.log(df / max_exact) / math.log(REL_MAX_DIST / max_exact)
                         * (REL_BUCKETS - max_exact)).astype(np.int32)
    large = np.minimum(large, REL_BUCKETS - 1)
    return np.where(d < max_exact, d, large).astype(np.int32)


LOG2E = math.log2(math.e)


def _bias_kernel(tab_ref, idx_ref, o_ref, *, ranges):
    h = pl.program_id(0)
    for t, (lo, hi) in enumerate(ranges):
        idx = idx_ref[t]
        acc = jnp.full(idx.shape, NEG, F32)
        for bkt in range(lo, hi + 1):
            acc = jnp.where(idx == bkt, tab_ref[bkt, h] * LOG2E, acc)
        o_ref[0, t] = acc


def _bias_tiles(rel_bias, idx_np):
    t, r, c = idx_np.shape
    ranges = tuple((int(tile[tile >= 0].min()), int(tile.max())) for tile in idx_np)
    return pl.pallas_call(
        functools.partial(_bias_kernel, ranges=ranges),
        grid=(N_ATTN_HEADS,),
        in_specs=[pl.BlockSpec(memory_space=pltpu.SMEM),
                  pl.BlockSpec((t, r, c), lambda h: (0, 0, 0))],
        out_specs=pl.BlockSpec((1, t, r, c), lambda h: (h, 0, 0, 0)),
        out_shape=jax.ShapeDtypeStruct((N_ATTN_HEADS, t, r, c), F32),
        compiler_params=_cparams(("arbitrary",)),
        name="rel_bias_tiles",
    )(rel_bias, jnp.asarray(idx_np))


def _swa_bucket_idx():
    c = np.arange(2 * SWA_BLOCK)[:, None]
    r = np.arange(SWA_BLOCK)[None, :]
    dist = SWA_BLOCK + r - c
    return np.where((dist >= 0) & (dist < SWA_BLOCK), _rel_bucket_np(dist), -1).astype(np.int32)[None]


def _moba_bucket_idx(nblk):
    c = np.arange(MOBA_BLOCK)[:, None]
    r = np.arange(MOBA_BLOCK)[None, :]
    tiles = [np.where(m * MOBA_BLOCK + r - c >= 0, _rel_bucket_np(m * MOBA_BLOCK + r - c), -1)
             for m in range(nblk)]
    return np.stack(tiles).astype(np.int32)


def _expand_heads(v, e):
    hi = v.astype(BF16)
    lo = (v - hi.astype(F32)).astype(BF16)
    return (jnp.dot(hi, e, preferred_element_type=F32) + jnp.dot(lo, e, preferred_element_type=F32))


def _ssd_kernel(xbc_ref, z_ref, dtc_ref, dtr_ref, bias_c_ref, bias_r_ref, alog_c_ref, alog_r_ref,
                dskip_ref, nw_ref, e_ref, o_ref, state_ref):
    q = SSD_CHUNK
    gw = SSD_INNER // SSD_GROUPS
    hpg = SSD_HEADS // SSD_GROUPS

    @pl.when(pl.program_id(1) == 0)
    def _():
        state_ref[...] = jnp.zeros(state_ref.shape, F32)

    row = lax.broadcasted_iota(jnp.int32, (q, q), 0)
    col = lax.broadcasted_iota(jnp.int32, (q, q), 1)
    tril = row >= col
    tri_b = jnp.where(tril, 1.0, 0.0).astype(BF16)
    triu_b = jnp.where(row <= col, 1.0, 0.0).astype(BF16)

    dt_c = _softplus(dtc_ref[0].astype(F32) + bias_c_ref[...])
    da_c = dt_c * (-jnp.exp(alog_c_ref[...]))
    acs_c = _sum01_left(tri_b, da_c)
    dt_r = _softplus(dtr_ref[0] + bias_r_ref[...])
    da_r = dt_r * (-jnp.exp(alog_r_ref[...]))
    acs_r = _sum01_right(da_r, triu_b)

    acs_last = acs_c[q - 1:q, :]
    e = e_ref[...]
    dt_full = _expand_heads(dt_c, e)
    dtdec_full = _expand_heads(dt_c * jnp.exp(acs_last - acs_c), e)
    eacs_full = _expand_heads(jnp.exp(acs_c), e)
    cdecay_full = eacs_full[q - 1:q, :]

    xbc = xbc_ref[0].astype(F32)
    xs = xbc[:, :SSD_INNER]
    xdt = xs * dt_full
    xdec = xs * dtdec_full
    lane_half = lax.broadcasted_iota(jnp.int32, (1, LANES), 1) >> 6

    y_parts = []
    for g in range(SSD_GROUPS):
        b_g = xbc[:, SSD_INNER + g * SSD_STATE:SSD_INNER + (g + 1) * SSD_STATE]
        c_g = xbc[:, SSD_INNER + SSD_GROUPS * SSD_STATE + g * SSD_STATE:
                  SSD_INNER + SSD_GROUPS * SSD_STATE + (g + 1) * SSD_STATE]
        cb = jnp.where(tril, _bdot_nt(c_g, b_g), 0.0)
        st = state_ref[g]
        y_off = _bdot(c_g, st) * eacs_full[:, g * gw:(g + 1) * gw]
        state_ref[g] = st * cdecay_full[:, g * gw:(g + 1) * gw] + _bdot(b_g.T, xdec[:, g * gw:(g + 1) * gw])
        pair_parts = []
        for pr in range(hpg // 2):
            acc = None
            lo = g * gw + pr * LANES
            x_pair = xdt[:, lo:lo + LANES]
            for half in range(2):
                h = g * hpg + pr * 2 + half
                diff = acs_c[:, h:h + 1] - acs_r[h:h + 1, :]
                lmat = jnp.exp(jnp.minimum(diff, 0.0))
                part = _bdot(cb * lmat, jnp.where(lane_half == half, x_pair, 0.0))
                acc = part if acc is None else acc + part
            pair_parts.append(acc)
        y_diag = jnp.concatenate(pair_parts, axis=1)
        y = y_diag + y_off + dskip_ref[:, g * gw:(g + 1) * gw] * xs[:, g * gw:(g + 1) * gw]
        y = y * _silu(z_ref[0, :, g * gw:(g + 1) * gw].astype(F32))
        ms = jnp.mean(y * y, axis=-1, keepdims=True)
        y_parts.append(y * lax.rsqrt(ms + RMS_EPS) * nw_ref[:, g * gw:(g + 1) * gw])
    o_ref[0] = jnp.concatenate(y_parts, axis=1).astype(o_ref.dtype)


def _ssd_mixer(xbc_act, proj, dt_rows, dt_bias, a_log, d_skip, norm_w):
    bsz, s, _ = proj.shape
    q = SSD_CHUNK
    pad = LANES - SSD_HEADS
    bias_c = jnp.pad(dt_bias, (0, pad)).reshape(1, LANES)
    alog_c = jnp.pad(a_log, (0, pad)).reshape(1, LANES)
    e_np = np.zeros((LANES, SSD_INNER), np.float32)
    for h in range(SSD_HEADS):
        e_np[h, h * SSD_HEAD_DIM:(h + 1) * SSD_HEAD_DIM] = 1.0
    small = lambda shape: pl.BlockSpec(shape, lambda b_, c: (0, 0))
    return pl.pallas_call(
        _ssd_kernel,
        grid=(bsz, s // q),
        in_specs=[pl.BlockSpec((1, q, SSD_XBC), lambda b_, c: (b_, c, 0)),
                  pl.BlockSpec((1, q, SSD_INNER), lambda b_, c: (b_, c, AB_Z // SSD_INNER)),
                  pl.BlockSpec((1, q, LANES), lambda b_, c: (b_, c, AB_DT // LANES)),
                  pl.BlockSpec((1, LANES, q), lambda b_, c: (b_, 0, c)),
                  small((1, LANES)), small((LANES, 1)), small((1, LANES)), small((LANES, 1)),
                  small((1, SSD_INNER)), small((1, SSD_INNER)), small((LANES, SSD_INNER))],
        out_specs=pl.BlockSpec((1, q, SSD_INNER), lambda b_, c: (b_, c, 0)),
        out_shape=jax.ShapeDtypeStruct((bsz, s, SSD_INNER), BF16),
        scratch_shapes=[pltpu.VMEM((SSD_GROUPS, SSD_STATE, SSD_INNER // SSD_GROUPS), F32)],
        compiler_params=_cparams(("arbitrary", "arbitrary")),
        name="ssd_mixer",
    )(xbc_act, proj, proj, dt_rows, bias_c, bias_c.reshape(LANES, 1),
      alog_c, alog_c.reshape(LANES, 1), jnp.repeat(d_skip, SSD_HEAD_DIM).reshape(1, SSD_INNER),
      norm_w.reshape(1, SSD_INNER), jnp.asarray(e_np, BF16))


def _swa_kernel(sink_ref, q_ref, kp_ref, kc_ref, vp_ref, vc_ref, bias_ref, o_ref, s_ref):
    blk = SWA_BLOCK
    n = pl.program_id(1)
    grp = N_ATTN_HEADS // SWA_KV_HEADS
    scale = ATTN_HEAD_DIM ** -0.5
    kk = jnp.concatenate([kp_ref[0], kc_ref[0]], axis=0).astype(F32) * (scale * LOG2E)
    vv_t = jnp.concatenate([vp_ref[0], vc_ref[0]], axis=0).astype(F32).T
    lane_half = lax.broadcasted_iota(jnp.int32, (1, LANES), 1) >> 6
    c = lax.broadcasted_iota(jnp.int32, (2 * blk, blk), 0)
    valid = (c >= blk) | (n > 0)
    ms = []
    for kv in range(SWA_KV_HEADS):
        k_own = jnp.where(lane_half == kv, kk, 0.0).astype(BF16)
        k_var = [None, None]
        k_var[kv] = k_own
        k_var[1 - kv] = pltpu.roll(jnp.where(lane_half == kv, kk, 0.0), ATTN_HEAD_DIM, axis=1).astype(BF16)
        for gq in range(grp):
            h = kv * grp + gq
            q_tile = q_ref[0, :, (h // 2) * LANES:(h // 2 + 1) * LANES]
            s_t = _bdot_nt(k_var[h % 2], q_tile) + bias_ref[h, 0]
            s_t = jnp.where(valid, s_t, NEG)
            s_ref[h] = s_t
            ms.append(jnp.maximum(jnp.max(s_t, axis=0, keepdims=True), sink_ref[h] * LOG2E))
    outs = []
    for h in range(N_ATTN_HEADS):
        kv = h // grp
        p = jnp.exp2(s_ref[h] - ms[h])
        l = jnp.sum(p, axis=0, keepdims=True) + jnp.exp2(sink_ref[h] * LOG2E - ms[h])
        outs.append(_bdot(vv_t[kv * ATTN_HEAD_DIM:(kv + 1) * ATTN_HEAD_DIM, :], p) / l)
    for t in range(N_ATTN_HEADS // 2):
        pair = jnp.concatenate([outs[2 * t], outs[2 * t + 1]], axis=0)
        o_ref[0, :, t * LANES:(t + 1) * LANES] = pair.T.astype(o_ref.dtype)


def _swa_attention(proj, sinks, bias):
    bsz, s, _ = proj.shape
    blk = SWA_BLOCK
    kvw = SWA_KV_HEADS * ATTN_HEAD_DIM
    prev = lambda col: pl.BlockSpec((1, blk, kvw), lambda b_, n: (b_, jnp.maximum(n - 1, 0), col))
    cur = lambda col: pl.BlockSpec((1, blk, kvw), lambda b_, n: (b_, n, col))
    return pl.pallas_call(
        _swa_kernel,
        grid=(bsz, s // blk),
        in_specs=[pl.BlockSpec(memory_space=pltpu.SMEM),
                  pl.BlockSpec((1, blk, ATTN_WIDTH), lambda b_, n: (b_, n, AB_Q // ATTN_WIDTH)),
                  prev(AB_K // kvw), cur(AB_K // kvw), prev(AB_V // kvw), cur(AB_V // kvw),
                  pl.BlockSpec((N_ATTN_HEADS, 1, 2 * blk, blk), lambda b_, n: (0, 0, 0, 0))],
        out_specs=pl.BlockSpec((1, blk, ATTN_WIDTH), lambda b_, n: (b_, n, 0)),
        out_shape=jax.ShapeDtypeStruct((bsz, s, ATTN_WIDTH), BF16),
        scratch_shapes=[pltpu.VMEM((N_ATTN_HEADS, 2 * blk, blk), F32)],
        compiler_params=_cparams(("arbitrary", "arbitrary")),
        name="swa_attention",
    )(sinks, proj, proj, proj, proj, proj, bias)


def _moba_kernel(q_ref, k_ref, v_ref, bias_ref, o_ref, vt_ref, qm_ref, negrow_ref, s_ref, *, nblk):
    mb = MOBA_BLOCK
    dh = ATTN_HEAD_DIM
    own = pl.program_id(2)
    scale = dh ** -0.5

    @pl.when(own == 0)
    def _():
        means = []
        for j in range(nblk):
            vt_ref[j] = v_ref[0, j * mb:(j + 1) * mb, :].astype(F32).T.astype(BF16)
            means.append(jnp.mean(k_ref[0, j * mb:(j + 1) * mb, :].astype(F32), axis=0, keepdims=True))
        kmean = jnp.concatenate(means, axis=0)
        lane_half = lax.broadcasted_iota(jnp.int32, (1, LANES), 1) >> 6
        blk_id = lax.broadcasted_iota(jnp.int32, (nblk, mb), 0)
        for qb in range(nblk):
            q = q_ref[0, qb * mb:(qb + 1) * mb, :].astype(F32) * (scale * LOG2E)
            for hh in range(2):
                qm = jnp.where(lane_half == hh, q, 0.0)
                qm_ref[hh, qb] = qm.astype(BF16)
                gate = lax.dot_general(kmean, qm, (((1,), (1,)), ((), ())),
                                       preferred_element_type=F32, precision=HIGHEST)
                gate = jnp.where(blk_id < qb, gate, NEG)
                rank = jnp.zeros((nblk, mb), jnp.int32)
                for i in range(nblk):
                    gi = gate[i:i + 1, :]
                    ahead = (gi > gate) | ((gi == gate) & (i < blk_id))
                    rank = rank + jnp.where(ahead, 1, 0)
                keep = ((rank < MOBA_TOPK) & (blk_id < qb)) | (blk_id == qb)
                negrow_ref[hh, qb] = jnp.where(keep, 0.0, NEG).astype(F32)

    qms = [qm_ref[hh, own] for hh in range(2)]
    negrows = [negrow_ref[hh, own] for hh in range(2)]

    def attend(nb):
        m8 = [None, None]
        l8 = [None, None]
        acc = [None, None]

        def scores(hh, j):
            s_t = _bdot_nt(k_ref[0, j * mb:(j + 1) * mb, :], qms[hh]) + bias_ref[hh, jnp.maximum(own - j, 0)]
            s_t = s_t + negrows[hh][j:j + 1, :]
            s_ref[hh, j] = s_t
            m_j = jnp.max(s_t.reshape(mb // SUBLANES, SUBLANES, mb), axis=0)
            m8[hh] = m_j if m8[hh] is None else jnp.maximum(m8[hh], m_j)

        def weights(hh, j, m):
            p = jnp.exp2(s_ref[hh, j] - m)
            l_j = jnp.sum(p.reshape(mb // SUBLANES, SUBLANES, mb), axis=0)
            a_j = _bdot(vt_ref[j, hh * dh:(hh + 1) * dh, :], p)
            l8[hh] = l_j if l8[hh] is None else l8[hh] + l_j
            acc[hh] = a_j if acc[hh] is None else acc[hh] + a_j

        for j in range(nb):
            for hh in range(2):
                scores(hh, j)
        m = [jnp.max(m8[hh], axis=0, keepdims=True) for hh in range(2)]
        for j in range(nb):
            for hh in range(2):
                weights(hh, j, m[hh])
        out_t = jnp.concatenate([acc[hh] / jnp.sum(l8[hh], axis=0, keepdims=True) for hh in range(2)], axis=0)
        o_ref[0] = out_t.T.astype(o_ref.dtype)

    for nb in range(2, nblk + 1, 2):
        pl.when((own >= nb - 2) & (own < nb))(functools.partial(attend, nb))


def _moba_attention(proj, bias):
    bsz, s, _ = proj.shape
    mb = MOBA_BLOCK
    nblk = s // mb
    npair = N_ATTN_HEADS // 2
    return pl.pallas_call(
        functools.partial(_moba_kernel, nblk=nblk),
        grid=(bsz, npair, nblk),
        in_specs=[pl.BlockSpec((1, s, LANES), lambda b_, p, i: (b_, 0, CD_QD // LANES + p)),
                  pl.BlockSpec((1, s, LANES), lambda b_, p, i: (b_, 0, CD_KD // LANES + p)),
                  pl.BlockSpec((1, s, LANES), lambda b_, p, i: (b_, 0, CD_VD // LANES + p)),
                  pl.BlockSpec((2, nblk, mb, mb), lambda b_, p, i: (p, 0, 0, 0))],
        out_specs=pl.BlockSpec((1, mb, LANES), lambda b_, p, i: (b_, i, p)),
        out_shape=jax.ShapeDtypeStruct((bsz, s, ATTN_WIDTH), BF16),
        scratch_shapes=[pltpu.VMEM((nblk, LANES, mb), BF16),
                        pltpu.VMEM((2, nblk, mb, LANES), BF16),
                        pltpu.VMEM((2, nblk, nblk, mb), F32),
                        pltpu.VMEM((2, nblk, mb, mb), F32)],
        compiler_params=_cparams(("arbitrary", "arbitrary", "arbitrary")),
        name="moba_attention",
    )(proj, proj, proj, bias)


def _gdn_kernel(qkv_ref, z_ref, bac_ref, bar_ref, bias_c_ref, bias_r_ref, alog_c_ref, alog_r_ref,
                nw_ref, o_ref, state_ref, m_ref, attn_ref, t_ref, x_ref, rhs_ref, u_ref, wq_ref,
                kdt_ref, r_ref, glc_ref):
    t = 2 * GDN_CHUNK
    ck = GDN_CHUNK
    dk = GDN_HEAD_DIM
    nh = GDN_HEADS

    @pl.when(pl.program_id(1) == 0)
    def _():
        state_ref[...] = jnp.zeros(state_ref.shape, F32)

    row = lax.broadcasted_iota(jnp.int32, (t, t), 0)
    col = lax.broadcasted_iota(jnp.int32, (t, t), 1)
    same = (row >> 6) == (col >> 6)
    tril = same & (row >= col)
    strict = same & (row > col)
    tri_b = jnp.where(tril, 1.0, 0.0).astype(BF16)
    triu_b = jnp.where(same & (row <= col), 1.0, 0.0).astype(BF16)
    blk_b = jnp.where(same, 1.0, 0.0).astype(BF16)
    eye_f = jnp.where(row == col, 1.0, 0.0).astype(F32)
    merge_masks = [((row >> (l + 1)) == (col >> (l + 1))) & (((row >> l) & 1) == 1) & (((col >> l) & 1) == 0)
                   for l in range(int(math.log2(ck)))]
    mask_bf = [jnp.where(m, 1.0, 0.0).astype(BF16) for m in merge_masks[1:]]

    ba_c = bac_ref[0].astype(F32)
    g_c = -jnp.exp(alog_c_ref[...]) * _softplus(ba_c + bias_c_ref[...])
    sums_c = _sum01_left(jnp.concatenate([tri_b, blk_b], axis=0), g_c)
    gc_c = sums_c[:t]
    gl_c = sums_c[t:]
    g_r = -jnp.exp(alog_r_ref[...]) * _softplus(bar_ref[0] + bias_r_ref[...])
    gc_r = _sum01_right(g_r, triu_b)
    glc_ref[...] = gl_c

    def prep(h):
        q = qkv_ref[0, :, h * dk:(h + 1) * dk].astype(F32)
        k = qkv_ref[0, :, (nh + h) * dk:(nh + h + 1) * dk].astype(F32)
        v = qkv_ref[0, :, (2 * nh + h) * dk:(2 * nh + h + 1) * dk].astype(F32)
        qn = q * lax.rsqrt(jnp.sum(q * q, axis=-1, keepdims=True) + 1e-6) * (dk ** -0.5)
        kn = k * lax.rsqrt(jnp.sum(k * k, axis=-1, keepdims=True) + 1e-6)
        beta = _sigmoid(ba_c[:, h:h + 1])
        gcc = gc_c[:, nh + h:nh + h + 1]
        gcr = gc_r[nh + h:nh + h + 1, :]
        glc = gl_c[:, nh + h:nh + h + 1]
        decay = jnp.where(tril, jnp.exp(jnp.where(tril, gcc - gcr, 0.0)), 0.0)
        kb = kn * beta
        kk = _bdot_nt(jnp.concatenate([kb, qn], axis=0), kn)
        mm = jnp.where(strict, kk[:t] * decay, 0.0)
        m_ref[h] = mm.astype(BF16)
        attn_ref[h] = (kk[t:] * decay).astype(BF16)
        t_ref[h] = eye_f - jnp.where(merge_masks[0], mm, 0.0)
        egc = jnp.exp(gcc)
        rhs_ref[h] = jnp.concatenate([v * beta, kb * egc], axis=1).astype(BF16)
        q_dec = (qn * egc).astype(BF16)
        for a in range(2):
            wq_ref[h, a, ck:, :] = q_dec[a * ck:(a + 1) * ck]
        kdt_ref[h] = (kn * jnp.exp(glc - gcc)).T.astype(BF16)

    def merge_a(lvl, h):
        x_ref[h] = jnp.dot(t_ref[h].astype(BF16), m_ref[h] * mask_bf[lvl],
                           preferred_element_type=F32).astype(BF16)

    def merge_b(lvl, h):
        t_h = t_ref[h]
        t_ref[h] = t_h - jnp.dot(x_ref[h], t_h.astype(BF16), preferred_element_type=F32)

    def solve(h):
        sol = jnp.dot(t_ref[h].astype(BF16), rhs_ref[h], preferred_element_type=F32)
        u_ref[h] = sol[:, :dk]
        for a in range(2):
            wq_ref[h, a, :ck, :] = sol[a * ck:(a + 1) * ck, dk:].astype(BF16)

    zeros_half = jnp.zeros((ck, dk), F32)

    def read_state(a, h):
        r_ref[h] = jnp.dot(wq_ref[h, a], state_ref[h].astype(BF16), preferred_element_type=F32)

    def update(a, h):
        sl = slice(a * ck, (a + 1) * ck)
        v_new = u_ref[h, sl, :] - r_ref[h, :ck, :]
        v_full = jnp.concatenate([v_new, zeros_half] if a == 0 else [zeros_half, v_new], axis=0).astype(BF16)
        o = r_ref[h, ck:, :] + jnp.dot(attn_ref[h, sl, :], v_full, preferred_element_type=F32)
        gl = glc_ref[a * ck:a * ck + 1, nh + h:nh + h + 1]
        state_ref[h] = state_ref[h] * jnp.exp(gl) + jnp.dot(kdt_ref[h], v_full, preferred_element_type=F32)
        ms = jnp.mean(o * o, axis=-1, keepdims=True)
        y = o * lax.rsqrt(ms + RMS_EPS) * nw_ref[...] * _silu(z_ref[0, sl, h * dk:(h + 1) * dk].astype(F32))
        o_ref[0, sl, h * dk:(h + 1) * dk] = y.astype(o_ref.dtype)

    def steps(phase, heads):
        if phase == "prep":
            return [functools.partial(prep, h) for h in heads]
        if phase == "merge":
            return [functools.partial(fn, lvl, h) for lvl in range(len(mask_bf))
                    for fn in (merge_a, merge_b) for h in heads]
        return ([functools.partial(solve, h) for h in heads]
                + [functools.partial(fn, a, h) for a in range(2) for fn in (read_state, update) for h in heads])

    for phase in ("prep", "merge", "tail"):
        for step in steps(phase, range(nh)):
            step()


def _gdn_mixer(qkv_act, proj, ba_rows, dt_bias, a_log, norm_w):
    bsz, s, _ = proj.shape
    t = 2 * GDN_CHUNK
    nh = GDN_HEADS
    dk = GDN_HEAD_DIM
    bias_c = jnp.pad(dt_bias, (nh, LANES - 2 * nh)).reshape(1, LANES)
    alog_c = jnp.pad(a_log, (nh, LANES - 2 * nh)).reshape(1, LANES)
    small = lambda shape: pl.BlockSpec(shape, lambda b_, c: (0, 0))
    return pl.pallas_call(
        _gdn_kernel,
        grid=(bsz, s // t),
        in_specs=[pl.BlockSpec((1, t, 3 * GDN_INNER), lambda b_, c: (b_, c, 0)),
                  pl.BlockSpec((1, t, GDN_INNER), lambda b_, c: (b_, c, CD_Z // GDN_INNER)),
                  pl.BlockSpec((1, t, LANES), lambda b_, c: (b_, c, CD_BA // LANES)),
                  pl.BlockSpec((1, LANES, t), lambda b_, c: (b_, 0, c)),
                  small((1, LANES)), small((LANES, 1)), small((1, LANES)), small((LANES, 1)),
                  small((1, dk))],
        out_specs=pl.BlockSpec((1, t, GDN_INNER), lambda b_, c: (b_, c, 0)),
        out_shape=jax.ShapeDtypeStruct((bsz, s, GDN_INNER), BF16),
        scratch_shapes=[pltpu.VMEM((nh, dk, dk), F32),
                        pltpu.VMEM((nh, t, t), BF16),
                        pltpu.VMEM((nh, t, t), BF16),
                        pltpu.VMEM((nh, t, t), F32),
                        pltpu.VMEM((nh, t, t), BF16),
                        pltpu.VMEM((nh, t, 2 * dk), BF16),
                        pltpu.VMEM((nh, t, dk), F32),
                        pltpu.VMEM((nh, 2, t, dk), BF16),
                        pltpu.VMEM((nh, dk, t), BF16),
                        pltpu.VMEM((nh, t, dk), F32),
                        pltpu.VMEM((t, LANES), F32)],
        compiler_params=_cparams(("arbitrary", "arbitrary")),
        name="gdn_mixer",
    )(qkv_act, proj, proj, ba_rows, bias_c, bias_c.reshape(LANES, 1), alog_c, alog_c.reshape(LANES, 1),
      norm_w.reshape(1, dk))


def _reorder_kernel(w_ref, o_ref, *, segments, zero_from):
    rows, n = o_ref.shape
    o_ref[:, zero_from:] = jnp.zeros((rows, n - zero_from), o_ref.dtype)
    for lo, hi, dst in segments:
        o_ref[:, dst:dst + hi - lo] = w_ref[0, :, lo:hi].astype(o_ref.dtype)


def _reorder_cast(w_stack, layer, segments, zero_from, n_out, tr=256):
    _, d, n = w_stack.shape
    return pl.pallas_call(
        functools.partial(_reorder_kernel, segments=segments, zero_from=zero_from),
        grid=(d // tr,),
        in_specs=[pl.BlockSpec((1, tr, n), lambda r: (layer, r, 0))],
        out_specs=pl.BlockSpec((tr, n_out), lambda r: (r, 0)),
        out_shape=jax.ShapeDtypeStruct((d, n_out), BF16),
        compiler_params=_cparams(("arbitrary",)),
        name="reorder_cast_weights",
    )(w_stack)


def _gate_rows(proj, col0):
    return jnp.swapaxes(proj[:, :, col0:col0 + LANES].astype(F32), 1, 2)


def kernel(x, c, rel_bias, norm_w, ada_w, ada_b, ab_w_in, ab_w_out, ssd_conv_w, ssd_conv_b,
           ssd_dt_bias, ssd_a_log, ssd_d, ssd_norm_w, swa_sinks, cd_w_in, cd_w_out, gdn_conv_w,
           gdn_dt_bias, gdn_a_log, gdn_norm_w, ffn_w_up, ffn_conv_w, ffn_conv_b, ffn_w_down):
    bsz, s, d = x.shape
    depth = norm_w.shape[0]
    mods = _mods(c, ada_w, ada_b)
    swa_bias = _bias_tiles(rel_bias, _swa_bucket_idx())
    moba_bias = _bias_tiles(rel_bias, _moba_bucket_idx(s // MOBA_BLOCK))
    ffn_w_up_bf = ffn_w_up.astype(BF16)

    for i in range(depth):
        sh_m, sc_m, g_m, sh_f, sc_f, g_f = [m.reshape(bsz, 1, d) for m in jnp.split(mods[i], 6, axis=-1)]
        j = i // 2
        if i % 2 == 0:
            dt0 = SSD_INNER + SSD_XBC
            n_in = ab_w_in.shape[-1]
            w_in = _reorder_cast(ab_w_in, j, ((SSD_INNER, dt0, 0), (0, SSD_INNER, SSD_XBC + AB_Z),
                                              (dt0 + SSD_HEADS, n_in, SSD_XBC + AB_Q), (dt0, dt0 + SSD_HEADS, SSD_XBC + AB_DT)),
                                 SSD_XBC + AB_DT, SSD_XBC + AB_COLS)
            xbc_act, proj = _proj_conv_act(x, norm_w[i, 0], sc_m, sh_m, w_in, ssd_conv_w[j], ssd_conv_b[j],
                                           "silu", n_plain=AB_COLS)
            y_a = _ssd_mixer(xbc_act, proj, _gate_rows(proj, AB_DT), ssd_dt_bias[j], ssd_a_log[j], ssd_d[j],
                             ssd_norm_w[j])
            y_b = _swa_attention(proj, swa_sinks[j], swa_bias)
            x = _matmul_resid([y_a, y_b], ab_w_out, j, x, g_m, norm_w[i, 1])
        else:
            nqkv = 3 * GDN_INNER
            ba0 = nqkv + GDN_INNER
            n_in = cd_w_in.shape[-1]
            w_in = _reorder_cast(cd_w_in, j, ((0, ba0, 0), (ba0 + 2 * GDN_HEADS, n_in, nqkv + CD_QD),
                                              (ba0, ba0 + 2 * GDN_HEADS, nqkv + CD_BA)),
                                 nqkv + CD_BA, nqkv + CD_COLS)
            qkv_act, proj = _proj_conv_act(x, norm_w[i, 0], sc_m, sh_m, w_in, gdn_conv_w[j],
                                           jnp.zeros((nqkv,), F32), "silu", n_plain=CD_COLS)
            y_c = _gdn_mixer(qkv_act, proj, _gate_rows(proj, CD_BA), gdn_dt_bias[j], gdn_a_log[j], gdn_norm_w[j])
            y_d = _moba_attention(proj, moba_bias)
            x = _matmul_resid([y_c, y_d], cd_w_out, j, x, g_m, norm_w[i, 1])
        act = _proj_conv_act(x, norm_w[i, 2], sc_f, sh_f, ffn_w_up_bf, ffn_conv_w[i], ffn_conv_b[i], "geglu",
                             w_layer=i)
        x = _matmul_resid([act], ffn_w_down, i, x, g_f, norm_w[i, 3])
    return x
```

```python
import functools
import math

import numpy as np
import jax
import jax.numpy as jnp
from jax import lax
from jax.experimental import pallas as pl
from jax.experimental.pallas import tpu as pltpu

F32 = jnp.float32
BF16 = jnp.bfloat16
HIGHEST = lax.Precision.HIGHEST

D_MODEL = 1024
RMS_EPS = 1e-6
NEG = -1e30
LANES = 128
SUBLANES = 8
N_ATTN_HEADS = 8
ATTN_HEAD_DIM = 64
ATTN_WIDTH = N_ATTN_HEADS * ATTN_HEAD_DIM
REL_BUCKETS = 32
REL_MAX_DIST = 1024
SSD_HEADS = 24
SSD_HEAD_DIM = 64
SSD_INNER = SSD_HEADS * SSD_HEAD_DIM
SSD_GROUPS = 4
SSD_STATE = 128
SSD_CONV = 4
SSD_CHUNK = 128
SSD_XBC = SSD_INNER + 2 * SSD_GROUPS * SSD_STATE
SWA_KV_HEADS = 2
SWA_BLOCK = 128
GDN_HEADS = 12
GDN_HEAD_DIM = 128
GDN_INNER = GDN_HEADS * GDN_HEAD_DIM
GDN_CONV = 4
GDN_CHUNK = 64
MOBA_BLOCK = 256
MOBA_TOPK = 3
FFN_DIM = 2816
FFN_CONV = 3

AB_Z, AB_Q, AB_K, AB_V, AB_DT, AB_COLS = 0, 1536, 2048, 2176, 2304, 2560
CD_Z, CD_QD, CD_KD, CD_VD, CD_BA, CD_COLS = 0, 1536, 2048, 2560, 3072, 3328

VMEM_LIMIT = 48 * 1024 * 1024
HALO = 16


def _cparams(sem):
    return pltpu.CompilerParams(dimension_semantics=sem, vmem_limit_bytes=VMEM_LIMIT)


def _bdot(a, b):
    return jnp.dot(a.astype(BF16), b.astype(BF16), preferred_element_type=F32)


def _bdot_nt(a, b):
    return lax.dot_general(a.astype(BF16), b.astype(BF16), (((1,), (1,)), ((), ())),
                           preferred_element_type=F32)


def _split3(x):
    hi = x.astype(BF16)
    r1 = x - hi.astype(F32)
    mid = r1.astype(BF16)
    lo = (r1 - mid.astype(F32)).astype(BF16)
    return hi, mid, lo


def _sum01_left(m01, x):
    n = x.shape[1]
    y = jnp.dot(m01, jnp.concatenate(_split3(x), axis=1), preferred_element_type=F32)
    return y[:, :n] + y[:, n:2 * n] + y[:, 2 * n:]


def _sum01_right(x, m01):
    n = x.shape[0]
    y = jnp.dot(jnp.concatenate(_split3(x), axis=0), m01, preferred_element_type=F32)
    return y[:n] + y[n:2 * n] + y[2 * n:]


def _softplus(x):
    return jnp.maximum(x, 0.0) + jnp.log(1.0 + jnp.exp(-jnp.abs(x)))


def _sigmoid(x):
    return 1.0 / (1.0 + jnp.exp(-x))


def _silu(x):
    return x * _sigmoid(x)


def _window_conv(win_ref, lo, ncols, rows, w, width, row0=0, shifted_loads=False):
    if shifted_loads:
        acc = None
        for s in range(width):
            tap = win_ref[pl.ds(HALO + row0 - s, rows), lo:lo + ncols] * w[width - 1 - s:width - s, :]
            acc = tap if acc is None else acc + tap
        return acc
    xin = win_ref[pl.ds(HALO + row0 - SUBLANES, rows + SUBLANES), lo:lo + ncols]
    acc = xin[SUBLANES:, :] * w[width - 1:width, :]
    for s in range(1, width):
        acc = acc + pltpu.roll(xin, s, axis=0)[SUBLANES:, :] * w[width - 1 - s:width - s, :]
    return acc


def _mods_kernel(c_ref, w_ref, b_ref, o_ref):
    o_ref[0] = _bdot(_silu(c_ref[...]), w_ref[0]) + b_ref[0]


def _mods(c, ada_w, ada_b):
    depth, d, n = ada_w.shape
    bsz = c.shape[0]
    tn = 512
    return pl.pallas_call(
        _mods_kernel,
        grid=(depth, n // tn),
        in_specs=[pl.BlockSpec((bsz, d), lambda l, j: (0, 0)),
                  pl.BlockSpec((1, d, tn), lambda l, j: (l, 0, j)),
                  pl.BlockSpec((1, 1, tn), lambda l, j: (l, 0, j))],
        out_specs=pl.BlockSpec((1, bsz, tn), lambda l, j: (l, 0, j)),
        out_shape=jax.ShapeDtypeStruct((depth, bsz, n), F32),
        compiler_params=_cparams(("arbitrary", "arbitrary")),
        name="adaln_mods",
    )(c, ada_w, ada_b.reshape(depth, 1, n))


def _modulated_norm(x, nw, sc, sh):
    ms = jnp.mean(x * x, axis=-1, keepdims=True)
    return x * lax.rsqrt(ms + RMS_EPS) * nw * (1.0 + sc) + sh


def _mmres_kernel(*refs, splits):
    na = len(splits)
    a_refs = refs[:na]
    w_ref, x_ref, g_ref, nw_ref, o_ref = refs[na:]
    acc = None
    lo = 0
    for a_ref, k in zip(a_refs, splits):
        part = jnp.dot(a_ref[0].astype(BF16), w_ref[lo:lo + k, :].astype(BF16), preferred_element_type=F32)
        acc = part if acc is None else acc + part
        lo += k
    ms = jnp.mean(acc * acc, axis=-1, keepdims=True)
    y = acc * lax.rsqrt(ms + RMS_EPS) * nw_ref[...]
    o_ref[0] = x_ref[0] + g_ref[0] * y


def _matmul_resid(a_list, w_stack, layer, x, gate, nw, tm=1024):
    bsz, s, d = x.shape
    spt = s // tm
    splits = tuple(a.shape[-1] for a in a_list)
    ktot = sum(splits)
    in_specs = [pl.BlockSpec((1, tm, k), lambda i: (i // spt, i % spt, 0)) for k in splits]
    in_specs += [pl.BlockSpec((None, ktot, d), lambda i: (layer, 0, 0)),
                 pl.BlockSpec((1, tm, d), lambda i: (i // spt, i % spt, 0)),
                 pl.BlockSpec((1, 1, d), lambda i: (i // spt, 0, 0)),
                 pl.BlockSpec((1, d), lambda i: (0, 0))]
    return pl.pallas_call(
        functools.partial(_mmres_kernel, splits=splits),
        grid=(bsz * spt,),
        in_specs=in_specs,
        out_specs=pl.BlockSpec((1, tm, d), lambda i: (i // spt, i % spt, 0)),
        out_shape=jax.ShapeDtypeStruct((bsz, s, d), F32),
        compiler_params=_cparams(("arbitrary",)),
        name="matmul_resid",
    )(*a_list, w_stack, x, gate, nw.reshape(1, d))


def _proj_conv_kernel(*refs, offs, tc, width, rb, act, n_plain):
    if n_plain:
        xh_ref, x_ref, nw_ref, sc_ref, sh_ref, w_ref, cw_ref, cb_ref, o_ref, op_ref, h_ref, u_ref = refs
    else:
        xh_ref, x_ref, nw_ref, sc_ref, sh_ref, w_ref, cw_ref, cb_ref, o_ref, h_ref, u_ref = refs
    tm = x_ref.shape[1]
    nchunk = o_ref.shape[2] // tc
    nplain = n_plain // tc
    plain0 = w_ref.shape[1] - n_plain
    h_halo = _modulated_norm(xh_ref[0], nw_ref[...], sc_ref[0], sh_ref[0])
    h_ref[:HALO, :] = jnp.where(pl.program_id(1) > 0, h_halo, 0.0).astype(BF16)
    h_ref[HALO:, :] = _modulated_norm(x_ref[0], nw_ref[...], sc_ref[0], sh_ref[0]).astype(BF16)
    c0 = math.sqrt(2.0 / math.pi)

    nrow = HALO + tm

    def matmuls(c):
        for k, off in enumerate(offs):
            lo = off + c * tc
            u = jnp.dot(h_ref[...], w_ref[:, lo:lo + tc], preferred_element_type=F32)
            for s in range(width):
                u_ref[c % 2, k, s, pl.ds(s, nrow), :] = u

    def epilogue(c):
        for r0 in range(0, tm, rb):
            conv = []
            for k, off in enumerate(offs):
                lo = off + c * tc
                acc = cb_ref[:, lo:lo + tc]
                for s in range(width):
                    acc = acc + (u_ref[c % 2, k, s, pl.ds(HALO + r0, rb), :]
                                 * cw_ref[width - 1 - s:width - s, lo:lo + tc])
                conv.append(acc)
            if act == "geglu":
                g, v = conv
                th = jnp.tanh(g * (c0 + (c0 * 0.044715) * (g * g)))
                hg = 0.5 * g
                out = (hg + hg * th) * v
            else:
                out = _silu(conv[0])
            o_ref[0, r0:r0 + rb, c * tc:(c + 1) * tc] = out.astype(o_ref.dtype)

    def plain(p):
        lo = plain0 + p * tc
        op_ref[0, :, p * tc:(p + 1) * tc] = jnp.dot(
            h_ref[HALO:, :], w_ref[:, lo:lo + tc], preferred_element_type=F32).astype(op_ref.dtype)

    matmuls(0)
    done = 0
    for c in range(nchunk):
        if c + 1 < nchunk:
            matmuls(c + 1)
        epilogue(c)
        upto = (c + 1) * nplain // nchunk
        for p in range(done, upto):
            plain(p)
        done = upto


def _proj_conv_act(x, nw, sc, sh, w, cw, cb, act, n_plain=0, tm=512, tc=256, rb=128):
    bsz, s, d = x.shape
    n2 = w.shape[-1] - n_plain
    f = n2 // 2 if act == "geglu" else n2
    offs = (0, f) if act == "geglu" else (0,)
    width = cw.shape[0]
    hb = tm // HALO
    const = lambda shape: pl.BlockSpec(shape, lambda b_, r: (0, 0))
    rows = lambda n: pl.BlockSpec((1, tm, n), lambda b_, r: (b_, r, 0))
    in_specs = [pl.BlockSpec((1, HALO, d), lambda b_, r: (b_, jnp.maximum(r * hb - 1, 0), 0)),
                rows(d), const((1, d)),
                pl.BlockSpec((1, 1, d), lambda b_, r: (b_, 0, 0)),
                pl.BlockSpec((1, 1, d), lambda b_, r: (b_, 0, 0)),
                const((d, n2 + n_plain)),
                const((width, n2)), const((1, n2))]
    args = [x, x, nw.reshape(1, d), sc, sh, w, cw, cb.reshape(1, n2)]
    out_specs = rows(f)
    out_shape = jax.ShapeDtypeStruct((bsz, s, f), BF16)
    if n_plain:
        out_specs = (out_specs, rows(n_plain))
        out_shape = (out_shape, jax.ShapeDtypeStruct((bsz, s, n_plain), BF16))
    return pl.pallas_call(
        functools.partial(_proj_conv_kernel, offs=offs, tc=tc, width=width, rb=rb, act=act, n_plain=n_plain),
        grid=(bsz, s // tm),
        in_specs=in_specs,
        out_specs=out_specs,
        out_shape=out_shape,
        scratch_shapes=[pltpu.VMEM((HALO + tm, d), BF16),
                        pltpu.VMEM((2, len(offs), width, HALO + tm + SUBLANES, tc), F32)],
        compiler_params=_cparams(("arbitrary", "arbitrary")),
        name="proj_conv_" + act,
    )(*args)


def _rel_bucket_np(d):
    max_exact = REL_BUCKETS // 2
    d = np.maximum(d, 0)
    df = np.maximum(d, 1).astype(np.float64)
    large = max_exact + (np.log(df / max_exact) / math.log(REL_MAX_DIST / max_exact)
                         * (REL_BUCKETS - max_exact)).astype(np.int32)
    large = np.minimum(large, REL_BUCKETS - 1)
    return np.where(d < max_exact, d, large).astype(np.int32)


LOG2E = math.log2(math.e)


def _bias_kernel(tab_ref, idx_ref, o_ref, *, ranges):
    h = pl.program_id(0)
    for t, (lo, hi) in enumerate(ranges):
        idx = idx_ref[t]
        acc = jnp.full(idx.shape, NEG, F32)
        for bkt in range(lo, hi + 1):
            acc = jnp.where(idx == bkt, tab_ref[bkt, h] * LOG2E, acc)
        o_ref[0, t] = acc


def _bias_tiles(rel_bias, idx_np):
    t, r, c = idx_np.shape
    ranges = tuple((int(tile[tile >= 0].min()), int(tile.max())) for tile in idx_np)
    return pl.pallas_call(
        functools.partial(_bias_kernel, ranges=ranges),
        grid=(N_ATTN_HEADS,),
        in_specs=[pl.BlockSpec(memory_space=pltpu.SMEM),
                  pl.BlockSpec((t, r, c), lambda h: (0, 0, 0))],
        out_specs=pl.BlockSpec((1, t, r, c), lambda h: (h, 0, 0, 0)),
        out_shape=jax.ShapeDtypeStruct((N_ATTN_HEADS, t, r, c), F32),
        compiler_params=_cparams(("arbitrary",)),
        name="rel_bias_tiles",
    )(rel_bias, jnp.asarray(idx_np))


def _swa_bucket_idx():
    c = np.arange(2 * SWA_BLOCK)[:, None]
    r = np.arange(SWA_BLOCK)[None, :]
    dist = SWA_BLOCK + r - c
    return np.where((dist >= 0) & (dist < SWA_BLOCK), _rel_bucket_np(dist), -1).astype(np.int32)[None]


def _moba_bucket_idx(nblk):
    c = np.arange(MOBA_BLOCK)[:, None]
    r = np.arange(MOBA_BLOCK)[None, :]
    tiles = [np.where(m * MOBA_BLOCK + r - c >= 0, _rel_bucket_np(m * MOBA_BLOCK + r - c), -1)
             for m in range(nblk)]
    return np.stack(tiles).astype(np.int32)


def _expand_heads(v, e):
    hi = v.astype(BF16)
    lo = (v - hi.astype(F32)).astype(BF16)
    return (jnp.dot(hi, e, preferred_element_type=F32) + jnp.dot(lo, e, preferred_element_type=F32))


def _ssd_kernel(xbc_ref, z_ref, dtc_ref, dtr_ref, bias_c_ref, bias_r_ref, alog_c_ref, alog_r_ref,
                dskip_ref, nw_ref, e_ref, o_ref, state_ref):
    q = SSD_CHUNK
    gw = SSD_INNER // SSD_GROUPS
    hpg = SSD_HEADS // SSD_GROUPS

    @pl.when(pl.program_id(1) == 0)
    def _():
        state_ref[...] = jnp.zeros(state_ref.shape, F32)

    row = lax.broadcasted_iota(jnp.int32, (q, q), 0)
    col = lax.broadcasted_iota(jnp.int32, (q, q), 1)
    tril = row >= col
    tri_b = jnp.where(tril, 1.0, 0.0).astype(BF16)
    triu_b = jnp.where(row <= col, 1.0, 0.0).astype(BF16)

    dt_c = _softplus(dtc_ref[0].astype(F32) + bias_c_ref[...])
    da_c = dt_c * (-jnp.exp(alog_c_ref[...]))
    acs_c = _sum01_left(tri_b, da_c)
    dt_r = _softplus(dtr_ref[0] + bias_r_ref[...])
    da_r = dt_r * (-jnp.exp(alog_r_ref[...]))
    acs_r = _sum01_right(da_r, triu_b)

    acs_last = acs_c[q - 1:q, :]
    e = e_ref[...]
    dt_full = _expand_heads(dt_c, e)
    dtdec_full = _expand_heads(dt_c * jnp.exp(acs_last - acs_c), e)
    eacs_full = _expand_heads(jnp.exp(acs_c), e)
    cdecay_full = eacs_full[q - 1:q, :]

    xbc = xbc_ref[0].astype(F32)
    xs = xbc[:, :SSD_INNER]
    xdt = xs * dt_full
    xdec = xs * dtdec_full
    lane_half = lax.broadcasted_iota(jnp.int32, (1, LANES), 1) >> 6

    y_parts = []
    for g in range(SSD_GROUPS):
        b_g = xbc[:, SSD_INNER + g * SSD_STATE:SSD_INNER + (g + 1) * SSD_STATE]
        c_g = xbc[:, SSD_INNER + SSD_GROUPS * SSD_STATE + g * SSD_STATE:
                  SSD_INNER + SSD_GROUPS * SSD_STATE + (g + 1) * SSD_STATE]
        cb = jnp.where(tril, _bdot_nt(c_g, b_g), 0.0)
        st = state_ref[g]
        y_off = _bdot(c_g, st) * eacs_full[:, g * gw:(g + 1) * gw]
        state_ref[g] = st * cdecay_full[:, g * gw:(g + 1) * gw] + _bdot(b_g.T, xdec[:, g * gw:(g + 1) * gw])
        pair_parts = []
        for pr in range(hpg // 2):
            acc = None
            lo = g * gw + pr * LANES
            x_pair = xdt[:, lo:lo + LANES]
            for half in range(2):
                h = g * hpg + pr * 2 + half
                diff = acs_c[:, h:h + 1] - acs_r[h:h + 1, :]
                lmat = jnp.exp(jnp.minimum(diff, 0.0))
                part = _bdot(cb * lmat, jnp.where(lane_half == half, x_pair, 0.0))
                acc = part if acc is None else acc + part
            pair_parts.append(acc)
        y_diag = jnp.concatenate(pair_parts, axis=1)
        y = y_diag + y_off + dskip_ref[:, g * gw:(g + 1) * gw] * xs[:, g * gw:(g + 1) * gw]
        y = y * _silu(z_ref[0, :, g * gw:(g + 1) * gw].astype(F32))
        ms = jnp.mean(y * y, axis=-1, keepdims=True)
        y_parts.append(y * lax.rsqrt(ms + RMS_EPS) * nw_ref[:, g * gw:(g + 1) * gw])
    o_ref[0] = jnp.concatenate(y_parts, axis=1).astype(o_ref.dtype)


def _ssd_mixer(xbc_act, proj, dt_rows, dt_bias, a_log, d_skip, norm_w):
    bsz, s, _ = proj.shape
    q = SSD_CHUNK
    pad = LANES - SSD_HEADS
    bias_c = jnp.pad(dt_bias, (0, pad)).reshape(1, LANES)
    alog_c = jnp.pad(a_log, (0, pad)).reshape(1, LANES)
    e_np = np.zeros((LANES, SSD_INNER), np.float32)
    for h in range(SSD_HEADS):
        e_np[h, h * SSD_HEAD_DIM:(h + 1) * SSD_HEAD_DIM] = 1.0
    small = lambda shape: pl.BlockSpec(shape, lambda b_, c: (0, 0))
    return pl.pallas_call(
        _ssd_kernel,
        grid=(bsz, s // q),
        in_specs=[pl.BlockSpec((1, q, SSD_XBC), lambda b_, c: (b_, c, 0)),
                  pl.BlockSpec((1, q, SSD_INNER), lambda b_, c: (b_, c, AB_Z // SSD_INNER)),
                  pl.BlockSpec((1, q, LANES), lambda b_, c: (b_, c, AB_DT // LANES)),
                  pl.BlockSpec((1, LANES, q), lambda b_, c: (b_, 0, c)),
                  small((1, LANES)), small((LANES, 1)), small((1, LANES)), small((LANES, 1)),
                  small((1, SSD_INNER)), small((1, SSD_INNER)), small((LANES, SSD_INNER))],
        out_specs=pl.BlockSpec((1, q, SSD_INNER), lambda b_, c: (b_, c, 0)),
        out_shape=jax.ShapeDtypeStruct((bsz, s, SSD_INNER), BF16),
        scratch_shapes=[pltpu.VMEM((SSD_GROUPS, SSD_STATE, SSD_INNER // SSD_GROUPS), F32)],
        compiler_params=_cparams(("arbitrary", "arbitrary")),
        name="ssd_mixer",
    )(xbc_act, proj, proj, dt_rows, bias_c, bias_c.reshape(LANES, 1),
      alog_c, alog_c.reshape(LANES, 1), jnp.repeat(d_skip, SSD_HEAD_DIM).reshape(1, SSD_INNER),
      norm_w.reshape(1, SSD_INNER), jnp.asarray(e_np, BF16))


def _swa_kernel(sink_ref, q_ref, kp_ref, kc_ref, vp_ref, vc_ref, bias_ref, o_ref, s_ref):
    blk = SWA_BLOCK
    n = pl.program_id(1)
    grp = N_ATTN_HEADS // SWA_KV_HEADS
    scale = ATTN_HEAD_DIM ** -0.5
    kk = jnp.concatenate([kp_ref[0], kc_ref[0]], axis=0).astype(F32) * (scale * LOG2E)
    vv_t = jnp.concatenate([vp_ref[0], vc_ref[0]], axis=0).astype(F32).T
    lane_half = lax.broadcasted_iota(jnp.int32, (1, LANES), 1) >> 6
    c = lax.broadcasted_iota(jnp.int32, (2 * blk, blk), 0)
    valid = (c >= blk) | (n > 0)
    ms = []
    for kv in range(SWA_KV_HEADS):
        k_own = jnp.where(lane_half == kv, kk, 0.0).astype(BF16)
        k_var = [None, None]
        k_var[kv] = k_own
        k_var[1 - kv] = pltpu.roll(jnp.where(lane_half == kv, kk, 0.0), ATTN_HEAD_DIM, axis=1).astype(BF16)
        for gq in range(grp):
            h = kv * grp + gq
            q_tile = q_ref[0, :, (h // 2) * LANES:(h // 2 + 1) * LANES]
            s_t = _bdot_nt(k_var[h % 2], q_tile) + bias_ref[h, 0]
            s_t = jnp.where(valid, s_t, NEG)
            s_ref[h] = s_t
            ms.append(jnp.maximum(jnp.max(s_t, axis=0, keepdims=True), sink_ref[h] * LOG2E))
    outs = []
    for h in range(N_ATTN_HEADS):
        kv = h // grp
        p = jnp.exp2(s_ref[h] - ms[h])
        l = jnp.sum(p, axis=0, keepdims=True) + jnp.exp2(sink_ref[h] * LOG2E - ms[h])
        outs.append(_bdot(vv_t[kv * ATTN_HEAD_DIM:(kv + 1) * ATTN_HEAD_DIM, :], p) / l)
    for t in range(N_ATTN_HEADS // 2):
        pair = jnp.concatenate([outs[2 * t], outs[2 * t + 1]], axis=0)
        o_ref[0, :, t * LANES:(t + 1) * LANES] = pair.T.astype(o_ref.dtype)


def _swa_attention(proj, sinks, bias):
    bsz, s, _ = proj.shape
    blk = SWA_BLOCK
    kvw = SWA_KV_HEADS * ATTN_HEAD_DIM
    prev = lambda col: pl.BlockSpec((1, blk, kvw), lambda b_, n: (b_, jnp.maximum(n - 1, 0), col))
    cur = lambda col: pl.BlockSpec((1, blk, kvw), lambda b_, n: (b_, n, col))
    return pl.pallas_call(
        _swa_kernel,
        grid=(bsz, s // blk),
        in_specs=[pl.BlockSpec(memory_space=pltpu.SMEM),
                  pl.BlockSpec((1, blk, ATTN_WIDTH), lambda b_, n: (b_, n, AB_Q // ATTN_WIDTH)),
                  prev(AB_K // kvw), cur(AB_K // kvw), prev(AB_V // kvw), cur(AB_V // kvw),
                  pl.BlockSpec((N_ATTN_HEADS, 1, 2 * blk, blk), lambda b_, n: (0, 0, 0, 0))],
        out_specs=pl.BlockSpec((1, blk, ATTN_WIDTH), lambda b_, n: (b_, n, 0)),
        out_shape=jax.ShapeDtypeStruct((bsz, s, ATTN_WIDTH), BF16),
        scratch_shapes=[pltpu.VMEM((N_ATTN_HEADS, 2 * blk, blk), F32)],
        compiler_params=_cparams(("arbitrary", "arbitrary")),
        name="swa_attention",
    )(sinks, proj, proj, proj, proj, proj, bias)


def _moba_kernel(q_ref, k_ref, v_ref, bias_ref, o_ref, vt_ref, qm_ref, negrow_ref, s_ref, *, nblk):
    mb = MOBA_BLOCK
    dh = ATTN_HEAD_DIM
    own = pl.program_id(2)
    scale = dh ** -0.5

    @pl.when(own == 0)
    def _():
        means = []
        for j in range(nblk):
            vt_ref[j] = v_ref[0, j * mb:(j + 1) * mb, :].astype(F32).T.astype(BF16)
            means.append(jnp.mean(k_ref[0, j * mb:(j + 1) * mb, :].astype(F32), axis=0, keepdims=True))
        kmean = jnp.concatenate(means, axis=0)
        lane_half = lax.broadcasted_iota(jnp.int32, (1, LANES), 1) >> 6
        blk_id = lax.broadcasted_iota(jnp.int32, (nblk, mb), 0)
        for qb in range(nblk):
            q = q_ref[0, qb * mb:(qb + 1) * mb, :].astype(F32) * (scale * LOG2E)
            for hh in range(2):
                qm = jnp.where(lane_half == hh, q, 0.0)
                qm_ref[hh, qb] = qm.astype(BF16)
                gate = lax.dot_general(kmean, qm, (((1,), (1,)), ((), ())),
                                       preferred_element_type=F32, precision=HIGHEST)
                gate = jnp.where(blk_id < qb, gate, NEG)
                rank = jnp.zeros((nblk, mb), jnp.int32)
                for i in range(nblk):
                    gi = gate[i:i + 1, :]
                    ahead = (gi > gate) | ((gi == gate) & (i < blk_id))
                    rank = rank + jnp.where(ahead, 1, 0)
                keep = ((rank < MOBA_TOPK) & (blk_id < qb)) | (blk_id == qb)
                negrow_ref[hh, qb] = jnp.where(keep, 0.0, NEG).astype(F32)

    qms = [qm_ref[hh, own] for hh in range(2)]
    negrows = [negrow_ref[hh, own] for hh in range(2)]

    def attend(nb):
        m8 = [None, None]
        l8 = [None, None]
        acc = [None, None]

        def scores(hh, j):
            s_t = _bdot_nt(k_ref[0, j * mb:(j + 1) * mb, :], qms[hh]) + bias_ref[hh, jnp.maximum(own - j, 0)]
            s_t = s_t + negrows[hh][j:j + 1, :]
            s_ref[hh, j] = s_t
            m_j = jnp.max(s_t.reshape(mb // SUBLANES, SUBLANES, mb), axis=0)
            m8[hh] = m_j if m8[hh] is None else jnp.maximum(m8[hh], m_j)

        def weights(hh, j, m):
            p = jnp.exp2(s_ref[hh, j] - m)
            l_j = jnp.sum(p.reshape(mb // SUBLANES, SUBLANES, mb), axis=0)
            a_j = _bdot(vt_ref[j, hh * dh:(hh + 1) * dh, :], p)
            l8[hh] = l_j if l8[hh] is None else l8[hh] + l_j
            acc[hh] = a_j if acc[hh] is None else acc[hh] + a_j

        for j in range(nb):
            for hh in range(2):
                scores(hh, j)
        m = [jnp.max(m8[hh], axis=0, keepdims=True) for hh in range(2)]
        for j in range(nb):
            for hh in range(2):
                weights(hh, j, m[hh])
        out_t = jnp.concatenate([acc[hh] / jnp.sum(l8[hh], axis=0, keepdims=True) for hh in range(2)], axis=0)
        o_ref[0] = out_t.T.astype(o_ref.dtype)

    for nb in range(2, nblk + 1, 2):
        pl.when((own >= nb - 2) & (own < nb))(functools.partial(attend, nb))


def _moba_attention(proj, bias):
    bsz, s, _ = proj.shape
    mb = MOBA_BLOCK
    nblk = s // mb
    npair = N_ATTN_HEADS // 2
    return pl.pallas_call(
        functools.partial(_moba_kernel, nblk=nblk),
        grid=(bsz, npair, nblk),
        in_specs=[pl.BlockSpec((1, s, LANES), lambda b_, p, i: (b_, 0, CD_QD // LANES + p)),
                  pl.BlockSpec((1, s, LANES), lambda b_, p, i: (b_, 0, CD_KD // LANES + p)),
                  pl.BlockSpec((1, s, LANES), lambda b_, p, i: (b_, 0, CD_VD // LANES + p)),
                  pl.BlockSpec((2, nblk, mb, mb), lambda b_, p, i: (p, 0, 0, 0))],
        out_specs=pl.BlockSpec((1, mb, LANES), lambda b_, p, i: (b_, i, p)),
        out_shape=jax.ShapeDtypeStruct((bsz, s, ATTN_WIDTH), BF16),
        scratch_shapes=[pltpu.VMEM((nblk, LANES, mb), BF16),
                        pltpu.VMEM((2, nblk, mb, LANES), BF16),
                        pltpu.VMEM((2, nblk, nblk, mb), F32),
                        pltpu.VMEM((2, nblk, mb, mb), F32)],
        compiler_params=_cparams(("arbitrary", "arbitrary", "arbitrary")),
        name="moba_attention",
    )(proj, proj, proj, bias)


def _gdn_kernel(qkv_ref, z_ref, bac_ref, bar_ref, bias_c_ref, bias_r_ref, alog_c_ref, alog_r_ref,
                nw_ref, o_ref, state_ref, m_ref, attn_ref, t_ref, x_ref, rhs_ref, u_ref, wq_ref,
                kdt_ref, r_ref, glc_ref):
    t = 2 * GDN_CHUNK
    ck = GDN_CHUNK
    dk = GDN_HEAD_DIM
    nh = GDN_HEADS

    @pl.when(pl.program_id(1) == 0)
    def _():
        state_ref[...] = jnp.zeros(state_ref.shape, F32)

    row = lax.broadcasted_iota(jnp.int32, (t, t), 0)
    col = lax.broadcasted_iota(jnp.int32, (t, t), 1)
    same = (row >> 6) == (col >> 6)
    tril = same & (row >= col)
    strict = same & (row > col)
    tri_b = jnp.where(tril, 1.0, 0.0).astype(BF16)
    triu_b = jnp.where(same & (row <= col), 1.0, 0.0).astype(BF16)
    blk_b = jnp.where(same, 1.0, 0.0).astype(BF16)
    eye_f = jnp.where(row == col, 1.0, 0.0).astype(F32)
    merge_masks = [((row >> (l + 1)) == (col >> (l + 1))) & (((row >> l) & 1) == 1) & (((col >> l) & 1) == 0)
                   for l in range(int(math.log2(ck)))]
    mask_bf = [jnp.where(m, 1.0, 0.0).astype(BF16) for m in merge_masks[1:]]

    ba_c = bac_ref[0].astype(F32)
    g_c = -jnp.exp(alog_c_ref[...]) * _softplus(ba_c + bias_c_ref[...])
    sums_c = _sum01_left(jnp.concatenate([tri_b, blk_b], axis=0), g_c)
    gc_c = sums_c[:t]
    gl_c = sums_c[t:]
    g_r = -jnp.exp(alog_r_ref[...]) * _softplus(bar_ref[0] + bias_r_ref[...])
    gc_r = _sum01_right(g_r, triu_b)
    glc_ref[...] = gl_c

    def prep(h):
        q = qkv_ref[0, :, h * dk:(h + 1) * dk].astype(F32)
        k = qkv_ref[0, :, (nh + h) * dk:(nh + h + 1) * dk].astype(F32)
        v = qkv_ref[0, :, (2 * nh + h) * dk:(2 * nh + h + 1) * dk].astype(F32)
        qn = q * lax.rsqrt(jnp.sum(q * q, axis=-1, keepdims=True) + 1e-6) * (dk ** -0.5)
        kn = k * lax.rsqrt(jnp.sum(k * k, axis=-1, keepdims=True) + 1e-6)
        beta = _sigmoid(ba_c[:, h:h + 1])
        gcc = gc_c[:, nh + h:nh + h + 1]
        gcr = gc_r[nh + h:nh + h + 1, :]
        glc = gl_c[:, nh + h:nh + h + 1]
        decay = jnp.where(tril, jnp.exp(jnp.where(tril, gcc - gcr, 0.0)), 0.0)
        kb = kn * beta
        kk = _bdot_nt(jnp.concatenate([kb, qn], axis=0), kn)
        mm = jnp.where(strict, kk[:t] * decay, 0.0)
        m_ref[h] = mm.astype(BF16)
        attn_ref[h] = (kk[t:] * decay).astype(BF16)
        t_ref[h] = eye_f - jnp.where(merge_masks[0], mm, 0.0)
        egc = jnp.exp(gcc)
        rhs_ref[h] = jnp.concatenate([v * beta, kb * egc], axis=1).astype(BF16)
        q_dec = (qn * egc).astype(BF16)
        for a in range(2):
            wq_ref[h, a, ck:, :] = q_dec[a * ck:(a + 1) * ck]
        kdt_ref[h] = (kn * jnp.exp(glc - gcc)).T.astype(BF16)

    def merge_a(lvl, h):
        x_ref[h] = jnp.dot(t_ref[h].astype(BF16), m_ref[h] * mask_bf[lvl],
                           preferred_element_type=F32).astype(BF16)

    def merge_b(lvl, h):
        t_h = t_ref[h]
        t_ref[h] = t_h - jnp.dot(x_ref[h], t_h.astype(BF16), preferred_element_type=F32)

    def solve(h):
        sol = jnp.dot(t_ref[h].astype(BF16), rhs_ref[h], preferred_element_type=F32)
        u_ref[h] = sol[:, :dk]
        for a in range(2):
            wq_ref[h, a, :ck, :] = sol[a * ck:(a + 1) * ck, dk:].astype(BF16)

    zeros_half = jnp.zeros((ck, dk), F32)

    def read_state(a, h):
        r_ref[h] = jnp.dot(wq_ref[h, a], state_ref[h].astype(BF16), preferred_element_type=F32)

    def update(a, h):
        sl = slice(a * ck, (a + 1) * ck)
        v_new = u_ref[h, sl, :] - r_ref[h, :ck, :]
        v_full = jnp.concatenate([v_new, zeros_half] if a == 0 else [zeros_half, v_new], axis=0).astype(BF16)
        o = r_ref[h, ck:, :] + jnp.dot(attn_ref[h, sl, :], v_full, preferred_element_type=F32)
        gl = glc_ref[a * ck:a * ck + 1, nh + h:nh + h + 1]
        state_ref[h] = state_ref[h] * jnp.exp(gl) + jnp.dot(kdt_ref[h], v_full, preferred_element_type=F32)
        ms = jnp.mean(o * o, axis=-1, keepdims=True)
        y = o * lax.rsqrt(ms + RMS_EPS) * nw_ref[...] * _silu(z_ref[0, sl, h * dk:(h + 1) * dk].astype(F32))
        o_ref[0, sl, h * dk:(h + 1) * dk] = y.astype(o_ref.dtype)

    def steps(phase, heads):
        if phase == "prep":
            return [functools.partial(prep, h) for h in heads]
        if phase == "merge":
            return [functools.partial(fn, lvl, h) for lvl in range(len(mask_bf))
                    for fn in (merge_a, merge_b) for h in heads]
        return ([functools.partial(solve, h) for h in heads]
                + [functools.partial(fn, a, h) for a in range(2) for fn in (read_state, update) for h in heads])

    for phase in ("prep", "merge", "tail"):
        for step in steps(phase, range(nh)):
            step()


def _gdn_mixer(qkv_act, proj, ba_rows, dt_bias, a_log, norm_w):
    bsz, s, _ = proj.shape
    t = 2 * GDN_CHUNK
    nh = GDN_HEADS
    dk = GDN_HEAD_DIM
    bias_c = jnp.pad(dt_bias, (nh, LANES - 2 * nh)).reshape(1, LANES)
    alog_c = jnp.pad(a_log, (nh, LANES - 2 * nh)).reshape(1, LANES)
    small = lambda shape: pl.BlockSpec(shape, lambda b_, c: (0, 0))
    return pl.pallas_call(
        _gdn_kernel,
        grid=(bsz, s // t),
        in_specs=[pl.BlockSpec((1, t, 3 * GDN_INNER), lambda b_, c: (b_, c, 0)),
                  pl.BlockSpec((1, t, GDN_INNER), lambda b_, c: (b_, c, CD_Z // GDN_INNER)),
                  pl.BlockSpec((1, t, LANES), lambda b_, c: (b_, c, CD_BA // LANES)),
                  pl.BlockSpec((1, LANES, t), lambda b_, c: (b_, 0, c)),
                  small((1, LANES)), small((LANES, 1)), small((1, LANES)), small((LANES, 1)),
                  small((1, dk))],
        out_specs=pl.BlockSpec((1, t, GDN_INNER), lambda b_, c: (b_, c, 0)),
        out_shape=jax.ShapeDtypeStruct((bsz, s, GDN_INNER), BF16),
        scratch_shapes=[pltpu.VMEM((nh, dk, dk), F32),
                        pltpu.VMEM((nh, t, t), BF16),
                        pltpu.VMEM((nh, t, t), BF16),
                        pltpu.VMEM((nh, t, t), F32),
                        pltpu.VMEM((nh, t, t), BF16),
                        pltpu.VMEM((nh, t, 2 * dk), BF16),
                        pltpu.VMEM((nh, t, dk), F32),
                        pltpu.VMEM((nh, 2, t, dk), BF16),
                        pltpu.VMEM((nh, dk, t), BF16),
                        pltpu.VMEM((nh, t, dk), F32),
                        pltpu.VMEM((t, LANES), F32)],
        compiler_params=_cparams(("arbitrary", "arbitrary")),
        name="gdn_mixer",
    )(qkv_act, proj, proj, ba_rows, bias_c, bias_c.reshape(LANES, 1), alog_c, alog_c.reshape(LANES, 1),
      norm_w.reshape(1, dk))


def _reorder_kernel(w_ref, o_ref, *, segments, zero_from):
    rows, n = o_ref.shape
    if zero_from < n:
        o_ref[:, zero_from:] = jnp.zeros((rows, n - zero_from), o_ref.dtype)
    for lo, hi, dst in segments:
        o_ref[:, dst:dst + hi - lo] = w_ref[0, :, lo:hi].astype(o_ref.dtype)


def _reorder_cast(w_stack, layer, segments, zero_from, n_out, tr=256):
    _, d, n = w_stack.shape
    return pl.pallas_call(
        functools.partial(_reorder_kernel, segments=segments, zero_from=zero_from),
        grid=(d // tr,),
        in_specs=[pl.BlockSpec((1, tr, n), lambda r: (layer, r, 0))],
        out_specs=pl.BlockSpec((tr, n_out), lambda r: (r, 0)),
        out_shape=jax.ShapeDtypeStruct((d, n_out), BF16),
        compiler_params=_cparams(("arbitrary",)),
        name="reorder_cast_weights",
    )(w_stack)


def _gate_rows(proj, col0):
    return jnp.swapaxes(proj[:, :, col0:col0 + LANES].astype(F32), 1, 2)


def kernel(x, c, rel_bias, norm_w, ada_w, ada_b, ab_w_in, ab_w_out, ssd_conv_w, ssd_conv_b,
           ssd_dt_bias, ssd_a_log, ssd_d, ssd_norm_w, swa_sinks, cd_w_in, cd_w_out, gdn_conv_w,
           gdn_dt_bias, gdn_a_log, gdn_norm_w, ffn_w_up, ffn_conv_w, ffn_conv_b, ffn_w_down):
    bsz, s, d = x.shape
    depth = norm_w.shape[0]
    mods = _mods(c, ada_w, ada_b)
    swa_bias = _bias_tiles(rel_bias, _swa_bucket_idx())
    moba_bias = _bias_tiles(rel_bias, _moba_bucket_idx(s // MOBA_BLOCK))

    for i in range(depth):
        sh_m, sc_m, g_m, sh_f, sc_f, g_f = [m.reshape(bsz, 1, d) for m in jnp.split(mods[i], 6, axis=-1)]
        j = i // 2
        if i % 2 == 0:
            dt0 = SSD_INNER + SSD_XBC
            n_in = ab_w_in.shape[-1]
            w_in = _reorder_cast(ab_w_in, j, ((SSD_INNER, dt0, 0), (0, SSD_INNER, SSD_XBC + AB_Z),
                                              (dt0 + SSD_HEADS, n_in, SSD_XBC + AB_Q), (dt0, dt0 + SSD_HEADS, SSD_XBC + AB_DT)),
                                 SSD_XBC + AB_DT, SSD_XBC + AB_COLS)
            xbc_act, proj = _proj_conv_act(x, norm_w[i, 0], sc_m, sh_m, w_in, ssd_conv_w[j], ssd_conv_b[j],
                                           "silu", n_plain=AB_COLS)
            y_a = _ssd_mixer(xbc_act, proj, _gate_rows(proj, AB_DT), ssd_dt_bias[j], ssd_a_log[j], ssd_d[j],
                             ssd_norm_w[j])
            y_b = _swa_attention(proj, swa_sinks[j], swa_bias)
            x = _matmul_resid([y_a, y_b], ab_w_out, j, x, g_m, norm_w[i, 1])
        else:
            nqkv = 3 * GDN_INNER
            ba0 = nqkv + GDN_INNER
            n_in = cd_w_in.shape[-1]
            w_in = _reorder_cast(cd_w_in, j, ((0, ba0, 0), (ba0 + 2 * GDN_HEADS, n_in, nqkv + CD_QD),
                                              (ba0, ba0 + 2 * GDN_HEADS, nqkv + CD_BA)),
                                 nqkv + CD_BA, nqkv + CD_COLS)
            qkv_act, proj = _proj_conv_act(x, norm_w[i, 0], sc_m, sh_m, w_in, gdn_conv_w[j],
                                           jnp.zeros((nqkv,), F32), "silu", n_plain=CD_COLS)
            y_c = _gdn_mixer(qkv_act, proj, _gate_rows(proj, CD_BA), gdn_dt_bias[j], gdn_a_log[j], gdn_norm_w[j])
            y_d = _moba_attention(proj, moba_bias)
            x = _matmul_resid([y_c, y_d], cd_w_out, j, x, g_m, norm_w[i, 1])
        n_up = ffn_w_up.shape[-1]
        w_up = _reorder_cast(ffn_w_up, i, ((0, n_up, 0),), n_up, n_up)
        act = _proj_conv_act(x, norm_w[i, 2], sc_f, sh_f, w_up, ffn_conv_w[i], ffn_conv_b[i], "geglu")
        x = _matmul_resid([act], ffn_w_down, i, x, g_f, norm_w[i, 3])
    return x
```

```python
import functools
import math

import numpy as np
import jax
import jax.numpy as jnp
from jax import lax
from jax.experimental import pallas as pl
from jax.experimental.pallas import tpu as pltpu

F32 = jnp.float32
BF16 = jnp.bfloat16
HIGHEST = lax.Precision.HIGHEST

D_MODEL = 1024
RMS_EPS = 1e-6
NEG = -1e30
LANES = 128
SUBLANES = 8
N_ATTN_HEADS = 8
ATTN_HEAD_DIM = 64
ATTN_WIDTH = N_ATTN_HEADS * ATTN_HEAD_DIM
REL_BUCKETS = 32
REL_MAX_DIST = 1024
SSD_HEADS = 24
SSD_HEAD_DIM = 64
SSD_INNER = SSD_HEADS * SSD_HEAD_DIM
SSD_GROUPS = 4
SSD_STATE = 128
SSD_CONV = 4
SSD_CHUNK = 128
SSD_XBC = SSD_INNER + 2 * SSD_GROUPS * SSD_STATE
SWA_KV_HEADS = 2
SWA_BLOCK = 128
GDN_HEADS = 12
GDN_HEAD_DIM = 128
GDN_INNER = GDN_HEADS * GDN_HEAD_DIM
GDN_CONV = 4
GDN_CHUNK = 64
MOBA_BLOCK = 256
MOBA_TOPK = 3
FFN_DIM = 2816
FFN_CONV = 3

AB_Z, AB_Q, AB_K, AB_V, AB_DT, AB_COLS = 0, 1536, 2048, 2176, 2304, 2560
CD_Z, CD_QD, CD_KD, CD_VD, CD_BA, CD_COLS = 0, 1536, 2048, 2560, 3072, 3328

VMEM_LIMIT = 48 * 1024 * 1024
HALO = 16


def _cparams(sem):
    return pltpu.CompilerParams(dimension_semantics=sem, vmem_limit_bytes=VMEM_LIMIT)


def _bdot(a, b):
    return jnp.dot(a.astype(BF16), b.astype(BF16), preferred_element_type=F32)


def _bdot_nt(a, b):
    return lax.dot_general(a.astype(BF16), b.astype(BF16), (((1,), (1,)), ((), ())),
                           preferred_element_type=F32)


def _split3(x):
    hi = x.astype(BF16)
    r1 = x - hi.astype(F32)
    mid = r1.astype(BF16)
    lo = (r1 - mid.astype(F32)).astype(BF16)
    return hi, mid, lo


def _sum01_left(m01, x):
    n = x.shape[1]
    y = jnp.dot(m01, jnp.concatenate(_split3(x), axis=1), preferred_element_type=F32)
    return y[:, :n] + y[:, n:2 * n] + y[:, 2 * n:]


def _sum01_right(x, m01):
    n = x.shape[0]
    y = jnp.dot(jnp.concatenate(_split3(x), axis=0), m01, preferred_element_type=F32)
    return y[:n] + y[n:2 * n] + y[2 * n:]


def _softplus(x):
    return jnp.maximum(x, 0.0) + jnp.log(1.0 + jnp.exp(-jnp.abs(x)))


def _sigmoid(x):
    return 1.0 / (1.0 + jnp.exp(-x))


def _silu(x):
    return x * _sigmoid(x)


def _window_conv(win_ref, lo, ncols, rows, w, width, row0=0, shifted_loads=False):
    if shifted_loads:
        acc = None
        for s in range(width):
            tap = win_ref[pl.ds(HALO + row0 - s, rows), lo:lo + ncols] * w[width - 1 - s:width - s, :]
            acc = tap if acc is None else acc + tap
        return acc
    xin = win_ref[pl.ds(HALO + row0 - SUBLANES, rows + SUBLANES), lo:lo + ncols]
    acc = xin[SUBLANES:, :] * w[width - 1:width, :]
    for s in range(1, width):
        acc = acc + pltpu.roll(xin, s, axis=0)[SUBLANES:, :] * w[width - 1 - s:width - s, :]
    return acc


def _mods_kernel(c_ref, w_ref, b_ref, o_ref):
    o_ref[0] = _bdot(_silu(c_ref[...]), w_ref[0]) + b_ref[0]


def _mods(c, ada_w, ada_b):
    depth, d, n = ada_w.shape
    bsz = c.shape[0]
    tn = 512
    return pl.pallas_call(
        _mods_kernel,
        grid=(depth, n // tn),
        in_specs=[pl.BlockSpec((bsz, d), lambda l, j: (0, 0)),
                  pl.BlockSpec((1, d, tn), lambda l, j: (l, 0, j)),
                  pl.BlockSpec((1, 1, tn), lambda l, j: (l, 0, j))],
        out_specs=pl.BlockSpec((1, bsz, tn), lambda l, j: (l, 0, j)),
        out_shape=jax.ShapeDtypeStruct((depth, bsz, n), F32),
        compiler_params=_cparams(("arbitrary", "arbitrary")),
        name="adaln_mods",
    )(c, ada_w, ada_b.reshape(depth, 1, n))


def _modulated_norm(x, nw, sc, sh):
    ms = jnp.mean(x * x, axis=-1, keepdims=True)
    return x * lax.rsqrt(ms + RMS_EPS) * nw * (1.0 + sc) + sh


def _mmres_kernel(*refs, splits):
    na = len(splits)
    a_refs = refs[:na]
    w_ref, x_ref, g_ref, nw_ref, o_ref = refs[na:]
    acc = None
    lo = 0
    for a_ref, k in zip(a_refs, splits):
        part = jnp.dot(a_ref[0].astype(BF16), w_ref[lo:lo + k, :].astype(BF16), preferred_element_type=F32)
        acc = part if acc is None else acc + part
        lo += k
    ms = jnp.mean(acc * acc, axis=-1, keepdims=True)
    y = acc * lax.rsqrt(ms + RMS_EPS) * nw_ref[...]
    o_ref[0] = x_ref[0] + g_ref[0] * y


def _matmul_resid(a_list, w_stack, layer, x, gate, nw, tm=1024):
    bsz, s, d = x.shape
    spt = s // tm
    splits = tuple(a.shape[-1] for a in a_list)
    ktot = sum(splits)
    in_specs = [pl.BlockSpec((1, tm, k), lambda i: (i // spt, i % spt, 0)) for k in splits]
    in_specs += [pl.BlockSpec((None, ktot, d), lambda i: (layer, 0, 0)),
                 pl.BlockSpec((1, tm, d), lambda i: (i // spt, i % spt, 0)),
                 pl.BlockSpec((1, 1, d), lambda i: (i // spt, 0, 0)),
                 pl.BlockSpec((1, d), lambda i: (0, 0))]
    return pl.pallas_call(
        functools.partial(_mmres_kernel, splits=splits),
        grid=(bsz * spt,),
        in_specs=in_specs,
        out_specs=pl.BlockSpec((1, tm, d), lambda i: (i // spt, i % spt, 0)),
        out_shape=jax.ShapeDtypeStruct((bsz, s, d), F32),
        compiler_params=_cparams(("arbitrary",)),
        name="matmul_resid",
    )(*a_list, w_stack, x, gate, nw.reshape(1, d))


def _proj_conv_kernel(*refs, offs, tc, width, rb, act, n_plain):
    if n_plain:
        xh_ref, x_ref, nw_ref, sc_ref, sh_ref, w_ref, cw_ref, cb_ref, o_ref, op_ref, h_ref, u_ref = refs
    else:
        xh_ref, x_ref, nw_ref, sc_ref, sh_ref, w_ref, cw_ref, cb_ref, o_ref, h_ref, u_ref = refs
    tm = x_ref.shape[1]
    nchunk = o_ref.shape[2] // tc
    nplain = n_plain // tc
    plain0 = w_ref.shape[1] - n_plain
    h_halo = _modulated_norm(xh_ref[0], nw_ref[...], sc_ref[0], sh_ref[0])
    h_ref[:HALO, :] = jnp.where(pl.program_id(1) > 0, h_halo, 0.0).astype(BF16)
    h_ref[HALO:, :] = _modulated_norm(x_ref[0], nw_ref[...], sc_ref[0], sh_ref[0]).astype(BF16)
    c0 = math.sqrt(2.0 / math.pi)

    nrow = HALO + tm

    def matmuls(c):
        for k, off in enumerate(offs):
            lo = off + c * tc
            u = jnp.dot(h_ref[...], w_ref[:, lo:lo + tc], preferred_element_type=F32)
            for s in range(width):
                u_ref[c % 2, k, s, pl.ds(s, nrow), :] = u

    def epilogue(c):
        for r0 in range(0, tm, rb):
            conv = []
            for k, off in enumerate(offs):
                lo = off + c * tc
                acc = cb_ref[:, lo:lo + tc]
                for s in range(width):
                    acc = acc + (u_ref[c % 2, k, s, pl.ds(HALO + r0, rb), :]
                                 * cw_ref[width - 1 - s:width - s, lo:lo + tc])
                conv.append(acc)
            if act == "geglu":
                g, v = conv
                th = jnp.tanh(g * (c0 + (c0 * 0.044715) * (g * g)))
                hg = 0.5 * g
                out = (hg + hg * th) * v
            else:
                out = _silu(conv[0])
            o_ref[0, r0:r0 + rb, c * tc:(c + 1) * tc] = out.astype(o_ref.dtype)

    def plain(p):
        lo = plain0 + p * tc
        op_ref[0, :, p * tc:(p + 1) * tc] = jnp.dot(
            h_ref[HALO:, :], w_ref[:, lo:lo + tc], preferred_element_type=F32).astype(op_ref.dtype)

    matmuls(0)
    done = 0
    for c in range(nchunk):
        if c + 1 < nchunk:
            matmuls(c + 1)
        epilogue(c)
        upto = (c + 1) * nplain // nchunk
        for p in range(done, upto):
            plain(p)
        done = upto


def _proj_conv_act(x, nw, sc, sh, w, cw, cb, act, n_plain=0, tm=512, tc=256, rb=128):
    bsz, s, d = x.shape
    n2 = w.shape[-1] - n_plain
    f = n2 // 2 if act == "geglu" else n2
    offs = (0, f) if act == "geglu" else (0,)
    width = cw.shape[0]
    hb = tm // HALO
    const = lambda shape: pl.BlockSpec(shape, lambda b_, r: (0, 0))
    rows = lambda n: pl.BlockSpec((1, tm, n), lambda b_, r: (b_, r, 0))
    in_specs = [pl.BlockSpec((1, HALO, d), lambda b_, r: (b_, jnp.maximum(r * hb - 1, 0), 0)),
                rows(d), const((1, d)),
                pl.BlockSpec((1, 1, d), lambda b_, r: (b_, 0, 0)),
                pl.BlockSpec((1, 1, d), lambda b_, r: (b_, 0, 0)),
                const((d, n2 + n_plain)),
                const((width, n2)), const((1, n2))]
    args = [x, x, nw.reshape(1, d), sc, sh, w, cw, cb.reshape(1, n2)]
    out_specs = rows(f)
    out_shape = jax.ShapeDtypeStruct((bsz, s, f), BF16)
    if n_plain:
        out_specs = (out_specs, rows(n_plain))
        out_shape = (out_shape, jax.ShapeDtypeStruct((bsz, s, n_plain), BF16))
    return pl.pallas_call(
        functools.partial(_proj_conv_kernel, offs=offs, tc=tc, width=width, rb=rb, act=act, n_plain=n_plain),
        grid=(bsz, s // tm),
        in_specs=in_specs,
        out_specs=out_specs,
        out_shape=out_shape,
        scratch_shapes=[pltpu.VMEM((HALO + tm, d), BF16),
                        pltpu.VMEM((2, len(offs), width, HALO + tm + SUBLANES, tc), F32)],
        compiler_params=_cparams(("arbitrary", "arbitrary")),
        name="proj_conv_" + act,
    )(*args)


def _rel_bucket_np(d):
    max_exact = REL_BUCKETS // 2
    d = np.maximum(d, 0)
    df = np.maximum(d, 1).astype(np.float64)
    large = max_exact + (np.log(df / max_exact) / math.log(REL_MAX_DIST / max_exact)
                         * (REL_BUCKETS - max_exact)).astype(np.int32)
    large = np.minimum(large, REL_BUCKETS - 1)
    return np.where(d < max_exact, d, large).astype(np.int32)


LOG2E = math.log2(math.e)


def _bias_kernel(tab_ref, idx_ref, o_ref, *, ranges):
    h = pl.program_id(0)
    for t, (lo, hi) in enumerate(ranges):
        idx = idx_ref[t]
        acc = jnp.full(idx.shape, NEG, F32)
        for bkt in range(lo, hi + 1):
            acc = jnp.where(idx == bkt, tab_ref[bkt, h] * LOG2E, acc)
        o_ref[0, t] = acc


def _bias_tiles(rel_bias, idx_np):
    t, r, c = idx_np.shape
    ranges = tuple((int(tile[tile >= 0].min()), int(tile.max())) for tile in idx_np)
    return pl.pallas_call(
        functools.partial(_bias_kernel, ranges=ranges),
        grid=(N_ATTN_HEADS,),
        in_specs=[pl.BlockSpec(memory_space=pltpu.SMEM),
                  pl.BlockSpec((t, r, c), lambda h: (0, 0, 0))],
        out_specs=pl.BlockSpec((1, t, r, c), lambda h: (h, 0, 0, 0)),
        out_shape=jax.ShapeDtypeStruct((N_ATTN_HEADS, t, r, c), F32),
        compiler_params=_cparams(("arbitrary",)),
        name="rel_bias_tiles",
    )(rel_bias, jnp.asarray(idx_np))


def _swa_bucket_idx():
    c = np.arange(2 * SWA_BLOCK)[:, None]
    r = np.arange(SWA_BLOCK)[None, :]
    dist = SWA_BLOCK + r - c
    return np.where((dist >= 0) & (dist < SWA_BLOCK), _rel_bucket_np(dist), -1).astype(np.int32)[None]


def _moba_bucket_idx(nblk):
    c = np.arange(MOBA_BLOCK)[:, None]
    r = np.arange(MOBA_BLOCK)[None, :]
    tiles = [np.where(m * MOBA_BLOCK + r - c >= 0, _rel_bucket_np(m * MOBA_BLOCK + r - c), -1)
             for m in range(nblk)]
    return np.stack(tiles).astype(np.int32)


def _expand_heads(v, e):
    hi = v.astype(BF16)
    lo = (v - hi.astype(F32)).astype(BF16)
    return (jnp.dot(hi, e, preferred_element_type=F32) + jnp.dot(lo, e, preferred_element_type=F32))


def _ssd_kernel(xbc_ref, z_ref, dtc_ref, dtr_ref, bias_c_ref, bias_r_ref, alog_c_ref, alog_r_ref,
                dskip_ref, nw_ref, e_ref, o_ref, state_ref):
    q = SSD_CHUNK
    gw = SSD_INNER // SSD_GROUPS
    hpg = SSD_HEADS // SSD_GROUPS

    @pl.when(pl.program_id(1) == 0)
    def _():
        state_ref[...] = jnp.zeros(state_ref.shape, F32)

    row = lax.broadcasted_iota(jnp.int32, (q, q), 0)
    col = lax.broadcasted_iota(jnp.int32, (q, q), 1)
    tril = row >= col
    tri_b = jnp.where(tril, 1.0, 0.0).astype(BF16)
    triu_b = jnp.where(row <= col, 1.0, 0.0).astype(BF16)

    dt_c = _softplus(dtc_ref[0].astype(F32) + bias_c_ref[...])
    da_c = dt_c * (-jnp.exp(alog_c_ref[...]))
    acs_c = _sum01_left(tri_b, da_c)
    dt_r = _softplus(dtr_ref[0] + bias_r_ref[...])
    da_r = dt_r * (-jnp.exp(alog_r_ref[...]))
    acs_r = _sum01_right(da_r, triu_b)

    acs_last = acs_c[q - 1:q, :]
    e = e_ref[...]
    dt_full = _expand_heads(dt_c, e)
    dtdec_full = _expand_heads(dt_c * jnp.exp(acs_last - acs_c), e)
    eacs_full = _expand_heads(jnp.exp(acs_c), e)
    cdecay_full = eacs_full[q - 1:q, :]

    xbc = xbc_ref[0].astype(F32)
    xs = xbc[:, :SSD_INNER]
    xdt = xs * dt_full
    xdec = xs * dtdec_full
    lane_half = lax.broadcasted_iota(jnp.int32, (1, LANES), 1) >> 6

    y_parts = []
    for g in range(SSD_GROUPS):
        b_g = xbc[:, SSD_INNER + g * SSD_STATE:SSD_INNER + (g + 1) * SSD_STATE]
        c_g = xbc[:, SSD_INNER + SSD_GROUPS * SSD_STATE + g * SSD_STATE:
                  SSD_INNER + SSD_GROUPS * SSD_STATE + (g + 1) * SSD_STATE]
        cb = jnp.where(tril, _bdot_nt(c_g, b_g), 0.0)
        st = state_ref[g]
        y_off = _bdot(c_g, st) * eacs_full[:, g * gw:(g + 1) * gw]
        state_ref[g] = st * cdecay_full[:, g * gw:(g + 1) * gw] + _bdot(b_g.T, xdec[:, g * gw:(g + 1) * gw])
        pair_parts = []
        for pr in range(hpg // 2):
            acc = None
            lo = g * gw + pr * LANES
            x_pair = xdt[:, lo:lo + LANES]
            for half in range(2):
                h = g * hpg + pr * 2 + half
                diff = acs_c[:, h:h + 1] - acs_r[h:h + 1, :]
                lmat = jnp.exp(jnp.minimum(diff, 0.0))
                part = _bdot(cb * lmat, jnp.where(lane_half == half, x_pair, 0.0))
                acc = part if acc is None else acc + part
            pair_parts.append(acc)
        y_diag = jnp.concatenate(pair_parts, axis=1)
        y = y_diag + y_off + dskip_ref[:, g * gw:(g + 1) * gw] * xs[:, g * gw:(g + 1) * gw]
        y = y * _silu(z_ref[0, :, g * gw:(g + 1) * gw].astype(F32))
        ms = jnp.mean(y * y, axis=-1, keepdims=True)
        y_parts.append(y * lax.rsqrt(ms + RMS_EPS) * nw_ref[:, g * gw:(g + 1) * gw])
    o_ref[0] = jnp.concatenate(y_parts, axis=1).astype(o_ref.dtype)


def _ssd_mixer(xbc_act, proj, dt_rows, dt_bias, a_log, d_skip, norm_w):
    bsz, s, _ = proj.shape
    q = SSD_CHUNK
    pad = LANES - SSD_HEADS
    bias_c = jnp.pad(dt_bias, (0, pad)).reshape(1, LANES)
    alog_c = jnp.pad(a_log, (0, pad)).reshape(1, LANES)
    e_np = np.zeros((LANES, SSD_INNER), np.float32)
    for h in range(SSD_HEADS):
        e_np[h, h * SSD_HEAD_DIM:(h + 1) * SSD_HEAD_DIM] = 1.0
    small = lambda shape: pl.BlockSpec(shape, lambda b_, c: (0, 0))
    return pl.pallas_call(
        _ssd_kernel,
        grid=(bsz, s // q),
        in_specs=[pl.BlockSpec((1, q, SSD_XBC), lambda b_, c: (b_, c, 0)),
                  pl.BlockSpec((1, q, SSD_INNER), lambda b_, c: (b_, c, AB_Z // SSD_INNER)),
                  pl.BlockSpec((1, q, LANES), lambda b_, c: (b_, c, AB_DT // LANES)),
                  pl.BlockSpec((1, LANES, q), lambda b_, c: (b_, 0, c)),
                  small((1, LANES)), small((LANES, 1)), small((1, LANES)), small((LANES, 1)),
                  small((1, SSD_INNER)), small((1, SSD_INNER)), small((LANES, SSD_INNER))],
        out_specs=pl.BlockSpec((1, q, SSD_INNER), lambda b_, c: (b_, c, 0)),
        out_shape=jax.ShapeDtypeStruct((bsz, s, SSD_INNER), BF16),
        scratch_shapes=[pltpu.VMEM((SSD_GROUPS, SSD_STATE, SSD_INNER // SSD_GROUPS), F32)],
        compiler_params=_cparams(("arbitrary", "arbitrary")),
        name="ssd_mixer",
    )(xbc_act, proj, proj, dt_rows, bias_c, bias_c.reshape(LANES, 1),
      alog_c, alog_c.reshape(LANES, 1), jnp.repeat(d_skip, SSD_HEAD_DIM).reshape(1, SSD_INNER),
      norm_w.reshape(1, SSD_INNER), jnp.asarray(e_np, BF16))


def _swa_kernel(sink_ref, q_ref, kp_ref, kc_ref, vp_ref, vc_ref, bias_ref, o_ref, s_ref):
    blk = SWA_BLOCK
    n = pl.program_id(1)
    grp = N_ATTN_HEADS // SWA_KV_HEADS
    scale = ATTN_HEAD_DIM ** -0.5
    kk = jnp.concatenate([kp_ref[0], kc_ref[0]], axis=0).astype(F32) * (scale * LOG2E)
    vv_t = jnp.concatenate([vp_ref[0], vc_ref[0]], axis=0).astype(F32).T
    lane_half = lax.broadcasted_iota(jnp.int32, (1, LANES), 1) >> 6
    c = lax.broadcasted_iota(jnp.int32, (2 * blk, blk), 0)
    valid = (c >= blk) | (n > 0)
    ms = []
    for kv in range(SWA_KV_HEADS):
        k_own = jnp.where(lane_half == kv, kk, 0.0).astype(BF16)
        k_var = [None, None]
        k_var[kv] = k_own
        k_var[1 - kv] = pltpu.roll(jnp.where(lane_half == kv, kk, 0.0), ATTN_HEAD_DIM, axis=1).astype(BF16)
        for gq in range(grp):
            h = kv * grp + gq
            q_tile = q_ref[0, :, (h // 2) * LANES:(h // 2 + 1) * LANES]
            s_t = _bdot_nt(k_var[h % 2], q_tile) + bias_ref[h, 0]
            s_t = jnp.where(valid, s_t, NEG)
            s_ref[h] = s_t
            ms.append(jnp.maximum(jnp.max(s_t, axis=0, keepdims=True), sink_ref[h] * LOG2E))
    outs = []
    for h in range(N_ATTN_HEADS):
        kv = h // grp
        p = jnp.exp2(s_ref[h] - ms[h])
        l = jnp.sum(p, axis=0, keepdims=True) + jnp.exp2(sink_ref[h] * LOG2E - ms[h])
        outs.append(_bdot(vv_t[kv * ATTN_HEAD_DIM:(kv + 1) * ATTN_HEAD_DIM, :], p) / l)
    for t in range(N_ATTN_HEADS // 2):
        pair = jnp.concatenate([outs[2 * t], outs[2 * t + 1]], axis=0)
        o_ref[0, :, t * LANES:(t + 1) * LANES] = pair.T.astype(o_ref.dtype)


def _swa_attention(proj, sinks, bias):
    bsz, s, _ = proj.shape
    blk = SWA_BLOCK
    kvw = SWA_KV_HEADS * ATTN_HEAD_DIM
    prev = lambda col: pl.BlockSpec((1, blk, kvw), lambda b_, n: (b_, jnp.maximum(n - 1, 0), col))
    cur = lambda col: pl.BlockSpec((1, blk, kvw), lambda b_, n: (b_, n, col))
    return pl.pallas_call(
        _swa_kernel,
        grid=(bsz, s // blk),
        in_specs=[pl.BlockSpec(memory_space=pltpu.SMEM),
                  pl.BlockSpec((1, blk, ATTN_WIDTH), lambda b_, n: (b_, n, AB_Q // ATTN_WIDTH)),
                  prev(AB_K // kvw), cur(AB_K // kvw), prev(AB_V // kvw), cur(AB_V // kvw),
                  pl.BlockSpec((N_ATTN_HEADS, 1, 2 * blk, blk), lambda b_, n: (0, 0, 0, 0))],
        out_specs=pl.BlockSpec((1, blk, ATTN_WIDTH), lambda b_, n: (b_, n, 0)),
        out_shape=jax.ShapeDtypeStruct((bsz, s, ATTN_WIDTH), BF16),
        scratch_shapes=[pltpu.VMEM((N_ATTN_HEADS, 2 * blk, blk), F32)],
        compiler_params=_cparams(("arbitrary", "arbitrary")),
        name="swa_attention",
    )(sinks, proj, proj, proj, proj, proj, bias)


def _moba_kernel(q_ref, k_ref, v_ref, bias_ref, o_ref, vt_ref, qm_ref, negrow_ref, s_ref, *, nblk):
    mb = MOBA_BLOCK
    dh = ATTN_HEAD_DIM
    own = pl.program_id(2)
    scale = dh ** -0.5

    @pl.when(own == 0)
    def _():
        means = []
        for j in range(nblk):
            vt_ref[j] = v_ref[0, j * mb:(j + 1) * mb, :].astype(F32).T.astype(BF16)
            means.append(jnp.mean(k_ref[0, j * mb:(j + 1) * mb, :].astype(F32), axis=0, keepdims=True))
        kmean = jnp.concatenate(means, axis=0)
        lane_half = lax.broadcasted_iota(jnp.int32, (1, LANES), 1) >> 6
        blk_id = lax.broadcasted_iota(jnp.int32, (nblk, mb), 0)
        for qb in range(nblk):
            q = q_ref[0, qb * mb:(qb + 1) * mb, :].astype(F32) * (scale * LOG2E)
            for hh in range(2):
                qm = jnp.where(lane_half == hh, q, 0.0)
                qm_ref[hh, qb] = qm.astype(BF16)
                gate = lax.dot_general(kmean, qm, (((1,), (1,)), ((), ())),
                                       preferred_element_type=F32, precision=HIGHEST)
                gate = jnp.where(blk_id < qb, gate, NEG)
                rank = jnp.zeros((nblk, mb), jnp.int32)
                for i in range(nblk):
                    gi = gate[i:i + 1, :]
                    ahead = (gi > gate) | ((gi == gate) & (i < blk_id))
                    rank = rank + jnp.where(ahead, 1, 0)
                keep = ((rank < MOBA_TOPK) & (blk_id < qb)) | (blk_id == qb)
                negrow_ref[hh, qb] = jnp.where(keep, 0.0, NEG).astype(F32)

    qms = [qm_ref[hh, own] for hh in range(2)]
    negrows = [negrow_ref[hh, own] for hh in range(2)]

    def attend(nb):
        m8 = [None, None]
        l8 = [None, None]
        acc = [None, None]

        def scores(hh, j):
            s_t = _bdot_nt(k_ref[0, j * mb:(j + 1) * mb, :], qms[hh]) + bias_ref[hh, jnp.maximum(own - j, 0)]
            s_t = s_t + negrows[hh][j:j + 1, :]
            s_ref[hh, j] = s_t
            m_j = jnp.max(s_t.reshape(mb // SUBLANES, SUBLANES, mb), axis=0)
            m8[hh] = m_j if m8[hh] is None else jnp.maximum(m8[hh], m_j)

        def weights(hh, j, m):
            p = jnp.exp2(s_ref[hh, j] - m)
            l_j = jnp.sum(p.reshape(mb // SUBLANES, SUBLANES, mb), axis=0)
            a_j = _bdot(vt_ref[j, hh * dh:(hh + 1) * dh, :], p)
            l8[hh] = l_j if l8[hh] is None else l8[hh] + l_j
            acc[hh] = a_j if acc[hh] is None else acc[hh] + a_j

        for j in range(nb):
            for hh in range(2):
                scores(hh, j)
        m = [jnp.max(m8[hh], axis=0, keepdims=True) for hh in range(2)]
        for j in range(nb):
            for hh in range(2):
                weights(hh, j, m[hh])
        out_t = jnp.concatenate([acc[hh] / jnp.sum(l8[hh], axis=0, keepdims=True) for hh in range(2)], axis=0)
        o_ref[0] = out_t.T.astype(o_ref.dtype)

    for nb in range(2, nblk + 1, 2):
        pl.when((own >= nb - 2) & (own < nb))(functools.partial(attend, nb))


def _moba_attention(proj, bias):
    bsz, s, _ = proj.shape
    mb = MOBA_BLOCK
    nblk = s // mb
    npair = N_ATTN_HEADS // 2
    return pl.pallas_call(
        functools.partial(_moba_kernel, nblk=nblk),
        grid=(bsz, npair, nblk),
        in_specs=[pl.BlockSpec((1, s, LANES), lambda b_, p, i: (b_, 0, CD_QD // LANES + p)),
                  pl.BlockSpec((1, s, LANES), lambda b_, p, i: (b_, 0, CD_KD // LANES + p)),
                  pl.BlockSpec((1, s, LANES), lambda b_, p, i: (b_, 0, CD_VD // LANES + p)),
                  pl.BlockSpec((2, nblk, mb, mb), lambda b_, p, i: (p, 0, 0, 0))],
        out_specs=pl.BlockSpec((1, mb, LANES), lambda b_, p, i: (b_, i, p)),
        out_shape=jax.ShapeDtypeStruct((bsz, s, ATTN_WIDTH), BF16),
        scratch_shapes=[pltpu.VMEM((nblk, LANES, mb), BF16),
                        pltpu.VMEM((2, nblk, mb, LANES), BF16),
                        pltpu.VMEM((2, nblk, nblk, mb), F32),
                        pltpu.VMEM((2, nblk, mb, mb), F32)],
        compiler_params=_cparams(("arbitrary", "arbitrary", "arbitrary")),
        name="moba_attention",
    )(proj, proj, proj, bias)


def _gdn_kernel(qkv_ref, z_ref, bac_ref, bar_ref, bias_c_ref, bias_r_ref, alog_c_ref, alog_r_ref,
                nw_ref, o_ref, state_ref, m_ref, attn_ref, t_ref, x_ref, rhs_ref, u_ref, wq_ref,
                kdt_ref, r_ref, glc_ref):
    t = 2 * GDN_CHUNK
    ck = GDN_CHUNK
    dk = GDN_HEAD_DIM
    nh = GDN_HEADS

    @pl.when(pl.program_id(1) == 0)
    def _():
        state_ref[...] = jnp.zeros(state_ref.shape, F32)

    row = lax.broadcasted_iota(jnp.int32, (t, t), 0)
    col = lax.broadcasted_iota(jnp.int32, (t, t), 1)
    same = (row >> 6) == (col >> 6)
    tril = same & (row >= col)
    strict = same & (row > col)
    tri_b = jnp.where(tril, 1.0, 0.0).astype(BF16)
    triu_b = jnp.where(same & (row <= col), 1.0, 0.0).astype(BF16)
    blk_b = jnp.where(same, 1.0, 0.0).astype(BF16)
    eye_f = jnp.where(row == col, 1.0, 0.0).astype(F32)
    merge_masks = [((row >> (l + 1)) == (col >> (l + 1))) & (((row >> l) & 1) == 1) & (((col >> l) & 1) == 0)
                   for l in range(int(math.log2(ck)))]
    mask_bf = [jnp.where(m, 1.0, 0.0).astype(BF16) for m in merge_masks[1:]]

    ba_c = bac_ref[0].astype(F32)
    g_c = -jnp.exp(alog_c_ref[...]) * _softplus(ba_c + bias_c_ref[...])
    sums_c = _sum01_left(jnp.concatenate([tri_b, blk_b], axis=0), g_c)
    gc_c = sums_c[:t]
    gl_c = sums_c[t:]
    g_r = -jnp.exp(alog_r_ref[...]) * _softplus(bar_ref[0] + bias_r_ref[...])
    gc_r = _sum01_right(g_r, triu_b)
    glc_ref[...] = gl_c

    def prep(h):
        q = qkv_ref[0, :, h * dk:(h + 1) * dk].astype(F32)
        k = qkv_ref[0, :, (nh + h) * dk:(nh + h + 1) * dk].astype(F32)
        v = qkv_ref[0, :, (2 * nh + h) * dk:(2 * nh + h + 1) * dk].astype(F32)
        qn = q * lax.rsqrt(jnp.sum(q * q, axis=-1, keepdims=True) + 1e-6) * (dk ** -0.5)
        kn = k * lax.rsqrt(jnp.sum(k * k, axis=-1, keepdims=True) + 1e-6)
        beta = _sigmoid(ba_c[:, h:h + 1])
        gcc = gc_c[:, nh + h:nh + h + 1]
        gcr = gc_r[nh + h:nh + h + 1, :]
        glc = gl_c[:, nh + h:nh + h + 1]
        decay = jnp.where(tril, jnp.exp(jnp.where(tril, gcc - gcr, 0.0)), 0.0)
        kb = kn * beta
        kk = _bdot_nt(jnp.concatenate([kb, qn], axis=0), kn)
        mm = jnp.where(strict, kk[:t] * decay, 0.0)
        m_ref[h] = mm.astype(BF16)
        attn_ref[h] = (kk[t:] * decay).astype(BF16)
        t_ref[h] = eye_f - jnp.where(merge_masks[0], mm, 0.0)
        egc = jnp.exp(gcc)
        rhs_ref[h] = jnp.concatenate([v * beta, kb * egc], axis=1).astype(BF16)
        q_dec = (qn * egc).astype(BF16)
        for a in range(2):
            wq_ref[h, a, ck:, :] = q_dec[a * ck:(a + 1) * ck]
        kdt_ref[h] = (kn * jnp.exp(glc - gcc)).T.astype(BF16)

    def merge_a(lvl, h):
        x_ref[h] = jnp.dot(t_ref[h].astype(BF16), m_ref[h] * mask_bf[lvl],
                           preferred_element_type=F32).astype(BF16)

    def merge_b(lvl, h):
        t_h = t_ref[h]
        t_ref[h] = t_h - jnp.dot(x_ref[h], t_h.astype(BF16), preferred_element_type=F32)

    def solve(h):
        sol = jnp.dot(t_ref[h].astype(BF16), rhs_ref[h], preferred_element_type=F32)
        u_ref[h] = sol[:, :dk]
        for a in range(2):
            wq_ref[h, a, :ck, :] = sol[a * ck:(a + 1) * ck, dk:].astype(BF16)

    zeros_half = jnp.zeros((ck, dk), F32)

    def read_state(a, h):
        r_ref[h] = jnp.dot(wq_ref[h, a], state_ref[h].astype(BF16), preferred_element_type=F32)

    def update(a, h):
        sl = slice(a * ck, (a + 1) * ck)
        v_new = u_ref[h, sl, :] - r_ref[h, :ck, :]
        v_full = jnp.concatenate([v_new, zeros_half] if a == 0 else [zeros_half, v_new], axis=0).astype(BF16)
        o = r_ref[h, ck:, :] + jnp.dot(attn_ref[h, sl, :], v_full, preferred_element_type=F32)
        gl = glc_ref[a * ck:a * ck + 1, nh + h:nh + h + 1]
        state_ref[h] = state_ref[h] * jnp.exp(gl) + jnp.dot(kdt_ref[h], v_full, preferred_element_type=F32)
        ms = jnp.mean(o * o, axis=-1, keepdims=True)
        y = o * lax.rsqrt(ms + RMS_EPS) * nw_ref[...] * _silu(z_ref[0, sl, h * dk:(h + 1) * dk].astype(F32))
        o_ref[0, sl, h * dk:(h + 1) * dk] = y.astype(o_ref.dtype)

    def steps(phase, heads):
        if phase == "prep":
            return [functools.partial(prep, h) for h in heads]
        if phase == "merge":
            return [functools.partial(fn, lvl, h) for lvl in range(len(mask_bf))
                    for fn in (merge_a, merge_b) for h in heads]
        return ([functools.partial(solve, h) for h in heads]
                + [functools.partial(fn, a, h) for a in range(2) for fn in (read_state, update) for h in heads])

    for phase in ("prep", "merge", "tail"):
        for step in steps(phase, range(nh)):
            step()


def _gdn_mixer(qkv_act, proj, ba_rows, dt_bias, a_log, norm_w):
    bsz, s, _ = proj.shape
    t = 2 * GDN_CHUNK
    nh = GDN_HEADS
    dk = GDN_HEAD_DIM
    bias_c = jnp.pad(dt_bias, (nh, LANES - 2 * nh)).reshape(1, LANES)
    alog_c = jnp.pad(a_log, (nh, LANES - 2 * nh)).reshape(1, LANES)
    small = lambda shape: pl.BlockSpec(shape, lambda b_, c: (0, 0))
    return pl.pallas_call(
        _gdn_kernel,
        grid=(bsz, s // t),
        in_specs=[pl.BlockSpec((1, t, 3 * GDN_INNER), lambda b_, c: (b_, c, 0)),
                  pl.BlockSpec((1, t, GDN_INNER), lambda b_, c: (b_, c, CD_Z // GDN_INNER)),
                  pl.BlockSpec((1, t, LANES), lambda b_, c: (b_, c, CD_BA // LANES)),
                  pl.BlockSpec((1, LANES, t), lambda b_, c: (b_, 0, c)),
                  small((1, LANES)), small((LANES, 1)), small((1, LANES)), small((LANES, 1)),
                  small((1, dk))],
        out_specs=pl.BlockSpec((1, t, GDN_INNER), lambda b_, c: (b_, c, 0)),
        out_shape=jax.ShapeDtypeStruct((bsz, s, GDN_INNER), BF16),
        scratch_shapes=[pltpu.VMEM((nh, dk, dk), F32),
                        pltpu.VMEM((nh, t, t), BF16),
                        pltpu.VMEM((nh, t, t), BF16),
                        pltpu.VMEM((nh, t, t), F32),
                        pltpu.VMEM((nh, t, t), BF16),
                        pltpu.VMEM((nh, t, 2 * dk), BF16),
                        pltpu.VMEM((nh, t, dk), F32),
                        pltpu.VMEM((nh, 2, t, dk), BF16),
                        pltpu.VMEM((nh, dk, t), BF16),
                        pltpu.VMEM((nh, t, dk), F32),
                        pltpu.VMEM((t, LANES), F32)],
        compiler_params=_cparams(("arbitrary", "arbitrary")),
        name="gdn_mixer",
    )(qkv_act, proj, proj, ba_rows, bias_c, bias_c.reshape(LANES, 1), alog_c, alog_c.reshape(LANES, 1),
      norm_w.reshape(1, dk))


def _reorder_kernel(w_ref, o_ref, *, segments, zero_from):
    rows, n = o_ref.shape
    if zero_from < n:
        o_ref[:, zero_from:] = jnp.zeros((rows, n - zero_from), o_ref.dtype)
    for lo, hi, dst in segments:
        o_ref[:, dst:dst + hi - lo] = w_ref[:, lo:hi].astype(o_ref.dtype)


def _reorder_cast(w_stack, layer, segments, zero_from, n_out, tr=256):
    nl, d, n = w_stack.shape
    if nl == 1:
        w_stack = w_stack.reshape(d, n)
        w_spec = pl.BlockSpec((tr, n), lambda r: (r, 0))
    else:
        w_spec = pl.BlockSpec((None, tr, n), lambda r: (layer, r, 0))
    return pl.pallas_call(
        functools.partial(_reorder_kernel, segments=segments, zero_from=zero_from),
        grid=(d // tr,),
        in_specs=[w_spec],
        out_specs=pl.BlockSpec((tr, n_out), lambda r: (r, 0)),
        out_shape=jax.ShapeDtypeStruct((d, n_out), BF16),
        compiler_params=_cparams(("arbitrary",)),
        name="reorder_cast_weights",
    )(w_stack)


def _gate_rows(proj, col0):
    return jnp.swapaxes(proj[:, :, col0:col0 + LANES].astype(F32), 1, 2)


def kernel(x, c, rel_bias, norm_w, ada_w, ada_b, ab_w_in, ab_w_out, ssd_conv_w, ssd_conv_b,
           ssd_dt_bias, ssd_a_log, ssd_d, ssd_norm_w, swa_sinks, cd_w_in, cd_w_out, gdn_conv_w,
           gdn_dt_bias, gdn_a_log, gdn_norm_w, ffn_w_up, ffn_conv_w, ffn_conv_b, ffn_w_down):
    bsz, s, d = x.shape
    depth = norm_w.shape[0]
    mods = _mods(c, ada_w, ada_b)
    swa_bias = _bias_tiles(rel_bias, _swa_bucket_idx())
    moba_bias = _bias_tiles(rel_bias, _moba_bucket_idx(s // MOBA_BLOCK))

    for i in range(depth):
        sh_m, sc_m, g_m, sh_f, sc_f, g_f = [m.reshape(bsz, 1, d) for m in jnp.split(mods[i], 6, axis=-1)]
        j = i // 2
        if i % 2 == 0:
            dt0 = SSD_INNER + SSD_XBC
            n_in = ab_w_in.shape[-1]
            w_in = _reorder_cast(ab_w_in, j, ((SSD_INNER, dt0, 0), (0, SSD_INNER, SSD_XBC + AB_Z),
                                              (dt0 + SSD_HEADS, n_in, SSD_XBC + AB_Q), (dt0, dt0 + SSD_HEADS, SSD_XBC + AB_DT)),
                                 SSD_XBC + AB_DT, SSD_XBC + AB_COLS)
            xbc_act, proj = _proj_conv_act(x, norm_w[i, 0], sc_m, sh_m, w_in, ssd_conv_w[j], ssd_conv_b[j],
                                           "silu", n_plain=AB_COLS)
            y_a = _ssd_mixer(xbc_act, proj, _gate_rows(proj, AB_DT), ssd_dt_bias[j], ssd_a_log[j], ssd_d[j],
                             ssd_norm_w[j])
            y_b = _swa_attention(proj, swa_sinks[j], swa_bias)
            x = _matmul_resid([y_a, y_b], ab_w_out, j, x, g_m, norm_w[i, 1])
        else:
            nqkv = 3 * GDN_INNER
            ba0 = nqkv + GDN_INNER
            n_in = cd_w_in.shape[-1]
            w_in = _reorder_cast(cd_w_in, j, ((0, ba0, 0), (ba0 + 2 * GDN_HEADS, n_in, nqkv + CD_QD),
                                              (ba0, ba0 + 2 * GDN_HEADS, nqkv + CD_BA)),
                                 nqkv + CD_BA, nqkv + CD_COLS)
            qkv_act, proj = _proj_conv_act(x, norm_w[i, 0], sc_m, sh_m, w_in, gdn_conv_w[j],
                                           jnp.zeros((nqkv,), F32), "silu", n_plain=CD_COLS)
            y_c = _gdn_mixer(qkv_act, proj, _gate_rows(proj, CD_BA), gdn_dt_bias[j], gdn_a_log[j], gdn_norm_w[j])
            y_d = _moba_attention(proj, moba_bias)
            x = _matmul_resid([y_c, y_d], cd_w_out, j, x, g_m, norm_w[i, 1])
        n_up = ffn_w_up.shape[-1]
        w_up = _reorder_cast(ffn_w_up, i, ((0, n_up, 0),), n_up, n_up)
        act = _proj_conv_act(x, norm_w[i, 2], sc_f, sh_f, w_up, ffn_conv_w[i], ffn_conv_b[i], "geglu")
        x = _matmul_resid([act], ffn_w_down, i, x, g_f, norm_w[i, 3])
    return x
```

```python
import functools
import math

import numpy as np
import jax
import jax.numpy as jnp
from jax import lax
from jax.experimental import pallas as pl
from jax.experimental.pallas import tpu as pltpu

F32 = jnp.float32
BF16 = jnp.bfloat16
HIGHEST = lax.Precision.HIGHEST

D_MODEL = 1024
RMS_EPS = 1e-6
NEG = -1e30
LANES = 128
SUBLANES = 8
N_ATTN_HEADS = 8
ATTN_HEAD_DIM = 64
ATTN_WIDTH = N_ATTN_HEADS * ATTN_HEAD_DIM
REL_BUCKETS = 32
REL_MAX_DIST = 1024
SSD_HEADS = 24
SSD_HEAD_DIM = 64
SSD_INNER = SSD_HEADS * SSD_HEAD_DIM
SSD_GROUPS = 4
SSD_STATE = 128
SSD_CONV = 4
SSD_CHUNK = 128
SSD_XBC = SSD_INNER + 2 * SSD_GROUPS * SSD_STATE
SWA_KV_HEADS = 2
SWA_BLOCK = 128
GDN_HEADS = 12
GDN_HEAD_DIM = 128
GDN_INNER = GDN_HEADS * GDN_HEAD_DIM
GDN_CONV = 4
GDN_CHUNK = 64
MOBA_BLOCK = 256
MOBA_TOPK = 3
FFN_DIM = 2816
FFN_CONV = 3

AB_Z, AB_Q, AB_K, AB_V, AB_DT, AB_COLS = 0, 1536, 2048, 2176, 2304, 2560
CD_Z, CD_QD, CD_KD, CD_VD, CD_BA, CD_COLS = 0, 1536, 2048, 2560, 3072, 3328

VMEM_LIMIT = 48 * 1024 * 1024
HALO = 16


def _cparams(sem):
    return pltpu.CompilerParams(dimension_semantics=sem, vmem_limit_bytes=VMEM_LIMIT)


def _bdot(a, b):
    return jnp.dot(a.astype(BF16), b.astype(BF16), preferred_element_type=F32)


def _bdot_nt(a, b):
    return lax.dot_general(a.astype(BF16), b.astype(BF16), (((1,), (1,)), ((), ())),
                           preferred_element_type=F32)


def _split3(x):
    hi = x.astype(BF16)
    r1 = x - hi.astype(F32)
    mid = r1.astype(BF16)
    lo = (r1 - mid.astype(F32)).astype(BF16)
    return hi, mid, lo


def _sum01_left(m01, x):
    n = x.shape[1]
    y = jnp.dot(m01, jnp.concatenate(_split3(x), axis=1), preferred_element_type=F32)
    return y[:, :n] + y[:, n:2 * n] + y[:, 2 * n:]


def _sum01_right(x, m01):
    n = x.shape[0]
    y = jnp.dot(jnp.concatenate(_split3(x), axis=0), m01, preferred_element_type=F32)
    return y[:n] + y[n:2 * n] + y[2 * n:]


def _softplus(x):
    return jnp.maximum(x, 0.0) + jnp.log(1.0 + jnp.exp(-jnp.abs(x)))


def _sigmoid(x):
    return 1.0 / (1.0 + jnp.exp(-x))


def _silu(x):
    return x * _sigmoid(x)


def _window_conv(win_ref, lo, ncols, rows, w, width, row0=0, shifted_loads=False):
    if shifted_loads:
        acc = None
        for s in range(width):
            tap = win_ref[pl.ds(HALO + row0 - s, rows), lo:lo + ncols] * w[width - 1 - s:width - s, :]
            acc = tap if acc is None else acc + tap
        return acc
    xin = win_ref[pl.ds(HALO + row0 - SUBLANES, rows + SUBLANES), lo:lo + ncols]
    acc = xin[SUBLANES:, :] * w[width - 1:width, :]
    for s in range(1, width):
        acc = acc + pltpu.roll(xin, s, axis=0)[SUBLANES:, :] * w[width - 1 - s:width - s, :]
    return acc


def _mods_kernel(c_ref, w_ref, b_ref, o_ref):
    o_ref[0] = _bdot(_silu(c_ref[...]), w_ref[0]) + b_ref[0]


def _mods(c, ada_w, ada_b):
    depth, d, n = ada_w.shape
    bsz = c.shape[0]
    tn = 512
    return pl.pallas_call(
        _mods_kernel,
        grid=(depth, n // tn),
        in_specs=[pl.BlockSpec((bsz, d), lambda l, j: (0, 0)),
                  pl.BlockSpec((1, d, tn), lambda l, j: (l, 0, j)),
                  pl.BlockSpec((1, 1, tn), lambda l, j: (l, 0, j))],
        out_specs=pl.BlockSpec((1, bsz, tn), lambda l, j: (l, 0, j)),
        out_shape=jax.ShapeDtypeStruct((depth, bsz, n), F32),
        compiler_params=_cparams(("arbitrary", "arbitrary")),
        name="adaln_mods",
    )(c, ada_w, ada_b.reshape(depth, 1, n))


def _modulated_norm(x, nw, sc, sh):
    ms = jnp.mean(x * x, axis=-1, keepdims=True)
    return x * lax.rsqrt(ms + RMS_EPS) * nw * (1.0 + sc) + sh


def _mmres_kernel(*refs, splits):
    na = len(splits)
    a_refs = refs[:na]
    w_ref, x_ref, g_ref, nw_ref, o_ref = refs[na:]
    acc = None
    lo = 0
    for a_ref, k in zip(a_refs, splits):
        part = jnp.dot(a_ref[0].astype(BF16), w_ref[lo:lo + k, :].astype(BF16), preferred_element_type=F32)
        acc = part if acc is None else acc + part
        lo += k
    ms = jnp.mean(acc * acc, axis=-1, keepdims=True)
    y = acc * lax.rsqrt(ms + RMS_EPS) * nw_ref[...]
    o_ref[0] = x_ref[0] + g_ref[0] * y


def _matmul_resid(a_list, w_stack, layer, x, gate, nw, tm=1024):
    bsz, s, d = x.shape
    spt = s // tm
    splits = tuple(a.shape[-1] for a in a_list)
    ktot = sum(splits)
    in_specs = [pl.BlockSpec((1, tm, k), lambda i: (i // spt, i % spt, 0)) for k in splits]
    in_specs += [pl.BlockSpec((None, ktot, d), lambda i: (layer, 0, 0)),
                 pl.BlockSpec((1, tm, d), lambda i: (i // spt, i % spt, 0)),
                 pl.BlockSpec((1, 1, d), lambda i: (i // spt, 0, 0)),
                 pl.BlockSpec((1, d), lambda i: (0, 0))]
    return pl.pallas_call(
        functools.partial(_mmres_kernel, splits=splits),
        grid=(bsz * spt,),
        in_specs=in_specs,
        out_specs=pl.BlockSpec((1, tm, d), lambda i: (i // spt, i % spt, 0)),
        out_shape=jax.ShapeDtypeStruct((bsz, s, d), F32),
        compiler_params=_cparams(("arbitrary",)),
        name="matmul_resid",
    )(*a_list, w_stack, x, gate, nw.reshape(1, d))


def _proj_conv_kernel(*refs, offs, tc, width, rb, act, n_plain):
    if n_plain:
        xh_ref, x_ref, nw_ref, sc_ref, sh_ref, w_ref, cw_ref, cb_ref, o_ref, op_ref, h_ref, u_ref = refs
    else:
        xh_ref, x_ref, nw_ref, sc_ref, sh_ref, w_ref, cw_ref, cb_ref, o_ref, h_ref, u_ref = refs
    tm = x_ref.shape[1]
    nchunk = o_ref.shape[2] // tc
    nplain = n_plain // tc
    plain0 = w_ref.shape[1] - n_plain
    h_halo = _modulated_norm(xh_ref[0], nw_ref[...], sc_ref[0], sh_ref[0])
    h_ref[:HALO, :] = jnp.where(pl.program_id(1) > 0, h_halo, 0.0).astype(BF16)
    h_ref[HALO:, :] = _modulated_norm(x_ref[0], nw_ref[...], sc_ref[0], sh_ref[0]).astype(BF16)
    c0 = math.sqrt(2.0 / math.pi)

    nrow = HALO + tm

    def matmuls(c):
        for k, off in enumerate(offs):
            lo = off + c * tc
            u = jnp.dot(h_ref[...], w_ref[:, lo:lo + tc], preferred_element_type=F32)
            for s in range(width):
                u_ref[c % 2, k, s, pl.ds(s, nrow), :] = u

    def epilogue(c):
        for r0 in range(0, tm, rb):
            conv = []
            for k, off in enumerate(offs):
                lo = off + c * tc
                acc = cb_ref[:, lo:lo + tc]
                for s in range(width):
                    acc = acc + (u_ref[c % 2, k, s, pl.ds(HALO + r0, rb), :]
                                 * cw_ref[width - 1 - s:width - s, lo:lo + tc])
                conv.append(acc)
            if act == "geglu":
                g, v = conv
                th = jnp.tanh(g * (c0 + (c0 * 0.044715) * (g * g)))
                hg = 0.5 * g
                out = (hg + hg * th) * v
            else:
                out = _silu(conv[0])
            o_ref[0, r0:r0 + rb, c * tc:(c + 1) * tc] = out.astype(o_ref.dtype)

    def plain(p):
        lo = plain0 + p * tc
        op_ref[0, :, p * tc:(p + 1) * tc] = jnp.dot(
            h_ref[HALO:, :], w_ref[:, lo:lo + tc], preferred_element_type=F32).astype(op_ref.dtype)

    matmuls(0)
    done = 0
    for c in range(nchunk):
        if c + 1 < nchunk:
            matmuls(c + 1)
        epilogue(c)
        upto = (c + 1) * nplain // nchunk
        for p in range(done, upto):
            plain(p)
        done = upto


def _proj_conv_act(x, nw, sc, sh, w, cw, cb, act, n_plain=0, tm=512, tc=256, rb=128):
    bsz, s, d = x.shape
    n2 = w.shape[-1] - n_plain
    f = n2 // 2 if act == "geglu" else n2
    offs = (0, f) if act == "geglu" else (0,)
    width = cw.shape[0]
    hb = tm // HALO
    const = lambda shape: pl.BlockSpec(shape, lambda b_, r: (0, 0))
    rows = lambda n: pl.BlockSpec((1, tm, n), lambda b_, r: (b_, r, 0))
    in_specs = [pl.BlockSpec((1, HALO, d), lambda b_, r: (b_, jnp.maximum(r * hb - 1, 0), 0)),
                rows(d), const((1, d)),
                pl.BlockSpec((1, 1, d), lambda b_, r: (b_, 0, 0)),
                pl.BlockSpec((1, 1, d), lambda b_, r: (b_, 0, 0)),
                const((d, n2 + n_plain)),
                const((width, n2)), const((1, n2))]
    args = [x, x, nw.reshape(1, d), sc, sh, w, cw, cb.reshape(1, n2)]
    out_specs = rows(f)
    out_shape = jax.ShapeDtypeStruct((bsz, s, f), BF16)
    if n_plain:
        out_specs = (out_specs, rows(n_plain))
        out_shape = (out_shape, jax.ShapeDtypeStruct((bsz, s, n_plain), BF16))
    return pl.pallas_call(
        functools.partial(_proj_conv_kernel, offs=offs, tc=tc, width=width, rb=rb, act=act, n_plain=n_plain),
        grid=(bsz, s // tm),
        in_specs=in_specs,
        out_specs=out_specs,
        out_shape=out_shape,
        scratch_shapes=[pltpu.VMEM((HALO + tm, d), BF16),
                        pltpu.VMEM((2, len(offs), width, HALO + tm + SUBLANES, tc), F32)],
        compiler_params=_cparams(("arbitrary", "arbitrary")),
        name="proj_conv_" + act,
    )(*args)


def _rel_bucket_np(d):
    max_exact = REL_BUCKETS // 2
    d = np.maximum(d, 0)
    df = np.maximum(d, 1).astype(np.float64)
    large = max_exact + (np.log(df / max_exact) / math.log(REL_MAX_DIST / max_exact)
                         * (REL_BUCKETS - max_exact)).astype(np.int32)
    large = np.minimum(large, REL_BUCKETS - 1)
    return np.where(d < max_exact, d, large).astype(np.int32)


LOG2E = math.log2(math.e)


def _bias_kernel(tab_ref, idx_ref, o_ref, *, ranges):
    h = pl.program_id(0)
    for t, (lo, hi) in enumerate(ranges):
        idx = idx_ref[t]
        acc = jnp.full(idx.shape, NEG, F32)
        for bkt in range(lo, hi + 1):
            acc = jnp.where(idx == bkt, tab_ref[bkt, h] * LOG2E, acc)
        o_ref[0, t] = acc


def _bias_tiles(rel_bias, idx_np):
    t, r, c = idx_np.shape
    ranges = tuple((int(tile[tile >= 0].min()), int(tile.max())) for tile in idx_np)
    return pl.pallas_call(
        functools.partial(_bias_kernel, ranges=ranges),
        grid=(N_ATTN_HEADS,),
        in_specs=[pl.BlockSpec(memory_space=pltpu.SMEM),
                  pl.BlockSpec((t, r, c), lambda h: (0, 0, 0))],
        out_specs=pl.BlockSpec((1, t, r, c), lambda h: (h, 0, 0, 0)),
        out_shape=jax.ShapeDtypeStruct((N_ATTN_HEADS, t, r, c), F32),
        compiler_params=_cparams(("arbitrary",)),
        name="rel_bias_tiles",
    )(rel_bias, jnp.asarray(idx_np))


def _swa_bucket_idx():
    c = np.arange(2 * SWA_BLOCK)[:, None]
    r = np.arange(SWA_BLOCK)[None, :]
    dist = SWA_BLOCK + r - c
    return np.where((dist >= 0) & (dist < SWA_BLOCK), _rel_bucket_np(dist), -1).astype(np.int32)[None]


def _moba_bucket_idx(nblk):
    c = np.arange(MOBA_BLOCK)[:, None]
    r = np.arange(MOBA_BLOCK)[None, :]
    tiles = [np.where(m * MOBA_BLOCK + r - c >= 0, _rel_bucket_np(m * MOBA_BLOCK + r - c), -1)
             for m in range(nblk)]
    return np.stack(tiles).astype(np.int32)


def _expand_heads(v, e):
    hi = v.astype(BF16)
    lo = (v - hi.astype(F32)).astype(BF16)
    return (jnp.dot(hi, e, preferred_element_type=F32) + jnp.dot(lo, e, preferred_element_type=F32))


def _ssd_kernel(xbc_ref, z_ref, dtc_ref, dtr_ref, bias_c_ref, bias_r_ref, alog_c_ref, alog_r_ref,
                dskip_ref, nw_ref, e_ref, o_ref, state_ref):
    q = SSD_CHUNK
    gw = SSD_INNER // SSD_GROUPS
    hpg = SSD_HEADS // SSD_GROUPS

    @pl.when(pl.program_id(1) == 0)
    def _():
        state_ref[...] = jnp.zeros(state_ref.shape, F32)

    row = lax.broadcasted_iota(jnp.int32, (q, q), 0)
    col = lax.broadcasted_iota(jnp.int32, (q, q), 1)
    tril = row >= col
    tri_b = jnp.where(tril, 1.0, 0.0).astype(BF16)
    triu_b = jnp.where(row <= col, 1.0, 0.0).astype(BF16)

    dt_c = _softplus(dtc_ref[0].astype(F32) + bias_c_ref[...])
    da_c = dt_c * (-jnp.exp(alog_c_ref[...]))
    acs_c = _sum01_left(tri_b, da_c)
    dt_r = _softplus(dtr_ref[0] + bias_r_ref[...])
    da_r = dt_r * (-jnp.exp(alog_r_ref[...]))
    acs_r = _sum01_right(da_r, triu_b)

    acs_last = acs_c[q - 1:q, :]
    e = e_ref[...]
    dt_full = _expand_heads(dt_c, e)
    dtdec_full = _expand_heads(dt_c * jnp.exp(acs_last - acs_c), e)
    eacs_full = _expand_heads(jnp.exp(acs_c), e)
    cdecay_full = eacs_full[q - 1:q, :]

    xbc = xbc_ref[0].astype(F32)
    xs = xbc[:, :SSD_INNER]
    xdt = xs * dt_full
    xdec = xs * dtdec_full
    lane_half = lax.broadcasted_iota(jnp.int32, (1, LANES), 1) >> 6

    y_parts = []
    for g in range(SSD_GROUPS):
        b_g = xbc[:, SSD_INNER + g * SSD_STATE:SSD_INNER + (g + 1) * SSD_STATE]
        c_g = xbc[:, SSD_INNER + SSD_GROUPS * SSD_STATE + g * SSD_STATE:
                  SSD_INNER + SSD_GROUPS * SSD_STATE + (g + 1) * SSD_STATE]
        cb = jnp.where(tril, _bdot_nt(c_g, b_g), 0.0)
        st = state_ref[g]
        y_off = _bdot(c_g, st) * eacs_full[:, g * gw:(g + 1) * gw]
        state_ref[g] = st * cdecay_full[:, g * gw:(g + 1) * gw] + _bdot(b_g.T, xdec[:, g * gw:(g + 1) * gw])
        pair_parts = []
        for pr in range(hpg // 2):
            acc = None
            lo = g * gw + pr * LANES
            x_pair = xdt[:, lo:lo + LANES]
            for half in range(2):
                h = g * hpg + pr * 2 + half
                diff = acs_c[:, h:h + 1] - acs_r[h:h + 1, :]
                lmat = jnp.exp(jnp.minimum(diff, 0.0))
                part = _bdot(cb * lmat, jnp.where(lane_half == half, x_pair, 0.0))
                acc = part if acc is None else acc + part
            pair_parts.append(acc)
        y_diag = jnp.concatenate(pair_parts, axis=1)
        y = y_diag + y_off + dskip_ref[:, g * gw:(g + 1) * gw] * xs[:, g * gw:(g + 1) * gw]
        y = y * _silu(z_ref[0, :, g * gw:(g + 1) * gw].astype(F32))
        ms = jnp.mean(y * y, axis=-1, keepdims=True)
        y_parts.append(y * lax.rsqrt(ms + RMS_EPS) * nw_ref[:, g * gw:(g + 1) * gw])
    o_ref[0] = jnp.concatenate(y_parts, axis=1).astype(o_ref.dtype)


N_SSD_IN = 11


def _layer0_mixers_kernel(*refs):
    ssd_in, swa_in = refs[:N_SSD_IN], refs[N_SSD_IN:N_SSD_IN + 7]
    o_ssd, o_swa, state_ref, s_ref = refs[N_SSD_IN + 7:]
    _ssd_kernel(*ssd_in, o_ssd, state_ref)
    _swa_kernel(*swa_in, o_swa, s_ref)


def _layer0_mixers(xbc_act, proj, dt_rows, dt_bias, a_log, d_skip, norm_w, sinks, swa_bias):
    bsz, s, _ = proj.shape
    q = SSD_CHUNK
    assert SWA_BLOCK == q
    pad = LANES - SSD_HEADS
    bias_c = jnp.pad(dt_bias, (0, pad)).reshape(1, LANES)
    alog_c = jnp.pad(a_log, (0, pad)).reshape(1, LANES)
    e_np = np.zeros((LANES, SSD_INNER), np.float32)
    for h in range(SSD_HEADS):
        e_np[h, h * SSD_HEAD_DIM:(h + 1) * SSD_HEAD_DIM] = 1.0
    small = lambda shape: pl.BlockSpec(shape, lambda b_, c: (0, 0))
    rows = lambda n, col: pl.BlockSpec((1, q, n), lambda b_, c: (b_, c, col))
    kvw = SWA_KV_HEADS * ATTN_HEAD_DIM
    prev = lambda col: pl.BlockSpec((1, q, kvw), lambda b_, c: (b_, jnp.maximum(c - 1, 0), col))
    ssd_specs = [rows(SSD_XBC, 0), rows(SSD_INNER, AB_Z // SSD_INNER), rows(LANES, AB_DT // LANES),
                 pl.BlockSpec((1, LANES, q), lambda b_, c: (b_, 0, c)),
                 small((1, LANES)), small((LANES, 1)), small((1, LANES)), small((LANES, 1)),
                 small((1, SSD_INNER)), small((1, SSD_INNER)), small((LANES, SSD_INNER))]
    assert len(ssd_specs) == N_SSD_IN
    swa_specs = [pl.BlockSpec(memory_space=pltpu.SMEM), rows(ATTN_WIDTH, AB_Q // ATTN_WIDTH),
                 prev(AB_K // kvw), rows(kvw, AB_K // kvw), prev(AB_V // kvw), rows(kvw, AB_V // kvw),
                 pl.BlockSpec((N_ATTN_HEADS, 1, 2 * q, q), lambda b_, c: (0, 0, 0, 0))]
    return pl.pallas_call(
        _layer0_mixers_kernel,
        grid=(bsz, s // q),
        in_specs=ssd_specs + swa_specs,
        out_specs=(rows(SSD_INNER, 0), rows(ATTN_WIDTH, 0)),
        out_shape=(jax.ShapeDtypeStruct((bsz, s, SSD_INNER), BF16),
                   jax.ShapeDtypeStruct((bsz, s, ATTN_WIDTH), BF16)),
        scratch_shapes=[pltpu.VMEM((SSD_GROUPS, SSD_STATE, SSD_INNER // SSD_GROUPS), F32),
                        pltpu.VMEM((N_ATTN_HEADS, 2 * q, q), F32)],
        compiler_params=_cparams(("arbitrary", "arbitrary")),
        name="ssd_swa_mixers",
    )(xbc_act, proj, proj, dt_rows, bias_c, bias_c.reshape(LANES, 1),
      alog_c, alog_c.reshape(LANES, 1), jnp.repeat(d_skip, SSD_HEAD_DIM).reshape(1, SSD_INNER),
      norm_w.reshape(1, SSD_INNER), jnp.asarray(e_np, BF16),
      sinks, proj, proj, proj, proj, proj, swa_bias)


def _swa_kernel(sink_ref, q_ref, kp_ref, kc_ref, vp_ref, vc_ref, bias_ref, o_ref, s_ref):
    blk = SWA_BLOCK
    n = pl.program_id(1)
    grp = N_ATTN_HEADS // SWA_KV_HEADS
    scale = ATTN_HEAD_DIM ** -0.5
    kk = jnp.concatenate([kp_ref[0], kc_ref[0]], axis=0).astype(F32) * (scale * LOG2E)
    vv_t = jnp.concatenate([vp_ref[0], vc_ref[0]], axis=0).astype(F32).T
    lane_half = lax.broadcasted_iota(jnp.int32, (1, LANES), 1) >> 6
    c = lax.broadcasted_iota(jnp.int32, (2 * blk, blk), 0)
    valid = (c >= blk) | (n > 0)
    ms = []
    for kv in range(SWA_KV_HEADS):
        k_own = jnp.where(lane_half == kv, kk, 0.0).astype(BF16)
        k_var = [None, None]
        k_var[kv] = k_own
        k_var[1 - kv] = pltpu.roll(jnp.where(lane_half == kv, kk, 0.0), ATTN_HEAD_DIM, axis=1).astype(BF16)
        for gq in range(grp):
            h = kv * grp + gq
            q_tile = q_ref[0, :, (h // 2) * LANES:(h // 2 + 1) * LANES]
            s_t = _bdot_nt(k_var[h % 2], q_tile) + bias_ref[h, 0]
            s_t = jnp.where(valid, s_t, NEG)
            s_ref[h] = s_t
            ms.append(jnp.maximum(jnp.max(s_t, axis=0, keepdims=True), sink_ref[h] * LOG2E))
    outs = []
    for h in range(N_ATTN_HEADS):
        kv = h // grp
        p = jnp.exp2(s_ref[h] - ms[h])
        l = jnp.sum(p, axis=0, keepdims=True) + jnp.exp2(sink_ref[h] * LOG2E - ms[h])
        outs.append(_bdot(vv_t[kv * ATTN_HEAD_DIM:(kv + 1) * ATTN_HEAD_DIM, :], p) / l)
    for t in range(N_ATTN_HEADS // 2):
        pair = jnp.concatenate([outs[2 * t], outs[2 * t + 1]], axis=0)
        o_ref[0, :, t * LANES:(t + 1) * LANES] = pair.T.astype(o_ref.dtype)


def _moba_kernel(q_ref, k_ref, v_ref, bias_ref, o_ref, vt_ref, qm_ref, negrow_ref, s_ref, *, nblk):
    mb = MOBA_BLOCK
    dh = ATTN_HEAD_DIM
    own = pl.program_id(2)
    scale = dh ** -0.5

    @pl.when(own == 0)
    def _():
        means = []
        for j in range(nblk):
            vt_ref[j] = v_ref[0, j * mb:(j + 1) * mb, :].astype(F32).T.astype(BF16)
            means.append(jnp.mean(k_ref[0, j * mb:(j + 1) * mb, :].astype(F32), axis=0, keepdims=True))
        kmean = jnp.concatenate(means, axis=0)
        lane_half = lax.broadcasted_iota(jnp.int32, (1, LANES), 1) >> 6
        blk_id = lax.broadcasted_iota(jnp.int32, (nblk, mb), 0)
        for qb in range(nblk):
            q = q_ref[0, qb * mb:(qb + 1) * mb, :].astype(F32) * (scale * LOG2E)
            for hh in range(2):
                qm = jnp.where(lane_half == hh, q, 0.0)
                qm_ref[hh, qb] = qm.astype(BF16)
                gate = lax.dot_general(kmean, qm, (((1,), (1,)), ((), ())),
                                       preferred_element_type=F32, precision=HIGHEST)
                gate = jnp.where(blk_id < qb, gate, NEG)
                rank = jnp.zeros((nblk, mb), jnp.int32)
                for i in range(nblk):
                    gi = gate[i:i + 1, :]
                    ahead = (gi > gate) | ((gi == gate) & (i < blk_id))
                    rank = rank + jnp.where(ahead, 1, 0)
                keep = ((rank < MOBA_TOPK) & (blk_id < qb)) | (blk_id == qb)
                negrow_ref[hh, qb] = jnp.where(keep, 0.0, NEG).astype(F32)

    qms = [qm_ref[hh, own] for hh in range(2)]
    negrows = [negrow_ref[hh, own] for hh in range(2)]

    def attend(nb):
        m8 = [None, None]
        l8 = [None, None]
        acc = [None, None]

        def scores(hh, j):
            s_t = _bdot_nt(k_ref[0, j * mb:(j + 1) * mb, :], qms[hh]) + bias_ref[hh, jnp.maximum(own - j, 0)]
            s_t = s_t + negrows[hh][j:j + 1, :]
            s_ref[hh, j] = s_t
            m_j = jnp.max(s_t.reshape(mb // SUBLANES, SUBLANES, mb), axis=0)
            m8[hh] = m_j if m8[hh] is None else jnp.maximum(m8[hh], m_j)

        def weights(hh, j, m):
            p = jnp.exp2(s_ref[hh, j] - m)
            l_j = jnp.sum(p.reshape(mb // SUBLANES, SUBLANES, mb), axis=0)
            a_j = _bdot(vt_ref[j, hh * dh:(hh + 1) * dh, :], p)
            l8[hh] = l_j if l8[hh] is None else l8[hh] + l_j
            acc[hh] = a_j if acc[hh] is None else acc[hh] + a_j

        for j in range(nb):
            for hh in range(2):
                scores(hh, j)
        m = [jnp.max(m8[hh], axis=0, keepdims=True) for hh in range(2)]
        for j in range(nb):
            for hh in range(2):
                weights(hh, j, m[hh])
        out_t = jnp.concatenate([acc[hh] / jnp.sum(l8[hh], axis=0, keepdims=True) for hh in range(2)], axis=0)
        o_ref[0] = out_t.T.astype(o_ref.dtype)

    for nb in range(2, nblk + 1, 2):
        pl.when((own >= nb - 2) & (own < nb))(functools.partial(attend, nb))


def _moba_attention(proj, bias):
    bsz, s, _ = proj.shape
    mb = MOBA_BLOCK
    nblk = s // mb
    npair = N_ATTN_HEADS // 2
    return pl.pallas_call(
        functools.partial(_moba_kernel, nblk=nblk),
        grid=(bsz, npair, nblk),
        in_specs=[pl.BlockSpec((1, s, LANES), lambda b_, p, i: (b_, 0, CD_QD // LANES + p)),
                  pl.BlockSpec((1, s, LANES), lambda b_, p, i: (b_, 0, CD_KD // LANES + p)),
                  pl.BlockSpec((1, s, LANES), lambda b_, p, i: (b_, 0, CD_VD // LANES + p)),
                  pl.BlockSpec((2, nblk, mb, mb), lambda b_, p, i: (p, 0, 0, 0))],
        out_specs=pl.BlockSpec((1, mb, LANES), lambda b_, p, i: (b_, i, p)),
        out_shape=jax.ShapeDtypeStruct((bsz, s, ATTN_WIDTH), BF16),
        scratch_shapes=[pltpu.VMEM((nblk, LANES, mb), BF16),
                        pltpu.VMEM((2, nblk, mb, LANES), BF16),
                        pltpu.VMEM((2, nblk, nblk, mb), F32),
                        pltpu.VMEM((2, nblk, mb, mb), F32)],
        compiler_params=_cparams(("arbitrary", "arbitrary", "arbitrary")),
        name="moba_attention",
    )(proj, proj, proj, bias)


def _gdn_kernel(qkv_ref, z_ref, bac_ref, bar_ref, bias_c_ref, bias_r_ref, alog_c_ref, alog_r_ref,
                nw_ref, o_ref, state_ref, m_ref, attn_ref, t_ref, x_ref, rhs_ref, u_ref, wq_ref,
                kdt_ref, r_ref, glc_ref):
    t = 2 * GDN_CHUNK
    ck = GDN_CHUNK
    dk = GDN_HEAD_DIM
    nh = GDN_HEADS

    @pl.when(pl.program_id(1) == 0)
    def _():
        state_ref[...] = jnp.zeros(state_ref.shape, F32)

    row = lax.broadcasted_iota(jnp.int32, (t, t), 0)
    col = lax.broadcasted_iota(jnp.int32, (t, t), 1)
    same = (row >> 6) == (col >> 6)
    tril = same & (row >= col)
    strict = same & (row > col)
    tri_b = jnp.where(tril, 1.0, 0.0).astype(BF16)
    triu_b = jnp.where(same & (row <= col), 1.0, 0.0).astype(BF16)
    blk_b = jnp.where(same, 1.0, 0.0).astype(BF16)
    eye_f = jnp.where(row == col, 1.0, 0.0).astype(F32)
    merge_masks = [((row >> (l + 1)) == (col >> (l + 1))) & (((row >> l) & 1) == 1) & (((col >> l) & 1) == 0)
                   for l in range(int(math.log2(ck)))]
    mask_bf = [jnp.where(m, 1.0, 0.0).astype(BF16) for m in merge_masks[1:]]

    ba_c = bac_ref[0].astype(F32)
    g_c = -jnp.exp(alog_c_ref[...]) * _softplus(ba_c + bias_c_ref[...])
    sums_c = _sum01_left(jnp.concatenate([tri_b, blk_b], axis=0), g_c)
    gc_c = sums_c[:t]
    gl_c = sums_c[t:]
    g_r = -jnp.exp(alog_r_ref[...]) * _softplus(bar_ref[0] + bias_r_ref[...])
    gc_r = _sum01_right(g_r, triu_b)
    glc_ref[...] = gl_c

    def prep(h):
        q = qkv_ref[0, :, h * dk:(h + 1) * dk].astype(F32)
        k = qkv_ref[0, :, (nh + h) * dk:(nh + h + 1) * dk].astype(F32)
        v = qkv_ref[0, :, (2 * nh + h) * dk:(2 * nh + h + 1) * dk].astype(F32)
        qn = q * lax.rsqrt(jnp.sum(q * q, axis=-1, keepdims=True) + 1e-6) * (dk ** -0.5)
        kn = k * lax.rsqrt(jnp.sum(k * k, axis=-1, keepdims=True) + 1e-6)
        beta = _sigmoid(ba_c[:, h:h + 1])
        gcc = gc_c[:, nh + h:nh + h + 1]
        gcr = gc_r[nh + h:nh + h + 1, :]
        glc = gl_c[:, nh + h:nh + h + 1]
        decay = jnp.where(tril, jnp.exp(jnp.where(tril, gcc - gcr, 0.0)), 0.0)
        kb = kn * beta
        kk = _bdot_nt(jnp.concatenate([kb, qn], axis=0), kn)
        mm = jnp.where(strict, kk[:t] * decay, 0.0)
        m_ref[h] = mm.astype(BF16)
        attn_ref[h] = (kk[t:] * decay).astype(BF16)
        t_ref[h] = eye_f - jnp.where(merge_masks[0], mm, 0.0)
        egc = jnp.exp(gcc)
        rhs_ref[h] = jnp.concatenate([v * beta, kb * egc], axis=1).astype(BF16)
        q_dec = (qn * egc).astype(BF16)
        for a in range(2):
            wq_ref[h, a, ck:, :] = q_dec[a * ck:(a + 1) * ck]
        kdt_ref[h] = (kn * jnp.exp(glc - gcc)).T.astype(BF16)

    def merge_a(lvl, h):
        x_ref[h] = jnp.dot(t_ref[h].astype(BF16), m_ref[h] * mask_bf[lvl],
                           preferred_element_type=F32).astype(BF16)

    def merge_b(lvl, h):
        t_h = t_ref[h]
        t_ref[h] = t_h - jnp.dot(x_ref[h], t_h.astype(BF16), preferred_element_type=F32)

    def solve(h):
        sol = jnp.dot(t_ref[h].astype(BF16), rhs_ref[h], preferred_element_type=F32)
        u_ref[h] = sol[:, :dk]
        for a in range(2):
            wq_ref[h, a, :ck, :] = sol[a * ck:(a + 1) * ck, dk:].astype(BF16)

    zeros_half = jnp.zeros((ck, dk), F32)

    def read_state(a, h):
        r_ref[h] = jnp.dot(wq_ref[h, a], state_ref[h].astype(BF16), preferred_element_type=F32)

    def update(a, h):
        sl = slice(a * ck, (a + 1) * ck)
        v_new = u_ref[h, sl, :] - r_ref[h, :ck, :]
        v_full = jnp.concatenate([v_new, zeros_half] if a == 0 else [zeros_half, v_new], axis=0).astype(BF16)
        o = r_ref[h, ck:, :] + jnp.dot(attn_ref[h, sl, :], v_full, preferred_element_type=F32)
        gl = glc_ref[a * ck:a * ck + 1, nh + h:nh + h + 1]
        state_ref[h] = state_ref[h] * jnp.exp(gl) + jnp.dot(kdt_ref[h], v_full, preferred_element_type=F32)
        ms = jnp.mean(o * o, axis=-1, keepdims=True)
        y = o * lax.rsqrt(ms + RMS_EPS) * nw_ref[...] * _silu(z_ref[0, sl, h * dk:(h + 1) * dk].astype(F32))
        o_ref[0, sl, h * dk:(h + 1) * dk] = y.astype(o_ref.dtype)

    def steps(phase, heads):
        if phase == "prep":
            return [functools.partial(prep, h) for h in heads]
        if phase == "merge":
            return [functools.partial(fn, lvl, h) for lvl in range(len(mask_bf))
                    for fn in (merge_a, merge_b) for h in heads]
        return ([functools.partial(solve, h) for h in heads]
                + [functools.partial(fn, a, h) for a in range(2) for fn in (read_state, update) for h in heads])

    for phase in ("prep", "merge", "tail"):
        for step in steps(phase, range(nh)):
            step()


def _gdn_mixer(qkv_act, proj, ba_rows, dt_bias, a_log, norm_w):
    bsz, s, _ = proj.shape
    t = 2 * GDN_CHUNK
    nh = GDN_HEADS
    dk = GDN_HEAD_DIM
    bias_c = jnp.pad(dt_bias, (nh, LANES - 2 * nh)).reshape(1, LANES)
    alog_c = jnp.pad(a_log, (nh, LANES - 2 * nh)).reshape(1, LANES)
    small = lambda shape: pl.BlockSpec(shape, lambda b_, c: (0, 0))
    return pl.pallas_call(
        _gdn_kernel,
        grid=(bsz, s // t),
        in_specs=[pl.BlockSpec((1, t, 3 * GDN_INNER), lambda b_, c: (b_, c, 0)),
                  pl.BlockSpec((1, t, GDN_INNER), lambda b_, c: (b_, c, CD_Z // GDN_INNER)),
                  pl.BlockSpec((1, t, LANES), lambda b_, c: (b_, c, CD_BA // LANES)),
                  pl.BlockSpec((1, LANES, t), lambda b_, c: (b_, 0, c)),
                  small((1, LANES)), small((LANES, 1)), small((1, LANES)), small((LANES, 1)),
                  small((1, dk))],
        out_specs=pl.BlockSpec((1, t, GDN_INNER), lambda b_, c: (b_, c, 0)),
        out_shape=jax.ShapeDtypeStruct((bsz, s, GDN_INNER), BF16),
        scratch_shapes=[pltpu.VMEM((nh, dk, dk), F32),
                        pltpu.VMEM((nh, t, t), BF16),
                        pltpu.VMEM((nh, t, t), BF16),
                        pltpu.VMEM((nh, t, t), F32),
                        pltpu.VMEM((nh, t, t), BF16),
                        pltpu.VMEM((nh, t, 2 * dk), BF16),
                        pltpu.VMEM((nh, t, dk), F32),
                        pltpu.VMEM((nh, 2, t, dk), BF16),
                        pltpu.VMEM((nh, dk, t), BF16),
                        pltpu.VMEM((nh, t, dk), F32),
                        pltpu.VMEM((t, LANES), F32)],
        compiler_params=_cparams(("arbitrary", "arbitrary")),
        name="gdn_mixer",
    )(qkv_act, proj, proj, ba_rows, bias_c, bias_c.reshape(LANES, 1), alog_c, alog_c.reshape(LANES, 1),
      norm_w.reshape(1, dk))


def _reorder_kernel(w_ref, o_ref, *, segments, zero_from):
    rows, n = o_ref.shape
    if zero_from < n:
        o_ref[:, zero_from:] = jnp.zeros((rows, n - zero_from), o_ref.dtype)
    for lo, hi, dst in segments:
        o_ref[:, dst:dst + hi - lo] = w_ref[:, lo:hi].astype(o_ref.dtype)


def _reorder_cast(w_stack, layer, segments, zero_from, n_out, tr=256):
    nl, d, n = w_stack.shape
    if nl == 1:
        w_stack = w_stack.reshape(d, n)
        w_spec = pl.BlockSpec((tr, n), lambda r: (r, 0))
    else:
        w_spec = pl.BlockSpec((None, tr, n), lambda r: (layer, r, 0))
    return pl.pallas_call(
        functools.partial(_reorder_kernel, segments=segments, zero_from=zero_from),
        grid=(d // tr,),
        in_specs=[w_spec],
        out_specs=pl.BlockSpec((tr, n_out), lambda r: (r, 0)),
        out_shape=jax.ShapeDtypeStruct((d, n_out), BF16),
        compiler_params=_cparams(("arbitrary",)),
        name="reorder_cast_weights",
    )(w_stack)


def _gate_rows(proj, col0):
    return jnp.swapaxes(proj[:, :, col0:col0 + LANES].astype(F32), 1, 2)


def kernel(x, c, rel_bias, norm_w, ada_w, ada_b, ab_w_in, ab_w_out, ssd_conv_w, ssd_conv_b,
           ssd_dt_bias, ssd_a_log, ssd_d, ssd_norm_w, swa_sinks, cd_w_in, cd_w_out, gdn_conv_w,
           gdn_dt_bias, gdn_a_log, gdn_norm_w, ffn_w_up, ffn_conv_w, ffn_conv_b, ffn_w_down):
    bsz, s, d = x.shape
    depth = norm_w.shape[0]
    mods = _mods(c, ada_w, ada_b)
    swa_bias = _bias_tiles(rel_bias, _swa_bucket_idx())
    moba_bias = _bias_tiles(rel_bias, _moba_bucket_idx(s // MOBA_BLOCK))

    for i in range(depth):
        sh_m, sc_m, g_m, sh_f, sc_f, g_f = [m.reshape(bsz, 1, d) for m in jnp.split(mods[i], 6, axis=-1)]
        j = i // 2
        if i % 2 == 0:
            dt0 = SSD_INNER + SSD_XBC
            n_in = ab_w_in.shape[-1]
            w_in = _reorder_cast(ab_w_in, j, ((SSD_INNER, dt0, 0), (0, SSD_INNER, SSD_XBC + AB_Z),
                                              (dt0 + SSD_HEADS, n_in, SSD_XBC + AB_Q), (dt0, dt0 + SSD_HEADS, SSD_XBC + AB_DT)),
                                 SSD_XBC + AB_DT, SSD_XBC + AB_COLS)
            xbc_act, proj = _proj_conv_act(x, norm_w[i, 0], sc_m, sh_m, w_in, ssd_conv_w[j], ssd_conv_b[j],
                                           "silu", n_plain=AB_COLS)
            y_a, y_b = _layer0_mixers(xbc_act, proj, _gate_rows(proj, AB_DT), ssd_dt_bias[j], ssd_a_log[j],
                                      ssd_d[j], ssd_norm_w[j], swa_sinks[j], swa_bias)
            x = _matmul_resid([y_a, y_b], ab_w_out, j, x, g_m, norm_w[i, 1])
        else:
            nqkv = 3 * GDN_INNER
            ba0 = nqkv + GDN_INNER
            n_in = cd_w_in.shape[-1]
            w_in = _reorder_cast(cd_w_in, j, ((0, ba0, 0), (ba0 + 2 * GDN_HEADS, n_in, nqkv + CD_QD),
                                              (ba0, ba0 + 2 * GDN_HEADS, nqkv + CD_BA)),
                                 nqkv + CD_BA, nqkv + CD_COLS)
            qkv_act, proj = _proj_conv_act(x, norm_w[i, 0], sc_m, sh_m, w_in, gdn_conv_w[j],
                                           jnp.zeros((nqkv,), F32), "silu", n_plain=CD_COLS)
            y_c = _gdn_mixer(qkv_act, proj, _gate_rows(proj, CD_BA), gdn_dt_bias[j], gdn_a_log[j], gdn_norm_w[j])
            y_d = _moba_attention(proj, moba_bias)
            x = _matmul_resid([y_c, y_d], cd_w_out, j, x, g_m, norm_w[i, 1])
        act = _proj_conv_act(x, norm_w[i, 2], sc_f, sh_f, ffn_w_up[i].astype(BF16), ffn_conv_w[i], ffn_conv_b[i],
                             "geglu")
        x = _matmul_resid([act], ffn_w_down, i, x, g_f, norm_w[i, 3])
    return x
```

```python
import functools
import math

import numpy as np
import jax
import jax.numpy as jnp
from jax import lax
from jax.experimental import pallas as pl
from jax.experimental.pallas import tpu as pltpu

F32 = jnp.float32
BF16 = jnp.bfloat16
HIGHEST = lax.Precision.HIGHEST

D_MODEL = 1024
RMS_EPS = 1e-6
NEG = -1e30
LANES = 128
SUBLANES = 8
N_ATTN_HEADS = 8
ATTN_HEAD_DIM = 64
ATTN_WIDTH = N_ATTN_HEADS * ATTN_HEAD_DIM
REL_BUCKETS = 32
REL_MAX_DIST = 1024
SSD_HEADS = 24
SSD_HEAD_DIM = 64
SSD_INNER = SSD_HEADS * SSD_HEAD_DIM
SSD_GROUPS = 4
SSD_STATE = 128
SSD_CONV = 4
SSD_CHUNK = 128
SSD_XBC = SSD_INNER + 2 * SSD_GROUPS * SSD_STATE
SWA_KV_HEADS = 2
SWA_BLOCK = 128
GDN_HEADS = 12
GDN_HEAD_DIM = 128
GDN_INNER = GDN_HEADS * GDN_HEAD_DIM
GDN_CONV = 4
GDN_CHUNK = 64
MOBA_BLOCK = 256
MOBA_TOPK = 3
FFN_DIM = 2816
FFN_CONV = 3

AB_Z, AB_Q, AB_K, AB_V, AB_DT, AB_COLS = 0, 1536, 2048, 2176, 2304, 2560
CD_Z, CD_QD, CD_KD, CD_VD, CD_BA, CD_COLS = 0, 1536, 2048, 2560, 3072, 3328

VMEM_LIMIT = 48 * 1024 * 1024
HALO = 16


def _cparams(sem):
    return pltpu.CompilerParams(dimension_semantics=sem, vmem_limit_bytes=VMEM_LIMIT)


def _bdot(a, b):
    return jnp.dot(a.astype(BF16), b.astype(BF16), preferred_element_type=F32)


def _bdot_nt(a, b):
    return lax.dot_general(a.astype(BF16), b.astype(BF16), (((1,), (1,)), ((), ())),
                           preferred_element_type=F32)


def _split3(x):
    hi = x.astype(BF16)
    r1 = x - hi.astype(F32)
    mid = r1.astype(BF16)
    lo = (r1 - mid.astype(F32)).astype(BF16)
    return hi, mid, lo


def _sum01_left(m01, x):
    n = x.shape[1]
    y = jnp.dot(m01, jnp.concatenate(_split3(x), axis=1), preferred_element_type=F32)
    return y[:, :n] + y[:, n:2 * n] + y[:, 2 * n:]


def _sum01_right(x, m01):
    n = x.shape[0]
    y = jnp.dot(jnp.concatenate(_split3(x), axis=0), m01, preferred_element_type=F32)
    return y[:n] + y[n:2 * n] + y[2 * n:]


def _softplus(x):
    return jnp.maximum(x, 0.0) + jnp.log(1.0 + jnp.exp(-jnp.abs(x)))


def _sigmoid(x):
    return 1.0 / (1.0 + jnp.exp(-x))


def _silu(x):
    return x * _sigmoid(x)


def _window_conv(win_ref, lo, ncols, rows, w, width, row0=0, shifted_loads=False):
    if shifted_loads:
        acc = None
        for s in range(width):
            tap = win_ref[pl.ds(HALO + row0 - s, rows), lo:lo + ncols] * w[width - 1 - s:width - s, :]
            acc = tap if acc is None else acc + tap
        return acc
    xin = win_ref[pl.ds(HALO + row0 - SUBLANES, rows + SUBLANES), lo:lo + ncols]
    acc = xin[SUBLANES:, :] * w[width - 1:width, :]
    for s in range(1, width):
        acc = acc + pltpu.roll(xin, s, axis=0)[SUBLANES:, :] * w[width - 1 - s:width - s, :]
    return acc


def _mods_kernel(c_ref, w_ref, b_ref, o_ref):
    o_ref[0] = _bdot(_silu(c_ref[...]), w_ref[0]) + b_ref[0]


def _mods(c, ada_w, ada_b):
    depth, d, n = ada_w.shape
    bsz = c.shape[0]
    tn = 512
    return pl.pallas_call(
        _mods_kernel,
        grid=(depth, n // tn),
        in_specs=[pl.BlockSpec((bsz, d), lambda l, j: (0, 0)),
                  pl.BlockSpec((1, d, tn), lambda l, j: (l, 0, j)),
                  pl.BlockSpec((1, 1, tn), lambda l, j: (l, 0, j))],
        out_specs=pl.BlockSpec((1, bsz, tn), lambda l, j: (l, 0, j)),
        out_shape=jax.ShapeDtypeStruct((depth, bsz, n), F32),
        compiler_params=_cparams(("arbitrary", "arbitrary")),
        name="adaln_mods",
    )(c, ada_w, ada_b.reshape(depth, 1, n))


def _modulated_norm(x, nw, sc, sh):
    ms = jnp.mean(x * x, axis=-1, keepdims=True)
    return x * lax.rsqrt(ms + RMS_EPS) * nw * (1.0 + sc) + sh


def _mmres_kernel(*refs, splits):
    na = len(splits)
    a_refs = refs[:na]
    w_ref, x_ref, g_ref, nw_ref, o_ref = refs[na:]
    acc = None
    lo = 0
    for a_ref, k in zip(a_refs, splits):
        part = jnp.dot(a_ref[0].astype(BF16), w_ref[lo:lo + k, :].astype(BF16), preferred_element_type=F32)
        acc = part if acc is None else acc + part
        lo += k
    ms = jnp.mean(acc * acc, axis=-1, keepdims=True)
    y = acc * lax.rsqrt(ms + RMS_EPS) * nw_ref[...]
    o_ref[0] = x_ref[0] + g_ref[0] * y


def _matmul_resid(a_list, w_stack, layer, x, gate, nw, tm=1024):
    bsz, s, d = x.shape
    spt = s // tm
    splits = tuple(a.shape[-1] for a in a_list)
    ktot = sum(splits)
    in_specs = [pl.BlockSpec((1, tm, k), lambda i: (i // spt, i % spt, 0)) for k in splits]
    in_specs += [pl.BlockSpec((None, ktot, d), lambda i: (layer, 0, 0)),
                 pl.BlockSpec((1, tm, d), lambda i: (i // spt, i % spt, 0)),
                 pl.BlockSpec((1, 1, d), lambda i: (i // spt, 0, 0)),
                 pl.BlockSpec((1, d), lambda i: (0, 0))]
    return pl.pallas_call(
        functools.partial(_mmres_kernel, splits=splits),
        grid=(bsz * spt,),
        in_specs=in_specs,
        out_specs=pl.BlockSpec((1, tm, d), lambda i: (i // spt, i % spt, 0)),
        out_shape=jax.ShapeDtypeStruct((bsz, s, d), F32),
        compiler_params=_cparams(("arbitrary",)),
        name="matmul_resid",
    )(*a_list, w_stack, x, gate, nw.reshape(1, d))


def _proj_conv_kernel(*refs, offs, tc, width, rb, act, n_plain):
    if n_plain:
        xh_ref, x_ref, nw_ref, sc_ref, sh_ref, w_ref, cw_ref, cb_ref, o_ref, op_ref, h_ref, u_ref = refs
    else:
        xh_ref, x_ref, nw_ref, sc_ref, sh_ref, w_ref, cw_ref, cb_ref, o_ref, h_ref, u_ref = refs
    tm = x_ref.shape[1]
    nchunk = o_ref.shape[2] // tc
    nplain = n_plain // tc
    plain0 = w_ref.shape[1] - n_plain
    h_halo = _modulated_norm(xh_ref[0], nw_ref[...], sc_ref[0], sh_ref[0])
    h_ref[:HALO, :] = jnp.where(pl.program_id(1) > 0, h_halo, 0.0).astype(BF16)
    h_ref[HALO:, :] = _modulated_norm(x_ref[0], nw_ref[...], sc_ref[0], sh_ref[0]).astype(BF16)
    c0 = math.sqrt(2.0 / math.pi)

    nrow = HALO + tm

    def matmuls(c):
        for k, off in enumerate(offs):
            lo = off + c * tc
            u = jnp.dot(h_ref[...], w_ref[:, lo:lo + tc], preferred_element_type=F32)
            for s in range(width):
                u_ref[c % 2, k, s, pl.ds(s, nrow), :] = u

    def epilogue(c):
        for r0 in range(0, tm, rb):
            conv = []
            for k, off in enumerate(offs):
                lo = off + c * tc
                acc = cb_ref[:, lo:lo + tc]
                for s in range(width):
                    acc = acc + (u_ref[c % 2, k, s, pl.ds(HALO + r0, rb), :]
                                 * cw_ref[width - 1 - s:width - s, lo:lo + tc])
                conv.append(acc)
            if act == "geglu":
                g, v = conv
                th = jnp.tanh(g * (c0 + (c0 * 0.044715) * (g * g)))
                hg = 0.5 * g
                out = (hg + hg * th) * v
            else:
                out = _silu(conv[0])
            o_ref[0, r0:r0 + rb, c * tc:(c + 1) * tc] = out.astype(o_ref.dtype)

    def plain(p):
        lo = plain0 + p * tc
        op_ref[0, :, p * tc:(p + 1) * tc] = jnp.dot(
            h_ref[HALO:, :], w_ref[:, lo:lo + tc], preferred_element_type=F32).astype(op_ref.dtype)

    matmuls(0)
    done = 0
    for c in range(nchunk):
        if c + 1 < nchunk:
            matmuls(c + 1)
        epilogue(c)
        upto = (c + 1) * nplain // nchunk
        for p in range(done, upto):
            plain(p)
        done = upto


def _proj_conv_act(x, nw, sc, sh, w, cw, cb, act, n_plain=0, tm=512, tc=256, rb=128):
    bsz, s, d = x.shape
    n2 = w.shape[-1] - n_plain
    f = n2 // 2 if act == "geglu" else n2
    offs = (0, f) if act == "geglu" else (0,)
    width = cw.shape[0]
    hb = tm // HALO
    const = lambda shape: pl.BlockSpec(shape, lambda b_, r: (0, 0))
    rows = lambda n: pl.BlockSpec((1, tm, n), lambda b_, r: (b_, r, 0))
    in_specs = [pl.BlockSpec((1, HALO, d), lambda b_, r: (b_, jnp.maximum(r * hb - 1, 0), 0)),
                rows(d), const((1, d)),
                pl.BlockSpec((1, 1, d), lambda b_, r: (b_, 0, 0)),
                pl.BlockSpec((1, 1, d), lambda b_, r: (b_, 0, 0)),
                const((d, n2 + n_plain)),
                const((width, n2)), const((1, n2))]
    args = [x, x, nw.reshape(1, d), sc, sh, w, cw, cb.reshape(1, n2)]
    out_specs = rows(f)
    out_shape = jax.ShapeDtypeStruct((bsz, s, f), BF16)
    if n_plain:
        out_specs = (out_specs, rows(n_plain))
        out_shape = (out_shape, jax.ShapeDtypeStruct((bsz, s, n_plain), BF16))
    return pl.pallas_call(
        functools.partial(_proj_conv_kernel, offs=offs, tc=tc, width=width, rb=rb, act=act, n_plain=n_plain),
        grid=(bsz, s // tm),
        in_specs=in_specs,
        out_specs=out_specs,
        out_shape=out_shape,
        scratch_shapes=[pltpu.VMEM((HALO + tm, d), BF16),
                        pltpu.VMEM((2, len(offs), width, HALO + tm + SUBLANES, tc), F32)],
        compiler_params=_cparams(("arbitrary", "arbitrary")),
        name="proj_conv_" + act,
    )(*args)


def _rel_bucket_np(d):
    max_exact = REL_BUCKETS // 2
    d = np.maximum(d, 0)
    df = np.maximum(d, 1).astype(np.float64)
    large = max_exact + (np.log(df / max_exact) / math.log(REL_MAX_DIST / max_exact)
                         * (REL_BUCKETS - max_exact)).astype(np.int32)
    large = np.minimum(large, REL_BUCKETS - 1)
    return np.where(d < max_exact, d, large).astype(np.int32)


LOG2E = math.log2(math.e)


def _bias_kernel(tab_ref, idx_ref, o_ref, *, ranges):
    h = pl.program_id(0)
    for t, (lo, hi) in enumerate(ranges):
        idx = idx_ref[t]
        acc = jnp.full(idx.shape, NEG, F32)
        for bkt in range(lo, hi + 1):
            acc = jnp.where(idx == bkt, tab_ref[bkt, h] * LOG2E, acc)
        o_ref[0, t] = acc


def _bias_tiles(rel_bias, idx_np):
    t, r, c = idx_np.shape
    ranges = tuple((int(tile[tile >= 0].min()), int(tile.max())) for tile in idx_np)
    return pl.pallas_call(
        functools.partial(_bias_kernel, ranges=ranges),
        grid=(N_ATTN_HEADS,),
        in_specs=[pl.BlockSpec(memory_space=pltpu.SMEM),
                  pl.BlockSpec((t, r, c), lambda h: (0, 0, 0))],
        out_specs=pl.BlockSpec((1, t, r, c), lambda h: (h, 0, 0, 0)),
        out_shape=jax.ShapeDtypeStruct((N_ATTN_HEADS, t, r, c), F32),
        compiler_params=_cparams(("arbitrary",)),
        name="rel_bias_tiles",
    )(rel_bias, jnp.asarray(idx_np))


def _swa_bucket_idx():
    c = np.arange(2 * SWA_BLOCK)[:, None]
    r = np.arange(SWA_BLOCK)[None, :]
    dist = SWA_BLOCK + r - c
    return np.where((dist >= 0) & (dist < SWA_BLOCK), _rel_bucket_np(dist), -1).astype(np.int32)[None]


def _moba_bucket_idx(nblk):
    c = np.arange(MOBA_BLOCK)[:, None]
    r = np.arange(MOBA_BLOCK)[None, :]
    tiles = [np.where(m * MOBA_BLOCK + r - c >= 0, _rel_bucket_np(m * MOBA_BLOCK + r - c), -1)
             for m in range(nblk)]
    return np.stack(tiles).astype(np.int32)


def _expand_heads(v, e):
    hi = v.astype(BF16)
    lo = (v - hi.astype(F32)).astype(BF16)
    return (jnp.dot(hi, e, preferred_element_type=F32) + jnp.dot(lo, e, preferred_element_type=F32))


def _ssd_kernel(xbc_ref, z_ref, dtc_ref, dtr_ref, bias_c_ref, bias_r_ref, alog_c_ref, alog_r_ref,
                dskip_ref, nw_ref, e_ref, o_ref, state_ref):
    q = SSD_CHUNK
    gw = SSD_INNER // SSD_GROUPS
    hpg = SSD_HEADS // SSD_GROUPS

    @pl.when(pl.program_id(1) == 0)
    def _():
        state_ref[...] = jnp.zeros(state_ref.shape, F32)

    row = lax.broadcasted_iota(jnp.int32, (q, q), 0)
    col = lax.broadcasted_iota(jnp.int32, (q, q), 1)
    tril = row >= col
    tri_b = jnp.where(tril, 1.0, 0.0).astype(BF16)
    triu_b = jnp.where(row <= col, 1.0, 0.0).astype(BF16)

    dt_c = _softplus(dtc_ref[0].astype(F32) + bias_c_ref[...])
    da_c = dt_c * (-jnp.exp(alog_c_ref[...]))
    acs_c = _sum01_left(tri_b, da_c)
    dt_r = _softplus(dtr_ref[0] + bias_r_ref[...])
    da_r = dt_r * (-jnp.exp(alog_r_ref[...]))
    acs_r = _sum01_right(da_r, triu_b)

    acs_last = acs_c[q - 1:q, :]
    e = e_ref[...]
    dt_full = _expand_heads(dt_c, e)
    dtdec_full = _expand_heads(dt_c * jnp.exp(acs_last - acs_c), e)
    eacs_full = _expand_heads(jnp.exp(acs_c), e)
    cdecay_full = eacs_full[q - 1:q, :]

    xbc = xbc_ref[0].astype(F32)
    xs = xbc[:, :SSD_INNER]
    xdt = xs * dt_full
    xdec = xs * dtdec_full
    lane_half = lax.broadcasted_iota(jnp.int32, (1, LANES), 1) >> 6

    y_parts = []
    for g in range(SSD_GROUPS):
        b_g = xbc[:, SSD_INNER + g * SSD_STATE:SSD_INNER + (g + 1) * SSD_STATE]
        c_g = xbc[:, SSD_INNER + SSD_GROUPS * SSD_STATE + g * SSD_STATE:
                  SSD_INNER + SSD_GROUPS * SSD_STATE + (g + 1) * SSD_STATE]
        cb = jnp.where(tril, _bdot_nt(c_g, b_g), 0.0)
        st = state_ref[g]
        y_off = _bdot(c_g, st) * eacs_full[:, g * gw:(g + 1) * gw]
        state_ref[g] = st * cdecay_full[:, g * gw:(g + 1) * gw] + _bdot(b_g.T, xdec[:, g * gw:(g + 1) * gw])
        pair_parts = []
        for pr in range(hpg // 2):
            acc = None
            lo = g * gw + pr * LANES
            x_pair = xdt[:, lo:lo + LANES]
            for half in range(2):
                h = g * hpg + pr * 2 + half
                diff = acs_c[:, h:h + 1] - acs_r[h:h + 1, :]
                lmat = jnp.exp(jnp.minimum(diff, 0.0))
                part = _bdot(cb * lmat, jnp.where(lane_half == half, x_pair, 0.0))
                acc = part if acc is None else acc + part
            pair_parts.append(acc)
        y_diag = jnp.concatenate(pair_parts, axis=1)
        y = y_diag + y_off + dskip_ref[:, g * gw:(g + 1) * gw] * xs[:, g * gw:(g + 1) * gw]
        y = y * _silu(z_ref[0, :, g * gw:(g + 1) * gw].astype(F32))
        ms = jnp.mean(y * y, axis=-1, keepdims=True)
        y_parts.append(y * lax.rsqrt(ms + RMS_EPS) * nw_ref[:, g * gw:(g + 1) * gw])
    o_ref[0] = jnp.concatenate(y_parts, axis=1).astype(o_ref.dtype)


N_SSD_IN = 11


def _layer0_mixers_kernel(*refs):
    ssd_in, swa_in = refs[:N_SSD_IN], refs[N_SSD_IN:N_SSD_IN + 7]
    o_ssd, o_swa, state_ref, s_ref = refs[N_SSD_IN + 7:]
    _ssd_kernel(*ssd_in, o_ssd, state_ref)
    _swa_kernel(*swa_in, o_swa, s_ref)


def _layer0_mixers(xbc_act, proj, dt_rows, dt_bias, a_log, d_skip, norm_w, sinks, swa_bias):
    bsz, s, _ = proj.shape
    q = SSD_CHUNK
    assert SWA_BLOCK == q
    pad = LANES - SSD_HEADS
    bias_c = jnp.pad(dt_bias, (0, pad)).reshape(1, LANES)
    alog_c = jnp.pad(a_log, (0, pad)).reshape(1, LANES)
    e_np = np.zeros((LANES, SSD_INNER), np.float32)
    for h in range(SSD_HEADS):
        e_np[h, h * SSD_HEAD_DIM:(h + 1) * SSD_HEAD_DIM] = 1.0
    small = lambda shape: pl.BlockSpec(shape, lambda b_, c: (0, 0))
    rows = lambda n, col: pl.BlockSpec((1, q, n), lambda b_, c: (b_, c, col))
    kvw = SWA_KV_HEADS * ATTN_HEAD_DIM
    prev = lambda col: pl.BlockSpec((1, q, kvw), lambda b_, c: (b_, jnp.maximum(c - 1, 0), col))
    ssd_specs = [rows(SSD_XBC, 0), rows(SSD_INNER, AB_Z // SSD_INNER), rows(LANES, AB_DT // LANES),
                 pl.BlockSpec((1, LANES, q), lambda b_, c: (b_, 0, c)),
                 small((1, LANES)), small((LANES, 1)), small((1, LANES)), small((LANES, 1)),
                 small((1, SSD_INNER)), small((1, SSD_INNER)), small((LANES, SSD_INNER))]
    assert len(ssd_specs) == N_SSD_IN
    swa_specs = [pl.BlockSpec(memory_space=pltpu.SMEM), rows(ATTN_WIDTH, AB_Q // ATTN_WIDTH),
                 prev(AB_K // kvw), rows(kvw, AB_K // kvw), prev(AB_V // kvw), rows(kvw, AB_V // kvw),
                 pl.BlockSpec((N_ATTN_HEADS, 1, 2 * q, q), lambda b_, c: (0, 0, 0, 0))]
    return pl.pallas_call(
        _layer0_mixers_kernel,
        grid=(bsz, s // q),
        in_specs=ssd_specs + swa_specs,
        out_specs=(rows(SSD_INNER, 0), rows(ATTN_WIDTH, 0)),
        out_shape=(jax.ShapeDtypeStruct((bsz, s, SSD_INNER), BF16),
                   jax.ShapeDtypeStruct((bsz, s, ATTN_WIDTH), BF16)),
        scratch_shapes=[pltpu.VMEM((SSD_GROUPS, SSD_STATE, SSD_INNER // SSD_GROUPS), F32),
                        pltpu.VMEM((N_ATTN_HEADS, 2 * q, q), F32)],
        compiler_params=_cparams(("arbitrary", "arbitrary")),
        name="ssd_swa_mixers",
    )(xbc_act, proj, proj, dt_rows, bias_c, bias_c.reshape(LANES, 1),
      alog_c, alog_c.reshape(LANES, 1), jnp.repeat(d_skip, SSD_HEAD_DIM).reshape(1, SSD_INNER),
      norm_w.reshape(1, SSD_INNER), jnp.asarray(e_np, BF16),
      sinks, proj, proj, proj, proj, proj, swa_bias)


def _swa_kernel(sink_ref, q_ref, kp_ref, kc_ref, vp_ref, vc_ref, bias_ref, o_ref, s_ref):
    blk = SWA_BLOCK
    n = pl.program_id(1)
    grp = N_ATTN_HEADS // SWA_KV_HEADS
    scale = ATTN_HEAD_DIM ** -0.5
    kk = jnp.concatenate([kp_ref[0], kc_ref[0]], axis=0).astype(F32) * (scale * LOG2E)
    vv_t = jnp.concatenate([vp_ref[0], vc_ref[0]], axis=0).astype(F32).T
    lane_half = lax.broadcasted_iota(jnp.int32, (1, LANES), 1) >> 6
    c = lax.broadcasted_iota(jnp.int32, (2 * blk, blk), 0)
    valid = (c >= blk) | (n > 0)
    ms = []
    for kv in range(SWA_KV_HEADS):
        k_own = jnp.where(lane_half == kv, kk, 0.0).astype(BF16)
        k_var = [None, None]
        k_var[kv] = k_own
        k_var[1 - kv] = pltpu.roll(jnp.where(lane_half == kv, kk, 0.0), ATTN_HEAD_DIM, axis=1).astype(BF16)
        for gq in range(grp):
            h = kv * grp + gq
            q_tile = q_ref[0, :, (h // 2) * LANES:(h // 2 + 1) * LANES]
            s_t = _bdot_nt(k_var[h % 2], q_tile) + bias_ref[h, 0]
            s_t = jnp.where(valid, s_t, NEG)
            s_ref[h] = s_t
            ms.append(jnp.maximum(jnp.max(s_t, axis=0, keepdims=True), sink_ref[h] * LOG2E))
    outs = []
    for h in range(N_ATTN_HEADS):
        kv = h // grp
        p = jnp.exp2(s_ref[h] - ms[h])
        l = jnp.sum(p, axis=0, keepdims=True) + jnp.exp2(sink_ref[h] * LOG2E - ms[h])
        outs.append(_bdot(vv_t[kv * ATTN_HEAD_DIM:(kv + 1) * ATTN_HEAD_DIM, :], p) / l)
    for t in range(N_ATTN_HEADS // 2):
        pair = jnp.concatenate([outs[2 * t], outs[2 * t + 1]], axis=0)
        o_ref[0, :, t * LANES:(t + 1) * LANES] = pair.T.astype(o_ref.dtype)


def _moba_kernel(q_ref, k_ref, v_ref, bias_ref, o_ref, vt_ref, qm_ref, negrow_ref, s_ref, *, nblk):
    mb = MOBA_BLOCK
    dh = ATTN_HEAD_DIM
    own0 = 2 * lax.rem(pl.program_id(1), nblk // 2)
    scale = dh ** -0.5

    @pl.when(own0 == 0)
    def _():
        means = []
        for j in range(nblk):
            vt_ref[j] = v_ref[0, j * mb:(j + 1) * mb, :].astype(F32).T.astype(BF16)
            means.append(jnp.mean(k_ref[0, j * mb:(j + 1) * mb, :].astype(F32), axis=0, keepdims=True))
        kmean = jnp.concatenate(means, axis=0)
        lane_half = lax.broadcasted_iota(jnp.int32, (1, LANES), 1) >> 6
        blk_id = lax.broadcasted_iota(jnp.int32, (nblk, mb), 0)
        for qb in range(nblk):
            q = q_ref[0, qb * mb:(qb + 1) * mb, :].astype(F32) * (scale * LOG2E)
            for hh in range(2):
                qm = jnp.where(lane_half == hh, q, 0.0)
                qm_ref[hh, qb] = qm.astype(BF16)
                gate = lax.dot_general(kmean, qm, (((1,), (1,)), ((), ())),
                                       preferred_element_type=F32, precision=HIGHEST)
                gate = jnp.where(blk_id < qb, gate, NEG)
                rank = jnp.zeros((nblk, mb), jnp.int32)
                for i in range(nblk):
                    gi = gate[i:i + 1, :]
                    ahead = (gi > gate) | ((gi == gate) & (i < blk_id))
                    rank = rank + jnp.where(ahead, 1, 0)
                keep = ((rank < MOBA_TOPK) & (blk_id < qb)) | (blk_id == qb)
                negrow_ref[hh, qb] = jnp.where(keep, 0.0, NEG).astype(F32)

    def attend(nb, qi):
        own = own0 + qi
        qms = [qm_ref[hh, own] for hh in range(2)]
        negrows = [negrow_ref[hh, own] for hh in range(2)]
        m8 = [None, None]
        l8 = [None, None]
        acc = [None, None]

        def scores(hh, j):
            s_t = _bdot_nt(k_ref[0, j * mb:(j + 1) * mb, :], qms[hh]) + bias_ref[hh, jnp.maximum(own - j, 0)]
            s_t = s_t + negrows[hh][j:j + 1, :]
            s_ref[qi, hh, j] = s_t
            m_j = jnp.max(s_t.reshape(mb // SUBLANES, SUBLANES, mb), axis=0)
            m8[hh] = m_j if m8[hh] is None else jnp.maximum(m8[hh], m_j)

        def weights(hh, j, m):
            p = jnp.exp2(s_ref[qi, hh, j] - m)
            l_j = jnp.sum(p.reshape(mb // SUBLANES, SUBLANES, mb), axis=0)
            a_j = _bdot(vt_ref[j, hh * dh:(hh + 1) * dh, :], p)
            l8[hh] = l_j if l8[hh] is None else l8[hh] + l_j
            acc[hh] = a_j if acc[hh] is None else acc[hh] + a_j

        for j in range(nb):
            for hh in range(2):
                scores(hh, j)
        m = [jnp.max(m8[hh], axis=0, keepdims=True) for hh in range(2)]
        for j in range(nb):
            for hh in range(2):
                weights(hh, j, m[hh])
        out_t = jnp.concatenate([acc[hh] / jnp.sum(l8[hh], axis=0, keepdims=True) for hh in range(2)], axis=0)
        o_ref[0, qi * mb:(qi + 1) * mb, :] = out_t.T.astype(o_ref.dtype)

    def attend_both(nb):
        attend(nb, 0)
        attend(nb, 1)

    for nb in range(2, nblk + 1, 2):
        pl.when(own0 == nb - 2)(functools.partial(attend_both, nb))


def _moba_specs(bsz, s):
    mb = MOBA_BLOCK
    nblk = s // mb
    half = nblk // 2
    col = lambda c0: (lambda b_, c: (b_, 0, c0 // LANES + c // half))
    in_specs = [pl.BlockSpec((1, s, LANES), col(CD_QD)),
                pl.BlockSpec((1, s, LANES), col(CD_KD)),
                pl.BlockSpec((1, s, LANES), col(CD_VD)),
                pl.BlockSpec((2, nblk, mb, mb), lambda b_, c: (c // half, 0, 0, 0))]
    out_spec = pl.BlockSpec((1, 2 * mb, LANES), lambda b_, c: (b_, c % half, c // half))
    out_shape = jax.ShapeDtypeStruct((bsz, s, ATTN_WIDTH), BF16)
    scratch = [pltpu.VMEM((nblk, LANES, mb), BF16),
               pltpu.VMEM((2, nblk, mb, LANES), BF16),
               pltpu.VMEM((2, nblk, nblk, mb), F32),
               pltpu.VMEM((2, 2, nblk, mb, mb), F32)]
    return in_specs, out_spec, out_shape, scratch


def _gdn_kernel(qkv_ref, z_ref, bac_ref, bar_ref, bias_c_ref, bias_r_ref, alog_c_ref, alog_r_ref,
                nw_ref, o_ref, state_ref, m_ref, attn_ref, t_ref, x_ref, rhs_ref, u_ref, wq_ref,
                kdt_ref, r_ref, glc_ref):
    t = 2 * GDN_CHUNK
    ck = GDN_CHUNK
    dk = GDN_HEAD_DIM
    nh = GDN_HEADS

    @pl.when(pl.program_id(1) == 0)
    def _():
        state_ref[...] = jnp.zeros(state_ref.shape, F32)

    row = lax.broadcasted_iota(jnp.int32, (t, t), 0)
    col = lax.broadcasted_iota(jnp.int32, (t, t), 1)
    same = (row >> 6) == (col >> 6)
    tril = same & (row >= col)
    strict = same & (row > col)
    tri_b = jnp.where(tril, 1.0, 0.0).astype(BF16)
    triu_b = jnp.where(same & (row <= col), 1.0, 0.0).astype(BF16)
    blk_b = jnp.where(same, 1.0, 0.0).astype(BF16)
    eye_f = jnp.where(row == col, 1.0, 0.0).astype(F32)
    merge_masks = [((row >> (l + 1)) == (col >> (l + 1))) & (((row >> l) & 1) == 1) & (((col >> l) & 1) == 0)
                   for l in range(int(math.log2(ck)))]
    mask_bf = [jnp.where(m, 1.0, 0.0).astype(BF16) for m in merge_masks[1:]]

    ba_c = bac_ref[0].astype(F32)
    g_c = -jnp.exp(alog_c_ref[...]) * _softplus(ba_c + bias_c_ref[...])
    sums_c = _sum01_left(jnp.concatenate([tri_b, blk_b], axis=0), g_c)
    gc_c = sums_c[:t]
    gl_c = sums_c[t:]
    g_r = -jnp.exp(alog_r_ref[...]) * _softplus(bar_ref[0] + bias_r_ref[...])
    gc_r = _sum01_right(g_r, triu_b)
    glc_ref[...] = gl_c

    def prep(h):
        q = qkv_ref[0, :, h * dk:(h + 1) * dk].astype(F32)
        k = qkv_ref[0, :, (nh + h) * dk:(nh + h + 1) * dk].astype(F32)
        v = qkv_ref[0, :, (2 * nh + h) * dk:(2 * nh + h + 1) * dk].astype(F32)
        qn = q * lax.rsqrt(jnp.sum(q * q, axis=-1, keepdims=True) + 1e-6) * (dk ** -0.5)
        kn = k * lax.rsqrt(jnp.sum(k * k, axis=-1, keepdims=True) + 1e-6)
        beta = _sigmoid(ba_c[:, h:h + 1])
        gcc = gc_c[:, nh + h:nh + h + 1]
        gcr = gc_r[nh + h:nh + h + 1, :]
        glc = gl_c[:, nh + h:nh + h + 1]
        decay = jnp.where(tril, jnp.exp(jnp.where(tril, gcc - gcr, 0.0)), 0.0)
        kb = kn * beta
        kk = _bdot_nt(jnp.concatenate([kb, qn], axis=0), kn)
        mm = jnp.where(strict, kk[:t] * decay, 0.0)
        m_ref[h] = mm.astype(BF16)
        attn_ref[h] = (kk[t:] * decay).astype(BF16)
        t_ref[h] = eye_f - jnp.where(merge_masks[0], mm, 0.0)
        egc = jnp.exp(gcc)
        rhs_ref[h] = jnp.concatenate([v * beta, kb * egc], axis=1).astype(BF16)
        q_dec = (qn * egc).astype(BF16)
        for a in range(2):
            wq_ref[h, a, ck:, :] = q_dec[a * ck:(a + 1) * ck]
        kdt_ref[h] = (kn * jnp.exp(glc - gcc)).T.astype(BF16)

    def merge_a(lvl, h):
        x_ref[h] = jnp.dot(t_ref[h].astype(BF16), m_ref[h] * mask_bf[lvl],
                           preferred_element_type=F32).astype(BF16)

    def merge_b(lvl, h):
        t_h = t_ref[h]
        t_ref[h] = t_h - jnp.dot(x_ref[h], t_h.astype(BF16), preferred_element_type=F32)

    def solve(h):
        sol = jnp.dot(t_ref[h].astype(BF16), rhs_ref[h], preferred_element_type=F32)
        u_ref[h] = sol[:, :dk]
        for a in range(2):
            wq_ref[h, a, :ck, :] = sol[a * ck:(a + 1) * ck, dk:].astype(BF16)

    zeros_half = jnp.zeros((ck, dk), F32)

    def read_state(a, h):
        r_ref[h] = jnp.dot(wq_ref[h, a], state_ref[h].astype(BF16), preferred_element_type=F32)

    def update(a, h):
        sl = slice(a * ck, (a + 1) * ck)
        v_new = u_ref[h, sl, :] - r_ref[h, :ck, :]
        v_full = jnp.concatenate([v_new, zeros_half] if a == 0 else [zeros_half, v_new], axis=0).astype(BF16)
        o = r_ref[h, ck:, :] + jnp.dot(attn_ref[h, sl, :], v_full, preferred_element_type=F32)
        gl = glc_ref[a * ck:a * ck + 1, nh + h:nh + h + 1]
        state_ref[h] = state_ref[h] * jnp.exp(gl) + jnp.dot(kdt_ref[h], v_full, preferred_element_type=F32)
        ms = jnp.mean(o * o, axis=-1, keepdims=True)
        y = o * lax.rsqrt(ms + RMS_EPS) * nw_ref[...] * _silu(z_ref[0, sl, h * dk:(h + 1) * dk].astype(F32))
        o_ref[0, sl, h * dk:(h + 1) * dk] = y.astype(o_ref.dtype)

    def steps(phase, heads):
        if phase == "prep":
            return [functools.partial(prep, h) for h in heads]
        if phase == "merge":
            return [functools.partial(fn, lvl, h) for lvl in range(len(mask_bf))
                    for fn in (merge_a, merge_b) for h in heads]
        return ([functools.partial(solve, h) for h in heads]
                + [functools.partial(fn, a, h) for a in range(2) for fn in (read_state, update) for h in heads])

    for phase in ("prep", "merge", "tail"):
        for step in steps(phase, range(nh)):
            step()


N_GDN_IN, N_GDN_SCRATCH, N_MOBA_IN = 9, 11, 4


def _layer1_mixers_kernel(*refs, nblk):
    gdn_in = refs[:N_GDN_IN]
    moba_in = refs[N_GDN_IN:N_GDN_IN + N_MOBA_IN]
    o_gdn, o_moba = refs[N_GDN_IN + N_MOBA_IN:N_GDN_IN + N_MOBA_IN + 2]
    scratch = refs[N_GDN_IN + N_MOBA_IN + 2:]
    _gdn_kernel(*gdn_in, o_gdn, *scratch[:N_GDN_SCRATCH])
    _moba_kernel(*moba_in, o_moba, *scratch[N_GDN_SCRATCH:], nblk=nblk)


def _layer1_mixers(qkv_act, proj, ba_rows, dt_bias, a_log, norm_w, moba_bias):
    bsz, s, _ = proj.shape
    t = 2 * GDN_CHUNK
    nh = GDN_HEADS
    dk = GDN_HEAD_DIM
    nblk = s // MOBA_BLOCK
    assert s // t == (N_ATTN_HEADS // 2) * (nblk // 2)
    bias_c = jnp.pad(dt_bias, (nh, LANES - 2 * nh)).reshape(1, LANES)
    alog_c = jnp.pad(a_log, (nh, LANES - 2 * nh)).reshape(1, LANES)
    small = lambda shape: pl.BlockSpec(shape, lambda b_, c: (0, 0))
    moba_in, moba_out, moba_shape, moba_scratch = _moba_specs(bsz, s)
    gdn_in = [pl.BlockSpec((1, t, 3 * GDN_INNER), lambda b_, c: (b_, c, 0)),
              pl.BlockSpec((1, t, GDN_INNER), lambda b_, c: (b_, c, CD_Z // GDN_INNER)),
              pl.BlockSpec((1, t, LANES), lambda b_, c: (b_, c, CD_BA // LANES)),
              pl.BlockSpec((1, LANES, t), lambda b_, c: (b_, 0, c)),
              small((1, LANES)), small((LANES, 1)), small((1, LANES)), small((LANES, 1)),
              small((1, dk))]
    assert len(gdn_in) == N_GDN_IN and len(moba_in) == N_MOBA_IN
    return pl.pallas_call(
        functools.partial(_layer1_mixers_kernel, nblk=nblk),
        grid=(bsz, s // t),
        in_specs=gdn_in + moba_in,
        out_specs=(pl.BlockSpec((1, t, GDN_INNER), lambda b_, c: (b_, c, 0)), moba_out),
        out_shape=(jax.ShapeDtypeStruct((bsz, s, GDN_INNER), BF16), moba_shape),
        scratch_shapes=[pltpu.VMEM((nh, dk, dk), F32),
                        pltpu.VMEM((nh, t, t), BF16),
                        pltpu.VMEM((nh, t, t), BF16),
                        pltpu.VMEM((nh, t, t), F32),
                        pltpu.VMEM((nh, t, t), BF16),
                        pltpu.VMEM((nh, t, 2 * dk), BF16),
                        pltpu.VMEM((nh, t, dk), F32),
                        pltpu.VMEM((nh, 2, t, dk), BF16),
                        pltpu.VMEM((nh, dk, t), BF16),
                        pltpu.VMEM((nh, t, dk), F32),
                        pltpu.VMEM((t, LANES), F32)]
        + moba_scratch,
        compiler_params=_cparams(("arbitrary", "arbitrary")),
        name="gdn_moba_mixers",
    )(qkv_act, proj, proj, ba_rows, bias_c, bias_c.reshape(LANES, 1), alog_c, alog_c.reshape(LANES, 1),
      norm_w.reshape(1, dk), proj, proj, proj, moba_bias)


def _reorder_kernel(w_ref, o_ref, *, segments, zero_from):
    rows, n = o_ref.shape
    if zero_from < n:
        o_ref[:, zero_from:] = jnp.zeros((rows, n - zero_from), o_ref.dtype)
    for lo, hi, dst in segments:
        o_ref[:, dst:dst + hi - lo] = w_ref[:, lo:hi].astype(o_ref.dtype)


def _reorder_cast(w_stack, layer, segments, zero_from, n_out, tr=256):
    nl, d, n = w_stack.shape
    if nl == 1:
        w_stack = w_stack.reshape(d, n)
        w_spec = pl.BlockSpec((tr, n), lambda r: (r, 0))
    else:
        w_spec = pl.BlockSpec((None, tr, n), lambda r: (layer, r, 0))
    return pl.pallas_call(
        functools.partial(_reorder_kernel, segments=segments, zero_from=zero_from),
        grid=(d // tr,),
        in_specs=[w_spec],
        out_specs=pl.BlockSpec((tr, n_out), lambda r: (r, 0)),
        out_shape=jax.ShapeDtypeStruct((d, n_out), BF16),
        compiler_params=_cparams(("arbitrary",)),
        name="reorder_cast_weights",
    )(w_stack)


def _gate_rows(proj, col0):
    return jnp.swapaxes(proj[:, :, col0:col0 + LANES].astype(F32), 1, 2)


def kernel(x, c, rel_bias, norm_w, ada_w, ada_b, ab_w_in, ab_w_out, ssd_conv_w, ssd_conv_b,
           ssd_dt_bias, ssd_a_log, ssd_d, ssd_norm_w, swa_sinks, cd_w_in, cd_w_out, gdn_conv_w,
           gdn_dt_bias, gdn_a_log, gdn_norm_w, ffn_w_up, ffn_conv_w, ffn_conv_b, ffn_w_down):
    bsz, s, d = x.shape
    depth = norm_w.shape[0]
    mods = _mods(c, ada_w, ada_b)
    swa_bias = _bias_tiles(rel_bias, _swa_bucket_idx())
    moba_bias = _bias_tiles(rel_bias, _moba_bucket_idx(s // MOBA_BLOCK))

    for i in range(depth):
        sh_m, sc_m, g_m, sh_f, sc_f, g_f = [m.reshape(bsz, 1, d) for m in jnp.split(mods[i], 6, axis=-1)]
        j = i // 2
        if i % 2 == 0:
            dt0 = SSD_INNER + SSD_XBC
            n_in = ab_w_in.shape[-1]
            w_in = _reorder_cast(ab_w_in, j, ((SSD_INNER, dt0, 0), (0, SSD_INNER, SSD_XBC + AB_Z),
                                              (dt0 + SSD_HEADS, n_in, SSD_XBC + AB_Q), (dt0, dt0 + SSD_HEADS, SSD_XBC + AB_DT)),
                                 SSD_XBC + AB_DT, SSD_XBC + AB_COLS)
            xbc_act, proj = _proj_conv_act(x, norm_w[i, 0], sc_m, sh_m, w_in, ssd_conv_w[j], ssd_conv_b[j],
                                           "silu", n_plain=AB_COLS)
            y_a, y_b = _layer0_mixers(xbc_act, proj, _gate_rows(proj, AB_DT), ssd_dt_bias[j], ssd_a_log[j],
                                      ssd_d[j], ssd_norm_w[j], swa_sinks[j], swa_bias)
            x = _matmul_resid([y_a, y_b], ab_w_out, j, x, g_m, norm_w[i, 1])
        else:
            nqkv = 3 * GDN_INNER
            ba0 = nqkv + GDN_INNER
            n_in = cd_w_in.shape[-1]
            w_in = _reorder_cast(cd_w_in, j, ((0, ba0, 0), (ba0 + 2 * GDN_HEADS, n_in, nqkv + CD_QD),
                                              (ba0, ba0 + 2 * GDN_HEADS, nqkv + CD_BA)),
                                 nqkv + CD_BA, nqkv + CD_COLS)
            qkv_act, proj = _proj_conv_act(x, norm_w[i, 0], sc_m, sh_m, w_in, gdn_conv_w[j],
                                           jnp.zeros((nqkv,), F32), "silu", n_plain=CD_COLS)
            y_c, y_d = _layer1_mixers(qkv_act, proj, _gate_rows(proj, CD_BA), gdn_dt_bias[j], gdn_a_log[j],
                                      gdn_norm_w[j], moba_bias)
            x = _matmul_resid([y_c, y_d], cd_w_out, j, x, g_m, norm_w[i, 1])
        act = _proj_conv_act(x, norm_w[i, 2], sc_f, sh_f, ffn_w_up[i].astype(BF16), ffn_conv_w[i], ffn_conv_b[i],
                             "geglu")
        x = _matmul_resid([act], ffn_w_down, i, x, g_f, norm_w[i, 3])
    return x
```

```python
import functools
import math

import numpy as np
import jax
import jax.numpy as jnp
from jax import lax
from jax.experimental import pallas as pl
from jax.experimental.pallas import tpu as pltpu

F32 = jnp.float32
BF16 = jnp.bfloat16
HIGHEST = lax.Precision.HIGHEST

D_MODEL = 1024
RMS_EPS = 1e-6
NEG = -1e30
LANES = 128
SUBLANES = 8
N_ATTN_HEADS = 8
ATTN_HEAD_DIM = 64
ATTN_WIDTH = N_ATTN_HEADS * ATTN_HEAD_DIM
REL_BUCKETS = 32
REL_MAX_DIST = 1024
SSD_HEADS = 24
SSD_HEAD_DIM = 64
SSD_INNER = SSD_HEADS * SSD_HEAD_DIM
SSD_GROUPS = 4
SSD_STATE = 128
SSD_CONV = 4
SSD_CHUNK = 128
SSD_XBC = SSD_INNER + 2 * SSD_GROUPS * SSD_STATE
SWA_KV_HEADS = 2
SWA_BLOCK = 128
GDN_HEADS = 12
GDN_HEAD_DIM = 128
GDN_INNER = GDN_HEADS * GDN_HEAD_DIM
GDN_CONV = 4
GDN_CHUNK = 64
MOBA_BLOCK = 256
MOBA_TOPK = 3
FFN_DIM = 2816
FFN_CONV = 3

AB_Z, AB_Q, AB_K, AB_V, AB_DT, AB_COLS = 0, 1536, 2048, 2176, 2304, 2560
CD_Z, CD_QD, CD_KD, CD_VD, CD_BA, CD_COLS = 0, 1536, 2048, 2560, 3072, 3328

VMEM_LIMIT = 48 * 1024 * 1024
HALO = 16


def _cparams(sem):
    return pltpu.CompilerParams(dimension_semantics=sem, vmem_limit_bytes=VMEM_LIMIT)


def _bdot(a, b):
    return jnp.dot(a.astype(BF16), b.astype(BF16), preferred_element_type=F32)


def _bdot_nt(a, b):
    return lax.dot_general(a.astype(BF16), b.astype(BF16), (((1,), (1,)), ((), ())),
                           preferred_element_type=F32)


def _split3(x):
    hi = x.astype(BF16)
    r1 = x - hi.astype(F32)
    mid = r1.astype(BF16)
    lo = (r1 - mid.astype(F32)).astype(BF16)
    return hi, mid, lo


def _sum01_left(m01, x):
    n = x.shape[1]
    y = jnp.dot(m01, jnp.concatenate(_split3(x), axis=1), preferred_element_type=F32)
    return y[:, :n] + y[:, n:2 * n] + y[:, 2 * n:]


def _sum01_right(x, m01):
    n = x.shape[0]
    y = jnp.dot(jnp.concatenate(_split3(x), axis=0), m01, preferred_element_type=F32)
    return y[:n] + y[n:2 * n] + y[2 * n:]


def _softplus(x):
    return jnp.maximum(x, 0.0) + jnp.log(1.0 + jnp.exp(-jnp.abs(x)))


def _sigmoid(x):
    return 1.0 / (1.0 + jnp.exp(-x))


def _silu(x):
    return x * _sigmoid(x)


def _window_conv(win_ref, lo, ncols, rows, w, width, row0=0, shifted_loads=False):
    if shifted_loads:
        acc = None
        for s in range(width):
            tap = win_ref[pl.ds(HALO + row0 - s, rows), lo:lo + ncols] * w[width - 1 - s:width - s, :]
            acc = tap if acc is None else acc + tap
        return acc
    xin = win_ref[pl.ds(HALO + row0 - SUBLANES, rows + SUBLANES), lo:lo + ncols]
    acc = xin[SUBLANES:, :] * w[width - 1:width, :]
    for s in range(1, width):
        acc = acc + pltpu.roll(xin, s, axis=0)[SUBLANES:, :] * w[width - 1 - s:width - s, :]
    return acc


def _mods_kernel(c_ref, w_ref, b_ref, o_ref):
    o_ref[0] = _bdot(_silu(c_ref[...]), w_ref[0]) + b_ref[0]


def _mods(c, ada_w, ada_b):
    depth, d, n = ada_w.shape
    bsz = c.shape[0]
    tn = 512
    return pl.pallas_call(
        _mods_kernel,
        grid=(depth, n // tn),
        in_specs=[pl.BlockSpec((bsz, d), lambda l, j: (0, 0)),
                  pl.BlockSpec((1, d, tn), lambda l, j: (l, 0, j)),
                  pl.BlockSpec((1, 1, tn), lambda l, j: (l, 0, j))],
        out_specs=pl.BlockSpec((1, bsz, tn), lambda l, j: (l, 0, j)),
        out_shape=jax.ShapeDtypeStruct((depth, bsz, n), F32),
        compiler_params=_cparams(("arbitrary", "arbitrary")),
        name="adaln_mods",
    )(c, ada_w, ada_b.reshape(depth, 1, n))


def _modulated_norm(x, nw, sc, sh):
    ms = jnp.mean(x * x, axis=-1, keepdims=True)
    return x * lax.rsqrt(ms + RMS_EPS) * nw * (1.0 + sc) + sh


def _mmres_kernel(*refs, splits):
    na = len(splits)
    a_refs = refs[:na]
    w_ref, x_ref, g_ref, nw_ref, o_ref = refs[na:]
    acc = None
    lo = 0
    for a_ref, k in zip(a_refs, splits):
        part = jnp.dot(a_ref[0].astype(BF16), w_ref[lo:lo + k, :].astype(BF16), preferred_element_type=F32)
        acc = part if acc is None else acc + part
        lo += k
    ms = jnp.mean(acc * acc, axis=-1, keepdims=True)
    y = acc * lax.rsqrt(ms + RMS_EPS) * nw_ref[...]
    o_ref[0] = x_ref[0] + g_ref[0] * y


def _matmul_resid(a_list, w_stack, layer, x, gate, nw, tm=1024):
    bsz, s, d = x.shape
    spt = s // tm
    splits = tuple(a.shape[-1] for a in a_list)
    ktot = sum(splits)
    in_specs = [pl.BlockSpec((1, tm, k), lambda i: (i // spt, i % spt, 0)) for k in splits]
    in_specs += [pl.BlockSpec((None, ktot, d), lambda i: (layer, 0, 0)),
                 pl.BlockSpec((1, tm, d), lambda i: (i // spt, i % spt, 0)),
                 pl.BlockSpec((1, 1, d), lambda i: (i // spt, 0, 0)),
                 pl.BlockSpec((1, d), lambda i: (0, 0))]
    return pl.pallas_call(
        functools.partial(_mmres_kernel, splits=splits),
        grid=(bsz * spt,),
        in_specs=in_specs,
        out_specs=pl.BlockSpec((1, tm, d), lambda i: (i // spt, i % spt, 0)),
        out_shape=jax.ShapeDtypeStruct((bsz, s, d), F32),
        compiler_params=_cparams(("arbitrary",)),
        name="matmul_resid",
    )(*a_list, w_stack, x, gate, nw.reshape(1, d))


def _proj_conv_kernel(*refs, offs, tc, width, rb, act, n_plain):
    if n_plain:
        xh_ref, x_ref, nw_ref, sc_ref, sh_ref, w_ref, cw_ref, cb_ref, o_ref, op_ref, h_ref, u_ref = refs
    else:
        xh_ref, x_ref, nw_ref, sc_ref, sh_ref, w_ref, cw_ref, cb_ref, o_ref, h_ref, u_ref = refs
    tm = x_ref.shape[1]
    nchunk = o_ref.shape[2] // tc
    nplain = n_plain // tc
    plain0 = w_ref.shape[1] - n_plain
    h_halo = _modulated_norm(xh_ref[0], nw_ref[...], sc_ref[0], sh_ref[0])
    h_ref[:HALO, :] = jnp.where(pl.program_id(1) > 0, h_halo, 0.0).astype(BF16)
    h_ref[HALO:, :] = _modulated_norm(x_ref[0], nw_ref[...], sc_ref[0], sh_ref[0]).astype(BF16)
    c0 = math.sqrt(2.0 / math.pi)

    nrow = HALO + tm

    def matmuls(c):
        for k, off in enumerate(offs):
            lo = off + c * tc
            u = jnp.dot(h_ref[...], w_ref[:, lo:lo + tc], preferred_element_type=F32)
            for s in range(width):
                u_ref[c % 2, k, s, pl.ds(s, nrow), :] = u

    def epilogue(c):
        for r0 in range(0, tm, rb):
            conv = []
            for k, off in enumerate(offs):
                lo = off + c * tc
                acc = cb_ref[:, lo:lo + tc]
                for s in range(width):
                    acc = acc + (u_ref[c % 2, k, s, pl.ds(HALO + r0, rb), :]
                                 * cw_ref[width - 1 - s:width - s, lo:lo + tc])
                conv.append(acc)
            if act == "geglu":
                g, v = conv
                th = jnp.tanh(g * (c0 + (c0 * 0.044715) * (g * g)))
                hg = 0.5 * g
                out = (hg + hg * th) * v
            else:
                out = _silu(conv[0])
            o_ref[0, r0:r0 + rb, c * tc:(c + 1) * tc] = out.astype(o_ref.dtype)

    def plain(p):
        lo = plain0 + p * tc
        op_ref[0, :, p * tc:(p + 1) * tc] = jnp.dot(
            h_ref[HALO:, :], w_ref[:, lo:lo + tc], preferred_element_type=F32).astype(op_ref.dtype)

    matmuls(0)
    done = 0
    for c in range(nchunk):
        if c + 1 < nchunk:
            matmuls(c + 1)
        epilogue(c)
        upto = (c + 1) * nplain // nchunk
        for p in range(done, upto):
            plain(p)
        done = upto


def _proj_conv_act(x, nw, sc, sh, w, cw, cb, act, n_plain=0, tm=512, tc=256, rb=128):
    bsz, s, d = x.shape
    n2 = w.shape[-1] - n_plain
    f = n2 // 2 if act == "geglu" else n2
    offs = (0, f) if act == "geglu" else (0,)
    width = cw.shape[0]
    hb = tm // HALO
    const = lambda shape: pl.BlockSpec(shape, lambda b_, r: (0, 0))
    rows = lambda n: pl.BlockSpec((1, tm, n), lambda b_, r: (b_, r, 0))
    in_specs = [pl.BlockSpec((1, HALO, d), lambda b_, r: (b_, jnp.maximum(r * hb - 1, 0), 0)),
                rows(d), const((1, d)),
                pl.BlockSpec((1, 1, d), lambda b_, r: (b_, 0, 0)),
                pl.BlockSpec((1, 1, d), lambda b_, r: (b_, 0, 0)),
                const((d, n2 + n_plain)),
                const((width, n2)), const((1, n2))]
    args = [x, x, nw.reshape(1, d), sc, sh, w, cw, cb.reshape(1, n2)]
    out_specs = rows(f)
    out_shape = jax.ShapeDtypeStruct((bsz, s, f), BF16)
    if n_plain:
        out_specs = (out_specs, rows(n_plain))
        out_shape = (out_shape, jax.ShapeDtypeStruct((bsz, s, n_plain), BF16))
    return pl.pallas_call(
        functools.partial(_proj_conv_kernel, offs=offs, tc=tc, width=width, rb=rb, act=act, n_plain=n_plain),
        grid=(bsz, s // tm),
        in_specs=in_specs,
        out_specs=out_specs,
        out_shape=out_shape,
        scratch_shapes=[pltpu.VMEM((HALO + tm, d), BF16),
                        pltpu.VMEM((2, len(offs), width, HALO + tm + SUBLANES, tc), F32)],
        compiler_params=_cparams(("arbitrary", "arbitrary")),
        name="proj_conv_" + act,
    )(*args)


def _rel_bucket_np(d):
    max_exact = REL_BUCKETS // 2
    d = np.maximum(d, 0)
    df = np.maximum(d, 1).astype(np.float64)
    large = max_exact + (np.log(df / max_exact) / math.log(REL_MAX_DIST / max_exact)
                         * (REL_BUCKETS - max_exact)).astype(np.int32)
    large = np.minimum(large, REL_BUCKETS - 1)
    return np.where(d < max_exact, d, large).astype(np.int32)


LOG2E = math.log2(math.e)


def _bias_kernel(tab_ref, idx_ref, o_ref, *, ranges):
    h = pl.program_id(0)
    for t, (lo, hi) in enumerate(ranges):
        idx = idx_ref[t]
        acc = jnp.full(idx.shape, NEG, F32)
        for bkt in range(lo, hi + 1):
            acc = jnp.where(idx == bkt, tab_ref[bkt, h] * LOG2E, acc)
        o_ref[0, t] = acc


def _bias_tiles(rel_bias, idx_np):
    t, r, c = idx_np.shape
    ranges = tuple((int(tile[tile >= 0].min()), int(tile.max())) for tile in idx_np)
    return pl.pallas_call(
        functools.partial(_bias_kernel, ranges=ranges),
        grid=(N_ATTN_HEADS,),
        in_specs=[pl.BlockSpec(memory_space=pltpu.SMEM),
                  pl.BlockSpec((t, r, c), lambda h: (0, 0, 0))],
        out_specs=pl.BlockSpec((1, t, r, c), lambda h: (h, 0, 0, 0)),
        out_shape=jax.ShapeDtypeStruct((N_ATTN_HEADS, t, r, c), F32),
        compiler_params=_cparams(("arbitrary",)),
        name="rel_bias_tiles",
    )(rel_bias, jnp.asarray(idx_np))


def _swa_bucket_idx():
    c = np.arange(2 * SWA_BLOCK)[:, None]
    r = np.arange(SWA_BLOCK)[None, :]
    dist = SWA_BLOCK + r - c
    return np.where((dist >= 0) & (dist < SWA_BLOCK), _rel_bucket_np(dist), -1).astype(np.int32)[None]


def _moba_bucket_idx(nblk):
    c = np.arange(MOBA_BLOCK)[:, None]
    r = np.arange(MOBA_BLOCK)[None, :]
    tiles = [np.where(m * MOBA_BLOCK + r - c >= 0, _rel_bucket_np(m * MOBA_BLOCK + r - c), -1)
             for m in range(nblk)]
    return np.stack(tiles).astype(np.int32)


def _expand_heads(v, e):
    hi = v.astype(BF16)
    lo = (v - hi.astype(F32)).astype(BF16)
    return (jnp.dot(hi, e, preferred_element_type=F32) + jnp.dot(lo, e, preferred_element_type=F32))


def _ssd_kernel(xbc_ref, z_ref, dtc_ref, dtr_ref, bias_c_ref, bias_r_ref, alog_c_ref, alog_r_ref,
                dskip_ref, nw_ref, e_ref, o_ref, state_ref):
    q = SSD_CHUNK
    gw = SSD_INNER // SSD_GROUPS
    hpg = SSD_HEADS // SSD_GROUPS

    @pl.when(pl.program_id(1) == 0)
    def _():
        state_ref[...] = jnp.zeros(state_ref.shape, F32)

    row = lax.broadcasted_iota(jnp.int32, (q, q), 0)
    col = lax.broadcasted_iota(jnp.int32, (q, q), 1)
    tril = row >= col
    tri_b = jnp.where(tril, 1.0, 0.0).astype(BF16)
    triu_b = jnp.where(row <= col, 1.0, 0.0).astype(BF16)

    dt_c = _softplus(dtc_ref[0].astype(F32) + bias_c_ref[...])
    da_c = dt_c * (-jnp.exp(alog_c_ref[...]))
    acs_c = _sum01_left(tri_b, da_c)
    dt_r = _softplus(dtr_ref[0] + bias_r_ref[...])
    da_r = dt_r * (-jnp.exp(alog_r_ref[...]))
    acs_r = _sum01_right(da_r, triu_b)

    acs_last = acs_c[q - 1:q, :]
    e = e_ref[...]
    dt_full = _expand_heads(dt_c, e)
    dtdec_full = _expand_heads(dt_c * jnp.exp(acs_last - acs_c), e)
    eacs_full = _expand_heads(jnp.exp(acs_c), e)
    cdecay_full = eacs_full[q - 1:q, :]

    xbc = xbc_ref[0].astype(F32)
    xs = xbc[:, :SSD_INNER]
    xdt = xs * dt_full
    xdec = xs * dtdec_full
    lane_half = lax.broadcasted_iota(jnp.int32, (1, LANES), 1) >> 6

    y_parts = []
    for g in range(SSD_GROUPS):
        b_g = xbc[:, SSD_INNER + g * SSD_STATE:SSD_INNER + (g + 1) * SSD_STATE]
        c_g = xbc[:, SSD_INNER + SSD_GROUPS * SSD_STATE + g * SSD_STATE:
                  SSD_INNER + SSD_GROUPS * SSD_STATE + (g + 1) * SSD_STATE]
        cb = jnp.where(tril, _bdot_nt(c_g, b_g), 0.0)
        st = state_ref[g]
        y_off = _bdot(c_g, st) * eacs_full[:, g * gw:(g + 1) * gw]
        state_ref[g] = st * cdecay_full[:, g * gw:(g + 1) * gw] + _bdot(b_g.T, xdec[:, g * gw:(g + 1) * gw])
        pair_parts = []
        for pr in range(hpg // 2):
            acc = None
            lo = g * gw + pr * LANES
            x_pair = xdt[:, lo:lo + LANES]
            for half in range(2):
                h = g * hpg + pr * 2 + half
                diff = acs_c[:, h:h + 1] - acs_r[h:h + 1, :]
                lmat = jnp.exp(jnp.minimum(diff, 0.0))
                part = _bdot(cb * lmat, jnp.where(lane_half == half, x_pair, 0.0))
                acc = part if acc is None else acc + part
            pair_parts.append(acc)
        y_diag = jnp.concatenate(pair_parts, axis=1)
        y = y_diag + y_off + dskip_ref[:, g * gw:(g + 1) * gw] * xs[:, g * gw:(g + 1) * gw]
        y = y * _silu(z_ref[0, :, g * gw:(g + 1) * gw].astype(F32))
        ms = jnp.mean(y * y, axis=-1, keepdims=True)
        y_parts.append(y * lax.rsqrt(ms + RMS_EPS) * nw_ref[:, g * gw:(g + 1) * gw])
    o_ref[0] = jnp.concatenate(y_parts, axis=1).astype(o_ref.dtype)


N_SSD_IN = 11


def _layer0_mixers_kernel(*refs):
    ssd_in, swa_in = refs[:N_SSD_IN], refs[N_SSD_IN:N_SSD_IN + 7]
    o_ssd, o_swa, state_ref, s_ref = refs[N_SSD_IN + 7:]
    _ssd_kernel(*ssd_in, o_ssd, state_ref)
    _swa_kernel(*swa_in, o_swa, s_ref)


def _layer0_mixers(xbc_act, proj, dt_rows, dt_bias, a_log, d_skip, norm_w, sinks, swa_bias):
    bsz, s, _ = proj.shape
    q = SSD_CHUNK
    assert SWA_BLOCK == q
    pad = LANES - SSD_HEADS
    bias_c = jnp.pad(dt_bias, (0, pad)).reshape(1, LANES)
    alog_c = jnp.pad(a_log, (0, pad)).reshape(1, LANES)
    e_np = np.zeros((LANES, SSD_INNER), np.float32)
    for h in range(SSD_HEADS):
        e_np[h, h * SSD_HEAD_DIM:(h + 1) * SSD_HEAD_DIM] = 1.0
    small = lambda shape: pl.BlockSpec(shape, lambda b_, c: (0, 0))
    rows = lambda n, col: pl.BlockSpec((1, q, n), lambda b_, c: (b_, c, col))
    kvw = SWA_KV_HEADS * ATTN_HEAD_DIM
    prev = lambda col: pl.BlockSpec((1, q, kvw), lambda b_, c: (b_, jnp.maximum(c - 1, 0), col))
    ssd_specs = [rows(SSD_XBC, 0), rows(SSD_INNER, AB_Z // SSD_INNER), rows(LANES, AB_DT // LANES),
                 pl.BlockSpec((1, LANES, q), lambda b_, c: (b_, 0, c)),
                 small((1, LANES)), small((LANES, 1)), small((1, LANES)), small((LANES, 1)),
                 small((1, SSD_INNER)), small((1, SSD_INNER)), small((LANES, SSD_INNER))]
    assert len(ssd_specs) == N_SSD_IN
    swa_specs = [pl.BlockSpec(memory_space=pltpu.SMEM), rows(ATTN_WIDTH, AB_Q // ATTN_WIDTH),
                 prev(AB_K // kvw), rows(kvw, AB_K // kvw), prev(AB_V // kvw), rows(kvw, AB_V // kvw),
                 pl.BlockSpec((N_ATTN_HEADS, 1, 2 * q, q), lambda b_, c: (0, 0, 0, 0))]
    return pl.pallas_call(
        _layer0_mixers_kernel,
        grid=(bsz, s // q),
        in_specs=ssd_specs + swa_specs,
        out_specs=(rows(SSD_INNER, 0), rows(ATTN_WIDTH, 0)),
        out_shape=(jax.ShapeDtypeStruct((bsz, s, SSD_INNER), BF16),
                   jax.ShapeDtypeStruct((bsz, s, ATTN_WIDTH), BF16)),
        scratch_shapes=[pltpu.VMEM((SSD_GROUPS, SSD_STATE, SSD_INNER // SSD_GROUPS), F32),
                        pltpu.VMEM((N_ATTN_HEADS, 2 * q, q), F32)],
        compiler_params=_cparams(("arbitrary", "arbitrary")),
        name="ssd_swa_mixers",
    )(xbc_act, proj, proj, dt_rows, bias_c, bias_c.reshape(LANES, 1),
      alog_c, alog_c.reshape(LANES, 1), jnp.repeat(d_skip, SSD_HEAD_DIM).reshape(1, SSD_INNER),
      norm_w.reshape(1, SSD_INNER), jnp.asarray(e_np, BF16),
      sinks, proj, proj, proj, proj, proj, swa_bias)


def _swa_kernel(sink_ref, q_ref, kp_ref, kc_ref, vp_ref, vc_ref, bias_ref, o_ref, s_ref):
    blk = SWA_BLOCK
    n = pl.program_id(1)
    grp = N_ATTN_HEADS // SWA_KV_HEADS
    scale = ATTN_HEAD_DIM ** -0.5
    kk = jnp.concatenate([kp_ref[0], kc_ref[0]], axis=0).astype(F32) * (scale * LOG2E)
    vv_t = jnp.concatenate([vp_ref[0], vc_ref[0]], axis=0).astype(F32).T
    lane_half = lax.broadcasted_iota(jnp.int32, (1, LANES), 1) >> 6
    c = lax.broadcasted_iota(jnp.int32, (2 * blk, blk), 0)
    valid = (c >= blk) | (n > 0)
    ms = []
    for kv in range(SWA_KV_HEADS):
        k_own = jnp.where(lane_half == kv, kk, 0.0).astype(BF16)
        k_var = [None, None]
        k_var[kv] = k_own
        k_var[1 - kv] = pltpu.roll(jnp.where(lane_half == kv, kk, 0.0), ATTN_HEAD_DIM, axis=1).astype(BF16)
        for gq in range(grp):
            h = kv * grp + gq
            q_tile = q_ref[0, :, (h // 2) * LANES:(h // 2 + 1) * LANES]
            s_t = _bdot_nt(k_var[h % 2], q_tile) + bias_ref[h, 0]
            s_t = jnp.where(valid, s_t, NEG)
            s_ref[h] = s_t
            ms.append(jnp.maximum(jnp.max(s_t, axis=0, keepdims=True), sink_ref[h] * LOG2E))
    outs = []
    for h in range(N_ATTN_HEADS):
        kv = h // grp
        p = jnp.exp2(s_ref[h] - ms[h])
        l = jnp.sum(p, axis=0, keepdims=True) + jnp.exp2(sink_ref[h] * LOG2E - ms[h])
        outs.append(_bdot(vv_t[kv * ATTN_HEAD_DIM:(kv + 1) * ATTN_HEAD_DIM, :], p) / l)
    for t in range(N_ATTN_HEADS // 2):
        pair = jnp.concatenate([outs[2 * t], outs[2 * t + 1]], axis=0)
        o_ref[0, :, t * LANES:(t + 1) * LANES] = pair.T.astype(o_ref.dtype)


def _moba_kernel(q_ref, k_ref, v_ref, bias_ref, o_ref, vt_ref, qm_ref, negrow_ref, s_ref, *, nblk):
    mb = MOBA_BLOCK
    dh = ATTN_HEAD_DIM
    own0 = 2 * lax.rem(pl.program_id(1), nblk // 2)
    scale = dh ** -0.5

    @pl.when(own0 == 0)
    def _():
        means = []
        for j in range(nblk):
            vt_ref[j] = v_ref[0, j * mb:(j + 1) * mb, :].astype(F32).T.astype(BF16)
            means.append(jnp.mean(k_ref[0, j * mb:(j + 1) * mb, :].astype(F32), axis=0, keepdims=True))
        kmean = jnp.concatenate(means, axis=0)
        lane_half = lax.broadcasted_iota(jnp.int32, (1, LANES), 1) >> 6
        blk_id = lax.broadcasted_iota(jnp.int32, (nblk, mb), 0)
        for qb in range(nblk):
            q = q_ref[0, qb * mb:(qb + 1) * mb, :].astype(F32) * (scale * LOG2E)
            for hh in range(2):
                qm = jnp.where(lane_half == hh, q, 0.0)
                qm_ref[hh, qb] = qm.astype(BF16)
                gate = lax.dot_general(kmean, qm, (((1,), (1,)), ((), ())),
                                       preferred_element_type=F32, precision=HIGHEST)
                gate = jnp.where(blk_id < qb, gate, NEG)
                rank = jnp.zeros((nblk, mb), jnp.int32)
                for i in range(nblk):
                    gi = gate[i:i + 1, :]
                    ahead = (gi > gate) | ((gi == gate) & (i < blk_id))
                    rank = rank + jnp.where(ahead, 1, 0)
                keep = ((rank < MOBA_TOPK) & (blk_id < qb)) | (blk_id == qb)
                negrow_ref[hh, qb] = jnp.where(keep, 0.0, NEG).astype(F32)

    def attend(nb, qi):
        own = own0 + qi
        qms = [qm_ref[hh, own] for hh in range(2)]
        negrows = [negrow_ref[hh, own] for hh in range(2)]
        m8 = [None, None]
        l8 = [None, None]
        acc = [None, None]

        def scores(hh, j):
            s_t = _bdot_nt(k_ref[0, j * mb:(j + 1) * mb, :], qms[hh]) + bias_ref[hh, jnp.maximum(own - j, 0)]
            s_t = s_t + negrows[hh][j:j + 1, :]
            s_ref[qi, hh, j] = s_t
            m_j = jnp.max(s_t.reshape(mb // SUBLANES, SUBLANES, mb), axis=0)
            m8[hh] = m_j if m8[hh] is None else jnp.maximum(m8[hh], m_j)

        def weights(hh, j, m):
            p = jnp.exp2(s_ref[qi, hh, j] - m)
            l_j = jnp.sum(p.reshape(mb // SUBLANES, SUBLANES, mb), axis=0)
            a_j = _bdot(vt_ref[j, hh * dh:(hh + 1) * dh, :], p)
            l8[hh] = l_j if l8[hh] is None else l8[hh] + l_j
            acc[hh] = a_j if acc[hh] is None else acc[hh] + a_j

        for j in range(nb):
            for hh in range(2):
                scores(hh, j)
        m = [jnp.max(m8[hh], axis=0, keepdims=True) for hh in range(2)]
        for j in range(nb):
            for hh in range(2):
                weights(hh, j, m[hh])
        out_t = jnp.concatenate([acc[hh] / jnp.sum(l8[hh], axis=0, keepdims=True) for hh in range(2)], axis=0)
        o_ref[0, qi * mb:(qi + 1) * mb, :] = out_t.T.astype(o_ref.dtype)

    def attend_both(nb):
        attend(nb, 0)
        attend(nb, 1)

    for nb in range(2, nblk + 1, 2):
        pl.when(own0 == nb - 2)(functools.partial(attend_both, nb))


def _moba_specs(bsz, s):
    mb = MOBA_BLOCK
    nblk = s // mb
    half = nblk // 2
    col = lambda c0: (lambda b_, c: (b_, 0, c0 // LANES + c // half))
    in_specs = [pl.BlockSpec((1, s, LANES), col(CD_QD)),
                pl.BlockSpec((1, s, LANES), col(CD_KD)),
                pl.BlockSpec((1, s, LANES), col(CD_VD)),
                pl.BlockSpec((2, nblk, mb, mb), lambda b_, c: (c // half, 0, 0, 0))]
    out_spec = pl.BlockSpec((1, 2 * mb, LANES), lambda b_, c: (b_, c % half, c // half))
    out_shape = jax.ShapeDtypeStruct((bsz, s, ATTN_WIDTH), BF16)
    scratch = [pltpu.VMEM((nblk, LANES, mb), BF16),
               pltpu.VMEM((2, nblk, mb, LANES), BF16),
               pltpu.VMEM((2, nblk, nblk, mb), F32),
               pltpu.VMEM((2, 2, nblk, mb, mb), F32)]
    return in_specs, out_spec, out_shape, scratch


def _gdn_kernel(qkv_ref, z_ref, bac_ref, bar_ref, bias_c_ref, bias_r_ref, alog_c_ref, alog_r_ref,
                nw_ref, o_ref, state_ref, m_ref, attn_ref, t_ref, x_ref, rhs_ref, u_ref, wq_ref,
                kdt_ref, r_ref, glc_ref):
    t = 2 * GDN_CHUNK
    ck = GDN_CHUNK
    dk = GDN_HEAD_DIM
    nh = GDN_HEADS

    @pl.when(pl.program_id(1) == 0)
    def _():
        state_ref[...] = jnp.zeros(state_ref.shape, F32)

    row = lax.broadcasted_iota(jnp.int32, (t, t), 0)
    col = lax.broadcasted_iota(jnp.int32, (t, t), 1)
    same = (row >> 6) == (col >> 6)
    tril = same & (row >= col)
    strict = same & (row > col)
    tri_b = jnp.where(tril, 1.0, 0.0).astype(BF16)
    triu_b = jnp.where(same & (row <= col), 1.0, 0.0).astype(BF16)
    blk_b = jnp.where(same, 1.0, 0.0).astype(BF16)
    eye_f = jnp.where(row == col, 1.0, 0.0).astype(F32)
    merge_masks = [((row >> (l + 1)) == (col >> (l + 1))) & (((row >> l) & 1) == 1) & (((col >> l) & 1) == 0)
                   for l in range(int(math.log2(ck)))]
    mask_bf = [jnp.where(m, 1.0, 0.0).astype(BF16) for m in merge_masks[1:]]

    ba_c = bac_ref[0].astype(F32)
    g_c = -jnp.exp(alog_c_ref[...]) * _softplus(ba_c + bias_c_ref[...])
    sums_c = _sum01_left(jnp.concatenate([tri_b, blk_b], axis=0), g_c)
    gc_c = sums_c[:t]
    gl_c = sums_c[t:]
    g_r = -jnp.exp(alog_r_ref[...]) * _softplus(bar_ref[0] + bias_r_ref[...])
    gc_r = _sum01_right(g_r, triu_b)
    glc_ref[...] = gl_c

    def prep(h):
        q = qkv_ref[0, :, h * dk:(h + 1) * dk].astype(F32)
        k = qkv_ref[0, :, (nh + h) * dk:(nh + h + 1) * dk].astype(F32)
        v = qkv_ref[0, :, (2 * nh + h) * dk:(2 * nh + h + 1) * dk].astype(F32)
        qn = q * lax.rsqrt(jnp.sum(q * q, axis=-1, keepdims=True) + 1e-6) * (dk ** -0.5)
        kn = k * lax.rsqrt(jnp.sum(k * k, axis=-1, keepdims=True) + 1e-6)
        beta = _sigmoid(ba_c[:, h:h + 1])
        gcc = gc_c[:, nh + h:nh + h + 1]
        gcr = gc_r[nh + h:nh + h + 1, :]
        glc = gl_c[:, nh + h:nh + h + 1]
        decay = jnp.where(tril, jnp.exp(jnp.where(tril, gcc - gcr, 0.0)), 0.0)
        kb = kn * beta
        kk = _bdot_nt(jnp.concatenate([kb, qn], axis=0), kn)
        mm = jnp.where(strict, kk[:t] * decay, 0.0)
        m_ref[h] = mm.astype(BF16)
        attn_ref[h] = (kk[t:] * decay).astype(BF16)
        t_ref[h] = eye_f - jnp.where(merge_masks[0], mm, 0.0)
        egc = jnp.exp(gcc)
        rhs_ref[h] = jnp.concatenate([v * beta, kb * egc], axis=1).astype(BF16)
        q_dec = (qn * egc).astype(BF16)
        for a in range(2):
            wq_ref[h, a, ck:, :] = q_dec[a * ck:(a + 1) * ck]
        kdt_ref[h] = (kn * jnp.exp(glc - gcc)).T.astype(BF16)

    def merge_a(lvl, h):
        x_ref[h] = jnp.dot(t_ref[h].astype(BF16), m_ref[h] * mask_bf[lvl],
                           preferred_element_type=F32).astype(BF16)

    def merge_b(lvl, h):
        t_h = t_ref[h]
        t_ref[h] = t_h - jnp.dot(x_ref[h], t_h.astype(BF16), preferred_element_type=F32)

    def solve(h):
        sol = jnp.dot(t_ref[h].astype(BF16), rhs_ref[h], preferred_element_type=F32)
        u_ref[h] = sol[:, :dk]
        for a in range(2):
            wq_ref[h, a, :ck, :] = sol[a * ck:(a + 1) * ck, dk:].astype(BF16)

    zeros_half = jnp.zeros((ck, dk), F32)

    def read_state(a, h):
        r_ref[h] = jnp.dot(wq_ref[h, a], state_ref[h].astype(BF16), preferred_element_type=F32)

    def update(a, h):
        sl = slice(a * ck, (a + 1) * ck)
        v_new = u_ref[h, sl, :] - r_ref[h, :ck, :]
        v_full = jnp.concatenate([v_new, zeros_half] if a == 0 else [zeros_half, v_new], axis=0).astype(BF16)
        o = r_ref[h, ck:, :] + jnp.dot(attn_ref[h, sl, :], v_full, preferred_element_type=F32)
        gl = glc_ref[a * ck:a * ck + 1, nh + h:nh + h + 1]
        state_ref[h] = state_ref[h] * jnp.exp(gl) + jnp.dot(kdt_ref[h], v_full, preferred_element_type=F32)
        ms = jnp.mean(o * o, axis=-1, keepdims=True)
        y = o * lax.rsqrt(ms + RMS_EPS) * nw_ref[...] * _silu(z_ref[0, sl, h * dk:(h + 1) * dk].astype(F32))
        o_ref[0, sl, h * dk:(h + 1) * dk] = y.astype(o_ref.dtype)

    def steps(phase, heads):
        if phase == "prep":
            return [functools.partial(prep, h) for h in heads]
        if phase == "merge":
            return [functools.partial(fn, lvl, h) for lvl in range(len(mask_bf))
                    for fn in (merge_a, merge_b) for h in heads]
        return ([functools.partial(solve, h) for h in heads]
                + [functools.partial(fn, a, h) for a in range(2) for fn in (read_state, update) for h in heads])

    for phase in ("prep", "merge", "tail"):
        for step in steps(phase, range(nh)):
            step()


N_GDN_IN, N_GDN_SCRATCH, N_MOBA_IN = 9, 11, 4


def _layer1_mixers_kernel(*refs, nblk):
    gdn_in = refs[:N_GDN_IN]
    moba_in = refs[N_GDN_IN:N_GDN_IN + N_MOBA_IN]
    o_gdn, o_moba = refs[N_GDN_IN + N_MOBA_IN:N_GDN_IN + N_MOBA_IN + 2]
    scratch = refs[N_GDN_IN + N_MOBA_IN + 2:]
    _gdn_kernel(*gdn_in, o_gdn, *scratch[:N_GDN_SCRATCH])
    _moba_kernel(*moba_in, o_moba, *scratch[N_GDN_SCRATCH:], nblk=nblk)


def _layer1_mixers(qkv_act, proj, ba_rows, dt_bias, a_log, norm_w, moba_bias):
    bsz, s, _ = proj.shape
    t = 2 * GDN_CHUNK
    nh = GDN_HEADS
    dk = GDN_HEAD_DIM
    nblk = s // MOBA_BLOCK
    assert s // t == (N_ATTN_HEADS // 2) * (nblk // 2)
    bias_c = jnp.pad(dt_bias, (nh, LANES - 2 * nh)).reshape(1, LANES)
    alog_c = jnp.pad(a_log, (nh, LANES - 2 * nh)).reshape(1, LANES)
    small = lambda shape: pl.BlockSpec(shape, lambda b_, c: (0, 0))
    moba_in, moba_out, moba_shape, moba_scratch = _moba_specs(bsz, s)
    gdn_in = [pl.BlockSpec((1, t, 3 * GDN_INNER), lambda b_, c: (b_, c, 0)),
              pl.BlockSpec((1, t, GDN_INNER), lambda b_, c: (b_, c, CD_Z // GDN_INNER)),
              pl.BlockSpec((1, t, LANES), lambda b_, c: (b_, c, CD_BA // LANES)),
              pl.BlockSpec((1, LANES, t), lambda b_, c: (b_, 0, c)),
              small((1, LANES)), small((LANES, 1)), small((1, LANES)), small((LANES, 1)),
              small((1, dk))]
    assert len(gdn_in) == N_GDN_IN and len(moba_in) == N_MOBA_IN
    return pl.pallas_call(
        functools.partial(_layer1_mixers_kernel, nblk=nblk),
        grid=(bsz, s // t),
        in_specs=gdn_in + moba_in,
        out_specs=(pl.BlockSpec((1, t, GDN_INNER), lambda b_, c: (b_, c, 0)), moba_out),
        out_shape=(jax.ShapeDtypeStruct((bsz, s, GDN_INNER), BF16), moba_shape),
        scratch_shapes=[pltpu.VMEM((nh, dk, dk), F32),
                        pltpu.VMEM((nh, t, t), BF16),
                        pltpu.VMEM((nh, t, t), BF16),
                        pltpu.VMEM((nh, t, t), F32),
                        pltpu.VMEM((nh, t, t), BF16),
                        pltpu.VMEM((nh, t, 2 * dk), BF16),
                        pltpu.VMEM((nh, t, dk), F32),
                        pltpu.VMEM((nh, 2, t, dk), BF16),
                        pltpu.VMEM((nh, dk, t), BF16),
                        pltpu.VMEM((nh, t, dk), F32),
                        pltpu.VMEM((t, LANES), F32)]
        + moba_scratch,
        compiler_params=_cparams(("arbitrary", "arbitrary")),
        name="gdn_moba_mixers",
    )(qkv_act, proj, proj, ba_rows, bias_c, bias_c.reshape(LANES, 1), alog_c, alog_c.reshape(LANES, 1),
      norm_w.reshape(1, dk), proj, proj, proj, moba_bias)


def _reorder_kernel(w_ref, o_ref, *, segments, zero_from):
    rows, n = o_ref.shape
    if zero_from < n:
        o_ref[:, zero_from:] = jnp.zeros((rows, n - zero_from), o_ref.dtype)
    for lo, hi, dst in segments:
        o_ref[:, dst:dst + hi - lo] = w_ref[:, lo:hi].astype(o_ref.dtype)


def _reorder_cast(w_stack, layer, segments, zero_from, n_out, tr=256):
    nl, d, n = w_stack.shape
    if nl == 1:
        w_stack = w_stack.reshape(d, n)
        w_spec = pl.BlockSpec((tr, n), lambda r: (r, 0))
    else:
        w_spec = pl.BlockSpec((None, tr, n), lambda r: (layer, r, 0))
    return pl.pallas_call(
        functools.partial(_reorder_kernel, segments=segments, zero_from=zero_from),
        grid=(d // tr,),
        in_specs=[w_spec],
        out_specs=pl.BlockSpec((tr, n_out), lambda r: (r, 0)),
        out_shape=jax.ShapeDtypeStruct((d, n_out), BF16),
        compiler_params=_cparams(("arbitrary",)),
        name="reorder_cast_weights",
    )(w_stack)


def _gate_rows(proj, col0):
    return jnp.swapaxes(proj[:, :, col0:col0 + LANES].astype(F32), 1, 2)


def kernel(x, c, rel_bias, norm_w, ada_w, ada_b, ab_w_in, ab_w_out, ssd_conv_w, ssd_conv_b,
           ssd_dt_bias, ssd_a_log, ssd_d, ssd_norm_w, swa_sinks, cd_w_in, cd_w_out, gdn_conv_w,
           gdn_dt_bias, gdn_a_log, gdn_norm_w, ffn_w_up, ffn_conv_w, ffn_conv_b, ffn_w_down):
    bsz, s, d = x.shape
    depth = norm_w.shape[0]
    mods = _mods(c, ada_w, ada_b)
    swa_bias = _bias_tiles(rel_bias, _swa_bucket_idx())
    moba_bias = _bias_tiles(rel_bias, _moba_bucket_idx(s // MOBA_BLOCK))

    for i in range(depth):
        sh_m, sc_m, g_m, sh_f, sc_f, g_f = [m.reshape(bsz, 1, d) for m in jnp.split(mods[i], 6, axis=-1)]
        j = i // 2
        if i % 2 == 0:
            dt0 = SSD_INNER + SSD_XBC
            n_in = ab_w_in.shape[-1]
            w_in = _reorder_cast(ab_w_in, j, ((SSD_INNER, dt0, 0), (0, SSD_INNER, SSD_XBC + AB_Z),
                                              (dt0 + SSD_HEADS, n_in, SSD_XBC + AB_Q), (dt0, dt0 + SSD_HEADS, SSD_XBC + AB_DT)),
                                 SSD_XBC + AB_DT, SSD_XBC + AB_COLS)
            xbc_act, proj = _proj_conv_act(x, norm_w[i, 0], sc_m, sh_m, w_in, ssd_conv_w[j], ssd_conv_b[j],
                                           "silu", n_plain=AB_COLS)
            y_a, y_b = _layer0_mixers(xbc_act, proj, _gate_rows(proj, AB_DT), ssd_dt_bias[j], ssd_a_log[j],
                                      ssd_d[j], ssd_norm_w[j], swa_sinks[j], swa_bias)
            x = _matmul_resid([y_a, y_b], ab_w_out, j, x, g_m, norm_w[i, 1])
        else:
            nqkv = 3 * GDN_INNER
            ba0 = nqkv + GDN_INNER
            n_in = cd_w_in.shape[-1]
            w_in = _reorder_cast(cd_w_in, j, ((0, ba0, 0), (ba0 + 2 * GDN_HEADS, n_in, nqkv + CD_QD),
                                              (ba0, ba0 + 2 * GDN_HEADS, nqkv + CD_BA)),
                                 nqkv + CD_BA, nqkv + CD_COLS)
            qkv_act, proj = _proj_conv_act(x, norm_w[i, 0], sc_m, sh_m, w_in, gdn_conv_w[j],
                                           jnp.zeros((nqkv,), F32), "silu", n_plain=CD_COLS)
            y_c, y_d = _layer1_mixers(qkv_act, proj, _gate_rows(proj, CD_BA), gdn_dt_bias[j], gdn_a_log[j],
                                      gdn_norm_w[j], moba_bias)
            x = _matmul_resid([y_c, y_d], cd_w_out, j, x, g_m, norm_w[i, 1])
        n_up = ffn_w_up.shape[-1]
        w_up = _reorder_cast(ffn_w_up, i, ((0, n_up, 0),), n_up, n_up)
        act = _proj_conv_act(x, norm_w[i, 2], sc_f, sh_f, w_up, ffn_conv_w[i], ffn_conv_b[i], "geglu")
        x = _matmul_resid([act], ffn_w_down, i, x, g_f, norm_w[i, 3])
    return x
```

```python
import functools
import math

import numpy as np
import jax
import jax.numpy as jnp
from jax import lax
from jax.experimental import pallas as pl
from jax.experimental.pallas import tpu as pltpu

F32 = jnp.float32
BF16 = jnp.bfloat16
HIGHEST = lax.Precision.HIGHEST

D_MODEL = 1024
RMS_EPS = 1e-6
NEG = -1e30
LANES = 128
SUBLANES = 8
N_ATTN_HEADS = 8
ATTN_HEAD_DIM = 64
ATTN_WIDTH = N_ATTN_HEADS * ATTN_HEAD_DIM
REL_BUCKETS = 32
REL_MAX_DIST = 1024
SSD_HEADS = 24
SSD_HEAD_DIM = 64
SSD_INNER = SSD_HEADS * SSD_HEAD_DIM
SSD_GROUPS = 4
SSD_STATE = 128
SSD_CONV = 4
SSD_CHUNK = 128
SSD_XBC = SSD_INNER + 2 * SSD_GROUPS * SSD_STATE
SWA_KV_HEADS = 2
SWA_BLOCK = 128
GDN_HEADS = 12
GDN_HEAD_DIM = 128
GDN_INNER = GDN_HEADS * GDN_HEAD_DIM
GDN_CONV = 4
GDN_CHUNK = 64
MOBA_BLOCK = 256
MOBA_TOPK = 3
FFN_DIM = 2816
FFN_CONV = 3

AB_Z, AB_Q, AB_K, AB_V, AB_DT, AB_COLS = 0, 1536, 2048, 2176, 2304, 2560
CD_Z, CD_QD, CD_KD, CD_VD, CD_BA, CD_COLS = 0, 1536, 2048, 2560, 3072, 3328

VMEM_LIMIT = 48 * 1024 * 1024
HALO = 16


def _cparams(sem):
    return pltpu.CompilerParams(dimension_semantics=sem, vmem_limit_bytes=VMEM_LIMIT)


def _bdot(a, b):
    return jnp.dot(a.astype(BF16), b.astype(BF16), preferred_element_type=F32)


def _bdot_nt(a, b):
    return lax.dot_general(a.astype(BF16), b.astype(BF16), (((1,), (1,)), ((), ())),
                           preferred_element_type=F32)


def _split3(x):
    hi = x.astype(BF16)
    r1 = x - hi.astype(F32)
    mid = r1.astype(BF16)
    lo = (r1 - mid.astype(F32)).astype(BF16)
    return hi, mid, lo


def _sum01_left(m01, x):
    n = x.shape[1]
    y = jnp.dot(m01, jnp.concatenate(_split3(x), axis=1), preferred_element_type=F32)
    return y[:, :n] + y[:, n:2 * n] + y[:, 2 * n:]


def _sum01_right(x, m01):
    n = x.shape[0]
    y = jnp.dot(jnp.concatenate(_split3(x), axis=0), m01, preferred_element_type=F32)
    return y[:n] + y[n:2 * n] + y[2 * n:]


def _softplus(x):
    return jnp.maximum(x, 0.0) + jnp.log(1.0 + jnp.exp(-jnp.abs(x)))


def _sigmoid(x):
    return 1.0 / (1.0 + jnp.exp(-x))


def _silu(x):
    return x * _sigmoid(x)


def _window_conv(win_ref, lo, ncols, rows, w, width, row0=0, shifted_loads=False):
    if shifted_loads:
        acc = None
        for s in range(width):
            tap = win_ref[pl.ds(HALO + row0 - s, rows), lo:lo + ncols] * w[width - 1 - s:width - s, :]
            acc = tap if acc is None else acc + tap
        return acc
    xin = win_ref[pl.ds(HALO + row0 - SUBLANES, rows + SUBLANES), lo:lo + ncols]
    acc = xin[SUBLANES:, :] * w[width - 1:width, :]
    for s in range(1, width):
        acc = acc + pltpu.roll(xin, s, axis=0)[SUBLANES:, :] * w[width - 1 - s:width - s, :]
    return acc


def _mods_kernel(c_ref, w_ref, b_ref, o_ref):
    o_ref[0] = _bdot(_silu(c_ref[...]), w_ref[0]) + b_ref[0]


def _mods(c, ada_w, ada_b):
    depth, d, n = ada_w.shape
    bsz = c.shape[0]
    tn = 512
    return pl.pallas_call(
        _mods_kernel,
        grid=(depth, n // tn),
        in_specs=[pl.BlockSpec((bsz, d), lambda l, j: (0, 0)),
                  pl.BlockSpec((1, d, tn), lambda l, j: (l, 0, j)),
                  pl.BlockSpec((1, 1, tn), lambda l, j: (l, 0, j))],
        out_specs=pl.BlockSpec((1, bsz, tn), lambda l, j: (l, 0, j)),
        out_shape=jax.ShapeDtypeStruct((depth, bsz, n), F32),
        compiler_params=_cparams(("arbitrary", "arbitrary")),
        name="adaln_mods",
    )(c, ada_w, ada_b.reshape(depth, 1, n))


def _modulated_norm(x, nw, sc, sh):
    ms = jnp.mean(x * x, axis=-1, keepdims=True)
    return x * lax.rsqrt(ms + RMS_EPS) * nw * (1.0 + sc) + sh


def _mmres_kernel(*refs, splits):
    na = len(splits)
    a_refs = refs[:na]
    w_ref, x_ref, g_ref, nw_ref, o_ref = refs[na:]
    acc = None
    lo = 0
    for a_ref, k in zip(a_refs, splits):
        part = jnp.dot(a_ref[0].astype(BF16), w_ref[lo:lo + k, :].astype(BF16), preferred_element_type=F32)
        acc = part if acc is None else acc + part
        lo += k
    ms = jnp.mean(acc * acc, axis=-1, keepdims=True)
    y = acc * lax.rsqrt(ms + RMS_EPS) * nw_ref[...]
    o_ref[0] = x_ref[0] + g_ref[0] * y


def _matmul_resid(a_list, w_stack, layer, x, gate, nw, tm=1024):
    bsz, s, d = x.shape
    spt = s // tm
    splits = tuple(a.shape[-1] for a in a_list)
    ktot = sum(splits)
    in_specs = [pl.BlockSpec((1, tm, k), lambda i: (i // spt, i % spt, 0)) for k in splits]
    in_specs += [pl.BlockSpec((None, ktot, d), lambda i: (layer, 0, 0)),
                 pl.BlockSpec((1, tm, d), lambda i: (i // spt, i % spt, 0)),
                 pl.BlockSpec((1, 1, d), lambda i: (i // spt, 0, 0)),
                 pl.BlockSpec((1, d), lambda i: (0, 0))]
    return pl.pallas_call(
        functools.partial(_mmres_kernel, splits=splits),
        grid=(bsz * spt,),
        in_specs=in_specs,
        out_specs=pl.BlockSpec((1, tm, d), lambda i: (i // spt, i % spt, 0)),
        out_shape=jax.ShapeDtypeStruct((bsz, s, d), F32),
        compiler_params=_cparams(("arbitrary",)),
        name="matmul_resid",
    )(*a_list, w_stack, x, gate, nw.reshape(1, d))


def _proj_conv_kernel(*refs, offs, tc, width, rb, act, n_plain):
    if n_plain:
        xh_ref, x_ref, nw_ref, sc_ref, sh_ref, w_ref, cw_ref, cb_ref, o_ref, op_ref, h_ref, u_ref = refs
    else:
        xh_ref, x_ref, nw_ref, sc_ref, sh_ref, w_ref, cw_ref, cb_ref, o_ref, h_ref, u_ref = refs
    tm = x_ref.shape[1]
    nchunk = o_ref.shape[2] // tc
    nplain = n_plain // tc
    plain0 = w_ref.shape[1] - n_plain
    h_halo = _modulated_norm(xh_ref[0], nw_ref[...], sc_ref[0], sh_ref[0])
    h_ref[:HALO, :] = jnp.where(pl.program_id(1) > 0, h_halo, 0.0).astype(BF16)
    h_ref[HALO:, :] = _modulated_norm(x_ref[0], nw_ref[...], sc_ref[0], sh_ref[0]).astype(BF16)
    c0 = math.sqrt(2.0 / math.pi)

    nrow = HALO + tm

    def matmuls(c):
        for k, off in enumerate(offs):
            lo = off + c * tc
            u = jnp.dot(h_ref[...], w_ref[:, lo:lo + tc], preferred_element_type=F32)
            for s in range(width):
                u_ref[c % 2, k, s, pl.ds(s, nrow), :] = u

    def epilogue(c):
        for r0 in range(0, tm, rb):
            conv = []
            for k, off in enumerate(offs):
                lo = off + c * tc
                acc = cb_ref[:, lo:lo + tc]
                for s in range(width):
                    acc = acc + (u_ref[c % 2, k, s, pl.ds(HALO + r0, rb), :]
                                 * cw_ref[width - 1 - s:width - s, lo:lo + tc])
                conv.append(acc)
            if act == "geglu":
                g, v = conv
                th = jnp.tanh(g * (c0 + (c0 * 0.044715) * (g * g)))
                hg = 0.5 * g
                out = (hg + hg * th) * v
            else:
                out = _silu(conv[0])
            o_ref[0, r0:r0 + rb, c * tc:(c + 1) * tc] = out.astype(o_ref.dtype)

    def plain(p):
        lo = plain0 + p * tc
        op_ref[0, :, p * tc:(p + 1) * tc] = jnp.dot(
            h_ref[HALO:, :], w_ref[:, lo:lo + tc], preferred_element_type=F32).astype(op_ref.dtype)

    matmuls(0)
    done = 0
    for c in range(nchunk):
        if c + 1 < nchunk:
            matmuls(c + 1)
        epilogue(c)
        upto = (c + 1) * nplain // nchunk
        for p in range(done, upto):
            plain(p)
        done = upto


def _proj_conv_act(x, nw, sc, sh, w, cw, cb, act, n_plain=0, tm=512, tc=256, rb=128):
    bsz, s, d = x.shape
    n2 = w.shape[-1] - n_plain
    f = n2 // 2 if act == "geglu" else n2
    offs = (0, f) if act == "geglu" else (0,)
    width = cw.shape[0]
    hb = tm // HALO
    const = lambda shape: pl.BlockSpec(shape, lambda b_, r: (0, 0))
    rows = lambda n: pl.BlockSpec((1, tm, n), lambda b_, r: (b_, r, 0))
    in_specs = [pl.BlockSpec((1, HALO, d), lambda b_, r: (b_, jnp.maximum(r * hb - 1, 0), 0)),
                rows(d), const((1, d)),
                pl.BlockSpec((1, 1, d), lambda b_, r: (b_, 0, 0)),
                pl.BlockSpec((1, 1, d), lambda b_, r: (b_, 0, 0)),
                const((d, n2 + n_plain)),
                const((width, n2)), const((1, n2))]
    args = [x, x, nw.reshape(1, d), sc, sh, w, cw, cb.reshape(1, n2)]
    out_specs = rows(f)
    out_shape = jax.ShapeDtypeStruct((bsz, s, f), BF16)
    if n_plain:
        out_specs = (out_specs, rows(n_plain))
        out_shape = (out_shape, jax.ShapeDtypeStruct((bsz, s, n_plain), BF16))
    return pl.pallas_call(
        functools.partial(_proj_conv_kernel, offs=offs, tc=tc, width=width, rb=rb, act=act, n_plain=n_plain),
        grid=(bsz, s // tm),
        in_specs=in_specs,
        out_specs=out_specs,
        out_shape=out_shape,
        scratch_shapes=[pltpu.VMEM((HALO + tm, d), BF16),
                        pltpu.VMEM((2, len(offs), width, HALO + tm + SUBLANES, tc), F32)],
        compiler_params=_cparams(("arbitrary", "arbitrary")),
        name="proj_conv_" + act,
    )(*args)


def _rel_bucket_np(d):
    max_exact = REL_BUCKETS // 2
    d = np.maximum(d, 0)
    df = np.maximum(d, 1).astype(np.float64)
    large = max_exact + (np.log(df / max_exact) / math.log(REL_MAX_DIST / max_exact)
                         * (REL_BUCKETS - max_exact)).astype(np.int32)
    large = np.minimum(large, REL_BUCKETS - 1)
    return np.where(d < max_exact, d, large).astype(np.int32)


LOG2E = math.log2(math.e)


def _bias_kernel(tab_ref, idx_ref, o_ref, *, ranges):
    h = pl.program_id(0)
    for t, (lo, hi) in enumerate(ranges):
        idx = idx_ref[t]
        acc = jnp.full(idx.shape, NEG, F32)
        for bkt in range(lo, hi + 1):
            acc = jnp.where(idx == bkt, tab_ref[bkt, h] * LOG2E, acc)
        o_ref[0, t] = acc


def _bias_tiles(rel_bias, idx_np):
    t, r, c = idx_np.shape
    ranges = tuple((int(tile[tile >= 0].min()), int(tile.max())) for tile in idx_np)
    return pl.pallas_call(
        functools.partial(_bias_kernel, ranges=ranges),
        grid=(N_ATTN_HEADS,),
        in_specs=[pl.BlockSpec(memory_space=pltpu.SMEM),
                  pl.BlockSpec((t, r, c), lambda h: (0, 0, 0))],
        out_specs=pl.BlockSpec((1, t, r, c), lambda h: (h, 0, 0, 0)),
        out_shape=jax.ShapeDtypeStruct((N_ATTN_HEADS, t, r, c), F32),
        compiler_params=_cparams(("arbitrary",)),
        name="rel_bias_tiles",
    )(rel_bias, jnp.asarray(idx_np))


def _swa_bucket_idx():
    c = np.arange(2 * SWA_BLOCK)[:, None]
    r = np.arange(SWA_BLOCK)[None, :]
    dist = SWA_BLOCK + r - c
    return np.where((dist >= 0) & (dist < SWA_BLOCK), _rel_bucket_np(dist), -1).astype(np.int32)[None]


def _moba_bucket_idx(nblk):
    c = np.arange(MOBA_BLOCK)[:, None]
    r = np.arange(MOBA_BLOCK)[None, :]
    tiles = [np.where(m * MOBA_BLOCK + r - c >= 0, _rel_bucket_np(m * MOBA_BLOCK + r - c), -1)
             for m in range(nblk)]
    return np.stack(tiles).astype(np.int32)


def _expand_heads(v, e):
    hi = v.astype(BF16)
    lo = (v - hi.astype(F32)).astype(BF16)
    return (jnp.dot(hi, e, preferred_element_type=F32) + jnp.dot(lo, e, preferred_element_type=F32))


def _ssd_kernel(xbc_ref, z_ref, dtc_ref, dtr_ref, bias_c_ref, bias_r_ref, alog_c_ref, alog_r_ref,
                dskip_ref, nw_ref, e_ref, o_ref, state_ref):
    q = SSD_CHUNK
    gw = SSD_INNER // SSD_GROUPS
    hpg = SSD_HEADS // SSD_GROUPS

    @pl.when(pl.program_id(1) == 0)
    def _():
        state_ref[...] = jnp.zeros(state_ref.shape, F32)

    row = lax.broadcasted_iota(jnp.int32, (q, q), 0)
    col = lax.broadcasted_iota(jnp.int32, (q, q), 1)
    tril = row >= col
    tri_b = jnp.where(tril, 1.0, 0.0).astype(BF16)
    triu_b = jnp.where(row <= col, 1.0, 0.0).astype(BF16)

    dt_c = _softplus(dtc_ref[0].astype(F32) + bias_c_ref[...])
    da_c = dt_c * (-jnp.exp(alog_c_ref[...]))
    acs_c = _sum01_left(tri_b, da_c)
    dt_r = _softplus(dtr_ref[0] + bias_r_ref[...])
    da_r = dt_r * (-jnp.exp(alog_r_ref[...]))
    acs_r = _sum01_right(da_r, triu_b)

    acs_last = acs_c[q - 1:q, :]
    e = e_ref[...]
    dt_full = _expand_heads(dt_c, e)
    dtdec_full = _expand_heads(dt_c * jnp.exp(acs_last - acs_c), e)
    eacs_full = _expand_heads(jnp.exp(acs_c), e)
    cdecay_full = eacs_full[q - 1:q, :]

    xbc = xbc_ref[0].astype(F32)
    xs = xbc[:, :SSD_INNER]
    xdt = xs * dt_full
    xdec = xs * dtdec_full
    lane_half = lax.broadcasted_iota(jnp.int32, (1, LANES), 1) >> 6

    y_parts = []
    for g in range(SSD_GROUPS):
        b_g = xbc[:, SSD_INNER + g * SSD_STATE:SSD_INNER + (g + 1) * SSD_STATE]
        c_g = xbc[:, SSD_INNER + SSD_GROUPS * SSD_STATE + g * SSD_STATE:
                  SSD_INNER + SSD_GROUPS * SSD_STATE + (g + 1) * SSD_STATE]
        cb = jnp.where(tril, _bdot_nt(c_g, b_g), 0.0)
        st = state_ref[g]
        y_off = _bdot(c_g, st) * eacs_full[:, g * gw:(g + 1) * gw]
        state_ref[g] = st * cdecay_full[:, g * gw:(g + 1) * gw] + _bdot(b_g.T, xdec[:, g * gw:(g + 1) * gw])
        pair_parts = []
        for pr in range(hpg // 2):
            acc = None
            lo = g * gw + pr * LANES
            x_pair = xdt[:, lo:lo + LANES]
            for half in range(2):
                h = g * hpg + pr * 2 + half
                diff = acs_c[:, h:h + 1] - acs_r[h:h + 1, :]
                lmat = jnp.exp(jnp.minimum(diff, 0.0))
                part = _bdot(cb * lmat, jnp.where(lane_half == half, x_pair, 0.0))
                acc = part if acc is None else acc + part
            pair_parts.append(acc)
        y_diag = jnp.concatenate(pair_parts, axis=1)
        y = y_diag + y_off + dskip_ref[:, g * gw:(g + 1) * gw] * xs[:, g * gw:(g + 1) * gw]
        y = y * _silu(z_ref[0, :, g * gw:(g + 1) * gw].astype(F32))
        ms = jnp.mean(y * y, axis=-1, keepdims=True)
        y_parts.append(y * lax.rsqrt(ms + RMS_EPS) * nw_ref[:, g * gw:(g + 1) * gw])
    o_ref[0] = jnp.concatenate(y_parts, axis=1).astype(o_ref.dtype)


N_SSD_IN = 11


def _layer0_mixers_kernel(*refs):
    ssd_in, swa_in = refs[:N_SSD_IN], refs[N_SSD_IN:N_SSD_IN + 7]
    o_ssd, o_swa, state_ref, s_ref = refs[N_SSD_IN + 7:]
    _ssd_kernel(*ssd_in, o_ssd, state_ref)
    _swa_kernel(*swa_in, o_swa, s_ref)


def _layer0_mixers(xbc_act, proj, dt_rows, dt_bias, a_log, d_skip, norm_w, sinks, swa_bias):
    bsz, s, _ = proj.shape
    q = SSD_CHUNK
    assert SWA_BLOCK == q
    pad = LANES - SSD_HEADS
    bias_c = jnp.pad(dt_bias, (0, pad)).reshape(1, LANES)
    alog_c = jnp.pad(a_log, (0, pad)).reshape(1, LANES)
    e_np = np.zeros((LANES, SSD_INNER), np.float32)
    for h in range(SSD_HEADS):
        e_np[h, h * SSD_HEAD_DIM:(h + 1) * SSD_HEAD_DIM] = 1.0
    small = lambda shape: pl.BlockSpec(shape, lambda b_, c: (0, 0))
    rows = lambda n, col: pl.BlockSpec((1, q, n), lambda b_, c: (b_, c, col))
    kvw = SWA_KV_HEADS * ATTN_HEAD_DIM
    prev = lambda col: pl.BlockSpec((1, q, kvw), lambda b_, c: (b_, jnp.maximum(c - 1, 0), col))
    ssd_specs = [rows(SSD_XBC, 0), rows(SSD_INNER, AB_Z // SSD_INNER), rows(LANES, AB_DT // LANES),
                 pl.BlockSpec((1, LANES, q), lambda b_, c: (b_, 0, c)),
                 small((1, LANES)), small((LANES, 1)), small((1, LANES)), small((LANES, 1)),
                 small((1, SSD_INNER)), small((1, SSD_INNER)), small((LANES, SSD_INNER))]
    assert len(ssd_specs) == N_SSD_IN
    swa_specs = [pl.BlockSpec(memory_space=pltpu.SMEM), rows(ATTN_WIDTH, AB_Q // ATTN_WIDTH),
                 prev(AB_K // kvw), rows(kvw, AB_K // kvw), prev(AB_V // kvw), rows(kvw, AB_V // kvw),
                 pl.BlockSpec((N_ATTN_HEADS, 1, 2 * q, q), lambda b_, c: (0, 0, 0, 0))]
    return pl.pallas_call(
        _layer0_mixers_kernel,
        grid=(bsz, s // q),
        in_specs=ssd_specs + swa_specs,
        out_specs=(rows(SSD_INNER, 0), rows(ATTN_WIDTH, 0)),
        out_shape=(jax.ShapeDtypeStruct((bsz, s, SSD_INNER), BF16),
                   jax.ShapeDtypeStruct((bsz, s, ATTN_WIDTH), BF16)),
        scratch_shapes=[pltpu.VMEM((SSD_GROUPS, SSD_STATE, SSD_INNER // SSD_GROUPS), F32),
                        pltpu.VMEM((N_ATTN_HEADS, 2 * q, q), F32)],
        compiler_params=_cparams(("arbitrary", "arbitrary")),
        name="ssd_swa_mixers",
    )(xbc_act, proj, proj, dt_rows, bias_c, bias_c.reshape(LANES, 1),
      alog_c, alog_c.reshape(LANES, 1), jnp.repeat(d_skip, SSD_HEAD_DIM).reshape(1, SSD_INNER),
      norm_w.reshape(1, SSD_INNER), jnp.asarray(e_np, BF16),
      sinks, proj, proj, proj, proj, proj, swa_bias)


def _swa_kernel(sink_ref, q_ref, kp_ref, kc_ref, vp_ref, vc_ref, bias_ref, o_ref, s_ref):
    blk = SWA_BLOCK
    n = pl.program_id(1)
    grp = N_ATTN_HEADS // SWA_KV_HEADS
    scale = ATTN_HEAD_DIM ** -0.5
    kk = jnp.concatenate([kp_ref[0], kc_ref[0]], axis=0).astype(F32) * (scale * LOG2E)
    vv_t = jnp.concatenate([vp_ref[0], vc_ref[0]], axis=0).astype(F32).T
    lane_half = lax.broadcasted_iota(jnp.int32, (1, LANES), 1) >> 6
    c = lax.broadcasted_iota(jnp.int32, (2 * blk, blk), 0)
    valid = (c >= blk) | (n > 0)
    ms = []
    for kv in range(SWA_KV_HEADS):
        k_own = jnp.where(lane_half == kv, kk, 0.0).astype(BF16)
        k_var = [None, None]
        k_var[kv] = k_own
        k_var[1 - kv] = pltpu.roll(jnp.where(lane_half == kv, kk, 0.0), ATTN_HEAD_DIM, axis=1).astype(BF16)
        for gq in range(grp):
            h = kv * grp + gq
            q_tile = q_ref[0, :, (h // 2) * LANES:(h // 2 + 1) * LANES]
            s_t = _bdot_nt(k_var[h % 2], q_tile) + bias_ref[h, 0]
            s_t = jnp.where(valid, s_t, NEG)
            s_ref[h] = s_t
            ms.append(jnp.maximum(jnp.max(s_t, axis=0, keepdims=True), sink_ref[h] * LOG2E))
    outs = []
    for h in range(N_ATTN_HEADS):
        kv = h // grp
        p = jnp.exp2(s_ref[h] - ms[h])
        l = jnp.sum(p, axis=0, keepdims=True) + jnp.exp2(sink_ref[h] * LOG2E - ms[h])
        outs.append(_bdot(vv_t[kv * ATTN_HEAD_DIM:(kv + 1) * ATTN_HEAD_DIM, :], p) / l)
    for t in range(N_ATTN_HEADS // 2):
        pair = jnp.concatenate([outs[2 * t], outs[2 * t + 1]], axis=0)
        o_ref[0, :, t * LANES:(t + 1) * LANES] = pair.T.astype(o_ref.dtype)


def _moba_kernel(q_ref, k_ref, v_ref, bias_ref, o_ref, vt_ref, qm_ref, negrow_ref, s_ref, *, nblk):
    mb = MOBA_BLOCK
    dh = ATTN_HEAD_DIM
    own0 = 2 * lax.rem(pl.program_id(1), nblk // 2)
    scale = dh ** -0.5

    @pl.when(own0 == 0)
    def _():
        means = []
        for j in range(nblk):
            vt_ref[j] = v_ref[0, j * mb:(j + 1) * mb, :].astype(F32).T.astype(BF16)
            means.append(jnp.mean(k_ref[0, j * mb:(j + 1) * mb, :].astype(F32), axis=0, keepdims=True))
        kmean = jnp.concatenate(means, axis=0)
        lane_half = lax.broadcasted_iota(jnp.int32, (1, LANES), 1) >> 6
        blk_id = lax.broadcasted_iota(jnp.int32, (nblk, mb), 0)
        for qb in range(nblk):
            q = q_ref[0, qb * mb:(qb + 1) * mb, :].astype(F32) * (scale * LOG2E)
            for hh in range(2):
                qm = jnp.where(lane_half == hh, q, 0.0)
                qm_ref[hh, qb] = qm.astype(BF16)
                gate = lax.dot_general(kmean, qm, (((1,), (1,)), ((), ())),
                                       preferred_element_type=F32, precision=HIGHEST)
                gate = jnp.where(blk_id < qb, gate, NEG)
                rank = jnp.zeros((nblk, mb), jnp.int32)
                for i in range(nblk):
                    gi = gate[i:i + 1, :]
                    ahead = (gi > gate) | ((gi == gate) & (i < blk_id))
                    rank = rank + jnp.where(ahead, 1, 0)
                keep = ((rank < MOBA_TOPK) & (blk_id < qb)) | (blk_id == qb)
                negrow_ref[hh, qb] = jnp.where(keep, 0.0, NEG).astype(F32)

    def attend(nb, qi):
        own = own0 + qi
        qms = [qm_ref[hh, own] for hh in range(2)]
        negrows = [negrow_ref[hh, own] for hh in range(2)]
        m8 = [None, None]
        l8 = [None, None]
        acc = [None, None]

        def scores(hh, j):
            s_t = _bdot_nt(k_ref[0, j * mb:(j + 1) * mb, :], qms[hh]) + bias_ref[hh, jnp.maximum(own - j, 0)]
            s_t = s_t + negrows[hh][j:j + 1, :]
            s_ref[qi, hh, j] = s_t
            m_j = jnp.max(s_t.reshape(mb // SUBLANES, SUBLANES, mb), axis=0)
            m8[hh] = m_j if m8[hh] is None else jnp.maximum(m8[hh], m_j)

        def weights(hh, j, m):
            p = jnp.exp2(s_ref[qi, hh, j] - m)
            l_j = jnp.sum(p.reshape(mb // SUBLANES, SUBLANES, mb), axis=0)
            a_j = _bdot(vt_ref[j, hh * dh:(hh + 1) * dh, :], p)
            l8[hh] = l_j if l8[hh] is None else l8[hh] + l_j
            acc[hh] = a_j if acc[hh] is None else acc[hh] + a_j

        for j in range(nb):
            for hh in range(2):
                scores(hh, j)
        m = [jnp.max(m8[hh], axis=0, keepdims=True) for hh in range(2)]
        for j in range(nb):
            for hh in range(2):
                weights(hh, j, m[hh])
        out_t = jnp.concatenate([acc[hh] / jnp.sum(l8[hh], axis=0, keepdims=True) for hh in range(2)], axis=0)
        o_ref[0, qi * mb:(qi + 1) * mb, :] = out_t.T.astype(o_ref.dtype)

    def attend_both(nb):
        attend(nb, 0)
        attend(nb, 1)

    for nb in range(2, nblk + 1, 2):
        pl.when(own0 == nb - 2)(functools.partial(attend_both, nb))


def _moba_specs(bsz, s):
    mb = MOBA_BLOCK
    nblk = s // mb
    half = nblk // 2
    col = lambda c0: (lambda b_, c: (b_, 0, c0 // LANES + c // half))
    in_specs = [pl.BlockSpec((1, s, LANES), col(CD_QD)),
                pl.BlockSpec((1, s, LANES), col(CD_KD)),
                pl.BlockSpec((1, s, LANES), col(CD_VD)),
                pl.BlockSpec((2, nblk, mb, mb), lambda b_, c: (c // half, 0, 0, 0))]
    out_spec = pl.BlockSpec((1, 2 * mb, LANES), lambda b_, c: (b_, c % half, c // half))
    out_shape = jax.ShapeDtypeStruct((bsz, s, ATTN_WIDTH), BF16)
    scratch = [pltpu.VMEM((nblk, LANES, mb), BF16),
               pltpu.VMEM((2, nblk, mb, LANES), BF16),
               pltpu.VMEM((2, nblk, nblk, mb), F32),
               pltpu.VMEM((2, 2, nblk, mb, mb), F32)]
    return in_specs, out_spec, out_shape, scratch


def _gdn_kernel(qkv_ref, z_ref, bac_ref, bar_ref, bias_c_ref, bias_r_ref, alog_c_ref, alog_r_ref,
                nw_ref, o_ref, state_ref, m_ref, attn_ref, t_ref, x_ref, rhs_ref, u_ref, wq_ref,
                kdt_ref, r_ref, glc_ref):
    t = 2 * GDN_CHUNK
    ck = GDN_CHUNK
    dk = GDN_HEAD_DIM
    nh = GDN_HEADS

    @pl.when(pl.program_id(1) == 0)
    def _():
        state_ref[...] = jnp.zeros(state_ref.shape, F32)

    row = lax.broadcasted_iota(jnp.int32, (t, t), 0)
    col = lax.broadcasted_iota(jnp.int32, (t, t), 1)
    same = (row >> 6) == (col >> 6)
    tril = same & (row >= col)
    strict = same & (row > col)
    tri_b = jnp.where(tril, 1.0, 0.0).astype(BF16)
    triu_b = jnp.where(same & (row <= col), 1.0, 0.0).astype(BF16)
    blk_b = jnp.where(same, 1.0, 0.0).astype(BF16)
    eye_f = jnp.where(row == col, 1.0, 0.0).astype(F32)
    merge_masks = [((row >> (l + 1)) == (col >> (l + 1))) & (((row >> l) & 1) == 1) & (((col >> l) & 1) == 0)
                   for l in range(int(math.log2(ck)))]
    mask_bf = [jnp.where(m, 1.0, 0.0).astype(BF16) for m in merge_masks[1:]]

    ba_c = bac_ref[0].astype(F32)
    g_c = -jnp.exp(alog_c_ref[...]) * _softplus(ba_c + bias_c_ref[...])
    sums_c = _sum01_left(jnp.concatenate([tri_b, blk_b], axis=0), g_c)
    gc_c = sums_c[:t]
    gl_c = sums_c[t:]
    g_r = -jnp.exp(alog_r_ref[...]) * _softplus(bar_ref[0] + bias_r_ref[...])
    gc_r = _sum01_right(g_r, triu_b)
    glc_ref[...] = gl_c

    def prep(h):
        q = qkv_ref[0, :, h * dk:(h + 1) * dk].astype(F32)
        k = qkv_ref[0, :, (nh + h) * dk:(nh + h + 1) * dk].astype(F32)
        v = qkv_ref[0, :, (2 * nh + h) * dk:(2 * nh + h + 1) * dk].astype(F32)
        qn = q * lax.rsqrt(jnp.sum(q * q, axis=-1, keepdims=True) + 1e-6) * (dk ** -0.5)
        kn = k * lax.rsqrt(jnp.sum(k * k, axis=-1, keepdims=True) + 1e-6)
        beta = _sigmoid(ba_c[:, h:h + 1])
        gcc = gc_c[:, nh + h:nh + h + 1]
        gcr = gc_r[nh + h:nh + h + 1, :]
        glc = gl_c[:, nh + h:nh + h + 1]
        decay = jnp.where(tril, jnp.exp(jnp.where(tril, gcc - gcr, 0.0)), 0.0)
        kb = kn * beta
        kk = _bdot_nt(jnp.concatenate([kb, qn], axis=0), kn)
        mm = jnp.where(strict, kk[:t] * decay, 0.0)
        m_ref[h] = mm.astype(BF16)
        attn_ref[h] = (kk[t:] * decay).astype(BF16)
        t_ref[h] = eye_f - jnp.where(merge_masks[0], mm, 0.0)
        egc = jnp.exp(gcc)
        rhs_ref[h] = jnp.concatenate([v * beta, kb * egc], axis=1).astype(BF16)
        q_dec = (qn * egc).astype(BF16)
        for a in range(2):
            wq_ref[h, a, ck:, :] = q_dec[a * ck:(a + 1) * ck]
        kdt_ref[h] = (kn * jnp.exp(glc - gcc)).T.astype(BF16)

    def merge_a(lvl, h):
        x_ref[h] = jnp.dot(t_ref[h].astype(BF16), m_ref[h] * mask_bf[lvl],
                           preferred_element_type=F32).astype(BF16)

    def merge_b(lvl, h):
        t_h = t_ref[h]
        t_ref[h] = t_h - jnp.dot(x_ref[h], t_h.astype(BF16), preferred_element_type=F32)

    def solve(h):
        sol = jnp.dot(t_ref[h].astype(BF16), rhs_ref[h], preferred_element_type=F32)
        u_ref[h] = sol[:, :dk]
        for a in range(2):
            wq_ref[h, a, :ck, :] = sol[a * ck:(a + 1) * ck, dk:].astype(BF16)

    zeros_half = jnp.zeros((ck, dk), F32)

    def read_state(a, h):
        r_ref[h] = jnp.dot(wq_ref[h, a], state_ref[h].astype(BF16), preferred_element_type=F32)

    def update(a, h):
        sl = slice(a * ck, (a + 1) * ck)
        v_new = u_ref[h, sl, :] - r_ref[h, :ck, :]
        v_full = jnp.concatenate([v_new, zeros_half] if a == 0 else [zeros_half, v_new], axis=0).astype(BF16)
        o = r_ref[h, ck:, :] + jnp.dot(attn_ref[h, sl, :], v_full, preferred_element_type=F32)
        gl = glc_ref[a * ck:a * ck + 1, nh + h:nh + h + 1]
        state_ref[h] = state_ref[h] * jnp.exp(gl) + jnp.dot(kdt_ref[h], v_full, preferred_element_type=F32)
        ms = jnp.mean(o * o, axis=-1, keepdims=True)
        y = o * lax.rsqrt(ms + RMS_EPS) * nw_ref[...] * _silu(z_ref[0, sl, h * dk:(h + 1) * dk].astype(F32))
        o_ref[0, sl, h * dk:(h + 1) * dk] = y.astype(o_ref.dtype)

    def steps(phase, heads):
        if phase == "prep":
            return [functools.partial(prep, h) for h in heads]
        if phase == "merge":
            return [functools.partial(fn, lvl, h) for lvl in range(len(mask_bf))
                    for fn in (merge_a, merge_b) for h in heads]
        return ([functools.partial(solve, h) for h in heads]
                + [functools.partial(fn, a, h) for a in range(2) for fn in (read_state, update) for h in heads])

    for phase in ("prep", "merge", "tail"):
        for step in steps(phase, range(nh)):
            step()


N_GDN_IN, N_GDN_SCRATCH, N_MOBA_IN = 9, 11, 4


def _layer1_mixers_kernel(*refs, nblk):
    gdn_in = refs[:N_GDN_IN]
    moba_in = refs[N_GDN_IN:N_GDN_IN + N_MOBA_IN]
    o_gdn, o_moba = refs[N_GDN_IN + N_MOBA_IN:N_GDN_IN + N_MOBA_IN + 2]
    scratch = refs[N_GDN_IN + N_MOBA_IN + 2:]
    _gdn_kernel(*gdn_in, o_gdn, *scratch[:N_GDN_SCRATCH])
    _moba_kernel(*moba_in, o_moba, *scratch[N_GDN_SCRATCH:], nblk=nblk)


def _layer1_mixers(qkv_act, proj, ba_rows, dt_bias, a_log, norm_w, moba_bias):
    bsz, s, _ = proj.shape
    t = 2 * GDN_CHUNK
    nh = GDN_HEADS
    dk = GDN_HEAD_DIM
    nblk = s // MOBA_BLOCK
    assert s // t == (N_ATTN_HEADS // 2) * (nblk // 2)
    bias_c = jnp.pad(dt_bias, (nh, LANES - 2 * nh)).reshape(1, LANES)
    alog_c = jnp.pad(a_log, (nh, LANES - 2 * nh)).reshape(1, LANES)
    small = lambda shape: pl.BlockSpec(shape, lambda b_, c: (0, 0))
    moba_in, moba_out, moba_shape, moba_scratch = _moba_specs(bsz, s)
    gdn_in = [pl.BlockSpec((1, t, 3 * GDN_INNER), lambda b_, c: (b_, c, 0)),
              pl.BlockSpec((1, t, GDN_INNER), lambda b_, c: (b_, c, CD_Z // GDN_INNER)),
              pl.BlockSpec((1, t, LANES), lambda b_, c: (b_, c, CD_BA // LANES)),
              pl.BlockSpec((1, LANES, t), lambda b_, c: (b_, 0, c)),
              small((1, LANES)), small((LANES, 1)), small((1, LANES)), small((LANES, 1)),
              small((1, dk))]
    assert len(gdn_in) == N_GDN_IN and len(moba_in) == N_MOBA_IN
    return pl.pallas_call(
        functools.partial(_layer1_mixers_kernel, nblk=nblk),
        grid=(bsz, s // t),
        in_specs=gdn_in + moba_in,
        out_specs=(pl.BlockSpec((1, t, GDN_INNER), lambda b_, c: (b_, c, 0)), moba_out),
        out_shape=(jax.ShapeDtypeStruct((bsz, s, GDN_INNER), BF16), moba_shape),
        scratch_shapes=[pltpu.VMEM((nh, dk, dk), F32),
                        pltpu.VMEM((nh, t, t), BF16),
                        pltpu.VMEM((nh, t, t), BF16),
                        pltpu.VMEM((nh, t, t), F32),
                        pltpu.VMEM((nh, t, t), BF16),
                        pltpu.VMEM((nh, t, 2 * dk), BF16),
                        pltpu.VMEM((nh, t, dk), F32),
                        pltpu.VMEM((nh, 2, t, dk), BF16),
                        pltpu.VMEM((nh, dk, t), BF16),
                        pltpu.VMEM((nh, t, dk), F32),
                        pltpu.VMEM((t, LANES), F32)]
        + moba_scratch,
        compiler_params=_cparams(("arbitrary", "arbitrary")),
        name="gdn_moba_mixers",
    )(qkv_act, proj, proj, ba_rows, bias_c, bias_c.reshape(LANES, 1), alog_c, alog_c.reshape(LANES, 1),
      norm_w.reshape(1, dk), proj, proj, proj, moba_bias)


def _reorder_kernel(w_ref, o_ref, *, segments, zero_from):
    rows, n = o_ref.shape
    if zero_from < n:
        o_ref[:, zero_from:] = jnp.zeros((rows, n - zero_from), o_ref.dtype)
    for lo, hi, dst in segments:
        o_ref[:, dst:dst + hi - lo] = w_ref[:, lo:hi].astype(o_ref.dtype)


def _reorder_cast(w_stack, layer, segments, zero_from, n_out, tr=256):
    nl, d, n = w_stack.shape
    w_rows = w_stack.reshape(nl * d, n)
    return pl.pallas_call(
        functools.partial(_reorder_kernel, segments=segments, zero_from=zero_from),
        grid=(d // tr,),
        in_specs=[pl.BlockSpec((tr, n), lambda r: (layer * (d // tr) + r, 0))],
        out_specs=pl.BlockSpec((tr, n_out), lambda r: (r, 0)),
        out_shape=jax.ShapeDtypeStruct((d, n_out), BF16),
        compiler_params=_cparams(("arbitrary",)),
        name="reorder_cast_weights",
    )(w_rows)


def _gate_rows(proj, col0):
    return jnp.swapaxes(proj[:, :, col0:col0 + LANES].astype(F32), 1, 2)


def kernel(x, c, rel_bias, norm_w, ada_w, ada_b, ab_w_in, ab_w_out, ssd_conv_w, ssd_conv_b,
           ssd_dt_bias, ssd_a_log, ssd_d, ssd_norm_w, swa_sinks, cd_w_in, cd_w_out, gdn_conv_w,
           gdn_dt_bias, gdn_a_log, gdn_norm_w, ffn_w_up, ffn_conv_w, ffn_conv_b, ffn_w_down):
    bsz, s, d = x.shape
    depth = norm_w.shape[0]
    mods = _mods(c, ada_w, ada_b)
    swa_bias = _bias_tiles(rel_bias, _swa_bucket_idx())
    moba_bias = _bias_tiles(rel_bias, _moba_bucket_idx(s // MOBA_BLOCK))

    for i in range(depth):
        sh_m, sc_m, g_m, sh_f, sc_f, g_f = [m.reshape(bsz, 1, d) for m in jnp.split(mods[i], 6, axis=-1)]
        j = i // 2
        if i % 2 == 0:
            dt0 = SSD_INNER + SSD_XBC
            n_in = ab_w_in.shape[-1]
            w_in = _reorder_cast(ab_w_in, j, ((SSD_INNER, dt0, 0), (0, SSD_INNER, SSD_XBC + AB_Z),
                                              (dt0 + SSD_HEADS, n_in, SSD_XBC + AB_Q), (dt0, dt0 + SSD_HEADS, SSD_XBC + AB_DT)),
                                 SSD_XBC + AB_DT, SSD_XBC + AB_COLS)
            xbc_act, proj = _proj_conv_act(x, norm_w[i, 0], sc_m, sh_m, w_in, ssd_conv_w[j], ssd_conv_b[j],
                                           "silu", n_plain=AB_COLS)
            y_a, y_b = _layer0_mixers(xbc_act, proj, _gate_rows(proj, AB_DT), ssd_dt_bias[j], ssd_a_log[j],
                                      ssd_d[j], ssd_norm_w[j], swa_sinks[j], swa_bias)
            x = _matmul_resid([y_a, y_b], ab_w_out, j, x, g_m, norm_w[i, 1])
        else:
            nqkv = 3 * GDN_INNER
            ba0 = nqkv + GDN_INNER
            n_in = cd_w_in.shape[-1]
            w_in = _reorder_cast(cd_w_in, j, ((0, ba0, 0), (ba0 + 2 * GDN_HEADS, n_in, nqkv + CD_QD),
                                              (ba0, ba0 + 2 * GDN_HEADS, nqkv + CD_BA)),
                                 nqkv + CD_BA, nqkv + CD_COLS)
            qkv_act, proj = _proj_conv_act(x, norm_w[i, 0], sc_m, sh_m, w_in, gdn_conv_w[j],
                                           jnp.zeros((nqkv,), F32), "silu", n_plain=CD_COLS)
            y_c, y_d = _layer1_mixers(qkv_act, proj, _gate_rows(proj, CD_BA), gdn_dt_bias[j], gdn_a_log[j],
                                      gdn_norm_w[j], moba_bias)
            x = _matmul_resid([y_c, y_d], cd_w_out, j, x, g_m, norm_w[i, 1])
        n_up = ffn_w_up.shape[-1]
        w_up = _reorder_cast(ffn_w_up, i, ((0, n_up, 0),), n_up, n_up)
        act = _proj_conv_act(x, norm_w[i, 2], sc_f, sh_f, w_up, ffn_conv_w[i], ffn_conv_b[i], "geglu")
        x = _matmul_resid([act], ffn_w_down, i, x, g_f, norm_w[i, 3])
    return x
```

```python
import functools
import math

import numpy as np
import jax
import jax.numpy as jnp
from jax import lax
from jax.experimental import pallas as pl
from jax.experimental.pallas import tpu as pltpu

F32 = jnp.float32
BF16 = jnp.bfloat16
HIGHEST = lax.Precision.HIGHEST

RMS_EPS = 1e-6
NEG = -1e30
LANES = 128
SUBLANES = 8
N_ATTN_HEADS = 8
ATTN_HEAD_DIM = 64
ATTN_WIDTH = N_ATTN_HEADS * ATTN_HEAD_DIM
REL_BUCKETS = 32
REL_MAX_DIST = 1024
SSD_HEADS = 24
SSD_HEAD_DIM = 64
SSD_INNER = SSD_HEADS * SSD_HEAD_DIM
SSD_GROUPS = 4
SSD_STATE = 128
SSD_CHUNK = 128
SSD_XBC = SSD_INNER + 2 * SSD_GROUPS * SSD_STATE
SWA_KV_HEADS = 2
SWA_BLOCK = 128
GDN_HEADS = 12
GDN_HEAD_DIM = 128
GDN_INNER = GDN_HEADS * GDN_HEAD_DIM
GDN_CHUNK = 64
MOBA_BLOCK = 256
MOBA_TOPK = 3

AB_Z, AB_Q, AB_K, AB_V, AB_DT, AB_COLS = 0, 1536, 2048, 2176, 2304, 2560
CD_Z, CD_QD, CD_KD, CD_VD, CD_BA, CD_COLS = 0, 1536, 2048, 2560, 3072, 3328

VMEM_LIMIT = 48 * 1024 * 1024
HALO = 16


def _cparams(sem):
    return pltpu.CompilerParams(dimension_semantics=sem, vmem_limit_bytes=VMEM_LIMIT)


def _bdot(a, b):
    return jnp.dot(a.astype(BF16), b.astype(BF16), preferred_element_type=F32)


def _bdot_nt(a, b):
    return lax.dot_general(a.astype(BF16), b.astype(BF16), (((1,), (1,)), ((), ())),
                           preferred_element_type=F32)


def _split3(x):
    hi = x.astype(BF16)
    r1 = x - hi.astype(F32)
    mid = r1.astype(BF16)
    lo = (r1 - mid.astype(F32)).astype(BF16)
    return hi, mid, lo


def _sum01_left(m01, x):
    n = x.shape[1]
    y = jnp.dot(m01, jnp.concatenate(_split3(x), axis=1), preferred_element_type=F32)
    return y[:, :n] + y[:, n:2 * n] + y[:, 2 * n:]


def _sum01_right(x, m01):
    n = x.shape[0]
    y = jnp.dot(jnp.concatenate(_split3(x), axis=0), m01, preferred_element_type=F32)
    return y[:n] + y[n:2 * n] + y[2 * n:]


def _softplus(x):
    return jnp.maximum(x, 0.0) + jnp.log(1.0 + jnp.exp(-jnp.abs(x)))


def _sigmoid(x):
    return 1.0 / (1.0 + jnp.exp(-x))


def _silu(x):
    return x * _sigmoid(x)


def _mods_kernel(c_ref, w_ref, b_ref, o_ref):
    o_ref[0] = _bdot(_silu(c_ref[...]), w_ref[0]) + b_ref[0]


def _mods(c, ada_w, ada_b):
    depth, d, n = ada_w.shape
    bsz = c.shape[0]
    tn = 512
    return pl.pallas_call(
        _mods_kernel,
        grid=(depth, n // tn),
        in_specs=[pl.BlockSpec((bsz, d), lambda l, j: (0, 0)),
                  pl.BlockSpec((1, d, tn), lambda l, j: (l, 0, j)),
                  pl.BlockSpec((1, 1, tn), lambda l, j: (l, 0, j))],
        out_specs=pl.BlockSpec((1, bsz, tn), lambda l, j: (l, 0, j)),
        out_shape=jax.ShapeDtypeStruct((depth, bsz, n), F32),
        compiler_params=_cparams(("arbitrary", "arbitrary")),
        name="adaln_mods",
    )(c, ada_w, ada_b.reshape(depth, 1, n))


def _modulated_norm(x, nw, sc, sh):
    ms = jnp.mean(x * x, axis=-1, keepdims=True)
    return x * lax.rsqrt(ms + RMS_EPS) * nw * (1.0 + sc) + sh


def _mmres_kernel(*refs, splits):
    na = len(splits)
    a_refs = refs[:na]
    w_ref, x_ref, g_ref, nw_ref, o_ref = refs[na:]
    acc = None
    lo = 0
    for a_ref, k in zip(a_refs, splits):
        part = jnp.dot(a_ref[0].astype(BF16), w_ref[lo:lo + k, :].astype(BF16), preferred_element_type=F32)
        acc = part if acc is None else acc + part
        lo += k
    ms = jnp.mean(acc * acc, axis=-1, keepdims=True)
    y = acc * lax.rsqrt(ms + RMS_EPS) * nw_ref[...]
    o_ref[0] = x_ref[0] + g_ref[0] * y


def _matmul_resid(a_list, w_stack, layer, x, gate, nw, tm=1024):
    bsz, s, d = x.shape
    spt = s // tm
    splits = tuple(a.shape[-1] for a in a_list)
    ktot = sum(splits)
    in_specs = [pl.BlockSpec((1, tm, k), lambda i: (i // spt, i % spt, 0)) for k in splits]
    in_specs += [pl.BlockSpec((None, ktot, d), lambda i: (layer, 0, 0)),
                 pl.BlockSpec((1, tm, d), lambda i: (i // spt, i % spt, 0)),
                 pl.BlockSpec((1, 1, d), lambda i: (i // spt, 0, 0)),
                 pl.BlockSpec((1, d), lambda i: (0, 0))]
    return pl.pallas_call(
        functools.partial(_mmres_kernel, splits=splits),
        grid=(bsz * spt,),
        in_specs=in_specs,
        out_specs=pl.BlockSpec((1, tm, d), lambda i: (i // spt, i % spt, 0)),
        out_shape=jax.ShapeDtypeStruct((bsz, s, d), F32),
        compiler_params=_cparams(("arbitrary",)),
        name="matmul_resid",
    )(*a_list, w_stack, x, gate, nw.reshape(1, d))


def _proj_conv_kernel(*refs, offs, tc, width, rb, act, n_plain):
    if n_plain:
        xh_ref, x_ref, nw_ref, sc_ref, sh_ref, w_ref, cw_ref, cb_ref, o_ref, op_ref, h_ref, u_ref = refs
    else:
        xh_ref, x_ref, nw_ref, sc_ref, sh_ref, w_ref, cw_ref, cb_ref, o_ref, h_ref, u_ref = refs
    tm = x_ref.shape[1]
    nchunk = o_ref.shape[2] // tc
    nplain = n_plain // tc
    plain0 = w_ref.shape[1] - n_plain
    h_halo = _modulated_norm(xh_ref[0], nw_ref[...], sc_ref[0], sh_ref[0])
    h_ref[:HALO, :] = jnp.where(pl.program_id(1) > 0, h_halo, 0.0).astype(BF16)
    h_ref[HALO:, :] = _modulated_norm(x_ref[0], nw_ref[...], sc_ref[0], sh_ref[0]).astype(BF16)
    c0 = math.sqrt(2.0 / math.pi)

    nrow = HALO + tm

    def matmuls(c):
        for k, off in enumerate(offs):
            lo = off + c * tc
            u = jnp.dot(h_ref[...], w_ref[:, lo:lo + tc], preferred_element_type=F32)
            for s in range(width):
                u_ref[c % 2, k, s, pl.ds(s, nrow), :] = u

    def epilogue(c):
        for r0 in range(0, tm, rb):
            conv = []
            for k, off in enumerate(offs):
                lo = off + c * tc
                acc = cb_ref[:, lo:lo + tc]
                for s in range(width):
                    acc = acc + (u_ref[c % 2, k, s, pl.ds(HALO + r0, rb), :]
                                 * cw_ref[width - 1 - s:width - s, lo:lo + tc])
                conv.append(acc)
            if act == "geglu":
                g, v = conv
                th = jnp.tanh(g * (c0 + (c0 * 0.044715) * (g * g)))
                hg = 0.5 * g
                out = (hg + hg * th) * v
            else:
                out = _silu(conv[0])
            o_ref[0, r0:r0 + rb, c * tc:(c + 1) * tc] = out.astype(o_ref.dtype)

    def plain(p):
        lo = plain0 + p * tc
        op_ref[0, :, p * tc:(p + 1) * tc] = jnp.dot(
            h_ref[HALO:, :], w_ref[:, lo:lo + tc], preferred_element_type=F32).astype(op_ref.dtype)

    matmuls(0)
    done = 0
    for c in range(nchunk):
        if c + 1 < nchunk:
            matmuls(c + 1)
        epilogue(c)
        upto = (c + 1) * nplain // nchunk
        for p in range(done, upto):
            plain(p)
        done = upto


def _proj_conv_act(x, nw, sc, sh, w, cw, cb, act, n_plain=0, tm=512, tc=256, rb=128):
    bsz, s, d = x.shape
    n2 = w.shape[-1] - n_plain
    f = n2 // 2 if act == "geglu" else n2
    offs = (0, f) if act == "geglu" else (0,)
    width = cw.shape[0]
    hb = tm // HALO
    const = lambda shape: pl.BlockSpec(shape, lambda b_, r: (0, 0))
    rows = lambda n: pl.BlockSpec((1, tm, n), lambda b_, r: (b_, r, 0))
    in_specs = [pl.BlockSpec((1, HALO, d), lambda b_, r: (b_, jnp.maximum(r * hb - 1, 0), 0)),
                rows(d), const((1, d)),
                pl.BlockSpec((1, 1, d), lambda b_, r: (b_, 0, 0)),
                pl.BlockSpec((1, 1, d), lambda b_, r: (b_, 0, 0)),
                const((d, n2 + n_plain)),
                const((width, n2)), const((1, n2))]
    args = [x, x, nw.reshape(1, d), sc, sh, w, cw, cb.reshape(1, n2)]
    out_specs = rows(f)
    out_shape = jax.ShapeDtypeStruct((bsz, s, f), BF16)
    if n_plain:
        out_specs = (out_specs, rows(n_plain))
        out_shape = (out_shape, jax.ShapeDtypeStruct((bsz, s, n_plain), BF16))
    return pl.pallas_call(
        functools.partial(_proj_conv_kernel, offs=offs, tc=tc, width=width, rb=rb, act=act, n_plain=n_plain),
        grid=(bsz, s // tm),
        in_specs=in_specs,
        out_specs=out_specs,
        out_shape=out_shape,
        scratch_shapes=[pltpu.VMEM((HALO + tm, d), BF16),
                        pltpu.VMEM((2, len(offs), width, HALO + tm + SUBLANES, tc), F32)],
        compiler_params=_cparams(("arbitrary", "arbitrary")),
        name="proj_conv_" + act,
    )(*args)


def _rel_bucket_np(d):
    max_exact = REL_BUCKETS // 2
    d = np.maximum(d, 0)
    df = np.maximum(d, 1).astype(np.float64)
    large = max_exact + (np.log(df / max_exact) / math.log(REL_MAX_DIST / max_exact)
                         * (REL_BUCKETS - max_exact)).astype(np.int32)
    large = np.minimum(large, REL_BUCKETS - 1)
    return np.where(d < max_exact, d, large).astype(np.int32)


LOG2E = math.log2(math.e)


def _bias_kernel(tab_ref, idx_ref, o_ref, *, ranges):
    h = pl.program_id(0)
    for t, (lo, hi) in enumerate(ranges):
        idx = idx_ref[t]
        acc = jnp.full(idx.shape, NEG, F32)
        for bkt in range(lo, hi + 1):
            acc = jnp.where(idx == bkt, tab_ref[bkt, h] * LOG2E, acc)
        o_ref[0, t] = acc


def _bias_tiles(rel_bias, idx_np):
    t, r, c = idx_np.shape
    ranges = tuple((int(tile[tile >= 0].min()), int(tile.max())) for tile in idx_np)
    return pl.pallas_call(
        functools.partial(_bias_kernel, ranges=ranges),
        grid=(N_ATTN_HEADS,),
        in_specs=[pl.BlockSpec(memory_space=pltpu.SMEM),
                  pl.BlockSpec((t, r, c), lambda h: (0, 0, 0))],
        out_specs=pl.BlockSpec((1, t, r, c), lambda h: (h, 0, 0, 0)),
        out_shape=jax.ShapeDtypeStruct((N_ATTN_HEADS, t, r, c), F32),
        compiler_params=_cparams(("arbitrary",)),
        name="rel_bias_tiles",
    )(rel_bias, jnp.asarray(idx_np))


def _swa_bucket_idx():
    c = np.arange(2 * SWA_BLOCK)[:, None]
    r = np.arange(SWA_BLOCK)[None, :]
    dist = SWA_BLOCK + r - c
    return np.where((dist >= 0) & (dist < SWA_BLOCK), _rel_bucket_np(dist), -1).astype(np.int32)[None]


def _moba_bucket_idx(nblk):
    c = np.arange(MOBA_BLOCK)[:, None]
    r = np.arange(MOBA_BLOCK)[None, :]
    tiles = [np.where(m * MOBA_BLOCK + r - c >= 0, _rel_bucket_np(m * MOBA_BLOCK + r - c), -1)
             for m in range(nblk)]
    return np.stack(tiles).astype(np.int32)


def _expand_heads(v, e):
    hi = v.astype(BF16)
    lo = (v - hi.astype(F32)).astype(BF16)
    return (jnp.dot(hi, e, preferred_element_type=F32) + jnp.dot(lo, e, preferred_element_type=F32))


def _ssd_kernel(xbc_ref, z_ref, dtc_ref, dtr_ref, bias_c_ref, bias_r_ref, alog_c_ref, alog_r_ref,
                dskip_ref, nw_ref, e_ref, o_ref, state_ref):
    q = SSD_CHUNK
    gw = SSD_INNER // SSD_GROUPS
    hpg = SSD_HEADS // SSD_GROUPS

    @pl.when(pl.program_id(1) == 0)
    def _():
        state_ref[...] = jnp.zeros(state_ref.shape, F32)

    row = lax.broadcasted_iota(jnp.int32, (q, q), 0)
    col = lax.broadcasted_iota(jnp.int32, (q, q), 1)
    tril = row >= col
    tri_b = jnp.where(tril, 1.0, 0.0).astype(BF16)
    triu_b = jnp.where(row <= col, 1.0, 0.0).astype(BF16)

    dt_c = _softplus(dtc_ref[0].astype(F32) + bias_c_ref[...])
    da_c = dt_c * (-jnp.exp(alog_c_ref[...]))
    acs_c = _sum01_left(tri_b, da_c)
    dt_r = _softplus(dtr_ref[0] + bias_r_ref[...])
    da_r = dt_r * (-jnp.exp(alog_r_ref[...]))
    acs_r = _sum01_right(da_r, triu_b)

    acs_last = acs_c[q - 1:q, :]
    e = e_ref[...]
    dt_full = _expand_heads(dt_c, e)
    dtdec_full = _expand_heads(dt_c * jnp.exp(acs_last - acs_c), e)
    eacs_full = _expand_heads(jnp.exp(acs_c), e)
    cdecay_full = eacs_full[q - 1:q, :]

    xbc = xbc_ref[0].astype(F32)
    xs = xbc[:, :SSD_INNER]
    xdt = xs * dt_full
    xdec = xs * dtdec_full
    lane_half = lax.broadcasted_iota(jnp.int32, (1, LANES), 1) >> 6

    y_parts = []
    for g in range(SSD_GROUPS):
        b_g = xbc[:, SSD_INNER + g * SSD_STATE:SSD_INNER + (g + 1) * SSD_STATE]
        c_g = xbc[:, SSD_INNER + SSD_GROUPS * SSD_STATE + g * SSD_STATE:
                  SSD_INNER + SSD_GROUPS * SSD_STATE + (g + 1) * SSD_STATE]
        cb = jnp.where(tril, _bdot_nt(c_g, b_g), 0.0)
        st = state_ref[g]
        y_off = _bdot(c_g, st) * eacs_full[:, g * gw:(g + 1) * gw]
        state_ref[g] = st * cdecay_full[:, g * gw:(g + 1) * gw] + _bdot(b_g.T, xdec[:, g * gw:(g + 1) * gw])
        pair_parts = []
        for pr in range(hpg // 2):
            acc = None
            lo = g * gw + pr * LANES
            x_pair = xdt[:, lo:lo + LANES]
            for half in range(2):
                h = g * hpg + pr * 2 + half
                diff = acs_c[:, h:h + 1] - acs_r[h:h + 1, :]
                lmat = jnp.exp(jnp.minimum(diff, 0.0))
                part = _bdot(cb * lmat, jnp.where(lane_half == half, x_pair, 0.0))
                acc = part if acc is None else acc + part
            pair_parts.append(acc)
        y_diag = jnp.concatenate(pair_parts, axis=1)
        y = y_diag + y_off + dskip_ref[:, g * gw:(g + 1) * gw] * xs[:, g * gw:(g + 1) * gw]
        y = y * _silu(z_ref[0, :, g * gw:(g + 1) * gw].astype(F32))
        ms = jnp.mean(y * y, axis=-1, keepdims=True)
        y_parts.append(y * lax.rsqrt(ms + RMS_EPS) * nw_ref[:, g * gw:(g + 1) * gw])
    o_ref[0] = jnp.concatenate(y_parts, axis=1).astype(o_ref.dtype)


N_SSD_IN = 11


def _layer0_mixers_kernel(*refs):
    ssd_in, swa_in = refs[:N_SSD_IN], refs[N_SSD_IN:N_SSD_IN + 7]
    o_ssd, o_swa, state_ref, s_ref = refs[N_SSD_IN + 7:]
    _ssd_kernel(*ssd_in, o_ssd, state_ref)
    _swa_kernel(*swa_in, o_swa, s_ref)


def _layer0_mixers(xbc_act, proj, dt_rows, dt_bias, a_log, d_skip, norm_w, sinks, swa_bias):
    bsz, s, _ = proj.shape
    q = SSD_CHUNK
    assert SWA_BLOCK == q
    pad = LANES - SSD_HEADS
    bias_c = jnp.pad(dt_bias, (0, pad)).reshape(1, LANES)
    alog_c = jnp.pad(a_log, (0, pad)).reshape(1, LANES)
    e_np = np.zeros((LANES, SSD_INNER), np.float32)
    for h in range(SSD_HEADS):
        e_np[h, h * SSD_HEAD_DIM:(h + 1) * SSD_HEAD_DIM] = 1.0
    small = lambda shape: pl.BlockSpec(shape, lambda b_, c: (0, 0))
    rows = lambda n, col: pl.BlockSpec((1, q, n), lambda b_, c: (b_, c, col))
    kvw = SWA_KV_HEADS * ATTN_HEAD_DIM
    prev = lambda col: pl.BlockSpec((1, q, kvw), lambda b_, c: (b_, jnp.maximum(c - 1, 0), col))
    ssd_specs = [rows(SSD_XBC, 0), rows(SSD_INNER, AB_Z // SSD_INNER), rows(LANES, AB_DT // LANES),
                 pl.BlockSpec((1, LANES, q), lambda b_, c: (b_, 0, c)),
                 small((1, LANES)), small((LANES, 1)), small((1, LANES)), small((LANES, 1)),
                 small((1, SSD_INNER)), small((1, SSD_INNER)), small((LANES, SSD_INNER))]
    assert len(ssd_specs) == N_SSD_IN
    swa_specs = [pl.BlockSpec(memory_space=pltpu.SMEM), rows(ATTN_WIDTH, AB_Q // ATTN_WIDTH),
                 prev(AB_K // kvw), rows(kvw, AB_K // kvw), prev(AB_V // kvw), rows(kvw, AB_V // kvw),
                 pl.BlockSpec((N_ATTN_HEADS, 1, 2 * q, q), lambda b_, c: (0, 0, 0, 0))]
    return pl.pallas_call(
        _layer0_mixers_kernel,
        grid=(bsz, s // q),
        in_specs=ssd_specs + swa_specs,
        out_specs=(rows(SSD_INNER, 0), rows(ATTN_WIDTH, 0)),
        out_shape=(jax.ShapeDtypeStruct((bsz, s, SSD_INNER), BF16),
                   jax.ShapeDtypeStruct((bsz, s, ATTN_WIDTH), BF16)),
        scratch_shapes=[pltpu.VMEM((SSD_GROUPS, SSD_STATE, SSD_INNER // SSD_GROUPS), F32),
                        pltpu.VMEM((N_ATTN_HEADS, 2 * q, q), F32)],
        compiler_params=_cparams(("arbitrary", "arbitrary")),
        name="ssd_swa_mixers",
    )(xbc_act, proj, proj, dt_rows, bias_c, bias_c.reshape(LANES, 1),
      alog_c, alog_c.reshape(LANES, 1), jnp.repeat(d_skip, SSD_HEAD_DIM).reshape(1, SSD_INNER),
      norm_w.reshape(1, SSD_INNER), jnp.asarray(e_np, BF16),
      sinks, proj, proj, proj, proj, proj, swa_bias)


def _swa_kernel(sink_ref, q_ref, kp_ref, kc_ref, vp_ref, vc_ref, bias_ref, o_ref, s_ref):
    blk = SWA_BLOCK
    n = pl.program_id(1)
    grp = N_ATTN_HEADS // SWA_KV_HEADS
    scale = ATTN_HEAD_DIM ** -0.5
    kk = jnp.concatenate([kp_ref[0], kc_ref[0]], axis=0).astype(F32) * (scale * LOG2E)
    vv_t = jnp.concatenate([vp_ref[0], vc_ref[0]], axis=0).astype(F32).T
    lane_half = lax.broadcasted_iota(jnp.int32, (1, LANES), 1) >> 6
    c = lax.broadcasted_iota(jnp.int32, (2 * blk, blk), 0)
    valid = (c >= blk) | (n > 0)
    ms = []
    for kv in range(SWA_KV_HEADS):
        k_own = jnp.where(lane_half == kv, kk, 0.0).astype(BF16)
        k_var = [None, None]
        k_var[kv] = k_own
        k_var[1 - kv] = pltpu.roll(jnp.where(lane_half == kv, kk, 0.0), ATTN_HEAD_DIM, axis=1).astype(BF16)
        for gq in range(grp):
            h = kv * grp + gq
            q_tile = q_ref[0, :, (h // 2) * LANES:(h // 2 + 1) * LANES]
            s_t = _bdot_nt(k_var[h % 2], q_tile) + bias_ref[h, 0]
            s_t = jnp.where(valid, s_t, NEG)
            s_ref[h] = s_t
            ms.append(jnp.maximum(jnp.max(s_t, axis=0, keepdims=True), sink_ref[h] * LOG2E))
    outs = []
    for h in range(N_ATTN_HEADS):
        kv = h // grp
        p = jnp.exp2(s_ref[h] - ms[h])
        l = jnp.sum(p, axis=0, keepdims=True) + jnp.exp2(sink_ref[h] * LOG2E - ms[h])
        outs.append(_bdot(vv_t[kv * ATTN_HEAD_DIM:(kv + 1) * ATTN_HEAD_DIM, :], p) / l)
    for t in range(N_ATTN_HEADS // 2):
        pair = jnp.concatenate([outs[2 * t], outs[2 * t + 1]], axis=0)
        o_ref[0, :, t * LANES:(t + 1) * LANES] = pair.T.astype(o_ref.dtype)


def _moba_kernel(q_ref, k_ref, v_ref, bias_ref, o_ref, vt_ref, qm_ref, negrow_ref, s_ref, *, nblk):
    mb = MOBA_BLOCK
    dh = ATTN_HEAD_DIM
    own0 = 2 * lax.rem(pl.program_id(1), nblk // 2)
    scale = dh ** -0.5

    @pl.when(own0 == 0)
    def _():
        means = []
        for j in range(nblk):
            vt_ref[j] = v_ref[0, j * mb:(j + 1) * mb, :].astype(F32).T.astype(BF16)
            means.append(jnp.mean(k_ref[0, j * mb:(j + 1) * mb, :].astype(F32), axis=0, keepdims=True))
        kmean = jnp.concatenate(means, axis=0)
        lane_half = lax.broadcasted_iota(jnp.int32, (1, LANES), 1) >> 6
        blk_id = lax.broadcasted_iota(jnp.int32, (nblk, mb), 0)
        for qb in range(nblk):
            q = q_ref[0, qb * mb:(qb + 1) * mb, :].astype(F32) * (scale * LOG2E)
            for hh in range(2):
                qm = jnp.where(lane_half == hh, q, 0.0)
                qm_ref[hh, qb] = qm.astype(BF16)
                gate = lax.dot_general(kmean, qm, (((1,), (1,)), ((), ())),
                                       preferred_element_type=F32, precision=HIGHEST)
                gate = jnp.where(blk_id < qb, gate, NEG)
                rank = jnp.zeros((nblk, mb), jnp.int32)
                for i in range(nblk):
                    gi = gate[i:i + 1, :]
                    ahead = (gi > gate) | ((gi == gate) & (i < blk_id))
                    rank = rank + jnp.where(ahead, 1, 0)
                keep = ((rank < MOBA_TOPK) & (blk_id < qb)) | (blk_id == qb)
                negrow_ref[hh, qb] = jnp.where(keep, 0.0, NEG).astype(F32)

    def attend(nb, qi):
        own = own0 + qi
        qms = [qm_ref[hh, own] for hh in range(2)]
        negrows = [negrow_ref[hh, own] for hh in range(2)]
        m8 = [None, None]
        l8 = [None, None]
        acc = [None, None]

        def scores(hh, j):
            s_t = _bdot_nt(k_ref[0, j * mb:(j + 1) * mb, :], qms[hh]) + bias_ref[hh, jnp.maximum(own - j, 0)]
            s_t = s_t + negrows[hh][j:j + 1, :]
            s_ref[qi, hh, j] = s_t
            m_j = jnp.max(s_t.reshape(mb // SUBLANES, SUBLANES, mb), axis=0)
            m8[hh] = m_j if m8[hh] is None else jnp.maximum(m8[hh], m_j)

        def weights(hh, j, m):
            p = jnp.exp2(s_ref[qi, hh, j] - m)
            l_j = jnp.sum(p.reshape(mb // SUBLANES, SUBLANES, mb), axis=0)
            a_j = _bdot(vt_ref[j, hh * dh:(hh + 1) * dh, :], p)
            l8[hh] = l_j if l8[hh] is None else l8[hh] + l_j
            acc[hh] = a_j if acc[hh] is None else acc[hh] + a_j

        for j in range(nb):
            for hh in range(2):
                scores(hh, j)
        m = [jnp.max(m8[hh], axis=0, keepdims=True) for hh in range(2)]
        for j in range(nb):
            for hh in range(2):
                weights(hh, j, m[hh])
        out_t = jnp.concatenate([acc[hh] / jnp.sum(l8[hh], axis=0, keepdims=True) for hh in range(2)], axis=0)
        o_ref[0, qi * mb:(qi + 1) * mb, :] = out_t.T.astype(o_ref.dtype)

    def attend_both(nb):
        attend(nb, 0)
        attend(nb, 1)

    for nb in range(2, nblk + 1, 2):
        pl.when(own0 == nb - 2)(functools.partial(attend_both, nb))


def _moba_specs(bsz, s):
    mb = MOBA_BLOCK
    nblk = s // mb
    half = nblk // 2
    col = lambda c0: (lambda b_, c: (b_, 0, c0 // LANES + c // half))
    in_specs = [pl.BlockSpec((1, s, LANES), col(CD_QD)),
                pl.BlockSpec((1, s, LANES), col(CD_KD)),
                pl.BlockSpec((1, s, LANES), col(CD_VD)),
                pl.BlockSpec((2, nblk, mb, mb), lambda b_, c: (c // half, 0, 0, 0))]
    out_spec = pl.BlockSpec((1, 2 * mb, LANES), lambda b_, c: (b_, c % half, c // half))
    out_shape = jax.ShapeDtypeStruct((bsz, s, ATTN_WIDTH), BF16)
    scratch = [pltpu.VMEM((nblk, LANES, mb), BF16),
               pltpu.VMEM((2, nblk, mb, LANES), BF16),
               pltpu.VMEM((2, nblk, nblk, mb), F32),
               pltpu.VMEM((2, 2, nblk, mb, mb), F32)]
    return in_specs, out_spec, out_shape, scratch


def _gdn_kernel(qkv_ref, z_ref, bac_ref, bar_ref, bias_c_ref, bias_r_ref, alog_c_ref, alog_r_ref,
                nw_ref, o_ref, state_ref, m_ref, attn_ref, t_ref, x_ref, rhs_ref, u_ref, wq_ref,
                kdt_ref, r_ref, glc_ref):
    t = 2 * GDN_CHUNK
    ck = GDN_CHUNK
    dk = GDN_HEAD_DIM
    nh = GDN_HEADS

    @pl.when(pl.program_id(1) == 0)
    def _():
        state_ref[...] = jnp.zeros(state_ref.shape, F32)

    row = lax.broadcasted_iota(jnp.int32, (t, t), 0)
    col = lax.broadcasted_iota(jnp.int32, (t, t), 1)
    same = (row >> 6) == (col >> 6)
    tril = same & (row >= col)
    strict = same & (row > col)
    tri_b = jnp.where(tril, 1.0, 0.0).astype(BF16)
    triu_b = jnp.where(same & (row <= col), 1.0, 0.0).astype(BF16)
    blk_b = jnp.where(same, 1.0, 0.0).astype(BF16)
    eye_f = jnp.where(row == col, 1.0, 0.0).astype(F32)
    merge_masks = [((row >> (l + 1)) == (col >> (l + 1))) & (((row >> l) & 1) == 1) & (((col >> l) & 1) == 0)
                   for l in range(int(math.log2(ck)))]
    mask_bf = [jnp.where(m, 1.0, 0.0).astype(BF16) for m in merge_masks[1:]]

    ba_c = bac_ref[0].astype(F32)
    g_c = -jnp.exp(alog_c_ref[...]) * _softplus(ba_c + bias_c_ref[...])
    sums_c = _sum01_left(jnp.concatenate([tri_b, blk_b], axis=0), g_c)
    gc_c = sums_c[:t]
    gl_c = sums_c[t:]
    g_r = -jnp.exp(alog_r_ref[...]) * _softplus(bar_ref[0] + bias_r_ref[...])
    gc_r = _sum01_right(g_r, triu_b)
    glc_ref[...] = gl_c

    def prep(h):
        q = qkv_ref[0, :, h * dk:(h + 1) * dk].astype(F32)
        k = qkv_ref[0, :, (nh + h) * dk:(nh + h + 1) * dk].astype(F32)
        v = qkv_ref[0, :, (2 * nh + h) * dk:(2 * nh + h + 1) * dk].astype(F32)
        qn = q * lax.rsqrt(jnp.sum(q * q, axis=-1, keepdims=True) + 1e-6) * (dk ** -0.5)
        kn = k * lax.rsqrt(jnp.sum(k * k, axis=-1, keepdims=True) + 1e-6)
        beta = _sigmoid(ba_c[:, h:h + 1])
        gcc = gc_c[:, nh + h:nh + h + 1]
        gcr = gc_r[nh + h:nh + h + 1, :]
        glc = gl_c[:, nh + h:nh + h + 1]
        decay = jnp.where(tril, jnp.exp(jnp.where(tril, gcc - gcr, 0.0)), 0.0)
        kb = kn * beta
        kk = _bdot_nt(jnp.concatenate([kb, qn], axis=0), kn)
        mm = jnp.where(strict, kk[:t] * decay, 0.0)
        m_ref[h] = mm.astype(BF16)
        attn_ref[h] = (kk[t:] * decay).astype(BF16)
        t_ref[h] = eye_f - jnp.where(merge_masks[0], mm, 0.0)
        egc = jnp.exp(gcc)
        rhs_ref[h] = jnp.concatenate([v * beta, kb * egc], axis=1).astype(BF16)
        q_dec = (qn * egc).astype(BF16)
        for a in range(2):
            wq_ref[h, a, ck:, :] = q_dec[a * ck:(a + 1) * ck]
        kdt_ref[h] = (kn * jnp.exp(glc - gcc)).T.astype(BF16)

    def merge_a(lvl, h):
        x_ref[h] = jnp.dot(t_ref[h].astype(BF16), m_ref[h] * mask_bf[lvl],
                           preferred_element_type=F32).astype(BF16)

    def merge_b(lvl, h):
        t_h = t_ref[h]
        t_ref[h] = t_h - jnp.dot(x_ref[h], t_h.astype(BF16), preferred_element_type=F32)

    def solve(h):
        sol = jnp.dot(t_ref[h].astype(BF16), rhs_ref[h], preferred_element_type=F32)
        u_ref[h] = sol[:, :dk]
        for a in range(2):
            wq_ref[h, a, :ck, :] = sol[a * ck:(a + 1) * ck, dk:].astype(BF16)

    zeros_half = jnp.zeros((ck, dk), F32)

    def read_state(a, h):
        r_ref[h] = jnp.dot(wq_ref[h, a], state_ref[h].astype(BF16), preferred_element_type=F32)

    def update(a, h):
        sl = slice(a * ck, (a + 1) * ck)
        v_new = u_ref[h, sl, :] - r_ref[h, :ck, :]
        v_full = jnp.concatenate([v_new, zeros_half] if a == 0 else [zeros_half, v_new], axis=0).astype(BF16)
        o = r_ref[h, ck:, :] + jnp.dot(attn_ref[h, sl, :], v_full, preferred_element_type=F32)
        gl = glc_ref[a * ck:a * ck + 1, nh + h:nh + h + 1]
        state_ref[h] = state_ref[h] * jnp.exp(gl) + jnp.dot(kdt_ref[h], v_full, preferred_element_type=F32)
        ms = jnp.mean(o * o, axis=-1, keepdims=True)
        y = o * lax.rsqrt(ms + RMS_EPS) * nw_ref[...] * _silu(z_ref[0, sl, h * dk:(h + 1) * dk].astype(F32))
        o_ref[0, sl, h * dk:(h + 1) * dk] = y.astype(o_ref.dtype)

    def steps(phase, heads):
        if phase == "prep":
            return [functools.partial(prep, h) for h in heads]
        if phase == "merge":
            return [functools.partial(fn, lvl, h) for lvl in range(len(mask_bf))
                    for fn in (merge_a, merge_b) for h in heads]
        return ([functools.partial(solve, h) for h in heads]
                + [functools.partial(fn, a, h) for a in range(2) for fn in (read_state, update) for h in heads])

    for phase in ("prep", "merge", "tail"):
        for step in steps(phase, range(nh)):
            step()


N_GDN_IN, N_GDN_SCRATCH, N_MOBA_IN = 9, 11, 4


def _layer1_mixers_kernel(*refs, nblk):
    gdn_in = refs[:N_GDN_IN]
    moba_in = refs[N_GDN_IN:N_GDN_IN + N_MOBA_IN]
    o_gdn, o_moba = refs[N_GDN_IN + N_MOBA_IN:N_GDN_IN + N_MOBA_IN + 2]
    scratch = refs[N_GDN_IN + N_MOBA_IN + 2:]
    _gdn_kernel(*gdn_in, o_gdn, *scratch[:N_GDN_SCRATCH])
    _moba_kernel(*moba_in, o_moba, *scratch[N_GDN_SCRATCH:], nblk=nblk)


def _layer1_mixers(qkv_act, proj, ba_rows, dt_bias, a_log, norm_w, moba_bias):
    bsz, s, _ = proj.shape
    t = 2 * GDN_CHUNK
    nh = GDN_HEADS
    dk = GDN_HEAD_DIM
    nblk = s // MOBA_BLOCK
    assert s // t == (N_ATTN_HEADS // 2) * (nblk // 2)
    bias_c = jnp.pad(dt_bias, (nh, LANES - 2 * nh)).reshape(1, LANES)
    alog_c = jnp.pad(a_log, (nh, LANES - 2 * nh)).reshape(1, LANES)
    small = lambda shape: pl.BlockSpec(shape, lambda b_, c: (0, 0))
    moba_in, moba_out, moba_shape, moba_scratch = _moba_specs(bsz, s)
    gdn_in = [pl.BlockSpec((1, t, 3 * GDN_INNER), lambda b_, c: (b_, c, 0)),
              pl.BlockSpec((1, t, GDN_INNER), lambda b_, c: (b_, c, CD_Z // GDN_INNER)),
              pl.BlockSpec((1, t, LANES), lambda b_, c: (b_, c, CD_BA // LANES)),
              pl.BlockSpec((1, LANES, t), lambda b_, c: (b_, 0, c)),
              small((1, LANES)), small((LANES, 1)), small((1, LANES)), small((LANES, 1)),
              small((1, dk))]
    assert len(gdn_in) == N_GDN_IN and len(moba_in) == N_MOBA_IN
    return pl.pallas_call(
        functools.partial(_layer1_mixers_kernel, nblk=nblk),
        grid=(bsz, s // t),
        in_specs=gdn_in + moba_in,
        out_specs=(pl.BlockSpec((1, t, GDN_INNER), lambda b_, c: (b_, c, 0)), moba_out),
        out_shape=(jax.ShapeDtypeStruct((bsz, s, GDN_INNER), BF16), moba_shape),
        scratch_shapes=[pltpu.VMEM((nh, dk, dk), F32),
                        pltpu.VMEM((nh, t, t), BF16),
                        pltpu.VMEM((nh, t, t), BF16),
                        pltpu.VMEM((nh, t, t), F32),
                        pltpu.VMEM((nh, t, t), BF16),
                        pltpu.VMEM((nh, t, 2 * dk), BF16),
                        pltpu.VMEM((nh, t, dk), F32),
                        pltpu.VMEM((nh, 2, t, dk), BF16),
                        pltpu.VMEM((nh, dk, t), BF16),
                        pltpu.VMEM((nh, t, dk), F32),
                        pltpu.VMEM((t, LANES), F32)]
        + moba_scratch,
        compiler_params=_cparams(("arbitrary", "arbitrary")),
        name="gdn_moba_mixers",
    )(qkv_act, proj, proj, ba_rows, bias_c, bias_c.reshape(LANES, 1), alog_c, alog_c.reshape(LANES, 1),
      norm_w.reshape(1, dk), proj, proj, proj, moba_bias)


def _reorder_kernel(w_ref, o_ref, *, segments, zero_from):
    rows, n = o_ref.shape
    if zero_from < n:
        o_ref[:, zero_from:] = jnp.zeros((rows, n - zero_from), o_ref.dtype)
    for lo, hi, dst in segments:
        o_ref[:, dst:dst + hi - lo] = w_ref[:, lo:hi].astype(o_ref.dtype)


def _reorder_cast(w_stack, layer, segments, zero_from, n_out, tr=256):
    nl, d, n = w_stack.shape
    if nl == 1:
        w_stack = w_stack.reshape(d, n)
        w_spec = pl.BlockSpec((tr, n), lambda r: (r, 0))
    else:
        w_spec = pl.BlockSpec((None, tr, n), lambda r: (layer, r, 0))
    return pl.pallas_call(
        functools.partial(_reorder_kernel, segments=segments, zero_from=zero_from),
        grid=(d // tr,),
        in_specs=[w_spec],
        out_specs=pl.BlockSpec((tr, n_out), lambda r: (r, 0)),
        out_shape=jax.ShapeDtypeStruct((d, n_out), BF16),
        compiler_params=_cparams(("arbitrary",)),
        name="reorder_cast_weights",
    )(w_stack)


def _gate_rows(proj, col0):
    return jnp.swapaxes(proj[:, :, col0:col0 + LANES].astype(F32), 1, 2)


def kernel(x, c, rel_bias, norm_w, ada_w, ada_b, ab_w_in, ab_w_out, ssd_conv_w, ssd_conv_b,
           ssd_dt_bias, ssd_a_log, ssd_d, ssd_norm_w, swa_sinks, cd_w_in, cd_w_out, gdn_conv_w,
           gdn_dt_bias, gdn_a_log, gdn_norm_w, ffn_w_up, ffn_conv_w, ffn_conv_b, ffn_w_down):
    bsz, s, d = x.shape
    depth = norm_w.shape[0]
    mods = _mods(c, ada_w, ada_b)
    swa_bias = _bias_tiles(rel_bias, _swa_bucket_idx())
    moba_bias = _bias_tiles(rel_bias, _moba_bucket_idx(s // MOBA_BLOCK))

    for i in range(depth):
        sh_m, sc_m, g_m, sh_f, sc_f, g_f = [m.reshape(bsz, 1, d) for m in jnp.split(mods[i], 6, axis=-1)]
        j = i // 2
        if i % 2 == 0:
            dt0 = SSD_INNER + SSD_XBC
            n_in = ab_w_in.shape[-1]
            w_in = _reorder_cast(ab_w_in, j, ((SSD_INNER, dt0, 0), (0, SSD_INNER, SSD_XBC + AB_Z),
                                              (dt0 + SSD_HEADS, n_in, SSD_XBC + AB_Q), (dt0, dt0 + SSD_HEADS, SSD_XBC + AB_DT)),
                                 SSD_XBC + AB_DT, SSD_XBC + AB_COLS)
            xbc_act, proj = _proj_conv_act(x, norm_w[i, 0], sc_m, sh_m, w_in, ssd_conv_w[j], ssd_conv_b[j],
                                           "silu", n_plain=AB_COLS)
            y_a, y_b = _layer0_mixers(xbc_act, proj, _gate_rows(proj, AB_DT), ssd_dt_bias[j], ssd_a_log[j],
                                      ssd_d[j], ssd_norm_w[j], swa_sinks[j], swa_bias)
            x = _matmul_resid([y_a, y_b], ab_w_out, j, x, g_m, norm_w[i, 1])
        else:
            nqkv = 3 * GDN_INNER
            ba0 = nqkv + GDN_INNER
            n_in = cd_w_in.shape[-1]
            w_in = _reorder_cast(cd_w_in, j, ((0, ba0, 0), (ba0 + 2 * GDN_HEADS, n_in, nqkv + CD_QD),
                                              (ba0, ba0 + 2 * GDN_HEADS, nqkv + CD_BA)),
                                 nqkv + CD_BA, nqkv + CD_COLS)
            qkv_act, proj = _proj_conv_act(x, norm_w[i, 0], sc_m, sh_m, w_in, gdn_conv_w[j],
                                           jnp.zeros((nqkv,), F32), "silu", n_plain=CD_COLS)
            y_c, y_d = _layer1_mixers(qkv_act, proj, _gate_rows(proj, CD_BA), gdn_dt_bias[j], gdn_a_log[j],
                                      gdn_norm_w[j], moba_bias)
            x = _matmul_resid([y_c, y_d], cd_w_out, j, x, g_m, norm_w[i, 1])
        n_up = ffn_w_up.shape[-1]
        w_up = _reorder_cast(ffn_w_up, i, ((0, n_up, 0),), n_up, n_up)
        act = _proj_conv_act(x, norm_w[i, 2], sc_f, sh_f, w_up, ffn_conv_w[i], ffn_conv_b[i], "geglu")
        x = _matmul_resid([act], ffn_w_down, i, x, g_f, norm_w[i, 3])
    return x
```

```python
import functools
import math

import numpy as np
import jax
import jax.numpy as jnp
from jax import lax
from jax.experimental import pallas as pl
from jax.experimental.pallas import tpu as pltpu

F32 = jnp.float32
BF16 = jnp.bfloat16
HIGHEST = lax.Precision.HIGHEST

RMS_EPS = 1e-6
NEG = -1e30
LANES = 128
SUBLANES = 8
N_ATTN_HEADS = 8
ATTN_HEAD_DIM = 64
ATTN_WIDTH = N_ATTN_HEADS * ATTN_HEAD_DIM
REL_BUCKETS = 32
REL_MAX_DIST = 1024
SSD_HEADS = 24
SSD_HEAD_DIM = 64
SSD_INNER = SSD_HEADS * SSD_HEAD_DIM
SSD_GROUPS = 4
SSD_STATE = 128
SSD_CHUNK = 128
SSD_XBC = SSD_INNER + 2 * SSD_GROUPS * SSD_STATE
SWA_KV_HEADS = 2
SWA_BLOCK = 128
GDN_HEADS = 12
GDN_HEAD_DIM = 128
GDN_INNER = GDN_HEADS * GDN_HEAD_DIM
GDN_CHUNK = 64
MOBA_BLOCK = 256
MOBA_TOPK = 3

AB_Z, AB_Q, AB_K, AB_V, AB_DT, AB_COLS = 0, 1536, 2048, 2176, 2304, 2560
CD_Z, CD_QD, CD_KD, CD_VD, CD_BA, CD_COLS = 0, 1536, 2048, 2560, 3072, 3328

VMEM_LIMIT = 48 * 1024 * 1024
HALO = 16


def _cparams(sem):
    return pltpu.CompilerParams(dimension_semantics=sem, vmem_limit_bytes=VMEM_LIMIT)


def _bdot(a, b):
    return jnp.dot(a.astype(BF16), b.astype(BF16), preferred_element_type=F32)


def _bdot_nt(a, b):
    return lax.dot_general(a.astype(BF16), b.astype(BF16), (((1,), (1,)), ((), ())),
                           preferred_element_type=F32)


def _split3(x):
    hi = x.astype(BF16)
    r1 = x - hi.astype(F32)
    mid = r1.astype(BF16)
    lo = (r1 - mid.astype(F32)).astype(BF16)
    return hi, mid, lo


def _sum01_left(m01, x):
    n = x.shape[1]
    y = jnp.dot(m01, jnp.concatenate(_split3(x), axis=1), preferred_element_type=F32)
    return y[:, :n] + y[:, n:2 * n] + y[:, 2 * n:]


def _sum01_right(x, m01):
    n = x.shape[0]
    y = jnp.dot(jnp.concatenate(_split3(x), axis=0), m01, preferred_element_type=F32)
    return y[:n] + y[n:2 * n] + y[2 * n:]


def _softplus(x):
    return jnp.maximum(x, 0.0) + jnp.log(1.0 + jnp.exp(-jnp.abs(x)))


def _sigmoid(x):
    return 1.0 / (1.0 + jnp.exp(-x))


def _silu(x):
    return x * _sigmoid(x)


def _mods_kernel(c_ref, w_ref, b_ref, o_ref):
    o_ref[0] = _bdot(_silu(c_ref[...]), w_ref[0]) + b_ref[0]


def _mods(c, ada_w, ada_b):
    depth, d, n = ada_w.shape
    bsz = c.shape[0]
    tn = 512
    return pl.pallas_call(
        _mods_kernel,
        grid=(depth, n // tn),
        in_specs=[pl.BlockSpec((bsz, d), lambda l, j: (0, 0)),
                  pl.BlockSpec((1, d, tn), lambda l, j: (l, 0, j)),
                  pl.BlockSpec((1, 1, tn), lambda l, j: (l, 0, j))],
        out_specs=pl.BlockSpec((1, bsz, tn), lambda l, j: (l, 0, j)),
        out_shape=jax.ShapeDtypeStruct((depth, bsz, n), F32),
        compiler_params=_cparams(("arbitrary", "arbitrary")),
        name="adaln_mods",
    )(c, ada_w, ada_b.reshape(depth, 1, n))


def _modulated_norm(x, nw, sc, sh):
    ms = jnp.mean(x * x, axis=-1, keepdims=True)
    return x * lax.rsqrt(ms + RMS_EPS) * nw * (1.0 + sc) + sh


def _mmres_kernel(*refs, splits):
    na = len(splits)
    a_refs = refs[:na]
    w_ref, x_ref, g_ref, nw_ref, o_ref = refs[na:]
    acc = None
    lo = 0
    for a_ref, k in zip(a_refs, splits):
        part = jnp.dot(a_ref[0].astype(BF16), w_ref[lo:lo + k, :].astype(BF16), preferred_element_type=F32)
        acc = part if acc is None else acc + part
        lo += k
    ms = jnp.mean(acc * acc, axis=-1, keepdims=True)
    y = acc * lax.rsqrt(ms + RMS_EPS) * nw_ref[...]
    o_ref[0] = x_ref[0] + g_ref[0] * y


def _matmul_resid(a_list, w_stack, layer, x, gate, nw, tm=1024):
    bsz, s, d = x.shape
    spt = s // tm
    splits = tuple(a.shape[-1] for a in a_list)
    ktot = sum(splits)
    in_specs = [pl.BlockSpec((1, tm, k), lambda i: (i // spt, i % spt, 0)) for k in splits]
    in_specs += [pl.BlockSpec((None, ktot, d), lambda i: (layer, 0, 0)),
                 pl.BlockSpec((1, tm, d), lambda i: (i // spt, i % spt, 0)),
                 pl.BlockSpec((1, 1, d), lambda i: (i // spt, 0, 0)),
                 pl.BlockSpec((1, d), lambda i: (0, 0))]
    return pl.pallas_call(
        functools.partial(_mmres_kernel, splits=splits),
        grid=(bsz * spt,),
        in_specs=in_specs,
        out_specs=pl.BlockSpec((1, tm, d), lambda i: (i // spt, i % spt, 0)),
        out_shape=jax.ShapeDtypeStruct((bsz, s, d), F32),
        compiler_params=_cparams(("arbitrary",)),
        name="matmul_resid",
    )(*a_list, w_stack, x, gate, nw.reshape(1, d))


def _proj_conv_kernel(*refs, offs, tc, width, rb, act, n_plain):
    if n_plain:
        xh_ref, x_ref, nw_ref, sc_ref, sh_ref, w_ref, cw_ref, cb_ref, o_ref, op_ref, h_ref, u_ref = refs
    else:
        xh_ref, x_ref, nw_ref, sc_ref, sh_ref, w_ref, cw_ref, cb_ref, o_ref, h_ref, u_ref = refs
    tm = x_ref.shape[1]
    nchunk = o_ref.shape[2] // tc
    nplain = n_plain // tc
    plain0 = w_ref.shape[1] - n_plain
    h_halo = _modulated_norm(xh_ref[0], nw_ref[...], sc_ref[0], sh_ref[0])
    h_ref[:HALO, :] = jnp.where(pl.program_id(1) > 0, h_halo, 0.0).astype(BF16)
    h_ref[HALO:, :] = _modulated_norm(x_ref[0], nw_ref[...], sc_ref[0], sh_ref[0]).astype(BF16)
    c0 = math.sqrt(2.0 / math.pi)

    nrow = HALO + tm

    def matmuls(c):
        for k, off in enumerate(offs):
            lo = off + c * tc
            u = jnp.dot(h_ref[...], w_ref[:, lo:lo + tc], preferred_element_type=F32)
            for s in range(width):
                u_ref[c % 2, k, s, pl.ds(s, nrow), :] = u

    def epilogue(c):
        for r0 in range(0, tm, rb):
            conv = []
            for k, off in enumerate(offs):
                lo = off + c * tc
                acc = cb_ref[:, lo:lo + tc]
                for s in range(width):
                    acc = acc + (u_ref[c % 2, k, s, pl.ds(HALO + r0, rb), :]
                                 * cw_ref[width - 1 - s:width - s, lo:lo + tc])
                conv.append(acc)
            if act == "geglu":
                g, v = conv
                th = jnp.tanh(g * (c0 + (c0 * 0.044715) * (g * g)))
                hg = 0.5 * g
                out = (hg + hg * th) * v
            else:
                out = _silu(conv[0])
            o_ref[0, r0:r0 + rb, c * tc:(c + 1) * tc] = out.astype(o_ref.dtype)

    def plain(p):
        lo = plain0 + p * tc
        op_ref[0, :, p * tc:(p + 1) * tc] = jnp.dot(
            h_ref[HALO:, :], w_ref[:, lo:lo + tc], preferred_element_type=F32).astype(op_ref.dtype)

    matmuls(0)
    done = 0
    for c in range(nchunk):
        if c + 1 < nchunk:
            matmuls(c + 1)
        epilogue(c)
        upto = (c + 1) * nplain // nchunk
        for p in range(done, upto):
            plain(p)
        done = upto


def _proj_conv_act(x, nw, sc, sh, w, cw, cb, act, n_plain=0, tm=512, tc=256, rb=128):
    bsz, s, d = x.shape
    n2 = w.shape[-1] - n_plain
    f = n2 // 2 if act == "geglu" else n2
    offs = (0, f) if act == "geglu" else (0,)
    width = cw.shape[0]
    hb = tm // HALO
    const = lambda shape: pl.BlockSpec(shape, lambda b_, r: (0, 0))
    rows = lambda n: pl.BlockSpec((1, tm, n), lambda b_, r: (b_, r, 0))
    in_specs = [pl.BlockSpec((1, HALO, d), lambda b_, r: (b_, jnp.maximum(r * hb - 1, 0), 0)),
                rows(d), const((1, d)),
                pl.BlockSpec((1, 1, d), lambda b_, r: (b_, 0, 0)),
                pl.BlockSpec((1, 1, d), lambda b_, r: (b_, 0, 0)),
                const((d, n2 + n_plain)),
                const((width, n2)), const((1, n2))]
    args = [x, x, nw.reshape(1, d), sc, sh, w, cw, cb.reshape(1, n2)]
    out_specs = rows(f)
    out_shape = jax.ShapeDtypeStruct((bsz, s, f), BF16)
    if n_plain:
        out_specs = (out_specs, rows(n_plain))
        out_shape = (out_shape, jax.ShapeDtypeStruct((bsz, s, n_plain), BF16))
    return pl.pallas_call(
        functools.partial(_proj_conv_kernel, offs=offs, tc=tc, width=width, rb=rb, act=act, n_plain=n_plain),
        grid=(bsz, s // tm),
        in_specs=in_specs,
        out_specs=out_specs,
        out_shape=out_shape,
        scratch_shapes=[pltpu.VMEM((HALO + tm, d), BF16),
                        pltpu.VMEM((2, len(offs), width, HALO + tm + SUBLANES, tc), F32)],
        compiler_params=_cparams(("arbitrary", "arbitrary")),
        name="proj_conv_" + act,
    )(*args)


def _rel_bucket_np(d):
    max_exact = REL_BUCKETS // 2
    d = np.maximum(d, 0)
    df = np.maximum(d, 1).astype(np.float64)
    large = max_exact + (np.log(df / max_exact) / math.log(REL_MAX_DIST / max_exact)
                         * (REL_BUCKETS - max_exact)).astype(np.int32)
    large = np.minimum(large, REL_BUCKETS - 1)
    return np.where(d < max_exact, d, large).astype(np.int32)


LOG2E = math.log2(math.e)


def _bias_kernel(tab_ref, idx_ref, o_ref, *, ranges):
    h = pl.program_id(0)
    for t, (lo, hi) in enumerate(ranges):
        idx = idx_ref[t]
        acc = jnp.full(idx.shape, NEG, F32)
        for bkt in range(lo, hi + 1):
            acc = jnp.where(idx == bkt, tab_ref[bkt, h] * LOG2E, acc)
        o_ref[0, t] = acc


def _bias_tiles(rel_bias, idx_np):
    t, r, c = idx_np.shape
    ranges = tuple((int(tile[tile >= 0].min()), int(tile.max())) for tile in idx_np)
    return pl.pallas_call(
        functools.partial(_bias_kernel, ranges=ranges),
        grid=(N_ATTN_HEADS,),
        in_specs=[pl.BlockSpec(memory_space=pltpu.SMEM),
                  pl.BlockSpec((t, r, c), lambda h: (0, 0, 0))],
        out_specs=pl.BlockSpec((1, t, r, c), lambda h: (h, 0, 0, 0)),
        out_shape=jax.ShapeDtypeStruct((N_ATTN_HEADS, t, r, c), F32),
        compiler_params=_cparams(("arbitrary",)),
        name="rel_bias_tiles",
    )(rel_bias, jnp.asarray(idx_np))


def _swa_bucket_idx():
    c = np.arange(2 * SWA_BLOCK)[:, None]
    r = np.arange(SWA_BLOCK)[None, :]
    dist = SWA_BLOCK + r - c
    return np.where((dist >= 0) & (dist < SWA_BLOCK), _rel_bucket_np(dist), -1).astype(np.int32)[None]


def _moba_bucket_idx(nblk):
    c = np.arange(MOBA_BLOCK)[:, None]
    r = np.arange(MOBA_BLOCK)[None, :]
    tiles = [np.where(m * MOBA_BLOCK + r - c >= 0, _rel_bucket_np(m * MOBA_BLOCK + r - c), -1)
             for m in range(nblk)]
    return np.stack(tiles).astype(np.int32)


def _expand_heads(v, e):
    hi = v.astype(BF16)
    lo = (v - hi.astype(F32)).astype(BF16)
    return (jnp.dot(hi, e, preferred_element_type=F32) + jnp.dot(lo, e, preferred_element_type=F32))


def _ssd_kernel(xbc_ref, z_ref, dtc_ref, dtr_ref, bias_c_ref, bias_r_ref, alog_c_ref, alog_r_ref,
                dskip_ref, nw_ref, e_ref, o_ref, state_ref):
    q = SSD_CHUNK
    gw = SSD_INNER // SSD_GROUPS
    hpg = SSD_HEADS // SSD_GROUPS

    @pl.when(pl.program_id(1) == 0)
    def _():
        state_ref[...] = jnp.zeros(state_ref.shape, F32)

    row = lax.broadcasted_iota(jnp.int32, (q, q), 0)
    col = lax.broadcasted_iota(jnp.int32, (q, q), 1)
    tril = row >= col
    tri_b = jnp.where(tril, 1.0, 0.0).astype(BF16)
    triu_b = jnp.where(row <= col, 1.0, 0.0).astype(BF16)

    dt_c = _softplus(dtc_ref[0].astype(F32) + bias_c_ref[...])
    da_c = dt_c * (-jnp.exp(alog_c_ref[...]))
    acs_c = _sum01_left(tri_b, da_c)
    dt_r = _softplus(dtr_ref[0] + bias_r_ref[...])
    da_r = dt_r * (-jnp.exp(alog_r_ref[...]))
    acs_r = _sum01_right(da_r, triu_b)

    acs_last = acs_c[q - 1:q, :]
    e = e_ref[...]
    dt_full = _expand_heads(dt_c, e)
    dtdec_full = _expand_heads(dt_c * jnp.exp(acs_last - acs_c), e)
    eacs_full = _expand_heads(jnp.exp(acs_c), e)
    cdecay_full = eacs_full[q - 1:q, :]

    xbc = xbc_ref[0].astype(F32)
    xs = xbc[:, :SSD_INNER]
    xdt = xs * dt_full
    xdec = xs * dtdec_full
    lane_half = lax.broadcasted_iota(jnp.int32, (1, LANES), 1) >> 6

    y_parts = []
    for g in range(SSD_GROUPS):
        b_g = xbc[:, SSD_INNER + g * SSD_STATE:SSD_INNER + (g + 1) * SSD_STATE]
        c_g = xbc[:, SSD_INNER + SSD_GROUPS * SSD_STATE + g * SSD_STATE:
                  SSD_INNER + SSD_GROUPS * SSD_STATE + (g + 1) * SSD_STATE]
        cb = jnp.where(tril, _bdot_nt(c_g, b_g), 0.0)
        st = state_ref[g]
        y_off = _bdot(c_g, st) * eacs_full[:, g * gw:(g + 1) * gw]
        state_ref[g] = st * cdecay_full[:, g * gw:(g + 1) * gw] + _bdot(b_g.T, xdec[:, g * gw:(g + 1) * gw])
        pair_parts = []
        for pr in range(hpg // 2):
            acc = None
            lo = g * gw + pr * LANES
            x_pair = xdt[:, lo:lo + LANES]
            for half in range(2):
                h = g * hpg + pr * 2 + half
                diff = acs_c[:, h:h + 1] - acs_r[h:h + 1, :]
                lmat = jnp.exp(jnp.minimum(diff, 0.0))
                part = _bdot(cb * lmat, jnp.where(lane_half == half, x_pair, 0.0))
                acc = part if acc is None else acc + part
            pair_parts.append(acc)
        y_diag = jnp.concatenate(pair_parts, axis=1)
        y = y_diag + y_off + dskip_ref[:, g * gw:(g + 1) * gw] * xs[:, g * gw:(g + 1) * gw]
        y = y * _silu(z_ref[0, :, g * gw:(g + 1) * gw].astype(F32))
        ms = jnp.mean(y * y, axis=-1, keepdims=True)
        y_parts.append(y * lax.rsqrt(ms + RMS_EPS) * nw_ref[:, g * gw:(g + 1) * gw])
    o_ref[0] = jnp.concatenate(y_parts, axis=1).astype(o_ref.dtype)


N_SSD_IN = 11


def _layer0_mixers_kernel(*refs):
    ssd_in, swa_in = refs[:N_SSD_IN], refs[N_SSD_IN:N_SSD_IN + 7]
    o_ssd, o_swa, state_ref, s_ref = refs[N_SSD_IN + 7:]
    _ssd_kernel(*ssd_in, o_ssd, state_ref)
    _swa_kernel(*swa_in, o_swa, s_ref)


def _layer0_mixers(xbc_act, proj, dt_rows, dt_bias, a_log, d_skip, norm_w, sinks, swa_bias):
    bsz, s, _ = proj.shape
    q = SSD_CHUNK
    assert SWA_BLOCK == q
    pad = LANES - SSD_HEADS
    bias_c = jnp.pad(dt_bias, (0, pad)).reshape(1, LANES)
    alog_c = jnp.pad(a_log, (0, pad)).reshape(1, LANES)
    e_np = np.zeros((LANES, SSD_INNER), np.float32)
    for h in range(SSD_HEADS):
        e_np[h, h * SSD_HEAD_DIM:(h + 1) * SSD_HEAD_DIM] = 1.0
    small = lambda shape: pl.BlockSpec(shape, lambda b_, c: (0, 0))
    rows = lambda n, col: pl.BlockSpec((1, q, n), lambda b_, c: (b_, c, col))
    kvw = SWA_KV_HEADS * ATTN_HEAD_DIM
    prev = lambda col: pl.BlockSpec((1, q, kvw), lambda b_, c: (b_, jnp.maximum(c - 1, 0), col))
    ssd_specs = [rows(SSD_XBC, 0), rows(SSD_INNER, AB_Z // SSD_INNER), rows(LANES, AB_DT // LANES),
                 pl.BlockSpec((1, LANES, q), lambda b_, c: (b_, 0, c)),
                 small((1, LANES)), small((LANES, 1)), small((1, LANES)), small((LANES, 1)),
                 small((1, SSD_INNER)), small((1, SSD_INNER)), small((LANES, SSD_INNER))]
    assert len(ssd_specs) == N_SSD_IN
    swa_specs = [pl.BlockSpec(memory_space=pltpu.SMEM), rows(ATTN_WIDTH, AB_Q // ATTN_WIDTH),
                 prev(AB_K // kvw), rows(kvw, AB_K // kvw), prev(AB_V // kvw), rows(kvw, AB_V // kvw),
                 pl.BlockSpec((N_ATTN_HEADS, 1, 2 * q, q), lambda b_, c: (0, 0, 0, 0))]
    return pl.pallas_call(
        _layer0_mixers_kernel,
        grid=(bsz, s // q),
        in_specs=ssd_specs + swa_specs,
        out_specs=(rows(SSD_INNER, 0), rows(ATTN_WIDTH, 0)),
        out_shape=(jax.ShapeDtypeStruct((bsz, s, SSD_INNER), BF16),
                   jax.ShapeDtypeStruct((bsz, s, ATTN_WIDTH), BF16)),
        scratch_shapes=[pltpu.VMEM((SSD_GROUPS, SSD_STATE, SSD_INNER // SSD_GROUPS), F32),
                        pltpu.VMEM((N_ATTN_HEADS, 2 * q, q), F32)],
        compiler_params=_cparams(("arbitrary", "arbitrary")),
        name="ssd_swa_mixers",
    )(xbc_act, proj, proj, dt_rows, bias_c, bias_c.reshape(LANES, 1),
      alog_c, alog_c.reshape(LANES, 1), jnp.repeat(d_skip, SSD_HEAD_DIM).reshape(1, SSD_INNER),
      norm_w.reshape(1, SSD_INNER), jnp.asarray(e_np, BF16),
      sinks, proj, proj, proj, proj, proj, swa_bias)


def _swa_kernel(sink_ref, q_ref, kp_ref, kc_ref, vp_ref, vc_ref, bias_ref, o_ref, s_ref):
    blk = SWA_BLOCK
    n = pl.program_id(1)
    grp = N_ATTN_HEADS // SWA_KV_HEADS
    scale = ATTN_HEAD_DIM ** -0.5
    kk = jnp.concatenate([kp_ref[0], kc_ref[0]], axis=0).astype(F32) * (scale * LOG2E)
    vv_t = jnp.concatenate([vp_ref[0], vc_ref[0]], axis=0).astype(F32).T
    lane_half = lax.broadcasted_iota(jnp.int32, (1, LANES), 1) >> 6
    c = lax.broadcasted_iota(jnp.int32, (2 * blk, blk), 0)
    valid = (c >= blk) | (n > 0)
    ms = []
    for kv in range(SWA_KV_HEADS):
        k_own = jnp.where(lane_half == kv, kk, 0.0).astype(BF16)
        k_var = [None, None]
        k_var[kv] = k_own
        k_var[1 - kv] = pltpu.roll(jnp.where(lane_half == kv, kk, 0.0), ATTN_HEAD_DIM, axis=1).astype(BF16)
        for gq in range(grp):
            h = kv * grp + gq
            q_tile = q_ref[0, :, (h // 2) * LANES:(h // 2 + 1) * LANES]
            s_t = _bdot_nt(k_var[h % 2], q_tile) + bias_ref[h, 0]
            s_t = jnp.where(valid, s_t, NEG)
            s_ref[h] = s_t
            ms.append(jnp.maximum(jnp.max(s_t, axis=0, keepdims=True), sink_ref[h] * LOG2E))
    outs = []
    for h in range(N_ATTN_HEADS):
        kv = h // grp
        p = jnp.exp2(s_ref[h] - ms[h])
        l = jnp.sum(p, axis=0, keepdims=True) + jnp.exp2(sink_ref[h] * LOG2E - ms[h])
        outs.append(_bdot(vv_t[kv * ATTN_HEAD_DIM:(kv + 1) * ATTN_HEAD_DIM, :], p) / l)
    for t in range(N_ATTN_HEADS // 2):
        pair = jnp.concatenate([outs[2 * t], outs[2 * t + 1]], axis=0)
        o_ref[0, :, t * LANES:(t + 1) * LANES] = pair.T.astype(o_ref.dtype)


def _moba_kernel(q_ref, k_ref, v_ref, bias_ref, o_ref, vt_ref, qm_ref, negrow_ref, s_ref, *, nblk):
    mb = MOBA_BLOCK
    dh = ATTN_HEAD_DIM
    own0 = 2 * lax.rem(pl.program_id(1), nblk // 2)
    scale = dh ** -0.5

    @pl.when(own0 == 0)
    def _():
        means = []
        for j in range(nblk):
            vt_ref[j] = v_ref[0, j * mb:(j + 1) * mb, :].astype(F32).T.astype(BF16)
            means.append(jnp.mean(k_ref[0, j * mb:(j + 1) * mb, :].astype(F32), axis=0, keepdims=True))
        kmean = jnp.concatenate(means, axis=0)
        lane_half = lax.broadcasted_iota(jnp.int32, (1, LANES), 1) >> 6
        blk_id = lax.broadcasted_iota(jnp.int32, (nblk, mb), 0)
        for qb in range(nblk):
            q = q_ref[0, qb * mb:(qb + 1) * mb, :].astype(F32) * (scale * LOG2E)
            for hh in range(2):
                qm = jnp.where(lane_half == hh, q, 0.0)
                qm_ref[hh, qb] = qm.astype(BF16)
                gate = lax.dot_general(kmean, qm, (((1,), (1,)), ((), ())),
                                       preferred_element_type=F32, precision=HIGHEST)
                gate = jnp.where(blk_id < qb, gate, NEG)
                rank = jnp.zeros((nblk, mb), jnp.int32)
                for i in range(nblk):
                    gi = gate[i:i + 1, :]
                    ahead = (gi > gate) | ((gi == gate) & (i < blk_id))
                    rank = rank + jnp.where(ahead, 1, 0)
                keep = ((rank < MOBA_TOPK) & (blk_id < qb)) | (blk_id == qb)
                negrow_ref[hh, qb] = jnp.where(keep, 0.0, NEG).astype(F32)

    def attend(nb, qi):
        own = own0 + qi
        qms = [qm_ref[hh, own] for hh in range(2)]
        negrows = [negrow_ref[hh, own] for hh in range(2)]
        m8 = [None, None]
        l8 = [None, None]
        acc = [None, None]

        def scores(hh, j):
            s_t = _bdot_nt(k_ref[0, j * mb:(j + 1) * mb, :], qms[hh]) + bias_ref[hh, jnp.maximum(own - j, 0)]
            s_t = s_t + negrows[hh][j:j + 1, :]
            s_ref[qi, hh, j] = s_t
            m_j = jnp.max(s_t.reshape(mb // SUBLANES, SUBLANES, mb), axis=0)
            m8[hh] = m_j if m8[hh] is None else jnp.maximum(m8[hh], m_j)

        def weights(hh, j, m):
            p = jnp.exp2(s_ref[qi, hh, j] - m)
            l_j = jnp.sum(p.reshape(mb // SUBLANES, SUBLANES, mb), axis=0)
            a_j = _bdot(vt_ref[j, hh * dh:(hh + 1) * dh, :], p)
            l8[hh] = l_j if l8[hh] is None else l8[hh] + l_j
            acc[hh] = a_j if acc[hh] is None else acc[hh] + a_j

        for j in range(nb):
            for hh in range(2):
                scores(hh, j)
        m = [jnp.max(m8[hh], axis=0, keepdims=True) for hh in range(2)]
        for j in range(nb):
            for hh in range(2):
                weights(hh, j, m[hh])
        out_t = jnp.concatenate([acc[hh] / jnp.sum(l8[hh], axis=0, keepdims=True) for hh in range(2)], axis=0)
        o_ref[0, qi * mb:(qi + 1) * mb, :] = out_t.T.astype(o_ref.dtype)

    def attend_both(nb):
        attend(nb, 0)
        attend(nb, 1)

    for nb in range(2, nblk + 1, 2):
        pl.when(own0 == nb - 2)(functools.partial(attend_both, nb))


def _moba_specs(bsz, s):
    mb = MOBA_BLOCK
    nblk = s // mb
    half = nblk // 2
    col = lambda c0: (lambda b_, c: (b_, 0, c0 // LANES + c // half))
    in_specs = [pl.BlockSpec((1, s, LANES), col(CD_QD)),
                pl.BlockSpec((1, s, LANES), col(CD_KD)),
                pl.BlockSpec((1, s, LANES), col(CD_VD)),
                pl.BlockSpec((2, nblk, mb, mb), lambda b_, c: (c // half, 0, 0, 0))]
    out_spec = pl.BlockSpec((1, 2 * mb, LANES), lambda b_, c: (b_, c % half, c // half))
    out_shape = jax.ShapeDtypeStruct((bsz, s, ATTN_WIDTH), BF16)
    scratch = [pltpu.VMEM((nblk, LANES, mb), BF16),
               pltpu.VMEM((2, nblk, mb, LANES), BF16),
               pltpu.VMEM((2, nblk, nblk, mb), F32),
               pltpu.VMEM((2, 2, nblk, mb, mb), F32)]
    return in_specs, out_spec, out_shape, scratch


def _gdn_kernel(qkv_ref, z_ref, bac_ref, bar_ref, bias_c_ref, bias_r_ref, alog_c_ref, alog_r_ref,
                nw_ref, o_ref, state_ref, m_ref, attn_ref, t_ref, x_ref, rhs_ref, u_ref, wq_ref,
                kdt_ref, r_ref, glc_ref):
    t = 2 * GDN_CHUNK
    ck = GDN_CHUNK
    dk = GDN_HEAD_DIM
    nh = GDN_HEADS

    @pl.when(pl.program_id(1) == 0)
    def _():
        state_ref[...] = jnp.zeros(state_ref.shape, F32)

    row = lax.broadcasted_iota(jnp.int32, (t, t), 0)
    col = lax.broadcasted_iota(jnp.int32, (t, t), 1)
    same = (row >> 6) == (col >> 6)
    tril = same & (row >= col)
    strict = same & (row > col)
    tri_b = jnp.where(tril, 1.0, 0.0).astype(BF16)
    triu_b = jnp.where(same & (row <= col), 1.0, 0.0).astype(BF16)
    blk_b = jnp.where(same, 1.0, 0.0).astype(BF16)
    eye_f = jnp.where(row == col, 1.0, 0.0).astype(F32)
    merge_masks = [((row >> (l + 1)) == (col >> (l + 1))) & (((row >> l) & 1) == 1) & (((col >> l) & 1) == 0)
                   for l in range(int(math.log2(ck)))]
    mask_bf = [jnp.where(m, 1.0, 0.0).astype(BF16) for m in merge_masks[1:]]

    ba_c = bac_ref[0].astype(F32)
    g_c = -jnp.exp(alog_c_ref[...]) * _softplus(ba_c + bias_c_ref[...])
    sums_c = _sum01_left(jnp.concatenate([tri_b, blk_b], axis=0), g_c)
    gc_c = sums_c[:t]
    gl_c = sums_c[t:]
    g_r = -jnp.exp(alog_r_ref[...]) * _softplus(bar_ref[0] + bias_r_ref[...])
    gc_r = _sum01_right(g_r, triu_b)
    glc_ref[...] = gl_c

    def prep(h):
        q = qkv_ref[0, :, h * dk:(h + 1) * dk].astype(F32)
        k = qkv_ref[0, :, (nh + h) * dk:(nh + h + 1) * dk].astype(F32)
        v = qkv_ref[0, :, (2 * nh + h) * dk:(2 * nh + h + 1) * dk].astype(F32)
        qn = q * lax.rsqrt(jnp.sum(q * q, axis=-1, keepdims=True) + 1e-6) * (dk ** -0.5)
        kn = k * lax.rsqrt(jnp.sum(k * k, axis=-1, keepdims=True) + 1e-6)
        beta = _sigmoid(ba_c[:, h:h + 1])
        gcc = gc_c[:, nh + h:nh + h + 1]
        gcr = gc_r[nh + h:nh + h + 1, :]
        glc = gl_c[:, nh + h:nh + h + 1]
        decay = jnp.where(tril, jnp.exp(jnp.where(tril, gcc - gcr, 0.0)), 0.0)
        kb = kn * beta
        kk = _bdot_nt(jnp.concatenate([kb, qn], axis=0), kn)
        mm = jnp.where(strict, kk[:t] * decay, 0.0)
        m_ref[h] = mm.astype(BF16)
        attn_ref[h] = (kk[t:] * decay).astype(BF16)
        t_ref[h] = eye_f - jnp.where(merge_masks[0], mm, 0.0)
        egc = jnp.exp(gcc)
        rhs_ref[h] = jnp.concatenate([v * beta, kb * egc], axis=1).astype(BF16)
        q_dec = (qn * egc).astype(BF16)
        for a in range(2):
            wq_ref[h, a, ck:, :] = q_dec[a * ck:(a + 1) * ck]
        kdt_ref[h] = (kn * jnp.exp(glc - gcc)).T.astype(BF16)

    def merge_a(lvl, h):
        x_ref[h] = jnp.dot(t_ref[h].astype(BF16), m_ref[h] * mask_bf[lvl],
                           preferred_element_type=F32).astype(BF16)

    def merge_b(lvl, h):
        t_h = t_ref[h]
        t_ref[h] = t_h - jnp.dot(x_ref[h], t_h.astype(BF16), preferred_element_type=F32)

    def solve(h):
        sol = jnp.dot(t_ref[h].astype(BF16), rhs_ref[h], preferred_element_type=F32)
        u_ref[h] = sol[:, :dk]
        for a in range(2):
            wq_ref[h, a, :ck, :] = sol[a * ck:(a + 1) * ck, dk:].astype(BF16)

    zeros_half = jnp.zeros((ck, dk), F32)

    def read_state(a, h):
        r_ref[h] = jnp.dot(wq_ref[h, a], state_ref[h].astype(BF16), preferred_element_type=F32)

    def update(a, h):
        sl = slice(a * ck, (a + 1) * ck)
        v_new = u_ref[h, sl, :] - r_ref[h, :ck, :]
        v_full = jnp.concatenate([v_new, zeros_half] if a == 0 else [zeros_half, v_new], axis=0).astype(BF16)
        o = r_ref[h, ck:, :] + jnp.dot(attn_ref[h, sl, :], v_full, preferred_element_type=F32)
        gl = glc_ref[a * ck:a * ck + 1, nh + h:nh + h + 1]
        state_ref[h] = state_ref[h] * jnp.exp(gl) + jnp.dot(kdt_ref[h], v_full, preferred_element_type=F32)
        ms = jnp.mean(o * o, axis=-1, keepdims=True)
        y = o * lax.rsqrt(ms + RMS_EPS) * nw_ref[...] * _silu(z_ref[0, sl, h * dk:(h + 1) * dk].astype(F32))
        o_ref[0, sl, h * dk:(h + 1) * dk] = y.astype(o_ref.dtype)

    def steps(phase, heads):
        if phase == "prep":
            return [functools.partial(prep, h) for h in heads]
        if phase == "merge":
            return [functools.partial(fn, lvl, h) for lvl in range(len(mask_bf))
                    for fn in (merge_a, merge_b) for h in heads]
        return ([functools.partial(solve, h) for h in heads]
                + [functools.partial(fn, a, h) for a in range(2) for fn in (read_state, update) for h in heads])

    for phase in ("prep", "merge", "tail"):
        for step in steps(phase, range(nh)):
            step()


N_GDN_IN, N_GDN_SCRATCH, N_MOBA_IN = 9, 11, 4


def _layer1_mixers_kernel(*refs, nblk):
    gdn_in = refs[:N_GDN_IN]
    moba_in = refs[N_GDN_IN:N_GDN_IN + N_MOBA_IN]
    o_gdn, o_moba = refs[N_GDN_IN + N_MOBA_IN:N_GDN_IN + N_MOBA_IN + 2]
    scratch = refs[N_GDN_IN + N_MOBA_IN + 2:]
    _gdn_kernel(*gdn_in, o_gdn, *scratch[:N_GDN_SCRATCH])
    _moba_kernel(*moba_in, o_moba, *scratch[N_GDN_SCRATCH:], nblk=nblk)


def _layer1_mixers(qkv_act, proj, ba_rows, dt_bias, a_log, norm_w, moba_bias):
    bsz, s, _ = proj.shape
    t = 2 * GDN_CHUNK
    nh = GDN_HEADS
    dk = GDN_HEAD_DIM
    nblk = s // MOBA_BLOCK
    assert s // t == (N_ATTN_HEADS // 2) * (nblk // 2)
    bias_c = jnp.pad(dt_bias, (nh, LANES - 2 * nh)).reshape(1, LANES)
    alog_c = jnp.pad(a_log, (nh, LANES - 2 * nh)).reshape(1, LANES)
    small = lambda shape: pl.BlockSpec(shape, lambda b_, c: (0, 0))
    moba_in, moba_out, moba_shape, moba_scratch = _moba_specs(bsz, s)
    gdn_in = [pl.BlockSpec((1, t, 3 * GDN_INNER), lambda b_, c: (b_, c, 0)),
              pl.BlockSpec((1, t, GDN_INNER), lambda b_, c: (b_, c, CD_Z // GDN_INNER)),
              pl.BlockSpec((1, t, LANES), lambda b_, c: (b_, c, CD_BA // LANES)),
              pl.BlockSpec((1, LANES, t), lambda b_, c: (b_, 0, c)),
              small((1, LANES)), small((LANES, 1)), small((1, LANES)), small((LANES, 1)),
              small((1, dk))]
    assert len(gdn_in) == N_GDN_IN and len(moba_in) == N_MOBA_IN
    return pl.pallas_call(
        functools.partial(_layer1_mixers_kernel, nblk=nblk),
        grid=(bsz, s // t),
        in_specs=gdn_in + moba_in,
        out_specs=(pl.BlockSpec((1, t, GDN_INNER), lambda b_, c: (b_, c, 0)), moba_out),
        out_shape=(jax.ShapeDtypeStruct((bsz, s, GDN_INNER), BF16), moba_shape),
        scratch_shapes=[pltpu.VMEM((nh, dk, dk), F32),
                        pltpu.VMEM((nh, t, t), BF16),
                        pltpu.VMEM((nh, t, t), BF16),
                        pltpu.VMEM((nh, t, t), F32),
                        pltpu.VMEM((nh, t, t), BF16),
                        pltpu.VMEM((nh, t, 2 * dk), BF16),
                        pltpu.VMEM((nh, t, dk), F32),
                        pltpu.VMEM((nh, 2, t, dk), BF16),
                        pltpu.VMEM((nh, dk, t), BF16),
                        pltpu.VMEM((nh, t, dk), F32),
                        pltpu.VMEM((t, LANES), F32)]
        + moba_scratch,
        compiler_params=_cparams(("arbitrary", "arbitrary")),
        name="gdn_moba_mixers",
    )(qkv_act, proj, proj, ba_rows, bias_c, bias_c.reshape(LANES, 1), alog_c, alog_c.reshape(LANES, 1),
      norm_w.reshape(1, dk), proj, proj, proj, moba_bias)


def _reorder_kernel(w_ref, o_ref, *, segments, zero_from):
    rows, n = o_ref.shape
    if zero_from < n:
        o_ref[:, zero_from:] = jnp.zeros((rows, n - zero_from), o_ref.dtype)
    for lo, hi, dst in segments:
        o_ref[:, dst:dst + hi - lo] = w_ref[:, lo:hi].astype(o_ref.dtype)


def _reorder_cast(w_stack, layer, segments, zero_from, n_out, tr=64):
    nl, d, n = w_stack.shape
    if nl == 1:
        w_stack = w_stack.reshape(d, n)
        w_spec = pl.BlockSpec((tr, n), lambda r: (r, 0))
    else:
        w_spec = pl.BlockSpec((None, tr, n), lambda r: (layer, r, 0))
    return pl.pallas_call(
        functools.partial(_reorder_kernel, segments=segments, zero_from=zero_from),
        grid=(d // tr,),
        in_specs=[w_spec],
        out_specs=pl.BlockSpec((tr, n_out), lambda r: (r, 0)),
        out_shape=jax.ShapeDtypeStruct((d, n_out), BF16),
        compiler_params=_cparams(("arbitrary",)),
        name="reorder_cast_weights",
    )(w_stack)


def _gate_rows(proj, col0):
    return jnp.swapaxes(proj[:, :, col0:col0 + LANES].astype(F32), 1, 2)


def kernel(x, c, rel_bias, norm_w, ada_w, ada_b, ab_w_in, ab_w_out, ssd_conv_w, ssd_conv_b,
           ssd_dt_bias, ssd_a_log, ssd_d, ssd_norm_w, swa_sinks, cd_w_in, cd_w_out, gdn_conv_w,
           gdn_dt_bias, gdn_a_log, gdn_norm_w, ffn_w_up, ffn_conv_w, ffn_conv_b, ffn_w_down):
    bsz, s, d = x.shape
    depth = norm_w.shape[0]
    mods = _mods(c, ada_w, ada_b)
    swa_bias = _bias_tiles(rel_bias, _swa_bucket_idx())
    moba_bias = _bias_tiles(rel_bias, _moba_bucket_idx(s // MOBA_BLOCK))

    for i in range(depth):
        sh_m, sc_m, g_m, sh_f, sc_f, g_f = [m.reshape(bsz, 1, d) for m in jnp.split(mods[i], 6, axis=-1)]
        j = i // 2
        if i % 2 == 0:
            dt0 = SSD_INNER + SSD_XBC
            n_in = ab_w_in.shape[-1]
            w_in = _reorder_cast(ab_w_in, j, ((SSD_INNER, dt0, 0), (0, SSD_INNER, SSD_XBC + AB_Z),
                                              (dt0 + SSD_HEADS, n_in, SSD_XBC + AB_Q), (dt0, dt0 + SSD_HEADS, SSD_XBC + AB_DT)),
                                 SSD_XBC + AB_DT, SSD_XBC + AB_COLS)
            xbc_act, proj = _proj_conv_act(x, norm_w[i, 0], sc_m, sh_m, w_in, ssd_conv_w[j], ssd_conv_b[j],
                                           "silu", n_plain=AB_COLS)
            y_a, y_b = _layer0_mixers(xbc_act, proj, _gate_rows(proj, AB_DT), ssd_dt_bias[j], ssd_a_log[j],
                                      ssd_d[j], ssd_norm_w[j], swa_sinks[j], swa_bias)
            x = _matmul_resid([y_a, y_b], ab_w_out, j, x, g_m, norm_w[i, 1])
        else:
            nqkv = 3 * GDN_INNER
            ba0 = nqkv + GDN_INNER
            n_in = cd_w_in.shape[-1]
            w_in = _reorder_cast(cd_w_in, j, ((0, ba0, 0), (ba0 + 2 * GDN_HEADS, n_in, nqkv + CD_QD),
                                              (ba0, ba0 + 2 * GDN_HEADS, nqkv + CD_BA)),
                                 nqkv + CD_BA, nqkv + CD_COLS)
            qkv_act, proj = _proj_conv_act(x, norm_w[i, 0], sc_m, sh_m, w_in, gdn_conv_w[j],
                                           jnp.zeros((nqkv,), F32), "silu", n_plain=CD_COLS)
            y_c, y_d = _layer1_mixers(qkv_act, proj, _gate_rows(proj, CD_BA), gdn_dt_bias[j], gdn_a_log[j],
                                      gdn_norm_w[j], moba_bias)
            x = _matmul_resid([y_c, y_d], cd_w_out, j, x, g_m, norm_w[i, 1])
        n_up = ffn_w_up.shape[-1]
        w_up = _reorder_cast(ffn_w_up, i, ((0, n_up, 0),), n_up, n_up)
        act = _proj_conv_act(x, norm_w[i, 2], sc_f, sh_f, w_up, ffn_conv_w[i], ffn_conv_b[i], "geglu")
        x = _matmul_resid([act], ffn_w_down, i, x, g_f, norm_w[i, 3])
    return x
```
